```python
import jax, jax.numpy as jnp
from jax import lax
import numpy as np

D_MODEL = 1024
BATCH = 2
SEQ = 16384
DEPTH = 1

N_MLSTM_HEADS = 4
MLSTM_WIDTH = D_MODEL // 2
V_HEAD_DIM = MLSTM_WIDTH // N_MLSTM_HEADS
QK_HEAD_DIM = V_HEAD_DIM // 2
QK_WIDTH = N_MLSTM_HEADS * QK_HEAD_DIM
N_GATES = 4 * N_MLSTM_HEADS
CHUNK = 128
N_FOURIER_GROUPS = 4
FOURIER_WIDTH = D_MODEL - MLSTM_WIDTH
FOURIER_GROUP_DIM = FOURIER_WIDTH // N_FOURIER_GROUPS
MIX_WIDTH = MLSTM_WIDTH + FOURIER_WIDTH
IN_SPLITS = [QK_WIDTH, QK_WIDTH, MLSTM_WIDTH, MLSTM_WIDTH, FOURIER_WIDTH, N_GATES]
IN_COLS = sum(IN_SPLITS)
D_FF = 4 * D_MODEL
N_MOD = 6
ALPHA = (2 * DEPTH) ** 0.25
BETA = (8 * DEPTH) ** -0.25
LN_EPS = 1e-5

kernel_name = "hybrid_mlstm_fnet_deepnorm_adaln_block"


def _ln_plain(x):
    xf = x.astype(jnp.float32)
    mu = xf.mean(-1, keepdims=True)
    var = jnp.square(xf - mu).mean(-1, keepdims=True)
    return ((xf - mu) * lax.rsqrt(var + LN_EPS)).astype(x.dtype)


def _ln_affine(x, g, b):
    xf = x.astype(jnp.float32)
    mu = xf.mean(-1, keepdims=True)
    var = jnp.square(xf - mu).mean(-1, keepdims=True)
    y = (xf - mu) * lax.rsqrt(var + LN_EPS) * g.astype(jnp.float32) + b.astype(jnp.float32)
    return y.astype(x.dtype)


def _mlstm_chunkwise(q, k, v, log_i, log_f):
    B, H, S, dk = q.shape
    dv = v.shape[-1]
    nc = S // CHUNK

    def to_chunks(a):
        return jnp.moveaxis(a.reshape((B, H, nc, CHUNK) + a.shape[3:]), 2, 0)

    xs = tuple(to_chunks(a) for a in (q, k, v, log_i, log_f))
    lower = jnp.tril(jnp.ones((CHUNK, CHUNK), dtype=bool))

    def step(carry, blk):
        C, n, m = carry
        qb, kb, vb, ib, fb = blk
        b = jnp.cumsum(fb, axis=-1)
        D = jnp.where(lower, b[..., :, None] - b[..., None, :] + ib[..., None, :], -jnp.inf)
        inter = b + m[..., None]
        m_t = jnp.maximum(inter, D.max(-1))
        w_intra = jnp.exp(D - m_t[..., None])
        w_inter = jnp.exp(inter - m_t)
        s = jnp.einsum('bhtd,bhsd->bhts', qb, kb) * w_intra
        num = jnp.einsum('bhts,bhsv->bhtv', s, vb) + w_inter[..., None] * jnp.einsum('bhtd,bhdv->bhtv', qb, C)
        den = s.sum(-1) + w_inter * jnp.einsum('bhtd,bhd->bht', qb, n)
        h = num / jnp.maximum(jnp.abs(den), jnp.exp(-m_t))[..., None]
        bL = b[..., -1]
        g = bL[..., None] - b + ib
        m_new = jnp.maximum(bL + m, g.max(-1))
        wk = jnp.exp(g - m_new[..., None])
        decay = jnp.exp(bL + m - m_new)
        C_new = decay[..., None, None] * C + jnp.einsum('bhs,bhsd,bhsv->bhdv', wk, kb, vb)
        n_new = decay[..., None] * n + jnp.einsum('bhs,bhsd->bhd', wk, kb)
        return (C_new, n_new, m_new), h

    init = (jnp.zeros((B, H, dk, dv), jnp.float32), jnp.zeros((B, H, dk), jnp.float32),
            jnp.zeros((B, H), jnp.float32))
    _, hs = lax.scan(step, init, xs)
    return jnp.moveaxis(hs, 0, 2).reshape(B, H, S, dv)


def _mlstm_group(q, k, v, o, gates, b_gate, norm_w):
    B, S, _ = q.shape
    H = N_MLSTM_HEADS
    f32 = jnp.float32
    qh = q.astype(f32).reshape(B, S, H, QK_HEAD_DIM).transpose(0, 2, 1, 3) * (QK_HEAD_DIM ** -0.5)
    kh = k.astype(f32).reshape(B, S, H, QK_HEAD_DIM).transpose(0, 2, 1, 3)
    vh = v.astype(f32).reshape(B, S, H, V_HEAD_DIM).transpose(0, 2, 1, 3)
    gt = (gates.astype(f32) + b_gate.astype(f32)).reshape(B, S, 4, H).transpose(2, 0, 3, 1)
    log_i_f, log_f_f = gt[0], jax.nn.log_sigmoid(gt[1])
    log_i_b, log_f_b = gt[2], jax.nn.log_sigmoid(gt[3])
    h_fwd = _mlstm_chunkwise(qh, kh, vh, log_i_f, log_f_f)
    rev = lambda a: jnp.flip(a, axis=2)
    h_bwd = rev(_mlstm_chunkwise(rev(qh), rev(kh), rev(vh), rev(log_i_b), rev(log_f_b)))
    h = h_fwd + h_bwd
    mu = h.mean(-1, keepdims=True)
    var = jnp.square(h - mu).mean(-1, keepdims=True)
    h = (h - mu) * lax.rsqrt(var + LN_EPS) * norm_w.astype(f32).reshape(H, 1, V_HEAD_DIM)
    h = h.transpose(0, 2, 1, 3).reshape(B, S, MLSTM_WIDTH)
    return (h * jax.nn.sigmoid(o.astype(f32))).astype(q.dtype)


def _fourier_group(z):
    B, S, _ = z.shape
    zg = z.astype(jnp.float32).reshape(B, S, N_FOURIER_GROUPS, FOURIER_GROUP_DIM)
    y = jnp.fft.fft2(zg, axes=(1, 3), norm='ortho').real
    return y.reshape(B, S, FOURIER_WIDTH).astype(z.dtype)


def _layer(x, c_act, w_ada, b_ada, w_in, b_gate, mlstm_norm_w, w_out, ln1_g, ln1_b,
           w_ff1, b_ff1, w_ff2, b_ff2, ln2_g, ln2_b):
    mod = (c_act @ w_ada + b_ada)[:, None, :]
    sh1, sc1, g1, sh2, sc2, g2 = jnp.split(mod, N_MOD, axis=-1)
    h = _ln_plain(x) * (1 + sc1) + sh1
    proj = h @ w_in
    q, k, v, o, fz, gates = jnp.split(proj, np.cumsum(IN_SPLITS)[:-1].tolist(), axis=-1)
    y_mlstm = _mlstm_group(q, k, v, o, gates, b_gate, mlstm_norm_w)
    y_fourier = _fourier_group(fz)
    mix = jnp.concatenate([y_mlstm, y_fourier], axis=-1) @ w_out
    x = _ln_affine(ALPHA * x + (1 + g1) * mix, ln1_g, ln1_b)
    h2 = _ln_plain(x) * (1 + sc2) + sh2
    ff = jnp.square(jax.nn.relu(h2 @ w_ff1 + b_ff1)) @ w_ff2 + b_ff2
    return _ln_affine(ALPHA * x + (1 + g2) * ff, ln2_g, ln2_b)


def setup_inputs(seed: int = 0) -> dict:
    key = jax.random.key(seed)
    ks = jax.random.split(key, 20)
    f32 = jnp.float32
    nrm = lambda k, shape, s: jax.random.normal(k, shape, f32) * s
    f_bias = jnp.linspace(3.0, 6.0, N_MLSTM_HEADS, dtype=f32)
    gate_base = jnp.concatenate([jnp.zeros((N_MLSTM_HEADS,), f32), f_bias,
                                 jnp.zeros((N_MLSTM_HEADS,), f32), f_bias])
    return {
        "x": nrm(ks[0], (BATCH, SEQ, D_MODEL), 1.0),
        "c": nrm(ks[1], (BATCH, D_MODEL), 1.0),
        "w_ada": nrm(ks[2], (DEPTH, D_MODEL, N_MOD * D_MODEL), 0.1 * D_MODEL ** -0.5),
        "b_ada": nrm(ks[3], (DEPTH, N_MOD * D_MODEL), 0.01),
        "w_in": nrm(ks[4], (DEPTH, D_MODEL, IN_COLS), D_MODEL ** -0.5),
        "b_gate": gate_base[None, :] + nrm(ks[5], (DEPTH, N_GATES), 0.1),
        "mlstm_norm_w": 1.0 + nrm(ks[6], (DEPTH, MLSTM_WIDTH), 0.02),
        "w_out": nrm(ks[7], (DEPTH, MIX_WIDTH, D_MODEL), BETA * MIX_WIDTH ** -0.5),
        "ln1_g": 1.0 + nrm(ks[8], (DEPTH, D_MODEL), 0.02),
        "ln1_b": nrm(ks[9], (DEPTH, D_MODEL), 0.02),
        "w_ff1": nrm(ks[10], (DEPTH, D_MODEL, D_FF), D_MODEL ** -0.5),
        "b_ff1": nrm(ks[11], (DEPTH, D_FF), 0.02),
        "w_ff2": nrm(ks[12], (DEPTH, D_FF, D_MODEL), BETA * D_FF ** -0.5),
        "b_ff2": nrm(ks[13], (DEPTH, D_MODEL), 0.02),
        "ln2_g": 1.0 + nrm(ks[14], (DEPTH, D_MODEL), 0.02),
        "ln2_b": nrm(ks[15], (DEPTH, D_MODEL), 0.02),
    }


def reference(x, c, w_ada, b_ada, w_in, b_gate, mlstm_norm_w, w_out, ln1_g, ln1_b,
              w_ff1, b_ff1, w_ff2, b_ff2, ln2_g, ln2_b):
    c_act = jax.nn.silu(c)
    for l in range(DEPTH):
        x = _layer(x, c_act, w_ada[l], b_ada[l], w_in[l], b_gate[l], mlstm_norm_w[l], w_out[l],
                   ln1_g[l], ln1_b[l], w_ff1[l], b_ff1[l], w_ff2[l], b_ff2[l], ln2_g[l], ln2_b[l])
    return x
```

```python
import functools

import numpy as np
import jax
import jax.numpy as jnp
from jax import lax
from jax.experimental import pallas as pl
from jax.experimental.pallas import tpu as pltpu

F32 = jnp.float32
BF16 = jnp.bfloat16

CHUNK = 128
N_HEADS = 4
QK_DIM = 64
V_DIM = 128
N_GROUPS = 4
GROUP_DIM = 128
N_MOD = 6
LN_EPS = 1e-5
MASKED = -1e30

_NT = (((1,), (1,)), ((), ()))


def _dot(a, b):
    return jnp.dot(a, b, preferred_element_type=F32)


def _dot_nt(a, b):
    return lax.dot_general(a, b, _NT, preferred_element_type=F32)


def _ln_plain(x):
    mu = jnp.mean(x, axis=-1, keepdims=True)
    xc = x - mu
    var = jnp.mean(xc * xc, axis=-1, keepdims=True)
    return xc * lax.rsqrt(var + LN_EPS)


def _log_sigmoid(x):
    return jnp.minimum(x, 0.0) - jnp.log1p(jnp.exp(-jnp.abs(x)))


def _split3(x):
    hi = x.astype(BF16)
    r1 = x - hi.astype(F32)
    mid = r1.astype(BF16)
    lo = (r1 - mid.astype(F32)).astype(BF16)
    return hi, mid, lo


def _vmem_params(semantics, limit_mb):
    return pltpu.CompilerParams(dimension_semantics=semantics, vmem_limit_bytes=limit_mb * 1024 * 1024)


def _adaln_kernel(ct_ref, w_ref, b_ref, o_ref):
    ct = ct_ref[...]
    act = ct * jax.nn.sigmoid(ct)
    w = w_ref[...]
    for b in range(ct.shape[1]):
        o_ref[b:b + 1, :] = jnp.sum(act[:, b:b + 1] * w, axis=0, keepdims=True) + b_ref[...]


def _adaln(c, w_ada, b_ada):
    bsz, d = c.shape
    n = w_ada.shape[1]
    tn = 1024
    return pl.pallas_call(
        _adaln_kernel,
        grid=(n // tn,),
        in_specs=[pl.BlockSpec((d, bsz), lambda j: (0, 0)),
                  pl.BlockSpec((d, tn), lambda j: (0, j)),
                  pl.BlockSpec((1, tn), lambda j: (0, j))],
        out_specs=pl.BlockSpec((bsz, tn), lambda j: (0, j)),
        out_shape=jax.ShapeDtypeStruct((bsz, n), F32),
        compiler_params=_vmem_params(("parallel",), 32),
        name="adaln",
    )(c.T, w_ada, b_ada.reshape(1, n))


def _inproj_kernel(x_ref, sh_ref, sc_ref, wq_ref, wkt_ref, wv_ref, wo_ref, wf_ref, wg_ref, wgt_ref,
                   q_ref, kt_ref, v_ref, o_ref, fz_ref, g_ref, gt_ref):
    h = _ln_plain(x_ref[...]) * (1.0 + sc_ref[...]) + sh_ref[...]
    hb = h.astype(BF16)
    q_ref[...] = _dot(hb, wq_ref[...]).astype(BF16)
    kt_ref[...] = _dot_nt(wkt_ref[...], hb).astype(BF16)
    v_ref[...] = _dot(hb, wv_ref[...]).astype(BF16)
    o_ref[...] = _dot(hb, wo_ref[...]).astype(BF16)
    fz_ref[...] = _dot(hb, wf_ref[...])
    for d in range(2):
        g_ref[d] = _dot(hb, wg_ref[d])
        gt_ref[d] = _dot_nt(wgt_ref[d], hb)


def _inproj(x2d, mod4, w, seq, tm):
    t, d = x2d.shape
    spb = seq // tm
    qk = N_HEADS * QK_DIM
    vw = N_HEADS * V_DIM
    fw = N_GROUPS * GROUP_DIM
    full = lambda a: pl.BlockSpec(a.shape, lambda i: (0,) * a.ndim)
    modspec = lambda j: pl.BlockSpec((None, None, 1, d), lambda i: (i // spb, j, 0, 0))
    return pl.pallas_call(
        _inproj_kernel,
        grid=(t // tm,),
        in_specs=[pl.BlockSpec((tm, d), lambda i: (i, 0)), modspec(0), modspec(1),
                  full(w["wq"]), full(w["wkt"]), full(w["wv"]), full(w["wo"]), full(w["wf"]),
                  full(w["wg"]), full(w["wgt"])],
        out_specs=[pl.BlockSpec((tm, qk), lambda i: (i, 0)),
                   pl.BlockSpec((qk, tm), lambda i: (0, i)),
                   pl.BlockSpec((tm, vw), lambda i: (i, 0)),
                   pl.BlockSpec((tm, vw), lambda i: (i, 0)),
                   pl.BlockSpec((tm, fw), lambda i: (i, 0)),
                   pl.BlockSpec((2, tm, 8), lambda i: (0, i, 0)),
                   pl.BlockSpec((2, 8, tm), lambda i: (0, 0, i))],
        out_shape=[jax.ShapeDtypeStruct((t, qk), BF16),
                   jax.ShapeDtypeStruct((qk, t), BF16),
                   jax.ShapeDtypeStruct((t, vw), BF16),
                   jax.ShapeDtypeStruct((t, vw), BF16),
                   jax.ShapeDtypeStruct((t, fw), F32),
                   jax.ShapeDtypeStruct((2, t, 8), F32),
                   jax.ShapeDtypeStruct((2, 8, t), F32)],
        compiler_params=_vmem_params(("parallel",), 48),
        name="inproj",
    )(x2d, mod4, mod4, w["wq"], w["wkt"], w["wv"], w["wo"], w["wf"], w["wg"], w["wgt"])


def _gate_prep_kernel(g_ref, gt_ref, bc_ref, br_ref, cols_ref, rows_ref):
    tp = g_ref.shape[0]
    sgn = 1 - 2 * pl.program_id(0)
    r = lax.broadcasted_iota(jnp.int32, (tp, tp), 0)
    c = lax.broadcasted_iota(jnp.int32, (tp, tp), 1)
    same = (r // CHUNK) == (c // CHUNK)
    m_row = jnp.where(same & ((c - r) * sgn >= 0), 1.0, 0.0).astype(BF16)
    m_col = jnp.where(same & ((r - c) * sgn >= 0), 1.0, 0.0).astype(BF16)

    gt = gt_ref[...] + bc_ref[...]
    cum_r = sum(_dot(p, m_row) for p in _split3(_log_sigmoid(gt)))
    is_f_r = lax.broadcasted_iota(jnp.int32, gt.shape, 0) >= N_HEADS
    rows_ref[...] = jnp.where(is_f_r, cum_r, gt)

    g = g_ref[...] + br_ref[...]
    cum_c = sum(_dot(m_col, p) for p in _split3(_log_sigmoid(g)))
    is_f_c = lax.broadcasted_iota(jnp.int32, g.shape, 1) >= N_HEADS
    cols_ref[...] = jnp.where(is_f_c, cum_c, g)


def _gate_prep(g, gt, b_gate, tp):
    t = g.shape[1]
    bias = b_gate.astype(F32).reshape(2, 8)
    return pl.pallas_call(
        _gate_prep_kernel,
        grid=(2, t // tp),
        in_specs=[pl.BlockSpec((None, tp, 8), lambda d, i: (d, i, 0)),
                  pl.BlockSpec((None, 8, tp), lambda d, i: (d, 0, i)),
                  pl.BlockSpec((None, 8, 1), lambda d, i: (d, 0, 0)),
                  pl.BlockSpec((None, 1, 8), lambda d, i: (d, 0, 0))],
        out_specs=[pl.BlockSpec((None, tp, 8), lambda d, i: (d, i, 0)),
                   pl.BlockSpec((None, 8, tp), lambda d, i: (d, 0, i))],
        out_shape=[jax.ShapeDtypeStruct((2, t, 8), F32), jax.ShapeDtypeStruct((2, 8, t), F32)],
        compiler_params=_vmem_params(("parallel", "parallel"), 32),
        name="gate_prep",
    )(g, gt, bias.reshape(2, 8, 1), bias.reshape(2, 1, 8))


def _mlstm_kernel(q_ref, kt_ref, v_ref, rows_ref, cols_ref, h_ref, c_scr, m_scr):
    d = pl.program_id(1)

    @pl.when(pl.program_id(2) == 0)
    def _():
        c_scr[...] = jnp.zeros_like(c_scr)
        m_scr[...] = jnp.zeros_like(m_scr)

    sgn = 1 - 2 * d
    t_i = lax.broadcasted_iota(jnp.int32, (CHUNK, CHUNK), 0)
    s_i = lax.broadcasted_iota(jnp.int32, (CHUNK, CHUNK), 1)
    visible = (t_i - s_i) * sgn >= 0
    rows = rows_ref[...]
    cols = cols_ref[...]
    ones = jnp.ones((CHUNK, V_DIM), BF16)

    for hd in range(N_HEADS):
        a_row = rows[hd:hd + 1, :] - rows[N_HEADS + hd:N_HEADS + hd + 1, :]
        b_col = cols[:, N_HEADS + hd:N_HEADS + hd + 1]
        m_prev = m_scr[hd][0:1, 0:1]
        c_prev = c_scr[hd]

        logw = jnp.where(visible, b_col + a_row, MASKED)
        m_t = jnp.maximum(b_col + m_prev, jnp.max(logw, axis=1, keepdims=True))
        w_intra = jnp.exp(logw - m_t)
        w_inter = jnp.exp(b_col + m_prev - m_t)

        qh = q_ref[:, hd * QK_DIM:(hd + 1) * QK_DIM]
        kth = kt_ref[hd * QK_DIM:(hd + 1) * QK_DIM, :]
        v_aug = jnp.concatenate([v_ref[:, hd * V_DIM:(hd + 1) * V_DIM], ones], axis=1)

        s = (_dot(qh, kth) * w_intra).astype(BF16)
        nd = _dot(s, v_aug) + w_inter * _dot(qh, c_prev.astype(BF16))
        den = jnp.maximum(jnp.abs(nd[:, V_DIM:]), jnp.exp(-m_t))
        h_ref[:, hd * V_DIM:(hd + 1) * V_DIM] = nd[:, :V_DIM] / den

        b_end = jnp.where(d == 0, b_col[CHUNK - 1:CHUNK, :], b_col[0:1, :])
        m_new = b_end + jnp.maximum(m_prev, jnp.max(a_row, axis=1, keepdims=True))
        wk_row = jnp.exp(b_end + a_row - m_new)
        decay = jnp.exp(b_end + m_prev - m_new)
        kw = (kth.astype(F32) * wk_row).astype(BF16)
        c_scr[hd] = decay * c_prev + _dot(kw, v_aug)
        m_scr[hd] = jnp.broadcast_to(m_new, m_scr.shape[1:])


def _mlstm(q, kt, v, rows, cols, bsz, seq):
    t = q.shape[0]
    nc = seq // CHUNK
    qk = N_HEADS * QK_DIM
    vw = N_HEADS * V_DIM

    def blk(b, d, c):
        return b * nc + c + d * (nc - 1 - 2 * c)

    return pl.pallas_call(
        _mlstm_kernel,
        grid=(bsz, 2, nc),
        in_specs=[pl.BlockSpec((CHUNK, qk), lambda b, d, c: (blk(b, d, c), 0)),
                  pl.BlockSpec((qk, CHUNK), lambda b, d, c: (0, blk(b, d, c))),
                  pl.BlockSpec((CHUNK, vw), lambda b, d, c: (blk(b, d, c), 0)),
                  pl.BlockSpec((None, 8, CHUNK), lambda b, d, c: (d, 0, blk(b, d, c))),
                  pl.BlockSpec((None, CHUNK, 8), lambda b, d, c: (d, blk(b, d, c), 0))],
        out_specs=pl.BlockSpec((None, CHUNK, vw), lambda b, d, c: (d, blk(b, d, c), 0)),
        out_shape=jax.ShapeDtypeStruct((2, t, vw), F32),
        scratch_shapes=[pltpu.VMEM((N_HEADS, QK_DIM, 2 * V_DIM), F32),
                        pltpu.VMEM((N_HEADS, 8, 128), F32)],
        compiler_params=_vmem_params(("parallel", "parallel", "arbitrary"), 32),
        name="mlstm",
    )(q, kt, v, rows, cols)


@functools.lru_cache(maxsize=None)
def _fft_tables(seq):
    n1 = seq // CHUNK
    two_pi = 2.0 * np.pi
    k1 = np.arange(n1, dtype=np.int64)
    n = 128 * np.arange(n1, dtype=np.int64)[None, None, :] + np.arange(128, dtype=np.int64)[:, None, None]
    ang = two_pi * ((k1[None, :, None] * n) % seq).astype(np.float64) / seq
    ga = np.concatenate([np.cos(ang), -np.sin(ang)], axis=1)
    j = np.arange(128, dtype=np.int64)
    a128 = two_pi * ((j[:, None] * j[None, :]) % 128).astype(np.float64) / 128.0
    fr, fi = np.cos(a128), -np.sin(a128)
    fb = np.block([[fr, -fi], [fi, fr]])
    cs = np.concatenate([np.cos(a128), np.sin(a128)], axis=0) / np.sqrt(128.0 * seq)
    return tuple(np.asarray(a, dtype=np.float32) for a in (ga, fb, cs))


def _fft_a_kernel(z_ref, ga_ref, o_ref):
    for j in range(z_ref.shape[1]):
        o_ref[j] = _dot(ga_ref[j], z_ref[:, j, :].astype(BF16)).astype(BF16)


def _fft_b_kernel(br_ref, bi_ref, fb_ref, cs_ref, y_ref):
    x = _dot(fb_ref[...], jnp.concatenate([br_ref[...], bi_ref[...]], axis=0))
    cs = cs_ref[...]
    for g in range(x.shape[1] // GROUP_DIM):
        sl = slice(g * GROUP_DIM, (g + 1) * GROUP_DIM)
        lhs = jnp.concatenate([x[:CHUNK, sl], x[CHUNK:, sl]], axis=1).astype(BF16)
        y_ref[:, sl] = _dot(lhs, cs).astype(y_ref.dtype)


def _fourier(fz, bsz, seq, nb_a, kb):
    t, cw = fz.shape
    n1 = seq // CHUNK
    ga, fb, cs = (jnp.asarray(a).astype(BF16) for a in _fft_tables(seq))
    z4 = fz.reshape(bsz, n1, CHUNK, cw)
    bt = pl.pallas_call(
        _fft_a_kernel,
        grid=(bsz, CHUNK // nb_a),
        in_specs=[pl.BlockSpec((None, n1, nb_a, cw), lambda b, j: (b, 0, j, 0)),
                  pl.BlockSpec((nb_a, 2 * n1, n1), lambda b, j: (j, 0, 0))],
        out_specs=pl.BlockSpec((None, nb_a, 2 * n1, cw), lambda b, j: (b, j, 0, 0)),
        out_shape=jax.ShapeDtypeStruct((bsz, CHUNK, 2 * n1, cw), BF16),
        compiler_params=_vmem_params(("parallel", "parallel"), 32),
        name="fft_a",
    )(z4, ga)
    wc = kb * cw
    nblk = n1 // kb
    bt2 = bt.reshape(bsz, CHUNK, 2 * n1 * cw)
    y = pl.pallas_call(
        _fft_b_kernel,
        grid=(bsz, nblk),
        in_specs=[pl.BlockSpec((None, CHUNK, wc), lambda b, j: (b, 0, j)),
                  pl.BlockSpec((None, CHUNK, wc), lambda b, j: (b, 0, nblk + j)),
                  pl.BlockSpec(fb.shape, lambda b, j: (0, 0)),
                  pl.BlockSpec(cs.shape, lambda b, j: (0, 0))],
        out_specs=pl.BlockSpec((None, CHUNK, wc), lambda b, j: (b, 0, j)),
        out_shape=jax.ShapeDtypeStruct((bsz, CHUNK, n1 * cw), BF16),
        compiler_params=_vmem_params(("parallel", "parallel"), 32),
        name="fft_b",
    )(bt2, bt2, fb, cs)
    return y.reshape(t, cw)


def _outmlp_kernel(alpha, ff_chunk, x_ref, hd_ref, o_ref, yf_ref, g1_ref, sc2_ref, sh2_ref, g2_ref,
                   nw_ref, wout_ref, l1g_ref, l1b_ref, w1_ref, b1_ref, w2_ref, b2_ref, l2g_ref, l2b_ref,
                   out_ref):
    hsum = hd_ref[0] + hd_ref[1]
    heads = [_ln_plain(hsum[:, j * V_DIM:(j + 1) * V_DIM]) for j in range(N_HEADS)]
    ym = jnp.concatenate(heads, axis=1) * nw_ref[...] * jax.nn.sigmoid(o_ref[...].astype(F32))
    mix = _dot(jnp.concatenate([ym.astype(BF16), yf_ref[...]], axis=1), wout_ref[...])
    x1 = _ln_plain(alpha * x_ref[...] + (1.0 + g1_ref[...]) * mix) * l1g_ref[...] + l1b_ref[...]
    h2 = (_ln_plain(x1) * (1.0 + sc2_ref[...]) + sh2_ref[...]).astype(BF16)
    ff = b2_ref[...]
    for j in range(w1_ref.shape[1] // ff_chunk):
        sl = slice(j * ff_chunk, (j + 1) * ff_chunk)
        hid = jnp.maximum(_dot(h2, w1_ref[:, sl]) + b1_ref[:, sl], 0.0)
        ff = ff + _dot((hid * hid).astype(BF16), w2_ref[sl, :])
    out_ref[...] = _ln_plain(alpha * x1 + (1.0 + g2_ref[...]) * ff) * l2g_ref[...] + l2b_ref[...]


def _outmlp(x2d, hdir, o, yf, mod4, w, seq, tm, alpha, ff_chunk):
    t, d = x2d.shape
    spb = seq // tm
    vw = N_HEADS * V_DIM
    fw = N_GROUPS * GROUP_DIM
    row = lambda width: pl.BlockSpec((tm, width), lambda i: (i, 0))
    modspec = lambda j: pl.BlockSpec((None, None, 1, d), lambda i: (i // spb, j, 0, 0))
    const = lambda a: pl.BlockSpec(a.shape, lambda i: (0,) * a.ndim, pipeline_mode=pl.Buffered(1))
    names = ["nw", "wout", "l1g", "l1b", "w1", "b1", "w2", "b2", "l2g", "l2b"]
    return pl.pallas_call(
        functools.partial(_outmlp_kernel, alpha, ff_chunk),
        grid=(t // tm,),
        in_specs=[row(d), pl.BlockSpec((2, tm, vw), lambda i: (0, i, 0)), row(vw), row(fw),
                  modspec(2), modspec(4), modspec(3), modspec(5)] + [const(w[k]) for k in names],
        out_specs=row(d),
        out_shape=jax.ShapeDtypeStruct((t, d), F32),
        compiler_params=_vmem_params(("parallel",), 56),
        name="outmlp",
    )(x2d, hdir, o, yf, mod4, mod4, mod4, mod4, *[w[k] for k in names])


def _layer_weights(w_in, w_out, w_ff1, w_ff2, b_ff1, b_ff2, mlstm_norm_w, ln1_g, ln1_b, ln2_g, ln2_b):
    qk = N_HEADS * QK_DIM
    vw = N_HEADS * V_DIM
    fw = N_GROUPS * GROUP_DIM
    o0, o1, o2, o3, o4 = qk, 2 * qk, 2 * qk + vw, 2 * qk + 2 * vw, 2 * qk + 2 * vw + fw
    wg = w_in[:, o4:].reshape(-1, 2, 8).transpose(1, 0, 2)
    r = lambda a: a.astype(F32).reshape(1, -1)
    return {
        "wq": (w_in[:, :o0] * (QK_DIM ** -0.5)).astype(BF16),
        "wkt": w_in[:, o0:o1].T.astype(BF16),
        "wv": w_in[:, o1:o2].astype(BF16),
        "wo": w_in[:, o2:o3].astype(BF16),
        "wf": w_in[:, o3:o4].astype(BF16),
        "wg": wg.astype(BF16),
        "wgt": wg.transpose(0, 2, 1).astype(BF16),
        "nw": r(mlstm_norm_w), "wout": w_out.astype(BF16), "l1g": r(ln1_g), "l1b": r(ln1_b),
        "w1": w_ff1.astype(BF16), "b1": r(b_ff1), "w2": w_ff2.astype(BF16), "b2": r(b_ff2),
        "l2g": r(ln2_g), "l2b": r(ln2_b),
    }


def kernel(x, c, w_ada, b_ada, w_in, b_gate, mlstm_norm_w, w_out, ln1_g, ln1_b,
           w_ff1, b_ff1, w_ff2, b_ff2, ln2_g, ln2_b):
    bsz, seq, d = x.shape
    depth = w_ada.shape[0]
    alpha = (2 * depth) ** 0.25
    assert seq % CHUNK == 0 and (seq // CHUNK) % 8 == 0 and d == N_HEADS * V_DIM + N_GROUPS * GROUP_DIM
    tm = min(512, seq)
    x2d = x.reshape(bsz * seq, d)
    for l in range(depth):
        w = _layer_weights(w_in[l], w_out[l], w_ff1[l], w_ff2[l], b_ff1[l], b_ff2[l], mlstm_norm_w[l],
                           ln1_g[l], ln1_b[l], ln2_g[l], ln2_b[l])
        mod4 = _adaln(c, w_ada[l], b_ada[l]).reshape(bsz, N_MOD, 1, d)
        q, kt, v, o, fz, g, gt = _inproj(x2d, mod4, w, seq, tm)
        cols, rows = _gate_prep(g, gt, b_gate[l], tm)
        hdir = _mlstm(q, kt, v, rows, cols, bsz, seq)
        yf = _fourier(fz, bsz, seq, nb_a=8, kb=min(2, seq // CHUNK))
        x2d = _outmlp(x2d, hdir, o, yf, mod4, w, seq, tm, alpha, ff_chunk=1024)
    return x2d.reshape(bsz, seq, d)
```

```python
import functools

import numpy as np
import jax
import jax.numpy as jnp
from jax import lax
from jax.experimental import pallas as pl
from jax.experimental.pallas import tpu as pltpu

F32 = jnp.float32
BF16 = jnp.bfloat16

CHUNK = 128
N_HEADS = 4
QK_DIM = 64
V_DIM = 128
N_GROUPS = 4
GROUP_DIM = 128
N_MOD = 6
LN_EPS = 1e-5
MASKED = -1e30
FFT_NB = 8

_NT = (((1,), (1,)), ((), ()))


def _dot(a, b):
    return jnp.dot(a, b, preferred_element_type=F32)


def _dot_nt(a, b):
    return lax.dot_general(a, b, _NT, preferred_element_type=F32)


def _ln_plain(x):
    mu = jnp.mean(x, axis=-1, keepdims=True)
    xc = x - mu
    var = jnp.mean(xc * xc, axis=-1, keepdims=True)
    return xc * lax.rsqrt(var + LN_EPS)


def _log_sigmoid(x):
    return jnp.minimum(x, 0.0) - jnp.log1p(jnp.exp(-jnp.abs(x)))


def _split3(x):
    hi = x.astype(BF16)
    r1 = x - hi.astype(F32)
    mid = r1.astype(BF16)
    lo = (r1 - mid.astype(F32)).astype(BF16)
    return hi, mid, lo


def _vmem_params(semantics, limit_mb):
    return pltpu.CompilerParams(dimension_semantics=semantics, vmem_limit_bytes=limit_mb * 1024 * 1024)


def _adaln_kernel(ct_ref, w_ref, b_ref, o_ref):
    ct = ct_ref[...]
    act = ct * jax.nn.sigmoid(ct)
    w = w_ref[...]
    for b in range(ct.shape[1]):
        o_ref[b:b + 1, :] = jnp.sum(act[:, b:b + 1] * w, axis=0, keepdims=True) + b_ref[...]


def _adaln(c, w_ada, b_ada):
    bsz, d = c.shape
    n = w_ada.shape[1]
    tn = 1024
    return pl.pallas_call(
        _adaln_kernel,
        grid=(n // tn,),
        in_specs=[pl.BlockSpec((d, bsz), lambda j: (0, 0)),
                  pl.BlockSpec((d, tn), lambda j: (0, j)),
                  pl.BlockSpec((1, tn), lambda j: (0, j))],
        out_specs=pl.BlockSpec((bsz, tn), lambda j: (0, j)),
        out_shape=jax.ShapeDtypeStruct((bsz, n), F32),
        compiler_params=_vmem_params(("parallel",), 32),
        name="adaln",
    )(c.T, w_ada, b_ada.reshape(1, n))


def _inproj_kernel(x_ref, sh_ref, sc_ref, wq_ref, wkt_ref, wv_ref, wo_ref, wf_ref, wg_ref, wgt_ref,
                   q_ref, kt_ref, v_ref, o_ref, z_ref, g_ref, gt_ref):
    h = _ln_plain(x_ref[...]) * (1.0 + sc_ref[...]) + sh_ref[...]
    hb = h.astype(BF16)
    q_ref[...] = _dot(hb, wq_ref[...]).astype(BF16)
    kt_ref[...] = _dot_nt(wkt_ref[...], hb).astype(BF16)
    v_ref[...] = _dot(hb, wv_ref[...]).astype(BF16)
    o_ref[...] = _dot(hb, wo_ref[...]).astype(BF16)
    fz = _dot(hb, wf_ref[...])
    for n1l in range(fz.shape[0] // CHUNK):
        for g in range(N_GROUPS):
            for jb in range(CHUNK // FFT_NB):
                r0 = n1l * CHUNK + jb * FFT_NB
                z_ref[g, jb, n1l * FFT_NB:(n1l + 1) * FFT_NB, :] = (
                    fz[r0:r0 + FFT_NB, g * GROUP_DIM:(g + 1) * GROUP_DIM])
    for d in range(2):
        g_ref[d] = _dot(hb, wg_ref[d])
        gt_ref[d] = _dot_nt(wgt_ref[d], hb)


def _inproj(x2d, mod4, w, seq, tm):
    t, d = x2d.shape
    spb = seq // tm
    qk = N_HEADS * QK_DIM
    vw = N_HEADS * V_DIM
    nblk = CHUNK // FFT_NB
    full = lambda a: pl.BlockSpec(a.shape, lambda i: (0,) * a.ndim)
    modspec = lambda j: pl.BlockSpec((None, None, 1, d), lambda i: (i // spb, j, 0, 0))
    return pl.pallas_call(
        _inproj_kernel,
        grid=(t // tm,),
        in_specs=[pl.BlockSpec((tm, d), lambda i: (i, 0)), modspec(0), modspec(1),
                  full(w["wq"]), full(w["wkt"]), full(w["wv"]), full(w["wo"]), full(w["wf"]),
                  full(w["wg"]), full(w["wgt"])],
        out_specs=[pl.BlockSpec((tm, qk), lambda i: (i, 0)),
                   pl.BlockSpec((qk, tm), lambda i: (0, i)),
                   pl.BlockSpec((tm, vw), lambda i: (i, 0)),
                   pl.BlockSpec((tm, vw), lambda i: (i, 0)),
                   pl.BlockSpec((None, N_GROUPS, nblk, (tm // CHUNK) * FFT_NB, GROUP_DIM),
                                lambda i: (i // spb, 0, 0, i % spb, 0)),
                   pl.BlockSpec((2, tm, 8), lambda i: (0, i, 0)),
                   pl.BlockSpec((2, 8, tm), lambda i: (0, 0, i))],
        out_shape=[jax.ShapeDtypeStruct((t, qk), BF16),
                   jax.ShapeDtypeStruct((qk, t), BF16),
                   jax.ShapeDtypeStruct((t, vw), BF16),
                   jax.ShapeDtypeStruct((t, vw), BF16),
                   jax.ShapeDtypeStruct((t // seq, N_GROUPS, nblk, (seq // CHUNK) * FFT_NB, GROUP_DIM), F32),
                   jax.ShapeDtypeStruct((2, t, 8), F32),
                   jax.ShapeDtypeStruct((2, 8, t), F32)],
        compiler_params=_vmem_params(("parallel",), 48),
        name="inproj",
    )(x2d, mod4, mod4, w["wq"], w["wkt"], w["wv"], w["wo"], w["wf"], w["wg"], w["wgt"])


def _gate_prep_kernel(g_ref, gt_ref, bc_ref, br_ref, cols_ref, rows_ref):
    tp = g_ref.shape[0]
    sgn = 1 - 2 * pl.program_id(0)
    r = lax.broadcasted_iota(jnp.int32, (tp, tp), 0)
    c = lax.broadcasted_iota(jnp.int32, (tp, tp), 1)
    same = (r // CHUNK) == (c // CHUNK)
    m_row = jnp.where(same & ((c - r) * sgn >= 0), 1.0, 0.0).astype(BF16)
    m_col = jnp.where(same & ((r - c) * sgn >= 0), 1.0, 0.0).astype(BF16)

    gt = gt_ref[...] + bc_ref[...]
    cum_r = sum(_dot(p, m_row) for p in _split3(_log_sigmoid(gt)))
    is_f_r = lax.broadcasted_iota(jnp.int32, gt.shape, 0) >= N_HEADS
    rows_ref[...] = jnp.where(is_f_r, cum_r, gt)

    g = g_ref[...] + br_ref[...]
    cum_c = sum(_dot(m_col, p) for p in _split3(_log_sigmoid(g)))
    is_f_c = lax.broadcasted_iota(jnp.int32, g.shape, 1) >= N_HEADS
    cols_ref[...] = jnp.where(is_f_c, cum_c, g)


def _gate_prep(g, gt, b_gate, tp):
    t = g.shape[1]
    bias = b_gate.astype(F32).reshape(2, 8)
    return pl.pallas_call(
        _gate_prep_kernel,
        grid=(2, t // tp),
        in_specs=[pl.BlockSpec((None, tp, 8), lambda d, i: (d, i, 0)),
                  pl.BlockSpec((None, 8, tp), lambda d, i: (d, 0, i)),
                  pl.BlockSpec((None, 8, 1), lambda d, i: (d, 0, 0)),
                  pl.BlockSpec((None, 1, 8), lambda d, i: (d, 0, 0))],
        out_specs=[pl.BlockSpec((None, tp, 8), lambda d, i: (d, i, 0)),
                   pl.BlockSpec((None, 8, tp), lambda d, i: (d, 0, i))],
        out_shape=[jax.ShapeDtypeStruct((2, t, 8), F32), jax.ShapeDtypeStruct((2, 8, t), F32)],
        compiler_params=_vmem_params(("parallel", "parallel"), 32),
        name="gate_prep",
    )(g, gt, bias.reshape(2, 8, 1), bias.reshape(2, 1, 8))


def _mlstm_kernel(q_ref, kt_ref, v_ref, rows_ref, cols_ref, h_ref, c_scr, m_scr):
    d = pl.program_id(1)

    @pl.when(pl.program_id(2) == 0)
    def _():
        c_scr[...] = jnp.zeros_like(c_scr)
        m_scr[...] = jnp.zeros_like(m_scr)

    sgn = 1 - 2 * d
    t_i = lax.broadcasted_iota(jnp.int32, (CHUNK, CHUNK), 0)
    s_i = lax.broadcasted_iota(jnp.int32, (CHUNK, CHUNK), 1)
    visible = (t_i - s_i) * sgn >= 0
    rows = rows_ref[...]
    cols = cols_ref[...]
    ones = jnp.ones((CHUNK, V_DIM), BF16)

    for hd in range(N_HEADS):
        a_row = rows[hd:hd + 1, :] - rows[N_HEADS + hd:N_HEADS + hd + 1, :]
        b_col = cols[:, N_HEADS + hd:N_HEADS + hd + 1]
        m_prev = m_scr[hd][0:1, 0:1]
        c_prev = c_scr[hd]

        logw = jnp.where(visible, b_col + a_row, MASKED)
        m_t = jnp.maximum(b_col + m_prev, jnp.max(logw, axis=1, keepdims=True))
        w_intra = jnp.exp(logw - m_t)
        w_inter = jnp.exp(b_col + m_prev - m_t)

        qh = q_ref[:, hd * QK_DIM:(hd + 1) * QK_DIM]
        kth = kt_ref[hd * QK_DIM:(hd + 1) * QK_DIM, :]
        v_aug = jnp.concatenate([v_ref[:, hd * V_DIM:(hd + 1) * V_DIM], ones], axis=1)

        s = (_dot(qh, kth) * w_intra).astype(BF16)
        nd = _dot(s, v_aug) + w_inter * _dot(qh, c_prev.astype(BF16))
        den = jnp.maximum(jnp.abs(nd[:, V_DIM:]), jnp.exp(-m_t))
        h_ref[:, hd * V_DIM:(hd + 1) * V_DIM] = nd[:, :V_DIM] / den

        b_end = jnp.where(d == 0, b_col[CHUNK - 1:CHUNK, :], b_col[0:1, :])
        m_new = b_end + jnp.maximum(m_prev, jnp.max(a_row, axis=1, keepdims=True))
        wk_row = jnp.exp(b_end + a_row - m_new)
        decay = jnp.exp(b_end + m_prev - m_new)
        kw = (kth.astype(F32) * wk_row).astype(BF16)
        c_scr[hd] = decay * c_prev + _dot(kw, v_aug)
        m_scr[hd] = jnp.broadcast_to(m_new, m_scr.shape[1:])


def _mlstm(q, kt, v, rows, cols, bsz, seq):
    t = q.shape[0]
    nc = seq // CHUNK
    qk = N_HEADS * QK_DIM
    vw = N_HEADS * V_DIM

    def blk(b, d, c):
        return b * nc + c + d * (nc - 1 - 2 * c)

    return pl.pallas_call(
        _mlstm_kernel,
        grid=(bsz, 2, nc),
        in_specs=[pl.BlockSpec((CHUNK, qk), lambda b, d, c: (blk(b, d, c), 0)),
                  pl.BlockSpec((qk, CHUNK), lambda b, d, c: (0, blk(b, d, c))),
                  pl.BlockSpec((CHUNK, vw), lambda b, d, c: (blk(b, d, c), 0)),
                  pl.BlockSpec((None, 8, CHUNK), lambda b, d, c: (d, 0, blk(b, d, c))),
                  pl.BlockSpec((None, CHUNK, 8), lambda b, d, c: (d, blk(b, d, c), 0))],
        out_specs=pl.BlockSpec((None, CHUNK, vw), lambda b, d, c: (d, blk(b, d, c), 0)),
        out_shape=jax.ShapeDtypeStruct((2, t, vw), F32),
        scratch_shapes=[pltpu.VMEM((N_HEADS, QK_DIM, 2 * V_DIM), F32),
                        pltpu.VMEM((N_HEADS, 8, 128), F32)],
        compiler_params=_vmem_params(("parallel", "parallel", "arbitrary"), 32),
        name="mlstm",
    )(q, kt, v, rows, cols)


def _fft_pitch(n1):
    return 8 * ((n1 // 4) | 1)


@functools.lru_cache(maxsize=None)
def _fft_tables(seq):
    n1 = seq // CHUNK
    two_pi = 2.0 * np.pi
    k1 = np.arange(n1, dtype=np.int64)
    n = 128 * np.arange(n1, dtype=np.int64)[None, None, :] + np.arange(128, dtype=np.int64)[:, None, None]
    ang = two_pi * ((k1[None, :, None] * n) % seq).astype(np.float64) / seq
    ga = np.concatenate([np.cos(ang), -np.sin(ang)], axis=1)
    j = np.arange(128, dtype=np.int64)
    a128 = two_pi * ((j[:, None] * j[None, :]) % 128).astype(np.float64) / 128.0
    ff = np.concatenate([np.cos(a128), -np.sin(a128)], axis=0)
    cs = np.concatenate([np.cos(a128), np.sin(a128)], axis=0) / np.sqrt(128.0 * seq)
    return tuple(np.asarray(a, dtype=np.float32) for a in (ga, ff, cs))


def _fft_kernel(z_ref, ga_ref, ff_ref, cs_ref, y_ref, scr):
    j = pl.program_id(2)
    n1 = ga_ref.shape[2]
    pitch = scr.shape[0] // CHUNK
    for l in range(FFT_NB):
        zl = z_ref[pl.ds(l, n1, stride=FFT_NB), :].astype(BF16)
        row0 = pl.multiple_of((j * FFT_NB + l) * pitch, 8)
        scr[pl.ds(row0, 2 * n1), :] = _dot(ga_ref[l], zl)

    @pl.when(j == pl.num_programs(2) - 1)
    def _():
        ff = ff_ref[...]
        cs = cs_ref[...]

        def body(k1, carry):
            pr = scr[pl.ds(k1, CHUNK, stride=pitch), :]
            pi = scr[pl.ds(n1 + k1, CHUNK, stride=pitch), :]
            m = _dot(ff, jnp.concatenate([pr, pi], axis=1).astype(BF16))
            xr = m[:CHUNK, :GROUP_DIM] - m[CHUNK:, GROUP_DIM:]
            xi = m[:CHUNK, GROUP_DIM:] + m[CHUNK:, :GROUP_DIM]
            y = _dot(jnp.concatenate([xr, xi], axis=1).astype(BF16), cs)
            row0 = pl.multiple_of(k1 * FFT_NB, 8)
            for kb in range(CHUNK // FFT_NB):
                y_ref[kb, pl.ds(row0, FFT_NB), :] = y[kb * FFT_NB:(kb + 1) * FFT_NB, :]
            return carry

        lax.fori_loop(0, n1, body, 0, unroll=2)


def _fourier(z5, seq):
    bsz = z5.shape[0]
    n1 = seq // CHUNK
    nblk = CHUNK // FFT_NB
    pitch = _fft_pitch(n1)
    ga, ff, cs = (jnp.asarray(a).astype(BF16) for a in _fft_tables(seq))
    return pl.pallas_call(
        _fft_kernel,
        grid=(bsz, N_GROUPS, nblk),
        in_specs=[pl.BlockSpec((None, None, None, n1 * FFT_NB, GROUP_DIM), lambda b, g, j: (b, g, j, 0, 0)),
                  pl.BlockSpec((FFT_NB, 2 * n1, n1), lambda b, g, j: (j, 0, 0)),
                  pl.BlockSpec(ff.shape, lambda b, g, j: (0, 0)),
                  pl.BlockSpec(cs.shape, lambda b, g, j: (0, 0))],
        out_specs=pl.BlockSpec((None, None, nblk, n1 * FFT_NB, GROUP_DIM), lambda b, g, j: (b, g, 0, 0, 0)),
        out_shape=jax.ShapeDtypeStruct(z5.shape, F32),
        scratch_shapes=[pltpu.VMEM((CHUNK * pitch, GROUP_DIM), F32)],
        compiler_params=_vmem_params(("parallel", "parallel", "arbitrary"), 48),
        name="fft",
    )(z5, ga, ff, cs)


def _gather_rows(y_refs, k2_off, n1, kpt):
    rows = [jnp.concatenate([ref[pl.ds(k2_off + kk, n1, stride=FFT_NB), :] for ref in y_refs], axis=1)
            for kk in range(kpt)]
    return jnp.concatenate(rows, axis=0)


def _outmlp_kernel(alpha, ff_chunk, spb, x_ref, hd_ref, o_ref, y0_ref, y1_ref, y2_ref, y3_ref,
                   g1_ref, sc2_ref, sh2_ref, g2_ref,
                   nw_ref, wout_ref, l1g_ref, l1b_ref, w1_ref, b1_ref, w2_ref, b2_ref, l2g_ref, l2b_ref,
                   out_ref):
    n1 = y0_ref.shape[0] // FFT_NB
    kpt = x_ref.shape[0] // n1
    k2_off = ((pl.program_id(0) % spb) * kpt) % FFT_NB
    yf = _gather_rows((y0_ref, y1_ref, y2_ref, y3_ref), k2_off, n1, kpt).astype(BF16)
    hsum = hd_ref[0] + hd_ref[1]
    heads = [_ln_plain(hsum[:, j * V_DIM:(j + 1) * V_DIM]) for j in range(N_HEADS)]
    ym = jnp.concatenate(heads, axis=1) * nw_ref[...] * jax.nn.sigmoid(o_ref[...].astype(F32))
    mix = _dot(jnp.concatenate([ym.astype(BF16), yf], axis=1), wout_ref[...])
    x1 = _ln_plain(alpha * x_ref[...] + (1.0 + g1_ref[...]) * mix) * l1g_ref[...] + l1b_ref[...]
    h2 = (_ln_plain(x1) * (1.0 + sc2_ref[...]) + sh2_ref[...]).astype(BF16)
    ff = b2_ref[...]
    for j in range(w1_ref.shape[1] // ff_chunk):
        sl = slice(j * ff_chunk, (j + 1) * ff_chunk)
        hid = jnp.maximum(_dot(h2, w1_ref[:, sl]) + b1_ref[:, sl], 0.0)
        ff = ff + _dot((hid * hid).astype(BF16), w2_ref[sl, :])
    out_ref[...] = _ln_plain(alpha * x1 + (1.0 + g2_ref[...]) * ff) * l2g_ref[...] + l2b_ref[...]


def _outmlp(x2d, hdir, o, y5, mod4, w, seq, tm, alpha, ff_chunk):
    t, d = x2d.shape
    spb = seq // tm
    n1 = seq // CHUNK
    kpt = tm // n1
    assert FFT_NB % kpt == 0
    vw = N_HEADS * V_DIM
    row = lambda width: pl.BlockSpec((tm, width), lambda i: (i, 0))
    yspec = lambda g: pl.BlockSpec((None, None, None, n1 * FFT_NB, GROUP_DIM),
                                   lambda i: (i // spb, g, ((i % spb) * kpt) // FFT_NB, 0, 0))
    modspec = lambda j: pl.BlockSpec((None, None, 1, d), lambda i: (i // spb, j, 0, 0))
    const = lambda a: pl.BlockSpec(a.shape, lambda i: (0,) * a.ndim, pipeline_mode=pl.Buffered(1))
    names = ["nw", "wout", "l1g", "l1b", "w1", "b1", "w2", "b2", "l2g", "l2b"]
    return pl.pallas_call(
        functools.partial(_outmlp_kernel, alpha, ff_chunk, spb),
        grid=(t // tm,),
        in_specs=[row(d), pl.BlockSpec((2, tm, vw), lambda i: (0, i, 0)), row(vw)]
                 + [yspec(g) for g in range(N_GROUPS)]
                 + [modspec(2), modspec(4), modspec(3), modspec(5)] + [const(w[k]) for k in names],
        out_specs=row(d),
        out_shape=jax.ShapeDtypeStruct((t, d), F32),
        compiler_params=_vmem_params(("parallel",), 56),
        name="outmlp",
    )(x2d, hdir, o, y5, y5, y5, y5, mod4, mod4, mod4, mod4, *[w[k] for k in names])


def _layer_weights(w_in, w_out, w_ff1, w_ff2, b_ff1, b_ff2, mlstm_norm_w, ln1_g, ln1_b, ln2_g, ln2_b):
    qk = N_HEADS * QK_DIM
    vw = N_HEADS * V_DIM
    fw = N_GROUPS * GROUP_DIM
    o0, o1, o2, o3, o4 = qk, 2 * qk, 2 * qk + vw, 2 * qk + 2 * vw, 2 * qk + 2 * vw + fw
    wg = w_in[:, o4:].reshape(-1, 2, 8).transpose(1, 0, 2)
    r = lambda a: a.astype(F32).reshape(1, -1)
    return {
        "wq": (w_in[:, :o0] * (QK_DIM ** -0.5)).astype(BF16),
        "wkt": w_in[:, o0:o1].T.astype(BF16),
        "wv": w_in[:, o1:o2].astype(BF16),
        "wo": w_in[:, o2:o3].astype(BF16),
        "wf": w_in[:, o3:o4].astype(BF16),
        "wg": wg.astype(BF16),
        "wgt": wg.transpose(0, 2, 1).astype(BF16),
        "nw": r(mlstm_norm_w), "wout": w_out.astype(BF16), "l1g": r(ln1_g), "l1b": r(ln1_b),
        "w1": w_ff1.astype(BF16), "b1": r(b_ff1), "w2": w_ff2.astype(BF16), "b2": r(b_ff2),
        "l2g": r(ln2_g), "l2b": r(ln2_b),
    }


def kernel(x, c, w_ada, b_ada, w_in, b_gate, mlstm_norm_w, w_out, ln1_g, ln1_b,
           w_ff1, b_ff1, w_ff2, b_ff2, ln2_g, ln2_b):
    bsz, seq, d = x.shape
    depth = w_ada.shape[0]
    alpha = (2 * depth) ** 0.25
    assert seq % CHUNK == 0 and (seq // CHUNK) % 8 == 0 and d == N_HEADS * V_DIM + N_GROUPS * GROUP_DIM
    tm = min(512, FFT_NB * (seq // CHUNK))
    x2d = x.reshape(bsz * seq, d)
    for l in range(depth):
        w = _layer_weights(w_in[l], w_out[l], w_ff1[l], w_ff2[l], b_ff1[l], b_ff2[l], mlstm_norm_w[l],
                           ln1_g[l], ln1_b[l], ln2_g[l], ln2_b[l])
        mod4 = _adaln(c, w_ada[l], b_ada[l]).reshape(bsz, N_MOD, 1, d)
        q, kt, v, o, z5, g, gt = _inproj(x2d, mod4, w, seq, tm)
        cols, rows = _gate_prep(g, gt, b_gate[l], tm)
        hdir = _mlstm(q, kt, v, rows, cols, bsz, seq)
        y5 = _fourier(z5, seq)
        x2d = _outmlp(x2d, hdir, o, y5, mod4, w, seq, tm, alpha, ff_chunk=1024)
    return x2d.reshape(bsz, seq, d)
```

```python
import functools

import numpy as np
import jax
import jax.numpy as jnp
from jax import lax
from jax.experimental import pallas as pl
from jax.experimental.pallas import tpu as pltpu

F32 = jnp.float32
BF16 = jnp.bfloat16

CHUNK = 128
N_HEADS = 4
QK_DIM = 64
V_DIM = 128
N_GROUPS = 4
GROUP_DIM = 128
N_MOD = 6
LN_EPS = 1e-5
MASKED = -1e30
FFT_NB = 8
FFT_KB = 4
MLSTM_CPB = 4

_NT = (((1,), (1,)), ((), ()))


def _dot(a, b):
    return jnp.dot(a, b, preferred_element_type=F32)


def _dot_nt(a, b):
    return lax.dot_general(a, b, _NT, preferred_element_type=F32)


def _ln_plain(x):
    mu = jnp.mean(x, axis=-1, keepdims=True)
    xc = x - mu
    var = jnp.mean(xc * xc, axis=-1, keepdims=True)
    return xc * lax.rsqrt(var + LN_EPS)


def _log_sigmoid(x):
    return jnp.minimum(x, 0.0) - jnp.log1p(jnp.exp(-jnp.abs(x)))


def _split3(x):
    hi = x.astype(BF16)
    r1 = x - hi.astype(F32)
    mid = r1.astype(BF16)
    lo = (r1 - mid.astype(F32)).astype(BF16)
    return hi, mid, lo


def _vmem_params(semantics, limit_mb):
    return pltpu.CompilerParams(dimension_semantics=semantics, vmem_limit_bytes=limit_mb * 1024 * 1024)


def _adaln_kernel(ct_ref, w_ref, b_ref, o_ref):
    ct = ct_ref[...]
    act = ct * jax.nn.sigmoid(ct)
    w = w_ref[...]
    for b in range(ct.shape[1]):
        o_ref[b:b + 1, :] = jnp.sum(act[:, b:b + 1] * w, axis=0, keepdims=True) + b_ref[...]


def _adaln(c, w_ada, b_ada):
    bsz, d = c.shape
    n = w_ada.shape[1]
    tn = 1024
    return pl.pallas_call(
        _adaln_kernel,
        grid=(n // tn,),
        in_specs=[pl.BlockSpec((d, bsz), lambda j: (0, 0)),
                  pl.BlockSpec((d, tn), lambda j: (0, j)),
                  pl.BlockSpec((1, tn), lambda j: (0, j))],
        out_specs=pl.BlockSpec((bsz, tn), lambda j: (0, j)),
        out_shape=jax.ShapeDtypeStruct((bsz, n), F32),
        compiler_params=_vmem_params(("parallel",), 32),
        name="adaln",
    )(c.T, w_ada, b_ada.reshape(1, n))


def _inproj_kernel(x_ref, sh_ref, sc_ref, wq_ref, wkt_ref, wv_ref, wo_ref, wf_ref, wgi_ref, wgf_ref,
                   bgi_ref, bgf_ref, q_ref, kt_ref, v_ref, o_ref, z_ref, gi_ref, gf_ref):
    h = _ln_plain(x_ref[...]) * (1.0 + sc_ref[...]) + sh_ref[...]
    hb = h.astype(BF16)
    n_chunks = hb.shape[0] // CHUNK
    q_ref[...] = _dot(hb, wq_ref[...]).astype(BF16)
    kt = _dot_nt(wkt_ref[...], hb).astype(BF16)
    for j in range(n_chunks):
        kt_ref[j] = kt[:, j * CHUNK:(j + 1) * CHUNK]
    v_ref[...] = _dot(hb, wv_ref[...]).astype(BF16)
    o_ref[...] = _dot(hb, wo_ref[...]).astype(BF16)
    fz = _dot(hb, wf_ref[...])
    for n1l in range(n_chunks):
        for g in range(N_GROUPS):
            for jb in range(CHUNK // FFT_NB):
                r0 = n1l * CHUNK + jb * FFT_NB
                z_ref[g, jb, n1l * FFT_NB:(n1l + 1) * FFT_NB, :] = (
                    fz[r0:r0 + FFT_NB, g * GROUP_DIM:(g + 1) * GROUP_DIM])
    for d in range(2):
        gi = _dot_nt(wgi_ref[d], hb) + bgi_ref[d]
        gf = _dot_nt(wgf_ref[d], hb) + bgf_ref[d]
        for j in range(n_chunks):
            gi_ref[d, j * 8:(j + 1) * 8, :] = gi[:, j * CHUNK:(j + 1) * CHUNK]
            gf_ref[d, j * 8:(j + 1) * 8, :] = gf[:, j * CHUNK:(j + 1) * CHUNK]


def _inproj(x2d, mod4, w, seq, tm):
    t, d = x2d.shape
    spb = seq // tm
    qk = N_HEADS * QK_DIM
    vw = N_HEADS * V_DIM
    nblk = CHUNK // FFT_NB
    cpt = tm // CHUNK
    full = lambda a: pl.BlockSpec(a.shape, lambda i: (0,) * a.ndim)
    modspec = lambda j: pl.BlockSpec((None, None, 1, d), lambda i: (i // spb, j, 0, 0))
    names = ["wq", "wkt", "wv", "wo", "wf", "wgi", "wgf", "bgi", "bgf"]
    return pl.pallas_call(
        _inproj_kernel,
        grid=(t // tm,),
        in_specs=[pl.BlockSpec((tm, d), lambda i: (i, 0)), modspec(0), modspec(1)] + [full(w[k]) for k in names],
        out_specs=[pl.BlockSpec((tm, qk), lambda i: (i, 0)),
                   pl.BlockSpec((cpt, qk, CHUNK), lambda i: (i, 0, 0)),
                   pl.BlockSpec((tm, vw), lambda i: (i, 0)),
                   pl.BlockSpec((tm, vw), lambda i: (i, 0)),
                   pl.BlockSpec((None, N_GROUPS, nblk, cpt * FFT_NB, GROUP_DIM),
                                lambda i: (i // spb, 0, 0, i % spb, 0)),
                   pl.BlockSpec((2, cpt * 8, CHUNK), lambda i: (0, i, 0)),
                   pl.BlockSpec((2, cpt * 8, CHUNK), lambda i: (0, i, 0))],
        out_shape=[jax.ShapeDtypeStruct((t, qk), BF16),
                   jax.ShapeDtypeStruct((t // CHUNK, qk, CHUNK), BF16),
                   jax.ShapeDtypeStruct((t, vw), BF16),
                   jax.ShapeDtypeStruct((t, vw), BF16),
                   jax.ShapeDtypeStruct((t // seq, N_GROUPS, nblk, (seq // CHUNK) * FFT_NB, GROUP_DIM), F32),
                   jax.ShapeDtypeStruct((2, t // CHUNK * 8, CHUNK), F32),
                   jax.ShapeDtypeStruct((2, t // CHUNK * 8, CHUNK), F32)],
        compiler_params=_vmem_params(("parallel",), 48),
        name="inproj",
    )(x2d, mod4, mod4, *[w[k] for k in names])


def _gate_prep_kernel(gi_ref, gf_ref, ra_ref, rb_ref, rc_ref, bend_scr, amax_scr, mprev_scr):
    rows = gi_ref.shape[1]
    n_chunks = rows // 8
    lane = lax.broadcasted_iota(jnp.int32, (rows, CHUNK), 1)
    first = (lax.broadcasted_iota(jnp.int32, (rows, CHUNK), 0) & 7) < N_HEADS
    src = lax.broadcasted_iota(jnp.int32, (CHUNK, 2 * CHUNK), 0)
    dst = lax.broadcasted_iota(jnp.int32, (CHUNK, 2 * CHUNK), 1)
    for d in range(2):
        feeds = (src <= dst) if d == 0 else (src >= dst)
        cum_and_total = jnp.where((dst >= CHUNK) | feeds, 1.0, 0.0).astype(BF16)
        bc = sum(_dot(p, cum_and_total) for p in _split3(_log_sigmoid(gf_ref[d])))
        b, b_end = bc[:, :CHUNK], bc[:, CHUNK:]
        a = gi_ref[d] - b
        cm = a
        for k in range(7):
            sh = 1 << k
            if d == 0:
                cm = jnp.maximum(cm, jnp.where(lane >= sh, pltpu.roll(cm, sh, 1), MASKED))
            else:
                cm = jnp.maximum(cm, jnp.where(lane < CHUNK - sh, pltpu.roll(cm, CHUNK - sh, 1), MASKED))
        a_max = jnp.broadcast_to(jnp.max(a, axis=1, keepdims=True), a.shape)
        bend_scr[...] = b_end
        amax_scr[...] = a_max

        def scan(c, m):
            r = pl.ds(pl.multiple_of((c if d == 0 else n_chunks - 1 - c) * 8, 8), 8)
            mprev_scr[r, :] = m
            return bend_scr[r, :] + jnp.maximum(m, amax_scr[r, :])

        lax.fori_loop(0, n_chunks, scan, jnp.zeros((8, CHUNK), F32))
        m_prev = mprev_scr[...]
        u = -jnp.maximum(m_prev, cm)
        m_new = b_end + jnp.maximum(m_prev, a_max)
        ra_ref[d] = jnp.where(first, a, jnp.exp(b_end + a - m_new))
        rb_ref[d] = jnp.where(first, u, jnp.exp(m_prev + u))
        rc_ref[d] = jnp.where(first, jnp.exp(u - b), jnp.exp(b_end + m_prev - m_new))


def _gate_prep(gi, gf, bsz):
    rows = gi.shape[1] // bsz
    spec = pl.BlockSpec((2, rows, CHUNK), lambda b: (0, b, 0))
    shape = jax.ShapeDtypeStruct(gi.shape, F32)
    return pl.pallas_call(
        _gate_prep_kernel,
        grid=(bsz,),
        in_specs=[spec, spec],
        out_specs=[spec, spec, spec],
        out_shape=[shape, shape, shape],
        scratch_shapes=[pltpu.VMEM((rows, CHUNK), F32)] * 3,
        compiler_params=_vmem_params(("parallel",), 32),
        name="gate_prep",
    )(gi, gf)


def _mlstm_kernel(qf_ref, ktf_ref, vf_ref, raf_ref, rbf_ref, rcf_ref,
                  qb_ref, ktb_ref, vb_ref, rab_ref, rbb_ref, rcb_ref, hf_ref, hb_ref, c_scr):
    @pl.when(pl.program_id(1) == 0)
    def _():
        c_scr[...] = jnp.zeros_like(c_scr)

    cpb = ktf_ref.shape[0]
    t_i = lax.broadcasted_iota(jnp.int32, (CHUNK, CHUNK), 0)
    s_i = lax.broadcasted_iota(jnp.int32, (CHUNK, CHUNK), 1)
    visible = (s_i <= t_i, s_i >= t_i)
    eye = jnp.where(s_i == t_i, 1.0, 0.0).astype(BF16)
    ones = jnp.ones((CHUNK, V_DIM), BF16)
    dirs = ((qf_ref, ktf_ref, vf_ref, raf_ref, rbf_ref, rcf_ref, hf_ref),
            (qb_ref, ktb_ref, vb_ref, rab_ref, rbb_ref, rcb_ref, hb_ref))

    def body(i, carry):
        for d, (q_ref, kt_ref, v_ref, ra_ref, rb_ref, rc_ref, h_ref) in enumerate(dirs):
            li = i if d == 0 else cpb - 1 - i
            r8 = pl.ds(pl.multiple_of(li * 8, 8), 8)
            rl = pl.ds(pl.multiple_of(li * CHUNK, CHUNK), CHUNK)
            ra = ra_ref[r8, :]
            colt = sum(_dot_nt(eye, p) for p in _split3(jnp.concatenate([rb_ref[r8, :], rc_ref[r8, :]], axis=0)))
            kt = kt_ref[li]
            for hd in range(N_HEADS):
                a_row = ra[hd:hd + 1, :]
                wk_row = ra[N_HEADS + hd:N_HEADS + hd + 1, :]
                u_col = colt[:, hd:hd + 1]
                w_inter = colt[:, N_HEADS + hd:N_HEADS + hd + 1]
                exp_neg_m = colt[:, 8 + hd:9 + hd]
                decay = colt[0:1, 12 + hd:13 + hd]
                c_prev = c_scr[d, hd]

                w_intra = jnp.exp(jnp.where(visible[d], u_col + a_row, MASKED))
                qh = q_ref[rl, hd * QK_DIM:(hd + 1) * QK_DIM]
                kth = kt[hd * QK_DIM:(hd + 1) * QK_DIM, :]
                v_aug = jnp.concatenate([v_ref[rl, hd * V_DIM:(hd + 1) * V_DIM], ones], axis=1)
                s = (_dot(qh, kth) * w_intra).astype(BF16)
                q_inter = (qh.astype(F32) * w_inter).astype(BF16)
                nd = _dot(s, v_aug) + _dot(q_inter, c_prev.astype(BF16))
                den = jnp.maximum(jnp.abs(nd[:, V_DIM:]), exp_neg_m)
                h_ref[rl, hd * V_DIM:(hd + 1) * V_DIM] = (nd[:, :V_DIM] / den).astype(h_ref.dtype)

                kw = (kth.astype(F32) * wk_row).astype(BF16)
                c_scr[d, hd] = decay * c_prev + _dot(kw, v_aug)
        return carry

    lax.fori_loop(0, cpb, body, 0)


def _mlstm(q, kt, v, ra, rb, rc, bsz, seq, cpb):
    t = q.shape[0]
    nblk = seq // (cpb * CHUNK)
    qk = N_HEADS * QK_DIM
    vw = N_HEADS * V_DIM
    fwd = lambda b, j: b * nblk + j
    bwd = lambda b, j: b * nblk + nblk - 1 - j

    def specs(blk, d):
        return [pl.BlockSpec((cpb * CHUNK, qk), lambda b, j: (blk(b, j), 0)),
                pl.BlockSpec((cpb, qk, CHUNK), lambda b, j: (blk(b, j), 0, 0)),
                pl.BlockSpec((cpb * CHUNK, vw), lambda b, j: (blk(b, j), 0))] + \
               [pl.BlockSpec((None, cpb * 8, CHUNK), lambda b, j: (d, blk(b, j), 0))] * 3

    return pl.pallas_call(
        _mlstm_kernel,
        grid=(bsz, nblk),
        in_specs=specs(fwd, 0) + specs(bwd, 1),
        out_specs=[pl.BlockSpec((cpb * CHUNK, vw), lambda b, j: (fwd(b, j), 0)),
                   pl.BlockSpec((cpb * CHUNK, vw), lambda b, j: (bwd(b, j), 0))],
        out_shape=[jax.ShapeDtypeStruct((t, vw), BF16)] * 2,
        scratch_shapes=[pltpu.VMEM((2, N_HEADS, QK_DIM, 2 * V_DIM), F32)],
        compiler_params=_vmem_params(("parallel", "arbitrary"), 32),
        name="mlstm",
    )(q, kt, v, ra, rb, rc, q, kt, v, ra, rb, rc)


def _fft_pitch(n1):
    return 8 * ((n1 // 4) | 1)


@functools.lru_cache(maxsize=None)
def _fft_tables(seq):
    n1 = seq // CHUNK
    two_pi = 2.0 * np.pi
    k1 = np.arange(n1, dtype=np.int64)
    n = 128 * np.arange(n1, dtype=np.int64)[None, None, :] + np.arange(128, dtype=np.int64)[:, None, None]
    ang = two_pi * ((k1[None, :, None] * n) % seq).astype(np.float64) / seq
    ga = np.concatenate([np.cos(ang), -np.sin(ang)], axis=1)
    j = np.arange(128, dtype=np.int64)
    a128 = two_pi * ((j[:, None] * j[None, :]) % 128).astype(np.float64) / 128.0
    ff = np.concatenate([np.cos(a128), -np.sin(a128)], axis=0)
    cs = np.concatenate([np.cos(a128), np.sin(a128)], axis=0) / np.sqrt(128.0 * seq)
    return tuple(np.asarray(a, dtype=np.float32) for a in (ga, ff, cs))


def _fft_kernel(z_ref, ga_ref, ff_ref, cs_ref, y_ref, scr):
    j = pl.program_id(2)
    n1 = ga_ref.shape[2]
    pitch = scr.shape[0] // CHUNK
    for l in range(FFT_NB):
        zl = z_ref[pl.ds(l, n1, stride=FFT_NB), :].astype(BF16)
        row0 = pl.multiple_of((j * FFT_NB + l) * pitch, 8)
        scr[pl.ds(row0, 2 * n1), :] = _dot(ga_ref[l], zl)

    @pl.when(j == pl.num_programs(2) - 1)
    def _():
        ff = ff_ref[...]
        cs = cs_ref[...]

        def body(it, carry):
            k0 = it * FFT_KB
            p = [scr[pl.ds(ri * n1 + k0 + kk, CHUNK, stride=pitch), :]
                 for kk in range(FFT_KB) for ri in range(2)]
            m = _dot(ff, jnp.concatenate(p, axis=1).astype(BF16))
            x = []
            for kk in range(FFT_KB):
                c0 = 2 * kk * GROUP_DIM
                xr = m[:CHUNK, c0:c0 + GROUP_DIM] - m[CHUNK:, c0 + GROUP_DIM:c0 + 2 * GROUP_DIM]
                xi = m[:CHUNK, c0 + GROUP_DIM:c0 + 2 * GROUP_DIM] + m[CHUNK:, c0:c0 + GROUP_DIM]
                x.append(jnp.concatenate([xr, xi], axis=1))
            y = _dot(jnp.concatenate(x, axis=0).astype(BF16), cs)
            for kk in range(FFT_KB):
                row0 = pl.multiple_of((k0 + kk) * FFT_NB, 8)
                for kb in range(CHUNK // FFT_NB):
                    r0 = kk * CHUNK + kb * FFT_NB
                    y_ref[kb, pl.ds(row0, FFT_NB), :] = y[r0:r0 + FFT_NB, :]
            return carry

        lax.fori_loop(0, n1 // FFT_KB, body, 0, unroll=2)


def _fourier(z5, seq):
    bsz = z5.shape[0]
    n1 = seq // CHUNK
    nblk = CHUNK // FFT_NB
    pitch = _fft_pitch(n1)
    ga, ff, cs = (jnp.asarray(a).astype(BF16) for a in _fft_tables(seq))
    return pl.pallas_call(
        _fft_kernel,
        grid=(bsz, N_GROUPS, nblk),
        in_specs=[pl.BlockSpec((None, None, None, n1 * FFT_NB, GROUP_DIM), lambda b, g, j: (b, g, j, 0, 0)),
                  pl.BlockSpec((FFT_NB, 2 * n1, n1), lambda b, g, j: (j, 0, 0)),
                  pl.BlockSpec(ff.shape, lambda b, g, j: (0, 0)),
                  pl.BlockSpec(cs.shape, lambda b, g, j: (0, 0))],
        out_specs=pl.BlockSpec((None, None, nblk, n1 * FFT_NB, GROUP_DIM), lambda b, g, j: (b, g, 0, 0, 0)),
        out_shape=jax.ShapeDtypeStruct(z5.shape, F32),
        scratch_shapes=[pltpu.VMEM((CHUNK * pitch, GROUP_DIM), F32)],
        compiler_params=_vmem_params(("parallel", "parallel", "arbitrary"), 48),
        name="fft",
    )(z5, ga, ff, cs)


def _gather_rows(y_refs, k2_off, n1, kpt):
    rows = [jnp.concatenate([ref[pl.ds(k2_off + kk, n1, stride=FFT_NB), :] for ref in y_refs], axis=1)
            for kk in range(kpt)]
    return jnp.concatenate(rows, axis=0)


def _outmlp_kernel(alpha, ff_chunk, spb, x_ref, hf_ref, hb_ref, o_ref, y0_ref, y1_ref, y2_ref, y3_ref,
                   g1_ref, sc2_ref, sh2_ref, g2_ref,
                   nw_ref, wout_ref, l1g_ref, l1b_ref, w1_ref, b1_ref, w2_ref, b2_ref, l2g_ref, l2b_ref,
                   out_ref):
    n1 = y0_ref.shape[0] // FFT_NB
    kpt = x_ref.shape[0] // n1
    k2_off = ((pl.program_id(0) % spb) * kpt) % FFT_NB
    yf = _gather_rows((y0_ref, y1_ref, y2_ref, y3_ref), k2_off, n1, kpt).astype(BF16)
    hsum = hf_ref[...].astype(F32) + hb_ref[...].astype(F32)
    heads = [_ln_plain(hsum[:, j * V_DIM:(j + 1) * V_DIM]) for j in range(N_HEADS)]
    ym = jnp.concatenate(heads, axis=1) * nw_ref[...] * jax.nn.sigmoid(o_ref[...].astype(F32))
    mix = _dot(jnp.concatenate([ym.astype(BF16), yf], axis=1), wout_ref[...])
    x1 = _ln_plain(alpha * x_ref[...] + (1.0 + g1_ref[...]) * mix) * l1g_ref[...] + l1b_ref[...]
    h2 = (_ln_plain(x1) * (1.0 + sc2_ref[...]) + sh2_ref[...]).astype(BF16)
    ff = b2_ref[...]
    for j in range(w1_ref.shape[1] // ff_chunk):
        sl = slice(j * ff_chunk, (j + 1) * ff_chunk)
        hid = jnp.maximum(_dot(h2, w1_ref[:, sl]) + b1_ref[:, sl], 0.0)
        ff = ff + _dot((hid * hid).astype(BF16), w2_ref[sl, :])
    out_ref[...] = _ln_plain(alpha * x1 + (1.0 + g2_ref[...]) * ff) * l2g_ref[...] + l2b_ref[...]


def _outmlp(x2d, hf, hb, o, y5, mod4, w, seq, tm, alpha, ff_chunk):
    t, d = x2d.shape
    spb = seq // tm
    n1 = seq // CHUNK
    kpt = tm // n1
    assert FFT_NB % kpt == 0
    vw = N_HEADS * V_DIM
    row = lambda width: pl.BlockSpec((tm, width), lambda i: (i, 0))
    yspec = lambda g: pl.BlockSpec((None, None, None, n1 * FFT_NB, GROUP_DIM),
                                   lambda i: (i // spb, g, ((i % spb) * kpt) // FFT_NB, 0, 0))
    modspec = lambda j: pl.BlockSpec((None, None, 1, d), lambda i: (i // spb, j, 0, 0))
    const = lambda a: pl.BlockSpec(a.shape, lambda i: (0,) * a.ndim, pipeline_mode=pl.Buffered(1))
    names = ["nw", "wout", "l1g", "l1b", "w1", "b1", "w2", "b2", "l2g", "l2b"]
    return pl.pallas_call(
        functools.partial(_outmlp_kernel, alpha, ff_chunk, spb),
        grid=(t // tm,),
        in_specs=[row(d), row(vw), row(vw), row(vw)]
                 + [yspec(g) for g in range(N_GROUPS)]
                 + [modspec(2), modspec(4), modspec(3), modspec(5)] + [const(w[k]) for k in names],
        out_specs=row(d),
        out_shape=jax.ShapeDtypeStruct((t, d), F32),
        compiler_params=_vmem_params(("parallel",), 56),
        name="outmlp",
    )(x2d, hf, hb, o, y5, y5, y5, y5, mod4, mod4, mod4, mod4, *[w[k] for k in names])


def _layer_weights(w_in, b_gate, w_out, w_ff1, w_ff2, b_ff1, b_ff2, mlstm_norm_w, ln1_g, ln1_b, ln2_g, ln2_b):
    qk = N_HEADS * QK_DIM
    vw = N_HEADS * V_DIM
    fw = N_GROUPS * GROUP_DIM
    o0, o1, o2, o3, o4 = qk, 2 * qk, 2 * qk + vw, 2 * qk + 2 * vw, 2 * qk + 2 * vw + fw
    wg = w_in[:, o4:].T.reshape(2, 2, N_HEADS, -1)
    bg = b_gate.astype(F32).reshape(2, 2, N_HEADS, 1)
    dup = lambda a: jnp.concatenate([a, a], axis=1)
    r = lambda a: a.astype(F32).reshape(1, -1)
    return {
        "wq": (w_in[:, :o0] * (QK_DIM ** -0.5)).astype(BF16),
        "wkt": w_in[:, o0:o1].T.astype(BF16),
        "wv": w_in[:, o1:o2].astype(BF16),
        "wo": w_in[:, o2:o3].astype(BF16),
        "wf": w_in[:, o3:o4].astype(BF16),
        "wgi": dup(wg[:, 0]).astype(BF16), "wgf": dup(wg[:, 1]).astype(BF16),
        "bgi": dup(bg[:, 0]), "bgf": dup(bg[:, 1]),
        "nw": r(mlstm_norm_w), "wout": w_out.astype(BF16), "l1g": r(ln1_g), "l1b": r(ln1_b),
        "w1": w_ff1.astype(BF16), "b1": r(b_ff1), "w2": w_ff2.astype(BF16), "b2": r(b_ff2),
        "l2g": r(ln2_g), "l2b": r(ln2_b),
    }


def kernel(x, c, w_ada, b_ada, w_in, b_gate, mlstm_norm_w, w_out, ln1_g, ln1_b,
           w_ff1, b_ff1, w_ff2, b_ff2, ln2_g, ln2_b):
    bsz, seq, d = x.shape
    depth = w_ada.shape[0]
    alpha = (2 * depth) ** 0.25
    assert seq % (MLSTM_CPB * CHUNK) == 0 and (seq // CHUNK) % 8 == 0
    assert d == N_HEADS * V_DIM + N_GROUPS * GROUP_DIM
    tm = min(512, FFT_NB * (seq // CHUNK))
    x2d = x.reshape(bsz * seq, d)
    for l in range(depth):
        w = _layer_weights(w_in[l], b_gate[l], w_out[l], w_ff1[l], w_ff2[l], b_ff1[l], b_ff2[l],
                           mlstm_norm_w[l], ln1_g[l], ln1_b[l], ln2_g[l], ln2_b[l])
        mod4 = _adaln(c, w_ada[l], b_ada[l]).reshape(bsz, N_MOD, 1, d)
        q, kt, v, o, z5, gi, gf = _inproj(x2d, mod4, w, seq, tm)
        ra, rb, rc = _gate_prep(gi, gf, bsz)
        hf, hb = _mlstm(q, kt, v, ra, rb, rc, bsz, seq, MLSTM_CPB)
        y5 = _fourier(z5, seq)
        x2d = _outmlp(x2d, hf, hb, o, y5, mod4, w, seq, tm, alpha, ff_chunk=1024)
    return x2d.reshape(bsz, seq, d)
```

```python
import functools

import numpy as np
import jax
import jax.numpy as jnp
from jax import lax
from jax.experimental import pallas as pl
from jax.experimental.pallas import tpu as pltpu

F32 = jnp.float32
BF16 = jnp.bfloat16

CHUNK = 128
N_HEADS = 4
QK_DIM = 64
V_DIM = 128
N_GROUPS = 4
GROUP_DIM = 128
N_MOD = 6
LN_EPS = 1e-5
MASKED = -1e30
FFT_NB = 8
FFT_KB = 4
MLSTM_CPB = 4
OUTMLP_SUBTILES = 2
INPROJ_SUBTILES = 2

_NT = (((1,), (1,)), ((), ()))


def _dot(a, b):
    return jnp.dot(a, b, preferred_element_type=F32)


def _dot_nt(a, b):
    return lax.dot_general(a, b, _NT, preferred_element_type=F32)


def _ln_plain(x):
    mu = jnp.mean(x, axis=-1, keepdims=True)
    xc = x - mu
    var = jnp.mean(xc * xc, axis=-1, keepdims=True)
    return xc * lax.rsqrt(var + LN_EPS)


def _log_sigmoid(x):
    return jnp.minimum(x, 0.0) - jnp.log1p(jnp.exp(-jnp.abs(x)))


def _split3(x):
    hi = x.astype(BF16)
    r1 = x - hi.astype(F32)
    mid = r1.astype(BF16)
    lo = (r1 - mid.astype(F32)).astype(BF16)
    return hi, mid, lo


def _vmem_params(semantics, limit_mb):
    return pltpu.CompilerParams(dimension_semantics=semantics, vmem_limit_bytes=limit_mb * 1024 * 1024)


def _adaln_kernel(ct_ref, w_ref, b_ref, o_ref):
    ct = ct_ref[...]
    act = ct * jax.nn.sigmoid(ct)
    w = w_ref[...]
    for b in range(ct.shape[1]):
        o_ref[b:b + 1, :] = jnp.sum(act[:, b:b + 1] * w, axis=0, keepdims=True) + b_ref[...]


def _adaln(c, w_ada, b_ada):
    bsz, d = c.shape
    n = w_ada.shape[1]
    tn = 1024
    return pl.pallas_call(
        _adaln_kernel,
        grid=(n // tn,),
        in_specs=[pl.BlockSpec((d, bsz), lambda j: (0, 0)),
                  pl.BlockSpec((d, tn), lambda j: (0, j)),
                  pl.BlockSpec((1, tn), lambda j: (0, j))],
        out_specs=pl.BlockSpec((bsz, tn), lambda j: (0, j)),
        out_shape=jax.ShapeDtypeStruct((bsz, n), F32),
        compiler_params=_vmem_params(("parallel",), 32),
        name="adaln",
    )(c.T, w_ada, b_ada.reshape(1, n))


def _inproj_kernel(x_ref, sh_ref, sc_ref, wrow_ref, wlane_ref, bg_ref,
                   q_ref, kt_ref, v_ref, o_ref, z_ref, gi_ref, gf_ref):
    qk, vw = q_ref.shape[1], v_ref.shape[1]
    n_chunks = x_ref.shape[0] // CHUNK
    cps = max(1, n_chunks // INPROJ_SUBTILES)
    for c0 in range(0, n_chunks, cps):
        rs = slice(c0 * CHUNK, (c0 + cps) * CHUNK)
        h = _ln_plain(x_ref[rs, :]) * (1.0 + sc_ref[...]) + sh_ref[...]
        hb = h.astype(BF16)
        rowp = _dot(hb, wrow_ref[...])
        q_ref[rs, :] = rowp[:, :qk].astype(BF16)
        v_ref[rs, :] = rowp[:, qk:qk + vw].astype(BF16)
        o_ref[rs, :] = rowp[:, qk + vw:qk + 2 * vw].astype(BF16)
        fz = rowp[:, qk + 2 * vw:]
        for j in range(cps):
            for g in range(N_GROUPS):
                for jb in range(CHUNK // FFT_NB):
                    r0 = j * CHUNK + jb * FFT_NB
                    z_ref[g, jb, (c0 + j) * FFT_NB:(c0 + j + 1) * FFT_NB, :] = (
                        fz[r0:r0 + FFT_NB, g * GROUP_DIM:(g + 1) * GROUP_DIM])
        lanep = _dot_nt(wlane_ref[...], hb)
        gates = lanep[qk:, :] + bg_ref[...]
        for j in range(cps):
            cl = slice(j * CHUNK, (j + 1) * CHUNK)
            kt_ref[c0 + j] = lanep[:qk, cl].astype(BF16)
            for d in range(2):
                gi_ref[d, (c0 + j) * 8:(c0 + j + 1) * 8, :] = gates[16 * d:16 * d + 8, cl]
                gf_ref[d, (c0 + j) * 8:(c0 + j + 1) * 8, :] = gates[16 * d + 8:16 * d + 16, cl]


def _inproj(x2d, mod4, w, seq, tm):
    t, d = x2d.shape
    spb = seq // tm
    qk = N_HEADS * QK_DIM
    vw = N_HEADS * V_DIM
    nblk = CHUNK // FFT_NB
    cpt = tm // CHUNK
    full = lambda a: pl.BlockSpec(a.shape, lambda i: (0,) * a.ndim)
    modspec = lambda j: pl.BlockSpec((None, None, 1, d), lambda i: (i // spb, j, 0, 0))
    names = ["wrow", "wlane", "bg"]
    return pl.pallas_call(
        _inproj_kernel,
        grid=(t // tm,),
        in_specs=[pl.BlockSpec((tm, d), lambda i: (i, 0)), modspec(0), modspec(1)] + [full(w[k]) for k in names],
        out_specs=[pl.BlockSpec((tm, qk), lambda i: (i, 0)),
                   pl.BlockSpec((cpt, qk, CHUNK), lambda i: (i, 0, 0)),
                   pl.BlockSpec((tm, vw), lambda i: (i, 0)),
                   pl.BlockSpec((tm, vw), lambda i: (i, 0)),
                   pl.BlockSpec((None, N_GROUPS, nblk, cpt * FFT_NB, GROUP_DIM),
                                lambda i: (i // spb, 0, 0, i % spb, 0)),
                   pl.BlockSpec((2, cpt * 8, CHUNK), lambda i: (0, i, 0)),
                   pl.BlockSpec((2, cpt * 8, CHUNK), lambda i: (0, i, 0))],
        out_shape=[jax.ShapeDtypeStruct((t, qk), BF16),
                   jax.ShapeDtypeStruct((t // CHUNK, qk, CHUNK), BF16),
                   jax.ShapeDtypeStruct((t, vw), BF16),
                   jax.ShapeDtypeStruct((t, vw), BF16),
                   jax.ShapeDtypeStruct((t // seq, N_GROUPS, nblk, (seq // CHUNK) * FFT_NB, GROUP_DIM), F32),
                   jax.ShapeDtypeStruct((2, t // CHUNK * 8, CHUNK), F32),
                   jax.ShapeDtypeStruct((2, t // CHUNK * 8, CHUNK), F32)],
        compiler_params=_vmem_params(("parallel",), 48),
        name="inproj",
    )(x2d, mod4, mod4, *[w[k] for k in names])


def _gate_prep_kernel(gi_ref, gf_ref, ra_ref, rb_ref, rc_ref, bend_scr, amax_scr, mprev_scr):
    rows = gi_ref.shape[1]
    n_chunks = rows // 8
    lane = lax.broadcasted_iota(jnp.int32, (rows, CHUNK), 1)
    first = (lax.broadcasted_iota(jnp.int32, (rows, CHUNK), 0) & 7) < N_HEADS
    src = lax.broadcasted_iota(jnp.int32, (CHUNK, 2 * CHUNK), 0)
    dst = lax.broadcasted_iota(jnp.int32, (CHUNK, 2 * CHUNK), 1)
    for d in range(2):
        feeds = (src <= dst) if d == 0 else (src >= dst)
        cum_and_total = jnp.where((dst >= CHUNK) | feeds, 1.0, 0.0).astype(BF16)
        bc = sum(_dot(p, cum_and_total) for p in _split3(_log_sigmoid(gf_ref[d])))
        b, b_end = bc[:, :CHUNK], bc[:, CHUNK:]
        a = gi_ref[d] - b
        cm = a
        for k in range(7):
            sh = 1 << k
            if d == 0:
                cm = jnp.maximum(cm, jnp.where(lane >= sh, pltpu.roll(cm, sh, 1), MASKED))
            else:
                cm = jnp.maximum(cm, jnp.where(lane < CHUNK - sh, pltpu.roll(cm, CHUNK - sh, 1), MASKED))
        a_max = jnp.broadcast_to(jnp.max(a, axis=1, keepdims=True), a.shape)
        bend_scr[...] = b_end
        amax_scr[...] = a_max

        def scan(c, m):
            r = pl.ds(pl.multiple_of((c if d == 0 else n_chunks - 1 - c) * 8, 8), 8)
            mprev_scr[r, :] = m
            return bend_scr[r, :] + jnp.maximum(m, amax_scr[r, :])

        lax.fori_loop(0, n_chunks, scan, jnp.zeros((8, CHUNK), F32))
        m_prev = mprev_scr[...]
        u = -jnp.maximum(m_prev, cm)
        m_new = b_end + jnp.maximum(m_prev, a_max)
        ra_ref[d] = jnp.where(first, a, jnp.exp(b_end + a - m_new))
        rb_ref[d] = jnp.where(first, u, jnp.exp(m_prev + u))
        rc_ref[d] = jnp.where(first, jnp.exp(u - b), jnp.exp(b_end + m_prev - m_new))


def _gate_prep(gi, gf, bsz):
    rows = gi.shape[1] // bsz
    spec = pl.BlockSpec((2, rows, CHUNK), lambda b: (0, b, 0))
    shape = jax.ShapeDtypeStruct(gi.shape, F32)
    return pl.pallas_call(
        _gate_prep_kernel,
        grid=(bsz,),
        in_specs=[spec, spec],
        out_specs=[spec, spec, spec],
        out_shape=[shape, shape, shape],
        scratch_shapes=[pltpu.VMEM((rows, CHUNK), F32)] * 3,
        compiler_params=_vmem_params(("parallel",), 32),
        name="gate_prep",
    )(gi, gf)


@functools.lru_cache(maxsize=None)
def _column_selector():
    sel = np.zeros((48, N_HEADS * 3 * CHUNK), np.float32)
    for hd in range(N_HEADS):
        for j, (arr, row) in enumerate(((0, hd), (0, N_HEADS + hd), (1, hd))):
            for piece in range(3):
                sel[arr * 24 + piece * 8 + row, (3 * hd + j) * CHUNK:(3 * hd + j + 1) * CHUNK] = 1.0
    return sel


def _mlstm_kernel(sel_ref, qf_ref, ktf_ref, vf_ref, raf_ref, rbf_ref, rcf_ref,
                  qb_ref, ktb_ref, vb_ref, rab_ref, rbb_ref, rcb_ref, hf_ref, hb_ref, c_scr):
    @pl.when(pl.program_id(1) == 0)
    def _():
        c_scr[...] = jnp.zeros_like(c_scr)

    cpb = ktf_ref.shape[0]
    t_i = lax.broadcasted_iota(jnp.int32, (CHUNK, CHUNK), 0)
    s_i = lax.broadcasted_iota(jnp.int32, (CHUNK, CHUNK), 1)
    visible = (s_i <= t_i, s_i >= t_i)
    ones = jnp.ones((CHUNK, V_DIM), BF16)
    dirs = ((qf_ref, ktf_ref, vf_ref, raf_ref, rbf_ref, rcf_ref, hf_ref),
            (qb_ref, ktb_ref, vb_ref, rab_ref, rbb_ref, rcb_ref, hb_ref))

    def body(i, carry):
        units = []
        for d, (q_ref, kt_ref, v_ref, ra_ref, rb_ref, rc_ref, h_ref) in enumerate(dirs):
            li = i if d == 0 else cpb - 1 - i
            r8 = pl.ds(pl.multiple_of(li * 8, 8), 8)
            rl = pl.ds(pl.multiple_of(li * CHUNK, CHUNK), CHUNK)
            ra = ra_ref[r8, :]
            rc = rc_ref[r8, :]
            pieces = [p.astype(F32) for arr in (rb_ref[r8, :], rc) for p in _split3(arr)]
            colb = lax.dot_general(jnp.concatenate(pieces, axis=0).astype(BF16), sel_ref[...],
                                   (((0,), (0,)), ((), ())), preferred_element_type=F32)
            kt = kt_ref[li]
            for hd in range(N_HEADS):
                qh = q_ref[rl, hd * QK_DIM:(hd + 1) * QK_DIM]
                kth = kt[hd * QK_DIM:(hd + 1) * QK_DIM, :]
                units.append(dict(d=d, hd=hd, rl=rl, h_ref=h_ref, ra=ra, rc=rc, colb=colb, qh=qh, kth=kth,
                                  v_ref=v_ref, qk=_dot(qh, kth)))
        for u in units:
            d, hd, colb = u["d"], u["hd"], u["colb"]
            c0 = 3 * hd * CHUNK
            u_b = colb[:, c0:c0 + CHUNK]
            w_inter = colb[:, c0 + CHUNK:c0 + CHUNK + QK_DIM]
            w_intra = jnp.exp(jnp.where(visible[d], u_b + u["ra"][hd:hd + 1, :], MASKED))
            s = (u["qk"] * w_intra).astype(BF16)
            q_inter = (u["qh"].astype(F32) * w_inter).astype(BF16)
            u["c_prev"] = c_scr[d, hd]
            u["v_aug"] = jnp.concatenate([u["v_ref"][u["rl"], hd * V_DIM:(hd + 1) * V_DIM], ones], axis=1)
            u["nd"] = _dot(jnp.concatenate([s, q_inter], axis=1),
                           jnp.concatenate([u["v_aug"], u["c_prev"].astype(BF16)], axis=0))
        for u in units:
            hd, nd = u["hd"], u["nd"]
            exp_neg_m = u["colb"][:, (3 * hd + 2) * CHUNK:(3 * hd + 3) * CHUNK]
            den = jnp.maximum(jnp.abs(nd[:, V_DIM:]), exp_neg_m)
            u["h_ref"][u["rl"], hd * V_DIM:(hd + 1) * V_DIM] = (nd[:, :V_DIM] / den).astype(u["h_ref"].dtype)
            wk_row = u["ra"][N_HEADS + hd:N_HEADS + hd + 1, :]
            decay = jnp.broadcast_to(u["rc"][N_HEADS + hd:N_HEADS + hd + 1, :], (QK_DIM, CHUNK))
            kw = (u["kth"].astype(F32) * wk_row).astype(BF16)
            c_scr[u["d"], hd] = jnp.concatenate([decay, decay], axis=1) * u["c_prev"] + _dot(kw, u["v_aug"])
        return carry

    lax.fori_loop(0, cpb, body, 0)


def _mlstm(q, kt, v, ra, rb, rc, bsz, seq, cpb):
    t = q.shape[0]
    nblk = seq // (cpb * CHUNK)
    qk = N_HEADS * QK_DIM
    vw = N_HEADS * V_DIM
    fwd = lambda b, j: b * nblk + j
    bwd = lambda b, j: b * nblk + nblk - 1 - j

    def specs(blk, d):
        return [pl.BlockSpec((cpb * CHUNK, qk), lambda b, j: (blk(b, j), 0)),
                pl.BlockSpec((cpb, qk, CHUNK), lambda b, j: (blk(b, j), 0, 0)),
                pl.BlockSpec((cpb * CHUNK, vw), lambda b, j: (blk(b, j), 0))] + \
               [pl.BlockSpec((None, cpb * 8, CHUNK), lambda b, j: (d, blk(b, j), 0))] * 3

    sel = jnp.asarray(_column_selector()).astype(BF16)
    return pl.pallas_call(
        _mlstm_kernel,
        grid=(bsz, nblk),
        in_specs=[pl.BlockSpec(sel.shape, lambda b, j: (0, 0))] + specs(fwd, 0) + specs(bwd, 1),
        out_specs=[pl.BlockSpec((cpb * CHUNK, vw), lambda b, j: (fwd(b, j), 0)),
                   pl.BlockSpec((cpb * CHUNK, vw), lambda b, j: (bwd(b, j), 0))],
        out_shape=[jax.ShapeDtypeStruct((t, vw), BF16)] * 2,
        scratch_shapes=[pltpu.VMEM((2, N_HEADS, QK_DIM, 2 * V_DIM), F32)],
        compiler_params=_vmem_params(("parallel", "arbitrary"), 32),
        name="mlstm",
    )(sel, q, kt, v, ra, rb, rc, q, kt, v, ra, rb, rc)


def _fft_pitch(n1):
    return 8 * ((n1 // 4) | 1)


@functools.lru_cache(maxsize=None)
def _fft_tables(seq):
    n1 = seq // CHUNK
    two_pi = 2.0 * np.pi
    k1 = np.arange(n1, dtype=np.int64)
    n = 128 * np.arange(n1, dtype=np.int64)[None, None, :] + np.arange(128, dtype=np.int64)[:, None, None]
    ang = two_pi * ((k1[None, :, None] * n) % seq).astype(np.float64) / seq
    ga = np.concatenate([np.cos(ang), -np.sin(ang)], axis=1)
    j = np.arange(128, dtype=np.int64)
    a128 = two_pi * ((j[:, None] * j[None, :]) % 128).astype(np.float64) / 128.0
    ff = np.concatenate([np.cos(a128), -np.sin(a128)], axis=0)
    cs = np.concatenate([np.cos(a128), np.sin(a128)], axis=0) / np.sqrt(128.0 * seq)
    return tuple(np.asarray(a, dtype=np.float32) for a in (ga, ff, cs))


def _fft_kernel(z_ref, ga_ref, ff_ref, cs_ref, y_ref, scr):
    j = pl.program_id(2)
    n1 = ga_ref.shape[2]
    pitch = scr.shape[0] // CHUNK
    for l in range(FFT_NB):
        zl = z_ref[pl.ds(l, n1, stride=FFT_NB), :].astype(BF16)
        row0 = pl.multiple_of((j * FFT_NB + l) * pitch, 8)
        scr[pl.ds(row0, 2 * n1), :] = _dot(ga_ref[l], zl)

    @pl.when(j == pl.num_programs(2) - 1)
    def _():
        ff = ff_ref[...]
        cs = cs_ref[...]

        def body(it, carry):
            k0 = it * FFT_KB
            p = [scr[pl.ds(ri * n1 + k0 + kk, CHUNK, stride=pitch), :]
                 for kk in range(FFT_KB) for ri in range(2)]
            m = _dot(ff, jnp.concatenate(p, axis=1).astype(BF16))
            x = []
            for kk in range(FFT_KB):
                c0 = 2 * kk * GROUP_DIM
                xr = m[:CHUNK, c0:c0 + GROUP_DIM] - m[CHUNK:, c0 + GROUP_DIM:c0 + 2 * GROUP_DIM]
                xi = m[:CHUNK, c0 + GROUP_DIM:c0 + 2 * GROUP_DIM] + m[CHUNK:, c0:c0 + GROUP_DIM]
                x.append(jnp.concatenate([xr, xi], axis=1))
            y = _dot(jnp.concatenate(x, axis=0).astype(BF16), cs)
            for kk in range(FFT_KB):
                row0 = pl.multiple_of((k0 + kk) * FFT_NB, 8)
                for kb in range(CHUNK // FFT_NB):
                    r0 = kk * CHUNK + kb * FFT_NB
                    y_ref[kb, pl.ds(row0, FFT_NB), :] = y[r0:r0 + FFT_NB, :]
            return carry

        lax.fori_loop(0, n1 // FFT_KB, body, 0, unroll=2)


def _fourier(z5, seq):
    bsz = z5.shape[0]
    n1 = seq // CHUNK
    nblk = CHUNK // FFT_NB
    pitch = _fft_pitch(n1)
    ga, ff, cs = (jnp.asarray(a).astype(BF16) for a in _fft_tables(seq))
    return pl.pallas_call(
        _fft_kernel,
        grid=(bsz, N_GROUPS, nblk),
        in_specs=[pl.BlockSpec((None, None, None, n1 * FFT_NB, GROUP_DIM), lambda b, g, j: (b, g, j, 0, 0)),
                  pl.BlockSpec((FFT_NB, 2 * n1, n1), lambda b, g, j: (j, 0, 0)),
                  pl.BlockSpec(ff.shape, lambda b, g, j: (0, 0)),
                  pl.BlockSpec(cs.shape, lambda b, g, j: (0, 0))],
        out_specs=pl.BlockSpec((None, None, nblk, n1 * FFT_NB, GROUP_DIM), lambda b, g, j: (b, g, 0, 0, 0)),
        out_shape=jax.ShapeDtypeStruct(z5.shape, F32),
        scratch_shapes=[pltpu.VMEM((CHUNK * pitch, GROUP_DIM), F32)],
        compiler_params=_vmem_params(("parallel", "parallel", "arbitrary"), 48),
        name="fft",
    )(z5, ga, ff, cs)


def _gather_rows(y_refs, k2_off, n1, kpt):
    rows = [jnp.concatenate([ref[pl.ds(k2_off + kk, n1, stride=FFT_NB), :] for ref in y_refs], axis=1)
            for kk in range(kpt)]
    return jnp.concatenate(rows, axis=0)


def _outmlp_kernel(alpha, ff_chunk, spb, x_ref, hf_ref, hb_ref, o_ref, y0_ref, y1_ref, y2_ref, y3_ref,
                   g1_ref, sc2_ref, sh2_ref, g2_ref,
                   nw_ref, wout_ref, l1g_ref, l1b_ref, w1_ref, b1_ref, w2_ref, b2_ref, l2g_ref, l2b_ref,
                   out_ref):
    n1 = y0_ref.shape[0] // FFT_NB
    kpt = x_ref.shape[0] // n1
    k2_off = ((pl.program_id(0) % spb) * kpt) % FFT_NB
    kps = max(1, kpt // OUTMLP_SUBTILES)
    subs = [dict(k2l=k2l, rs=slice(k2l * n1, (k2l + kps) * n1)) for k2l in range(0, kpt, kps)]
    for s in subs:
        rs = s["rs"]
        yf = _gather_rows((y0_ref, y1_ref, y2_ref, y3_ref), k2_off + s["k2l"], n1, kps).astype(BF16)
        hsum = hf_ref[rs, :].astype(F32) + hb_ref[rs, :].astype(F32)
        heads = [_ln_plain(hsum[:, j * V_DIM:(j + 1) * V_DIM]) for j in range(N_HEADS)]
        ym = jnp.concatenate(heads, axis=1) * nw_ref[...] * jax.nn.sigmoid(o_ref[rs, :].astype(F32))
        s["mix"] = _dot(jnp.concatenate([ym.astype(BF16), yf], axis=1), wout_ref[...])
    for s in subs:
        x1 = _ln_plain(alpha * x_ref[s["rs"], :] + (1.0 + g1_ref[...]) * s["mix"]) * l1g_ref[...] + l1b_ref[...]
        s["x1"] = x1
        s["h2"] = (_ln_plain(x1) * (1.0 + sc2_ref[...]) + sh2_ref[...]).astype(BF16)
    for s in subs:
        ff = b2_ref[...]
        for j in range(w1_ref.shape[1] // ff_chunk):
            sl = slice(j * ff_chunk, (j + 1) * ff_chunk)
            hid = jnp.maximum(_dot(s["h2"], w1_ref[:, sl]) + b1_ref[:, sl], 0.0)
            ff = ff + _dot((hid * hid).astype(BF16), w2_ref[sl, :])
        s["ff"] = ff
    for s in subs:
        out_ref[s["rs"], :] = (_ln_plain(alpha * s["x1"] + (1.0 + g2_ref[...]) * s["ff"]) * l2g_ref[...]
                               + l2b_ref[...])


def _outmlp(x2d, hf, hb, o, y5, mod4, w, seq, tm, alpha, ff_chunk):
    t, d = x2d.shape
    spb = seq // tm
    n1 = seq // CHUNK
    kpt = tm // n1
    assert FFT_NB % kpt == 0
    vw = N_HEADS * V_DIM
    row = lambda width: pl.BlockSpec((tm, width), lambda i: (i, 0))
    yspec = lambda g: pl.BlockSpec((None, None, None, n1 * FFT_NB, GROUP_DIM),
                                   lambda i: (i // spb, g, ((i % spb) * kpt) // FFT_NB, 0, 0))
    modspec = lambda j: pl.BlockSpec((None, None, 1, d), lambda i: (i // spb, j, 0, 0))
    const = lambda a: pl.BlockSpec(a.shape, lambda i: (0,) * a.ndim, pipeline_mode=pl.Buffered(1))
    names = ["nw", "wout", "l1g", "l1b", "w1", "b1", "w2", "b2", "l2g", "l2b"]
    return pl.pallas_call(
        functools.partial(_outmlp_kernel, alpha, ff_chunk, spb),
        grid=(t // tm,),
        in_specs=[row(d), row(vw), row(vw), row(vw)]
                 + [yspec(g) for g in range(N_GROUPS)]
                 + [modspec(2), modspec(4), modspec(3), modspec(5)] + [const(w[k]) for k in names],
        out_specs=row(d),
        out_shape=jax.ShapeDtypeStruct((t, d), F32),
        compiler_params=_vmem_params(("parallel",), 56),
        name="outmlp",
    )(x2d, hf, hb, o, y5, y5, y5, y5, mod4, mod4, mod4, mod4, *[w[k] for k in names])


def _layer_weights(w_in, b_gate, w_out, w_ff1, w_ff2, b_ff1, b_ff2, mlstm_norm_w, ln1_g, ln1_b, ln2_g, ln2_b):
    qk = N_HEADS * QK_DIM
    vw = N_HEADS * V_DIM
    fw = N_GROUPS * GROUP_DIM
    o0, o1, o2, o3, o4 = qk, 2 * qk, 2 * qk + vw, 2 * qk + 2 * vw, 2 * qk + 2 * vw + fw
    wg = w_in[:, o4:].T.reshape(2, 2, N_HEADS, -1)
    bg = b_gate.astype(F32).reshape(2, 2, N_HEADS, 1)
    dup = lambda a: jnp.concatenate([a, a], axis=1)
    r = lambda a: a.astype(F32).reshape(1, -1)
    wq = w_in[:, :o0] * (QK_DIM ** -0.5)
    gate_rows = lambda a: jnp.concatenate([dup(a[:, 0]), dup(a[:, 1])], axis=1).reshape(32, -1)
    return {
        "wrow": jnp.concatenate([wq, w_in[:, o1:o4]], axis=1).astype(BF16),
        "wlane": jnp.concatenate([w_in[:, o0:o1].T, gate_rows(wg)], axis=0).astype(BF16),
        "bg": gate_rows(bg),
        "nw": r(mlstm_norm_w), "wout": w_out.astype(BF16), "l1g": r(ln1_g), "l1b": r(ln1_b),
        "w1": w_ff1.astype(BF16), "b1": r(b_ff1), "w2": w_ff2.astype(BF16), "b2": r(b_ff2),
        "l2g": r(ln2_g), "l2b": r(ln2_b),
    }


def kernel(x, c, w_ada, b_ada, w_in, b_gate, mlstm_norm_w, w_out, ln1_g, ln1_b,
           w_ff1, b_ff1, w_ff2, b_ff2, ln2_g, ln2_b):
    bsz, seq, d = x.shape
    depth = w_ada.shape[0]
    alpha = (2 * depth) ** 0.25
    assert seq % (MLSTM_CPB * CHUNK) == 0 and (seq // CHUNK) % 8 == 0
    assert d == N_HEADS * V_DIM + N_GROUPS * GROUP_DIM
    tm = min(512, FFT_NB * (seq // CHUNK))
    x2d = x.reshape(bsz * seq, d)
    for l in range(depth):
        w = _layer_weights(w_in[l], b_gate[l], w_out[l], w_ff1[l], w_ff2[l], b_ff1[l], b_ff2[l],
                           mlstm_norm_w[l], ln1_g[l], ln1_b[l], ln2_g[l], ln2_b[l])
        mod4 = _adaln(c, w_ada[l], b_ada[l]).reshape(bsz, N_MOD, 1, d)
        q, kt, v, o, z5, gi, gf = _inproj(x2d, mod4, w, seq, tm)
        ra, rb, rc = _gate_prep(gi, gf, bsz)
        hf, hb = _mlstm(q, kt, v, ra, rb, rc, bsz, seq, MLSTM_CPB)
        y5 = _fourier(z5, seq)
        x2d = _outmlp(x2d, hf, hb, o, y5, mod4, w, seq, tm, alpha, ff_chunk=1024)
    return x2d.reshape(bsz, seq, d)
```

```python
import functools

import numpy as np
import jax
import jax.numpy as jnp
from jax import lax
from jax.experimental import pallas as pl
from jax.experimental.pallas import tpu as pltpu

F32 = jnp.float32
BF16 = jnp.bfloat16

CHUNK = 128
N_HEADS = 4
QK_DIM = 64
V_DIM = 128
N_GROUPS = 4
GROUP_DIM = 128
N_MOD = 6
LN_EPS = 1e-5
MASKED = -1e30
FFT_NB = 8
FFT_KB = 4
MLSTM_CPB = 4
OUTMLP_SUBTILES = 2
INPROJ_SUBTILES = 2

_NT = (((1,), (1,)), ((), ()))


def _dot(a, b):
    return jnp.dot(a, b, preferred_element_type=F32)


def _dot_nt(a, b):
    return lax.dot_general(a, b, _NT, preferred_element_type=F32)


def _ln_plain(x):
    mu = jnp.mean(x, axis=-1, keepdims=True)
    xc = x - mu
    var = jnp.mean(xc * xc, axis=-1, keepdims=True)
    return xc * lax.rsqrt(var + LN_EPS)


def _log_sigmoid(x):
    return jnp.minimum(x, 0.0) - jnp.log1p(jnp.exp(-jnp.abs(x)))


def _split3(x):
    hi = x.astype(BF16)
    r1 = x - hi.astype(F32)
    mid = r1.astype(BF16)
    lo = (r1 - mid.astype(F32)).astype(BF16)
    return hi, mid, lo


def _vmem_params(semantics, limit_mb):
    return pltpu.CompilerParams(dimension_semantics=semantics, vmem_limit_bytes=limit_mb * 1024 * 1024)


def _adaln_kernel(ct_ref, w_ref, b_ref, o_ref):
    ct = ct_ref[...]
    act = ct * jax.nn.sigmoid(ct)
    w = w_ref[...]
    for b in range(ct.shape[1]):
        o_ref[b:b + 1, :] = jnp.sum(act[:, b:b + 1] * w, axis=0, keepdims=True) + b_ref[...]


def _adaln(c, w_ada, b_ada):
    bsz, d = c.shape
    n = w_ada.shape[1]
    tn = 1024
    return pl.pallas_call(
        _adaln_kernel,
        grid=(n // tn,),
        in_specs=[pl.BlockSpec((d, bsz), lambda j: (0, 0)),
                  pl.BlockSpec((d, tn), lambda j: (0, j)),
                  pl.BlockSpec((1, tn), lambda j: (0, j))],
        out_specs=pl.BlockSpec((bsz, tn), lambda j: (0, j)),
        out_shape=jax.ShapeDtypeStruct((bsz, n), F32),
        compiler_params=_vmem_params(("parallel",), 32),
        name="adaln",
    )(c.T, w_ada, b_ada.reshape(1, n))


def _inproj_kernel(x_ref, sh_ref, sc_ref, wrow_ref, wlane_ref, bg_ref,
                   q_ref, kt_ref, v_ref, o_ref, z_ref, gi_ref, gf_ref):
    qk, vw = q_ref.shape[1], v_ref.shape[1]
    n_chunks = x_ref.shape[0] // CHUNK
    cps = max(1, n_chunks // INPROJ_SUBTILES)
    for c0 in range(0, n_chunks, cps):
        rs = slice(c0 * CHUNK, (c0 + cps) * CHUNK)
        h = _ln_plain(x_ref[rs, :]) * (1.0 + sc_ref[...]) + sh_ref[...]
        hb = h.astype(BF16)
        rowp = _dot(hb, wrow_ref[...])
        q_ref[rs, :] = rowp[:, :qk].astype(BF16)
        v_ref[rs, :] = rowp[:, qk:qk + vw].astype(BF16)
        o_ref[rs, :] = rowp[:, qk + vw:qk + 2 * vw].astype(BF16)
        fz = rowp[:, qk + 2 * vw:]
        for j in range(cps):
            for g in range(N_GROUPS):
                for jb in range(CHUNK // FFT_NB):
                    r0 = j * CHUNK + jb * FFT_NB
                    z_ref[g, jb, (c0 + j) * FFT_NB:(c0 + j + 1) * FFT_NB, :] = (
                        fz[r0:r0 + FFT_NB, g * GROUP_DIM:(g + 1) * GROUP_DIM])
        lanep = _dot_nt(wlane_ref[...], hb)
        gates = lanep[qk:, :] + bg_ref[...]
        for j in range(cps):
            cl = slice(j * CHUNK, (j + 1) * CHUNK)
            kt_ref[c0 + j] = lanep[:qk, cl].astype(BF16)
            for d in range(2):
                gi_ref[d, (c0 + j) * 8:(c0 + j + 1) * 8, :] = gates[16 * d:16 * d + 8, cl]
                gf_ref[d, (c0 + j) * 8:(c0 + j + 1) * 8, :] = gates[16 * d + 8:16 * d + 16, cl]


def _inproj(x2d, mod4, w, seq, tm):
    t, d = x2d.shape
    spb = seq // tm
    qk = N_HEADS * QK_DIM
    vw = N_HEADS * V_DIM
    nblk = CHUNK // FFT_NB
    cpt = tm // CHUNK
    full = lambda a: pl.BlockSpec(a.shape, lambda i: (0,) * a.ndim)
    modspec = lambda j: pl.BlockSpec((None, None, 1, d), lambda i: (i // spb, j, 0, 0))
    names = ["wrow", "wlane", "bg"]
    return pl.pallas_call(
        _inproj_kernel,
        grid=(t // tm,),
        in_specs=[pl.BlockSpec((tm, d), lambda i: (i, 0)), modspec(0), modspec(1)] + [full(w[k]) for k in names],
        out_specs=[pl.BlockSpec((tm, qk), lambda i: (i, 0)),
                   pl.BlockSpec((cpt, qk, CHUNK), lambda i: (i, 0, 0)),
                   pl.BlockSpec((tm, vw), lambda i: (i, 0)),
                   pl.BlockSpec((tm, vw), lambda i: (i, 0)),
                   pl.BlockSpec((None, N_GROUPS, nblk, cpt * FFT_NB, GROUP_DIM),
                                lambda i: (i // spb, 0, 0, i % spb, 0)),
                   pl.BlockSpec((2, cpt * 8, CHUNK), lambda i: (0, i, 0)),
                   pl.BlockSpec((2, cpt * 8, CHUNK), lambda i: (0, i, 0))],
        out_shape=[jax.ShapeDtypeStruct((t, qk), BF16),
                   jax.ShapeDtypeStruct((t // CHUNK, qk, CHUNK), BF16),
                   jax.ShapeDtypeStruct((t, vw), BF16),
                   jax.ShapeDtypeStruct((t, vw), BF16),
                   jax.ShapeDtypeStruct((t // seq, N_GROUPS, nblk, (seq // CHUNK) * FFT_NB, GROUP_DIM), F32),
                   jax.ShapeDtypeStruct((2, t // CHUNK * 8, CHUNK), F32),
                   jax.ShapeDtypeStruct((2, t // CHUNK * 8, CHUNK), F32)],
        compiler_params=_vmem_params(("parallel",), 48),
        name="inproj",
    )(x2d, mod4, mod4, *[w[k] for k in names])


def _gate_prep_kernel(gi_ref, gf_ref, ra_ref, rb_ref, rc_ref, bend_scr, amax_scr, mprev_scr):
    rows = gi_ref.shape[1]
    n_chunks = rows // 8
    lane = lax.broadcasted_iota(jnp.int32, (rows, CHUNK), 1)
    first = (lax.broadcasted_iota(jnp.int32, (rows, CHUNK), 0) & 7) < N_HEADS
    src = lax.broadcasted_iota(jnp.int32, (CHUNK, 2 * CHUNK), 0)
    dst = lax.broadcasted_iota(jnp.int32, (CHUNK, 2 * CHUNK), 1)
    for d in range(2):
        feeds = (src <= dst) if d == 0 else (src >= dst)
        cum_and_total = jnp.where((dst >= CHUNK) | feeds, 1.0, 0.0).astype(BF16)
        bc = sum(_dot(p, cum_and_total) for p in _split3(_log_sigmoid(gf_ref[d])))
        b, b_end = bc[:, :CHUNK], bc[:, CHUNK:]
        a = gi_ref[d] - b
        cm = a
        for k in range(7):
            sh = 1 << k
            if d == 0:
                cm = jnp.maximum(cm, jnp.where(lane >= sh, pltpu.roll(cm, sh, 1), MASKED))
            else:
                cm = jnp.maximum(cm, jnp.where(lane < CHUNK - sh, pltpu.roll(cm, CHUNK - sh, 1), MASKED))
        a_max = jnp.broadcast_to(jnp.max(a, axis=1, keepdims=True), a.shape)
        bend_scr[...] = b_end
        amax_scr[...] = a_max

        def scan(c, m):
            r = pl.ds(pl.multiple_of((c if d == 0 else n_chunks - 1 - c) * 8, 8), 8)
            mprev_scr[r, :] = m
            return bend_scr[r, :] + jnp.maximum(m, amax_scr[r, :])

        lax.fori_loop(0, n_chunks, scan, jnp.zeros((8, CHUNK), F32))
        m_prev = mprev_scr[...]
        u = -jnp.maximum(m_prev, cm)
        m_new = b_end + jnp.maximum(m_prev, a_max)
        ra_ref[d] = jnp.where(first, a, jnp.exp(b_end + a - m_new))
        rb_ref[d] = jnp.where(first, u, jnp.exp(m_prev + u))
        rc_ref[d] = jnp.where(first, jnp.exp(u - b), jnp.exp(b_end + m_prev - m_new))


def _gate_prep(gi, gf, bsz):
    rows = gi.shape[1] // bsz
    spec = pl.BlockSpec((2, rows, CHUNK), lambda b: (0, b, 0))
    shape = jax.ShapeDtypeStruct(gi.shape, F32)
    return pl.pallas_call(
        _gate_prep_kernel,
        grid=(bsz,),
        in_specs=[spec, spec],
        out_specs=[spec, spec, spec],
        out_shape=[shape, shape, shape],
        scratch_shapes=[pltpu.VMEM((rows, CHUNK), F32)] * 3,
        compiler_params=_vmem_params(("parallel",), 32),
        name="gate_prep",
    )(gi, gf)


@functools.lru_cache(maxsize=None)
def _column_selector():
    sel = np.zeros((48, N_HEADS * 3 * CHUNK), np.float32)
    for hd in range(N_HEADS):
        for j, (arr, row) in enumerate(((0, hd), (0, N_HEADS + hd), (1, hd))):
            for piece in range(3):
                sel[arr * 24 + piece * 8 + row, (3 * hd + j) * CHUNK:(3 * hd + j + 1) * CHUNK] = 1.0
    return sel


def _mlstm_kernel(sel_ref, qf_ref, ktf_ref, vf_ref, raf_ref, rbf_ref, rcf_ref,
                  qb_ref, ktb_ref, vb_ref, rab_ref, rbb_ref, rcb_ref, hf_ref, hb_ref, c_scr):
    @pl.when(pl.program_id(1) == 0)
    def _():
        c_scr[...] = jnp.zeros_like(c_scr)

    cpb = ktf_ref.shape[0]
    t_i = lax.broadcasted_iota(jnp.int32, (CHUNK, CHUNK), 0)
    s_i = lax.broadcasted_iota(jnp.int32, (CHUNK, CHUNK), 1)
    visible = (s_i <= t_i, s_i >= t_i)
    ones = jnp.ones((CHUNK, V_DIM), BF16)
    kzero = jnp.zeros((QK_DIM, CHUNK), BF16)
    dirs = ((qf_ref, ktf_ref, vf_ref, raf_ref, rbf_ref, rcf_ref, hf_ref),
            (qb_ref, ktb_ref, vb_ref, rab_ref, rbb_ref, rcb_ref, hb_ref))

    def body(i, carry):
        units = []
        for d, (q_ref, kt_ref, v_ref, ra_ref, rb_ref, rc_ref, h_ref) in enumerate(dirs):
            li = i if d == 0 else cpb - 1 - i
            r8 = pl.ds(pl.multiple_of(li * 8, 8), 8)
            rl = pl.ds(pl.multiple_of(li * CHUNK, CHUNK), CHUNK)
            ra = ra_ref[r8, :]
            rc = rc_ref[r8, :]
            pieces = [p.astype(F32) for arr in (rb_ref[r8, :], rc) for p in _split3(arr)]
            colb = lax.dot_general(jnp.concatenate(pieces, axis=0).astype(BF16), sel_ref[...],
                                   (((0,), (0,)), ((), ())), preferred_element_type=F32)
            kt = kt_ref[li]
            for pair in range(N_HEADS // 2):
                k_a = kt[2 * pair * QK_DIM:(2 * pair + 1) * QK_DIM, :]
                k_b = kt[(2 * pair + 1) * QK_DIM:(2 * pair + 2) * QK_DIM, :]
                kdiag = jnp.concatenate([jnp.concatenate([k_a, kzero], axis=1),
                                         jnp.concatenate([kzero, k_b], axis=1)], axis=0)
                qk2 = _dot(q_ref[rl, 2 * pair * QK_DIM:2 * (pair + 1) * QK_DIM], kdiag)
                for hd in (2 * pair, 2 * pair + 1):
                    qh = q_ref[rl, hd * QK_DIM:(hd + 1) * QK_DIM]
                    kth = kt[hd * QK_DIM:(hd + 1) * QK_DIM, :]
                    units.append(dict(d=d, hd=hd, rl=rl, h_ref=h_ref, ra=ra, rc=rc, colb=colb, qh=qh, kth=kth,
                                      v_ref=v_ref, qk=qk2[:, (hd % 2) * CHUNK:(hd % 2 + 1) * CHUNK]))
        for u in units:
            d, hd, colb = u["d"], u["hd"], u["colb"]
            c0 = 3 * hd * CHUNK
            u_b = colb[:, c0:c0 + CHUNK]
            w_inter = colb[:, c0 + CHUNK:c0 + CHUNK + QK_DIM]
            w_intra = jnp.exp(jnp.where(visible[d], u_b + u["ra"][hd:hd + 1, :], MASKED))
            s = (u["qk"] * w_intra).astype(BF16)
            q_inter = (u["qh"].astype(F32) * w_inter).astype(BF16)
            u["c_prev"] = c_scr[d, hd]
            u["v_aug"] = jnp.concatenate([u["v_ref"][u["rl"], hd * V_DIM:(hd + 1) * V_DIM], ones], axis=1)
            u["nd"] = _dot(jnp.concatenate([s, q_inter], axis=1),
                           jnp.concatenate([u["v_aug"], u["c_prev"].astype(BF16)], axis=0))
        for u in units:
            hd, nd = u["hd"], u["nd"]
            exp_neg_m = u["colb"][:, (3 * hd + 2) * CHUNK:(3 * hd + 3) * CHUNK]
            den = jnp.maximum(jnp.abs(nd[:, V_DIM:]), exp_neg_m)
            u["h_ref"][u["rl"], hd * V_DIM:(hd + 1) * V_DIM] = (nd[:, :V_DIM] / den).astype(u["h_ref"].dtype)
            wk_row = u["ra"][N_HEADS + hd:N_HEADS + hd + 1, :]
            decay = jnp.broadcast_to(u["rc"][N_HEADS + hd:N_HEADS + hd + 1, :], (QK_DIM, CHUNK))
            kw = (u["kth"].astype(F32) * wk_row).astype(BF16)
            c_scr[u["d"], hd] = jnp.concatenate([decay, decay], axis=1) * u["c_prev"] + _dot(kw, u["v_aug"])
        return carry

    lax.fori_loop(0, cpb, body, 0, unroll=2)


def _mlstm(q, kt, v, ra, rb, rc, bsz, seq, cpb):
    t = q.shape[0]
    nblk = seq // (cpb * CHUNK)
    qk = N_HEADS * QK_DIM
    vw = N_HEADS * V_DIM
    fwd = lambda b, j: b * nblk + j
    bwd = lambda b, j: b * nblk + nblk - 1 - j

    def specs(blk, d):
        return [pl.BlockSpec((cpb * CHUNK, qk), lambda b, j: (blk(b, j), 0)),
                pl.BlockSpec((cpb, qk, CHUNK), lambda b, j: (blk(b, j), 0, 0)),
                pl.BlockSpec((cpb * CHUNK, vw), lambda b, j: (blk(b, j), 0))] + \
               [pl.BlockSpec((None, cpb * 8, CHUNK), lambda b, j: (d, blk(b, j), 0))] * 3

    sel = jnp.asarray(_column_selector()).astype(BF16)
    return pl.pallas_call(
        _mlstm_kernel,
        grid=(bsz, nblk),
        in_specs=[pl.BlockSpec(sel.shape, lambda b, j: (0, 0))] + specs(fwd, 0) + specs(bwd, 1),
        out_specs=[pl.BlockSpec((cpb * CHUNK, vw), lambda b, j: (fwd(b, j), 0)),
                   pl.BlockSpec((cpb * CHUNK, vw), lambda b, j: (bwd(b, j), 0))],
        out_shape=[jax.ShapeDtypeStruct((t, vw), BF16)] * 2,
        scratch_shapes=[pltpu.VMEM((2, N_HEADS, QK_DIM, 2 * V_DIM), F32)],
        compiler_params=_vmem_params(("parallel", "arbitrary"), 32),
        name="mlstm",
    )(sel, q, kt, v, ra, rb, rc, q, kt, v, ra, rb, rc)


def _fft_pitch(n1):
    return 8 * ((n1 // 4) | 1)


@functools.lru_cache(maxsize=None)
def _fft_tables(seq):
    n1 = seq // CHUNK
    two_pi = 2.0 * np.pi
    k1 = np.arange(n1, dtype=np.int64)
    n = 128 * np.arange(n1, dtype=np.int64)[None, None, :] + np.arange(128, dtype=np.int64)[:, None, None]
    ang = two_pi * ((k1[None, :, None] * n) % seq).astype(np.float64) / seq
    ga = np.concatenate([np.cos(ang), -np.sin(ang)], axis=1)
    j = np.arange(128, dtype=np.int64)
    a128 = two_pi * ((j[:, None] * j[None, :]) % 128).astype(np.float64) / 128.0
    ff = np.concatenate([np.cos(a128), -np.sin(a128)], axis=0)
    cs = np.concatenate([np.cos(a128), np.sin(a128)], axis=0) / np.sqrt(128.0 * seq)
    return tuple(np.asarray(a, dtype=np.float32) for a in (ga, ff, cs))


def _fft_kernel(z_ref, ga_ref, ff_ref, cs_ref, y_ref, scr):
    j = pl.program_id(2)
    n1 = ga_ref.shape[2]
    pitch = scr.shape[0] // CHUNK
    for l in range(FFT_NB):
        zl = z_ref[pl.ds(l, n1, stride=FFT_NB), :].astype(BF16)
        row0 = pl.multiple_of((j * FFT_NB + l) * pitch, 8)
        scr[pl.ds(row0, 2 * n1), :] = _dot(ga_ref[j * FFT_NB + l], zl)

    @pl.when(j == pl.num_programs(2) - 1)
    def _():
        ff = ff_ref[...]
        cs = cs_ref[...]

        def body(it, carry):
            k0 = it * FFT_KB
            p = [scr[pl.ds(ri * n1 + k0 + kk, CHUNK, stride=pitch), :]
                 for kk in range(FFT_KB) for ri in range(2)]
            m = _dot(ff, jnp.concatenate(p, axis=1).astype(BF16))
            x = []
            for kk in range(FFT_KB):
                c0 = 2 * kk * GROUP_DIM
                xr = m[:CHUNK, c0:c0 + GROUP_DIM] - m[CHUNK:, c0 + GROUP_DIM:c0 + 2 * GROUP_DIM]
                xi = m[:CHUNK, c0 + GROUP_DIM:c0 + 2 * GROUP_DIM] + m[CHUNK:, c0:c0 + GROUP_DIM]
                x.append(jnp.concatenate([xr, xi], axis=1))
            y = _dot(jnp.concatenate(x, axis=0).astype(BF16), cs)
            for kk in range(FFT_KB):
                row0 = pl.multiple_of((k0 + kk) * FFT_NB, 8)
                for kb in range(CHUNK // FFT_NB):
                    r0 = kk * CHUNK + kb * FFT_NB
                    y_ref[kb, pl.ds(row0, FFT_NB), :] = y[r0:r0 + FFT_NB, :]
            return carry

        lax.fori_loop(0, n1 // FFT_KB, body, 0, unroll=2)


def _fourier(z5, seq):
    bsz = z5.shape[0]
    n1 = seq // CHUNK
    nblk = CHUNK // FFT_NB
    pitch = _fft_pitch(n1)
    ga, ff, cs = (jnp.asarray(a).astype(BF16) for a in _fft_tables(seq))
    return pl.pallas_call(
        _fft_kernel,
        grid=(bsz, N_GROUPS, nblk),
        in_specs=[pl.BlockSpec((None, None, None, n1 * FFT_NB, GROUP_DIM), lambda b, g, j: (b, g, j, 0, 0)),
                  pl.BlockSpec(ga.shape, lambda b, g, j: (0, 0, 0), pipeline_mode=pl.Buffered(1)),
                  pl.BlockSpec(ff.shape, lambda b, g, j: (0, 0)),
                  pl.BlockSpec(cs.shape, lambda b, g, j: (0, 0))],
        out_specs=pl.BlockSpec((None, None, nblk, n1 * FFT_NB, GROUP_DIM), lambda b, g, j: (b, g, 0, 0, 0)),
        out_shape=jax.ShapeDtypeStruct(z5.shape, F32),
        scratch_shapes=[pltpu.VMEM((CHUNK * pitch, GROUP_DIM), F32)],
        compiler_params=_vmem_params(("parallel", "parallel", "arbitrary"), 48),
        name="fft",
    )(z5, ga, ff, cs)


def _gather_rows(y_refs, k2_off, n1, kpt):
    rows = [jnp.concatenate([ref[pl.ds(k2_off + kk, n1, stride=FFT_NB), :] for ref in y_refs], axis=1)
            for kk in range(kpt)]
    return jnp.concatenate(rows, axis=0)


def _outmlp_kernel(alpha, ff_chunk, spb, x_ref, hf_ref, hb_ref, o_ref, y0_ref, y1_ref, y2_ref, y3_ref,
                   g1_ref, sc2_ref, sh2_ref, g2_ref,
                   nw_ref, wout_ref, l1g_ref, l1b_ref, w1_ref, b1_ref, w2_ref, b2_ref, l2g_ref, l2b_ref,
                   out_ref):
    n1 = y0_ref.shape[0] // FFT_NB
    kpt = x_ref.shape[0] // n1
    k2_off = ((pl.program_id(0) % spb) * kpt) % FFT_NB
    kps = max(1, kpt // OUTMLP_SUBTILES)
    subs = [dict(k2l=k2l, rs=slice(k2l * n1, (k2l + kps) * n1)) for k2l in range(0, kpt, kps)]
    for s in subs:
        rs = s["rs"]
        yf = _gather_rows((y0_ref, y1_ref, y2_ref, y3_ref), k2_off + s["k2l"], n1, kps).astype(BF16)
        hsum = hf_ref[rs, :].astype(F32) + hb_ref[rs, :].astype(F32)
        heads = [_ln_plain(hsum[:, j * V_DIM:(j + 1) * V_DIM]) for j in range(N_HEADS)]
        ym = jnp.concatenate(heads, axis=1) * nw_ref[...] * jax.nn.sigmoid(o_ref[rs, :].astype(F32))
        s["mix"] = _dot(jnp.concatenate([ym.astype(BF16), yf], axis=1), wout_ref[...])
    for s in subs:
        x1 = _ln_plain(alpha * x_ref[s["rs"], :] + (1.0 + g1_ref[...]) * s["mix"]) * l1g_ref[...] + l1b_ref[...]
        s["x1"] = x1
        s["h2"] = (_ln_plain(x1) * (1.0 + sc2_ref[...]) + sh2_ref[...]).astype(BF16)
    for s in subs:
        ff = b2_ref[...]
        for j in range(w1_ref.shape[1] // ff_chunk):
            sl = slice(j * ff_chunk, (j + 1) * ff_chunk)
            hid = jnp.maximum(_dot(s["h2"], w1_ref[:, sl]) + b1_ref[:, sl], 0.0)
            ff = ff + _dot((hid * hid).astype(BF16), w2_ref[sl, :])
        s["ff"] = ff
    for s in subs:
        out_ref[s["rs"], :] = (_ln_plain(alpha * s["x1"] + (1.0 + g2_ref[...]) * s["ff"]) * l2g_ref[...]
                               + l2b_ref[...])


def _outmlp(x2d, hf, hb, o, y5, mod4, w, seq, tm, alpha, ff_chunk):
    t, d = x2d.shape
    spb = seq // tm
    n1 = seq // CHUNK
    kpt = tm // n1
    assert FFT_NB % kpt == 0
    vw = N_HEADS * V_DIM
    row = lambda width: pl.BlockSpec((tm, width), lambda i: (i, 0))
    yspec = lambda g: pl.BlockSpec((None, None, None, n1 * FFT_NB, GROUP_DIM),
                                   lambda i: (i // spb, g, ((i % spb) * kpt) // FFT_NB, 0, 0))
    modspec = lambda j: pl.BlockSpec((None, None, 1, d), lambda i: (i // spb, j, 0, 0))
    const = lambda a: pl.BlockSpec(a.shape, lambda i: (0,) * a.ndim, pipeline_mode=pl.Buffered(1))
    names = ["nw", "wout", "l1g", "l1b", "w1", "b1", "w2", "b2", "l2g", "l2b"]
    return pl.pallas_call(
        functools.partial(_outmlp_kernel, alpha, ff_chunk, spb),
        grid=(t // tm,),
        in_specs=[row(d), row(vw), row(vw), row(vw)]
                 + [yspec(g) for g in range(N_GROUPS)]
                 + [modspec(2), modspec(4), modspec(3), modspec(5)] + [const(w[k]) for k in names],
        out_specs=row(d),
        out_shape=jax.ShapeDtypeStruct((t, d), F32),
        compiler_params=_vmem_params(("parallel",), 56),
        name="outmlp",
    )(x2d, hf, hb, o, y5, y5, y5, y5, mod4, mod4, mod4, mod4, *[w[k] for k in names])


def _layer_weights(w_in, b_gate, w_out, w_ff1, w_ff2, b_ff1, b_ff2, mlstm_norm_w, ln1_g, ln1_b, ln2_g, ln2_b):
    qk = N_HEADS * QK_DIM
    vw = N_HEADS * V_DIM
    fw = N_GROUPS * GROUP_DIM
    o0, o1, o2, o3, o4 = qk, 2 * qk, 2 * qk + vw, 2 * qk + 2 * vw, 2 * qk + 2 * vw + fw
    wg = w_in[:, o4:].T.reshape(2, 2, N_HEADS, -1)
    bg = b_gate.astype(F32).reshape(2, 2, N_HEADS, 1)
    dup = lambda a: jnp.concatenate([a, a], axis=1)
    r = lambda a: a.astype(F32).reshape(1, -1)
    wq = w_in[:, :o0] * (QK_DIM ** -0.5)
    gate_rows = lambda a: jnp.concatenate([dup(a[:, 0]), dup(a[:, 1])], axis=1).reshape(32, -1)
    return {
        "wrow": jnp.concatenate([wq, w_in[:, o1:o4]], axis=1).astype(BF16),
        "wlane": jnp.concatenate([w_in[:, o0:o1].T, gate_rows(wg)], axis=0).astype(BF16),
        "bg": gate_rows(bg),
        "nw": r(mlstm_norm_w), "wout": w_out.astype(BF16), "l1g": r(ln1_g), "l1b": r(ln1_b),
        "w1": w_ff1.astype(BF16), "b1": r(b_ff1), "w2": w_ff2.astype(BF16), "b2": r(b_ff2),
        "l2g": r(ln2_g), "l2b": r(ln2_b),
    }


def kernel(x, c, w_ada, b_ada, w_in, b_gate, mlstm_norm_w, w_out, ln1_g, ln1_b,
           w_ff1, b_ff1, w_ff2, b_ff2, ln2_g, ln2_b):
    bsz, seq, d = x.shape
    depth = w_ada.shape[0]
    alpha = (2 * depth) ** 0.25
    assert seq % (MLSTM_CPB * CHUNK) == 0 and (seq // CHUNK) % 8 == 0
    assert d == N_HEADS * V_DIM + N_GROUPS * GROUP_DIM
    tm = min(512, FFT_NB * (seq // CHUNK))
    x2d = x.reshape(bsz * seq, d)
    for l in range(depth):
        w = _layer_weights(w_in[l], b_gate[l], w_out[l], w_ff1[l], w_ff2[l], b_ff1[l], b_ff2[l],
                           mlstm_norm_w[l], ln1_g[l], ln1_b[l], ln2_g[l], ln2_b[l])
        mod4 = _adaln(c, w_ada[l], b_ada[l]).reshape(bsz, N_MOD, 1, d)
        q, kt, v, o, z5, gi, gf = _inproj(x2d, mod4, w, seq, tm)
        ra, rb, rc = _gate_prep(gi, gf, bsz)
        hf, hb = _mlstm(q, kt, v, ra, rb, rc, bsz, seq, MLSTM_CPB)
        y5 = _fourier(z5, seq)
        x2d = _outmlp(x2d, hf, hb, o, y5, mod4, w, seq, tm, alpha, ff_chunk=1024)
    return x2d.reshape(bsz, seq, d)
```

```python
import functools

import numpy as np
import jax
import jax.numpy as jnp
from jax import lax
from jax.experimental import pallas as pl
from jax.experimental.pallas import tpu as pltpu

F32 = jnp.float32
BF16 = jnp.bfloat16

CHUNK = 128
N_HEADS = 4
QK_DIM = 64
V_DIM = 128
N_GROUPS = 4
GROUP_DIM = 128
N_MOD = 6
LN_EPS = 1e-5
MASKED = -1e30
FFT_NB = 8
FFT_KB = 4
FFT_SB = 4
MLSTM_CPB = 8
OUTMLP_SUBTILES = 2
INPROJ_SUBTILES = 4

_NT = (((1,), (1,)), ((), ()))


def _dot(a, b):
    return jnp.dot(a, b, preferred_element_type=F32)


def _dot_nt(a, b):
    return lax.dot_general(a, b, _NT, preferred_element_type=F32)


def _ln_plain(x):
    mu = jnp.mean(x, axis=-1, keepdims=True)
    xc = x - mu
    var = jnp.mean(xc * xc, axis=-1, keepdims=True)
    return xc * lax.rsqrt(var + LN_EPS)


def _log_sigmoid(x):
    return jnp.minimum(x, 0.0) - jnp.log1p(jnp.exp(-jnp.abs(x)))


def _split3(x):
    hi = x.astype(BF16)
    r1 = x - hi.astype(F32)
    mid = r1.astype(BF16)
    lo = (r1 - mid.astype(F32)).astype(BF16)
    return hi, mid, lo


def _vmem_params(semantics, limit_mb):
    return pltpu.CompilerParams(dimension_semantics=semantics, vmem_limit_bytes=limit_mb * 1024 * 1024)


def _adaln_kernel(ct_ref, w_ref, b_ref, o_ref):
    ct = ct_ref[...]
    act = ct * jax.nn.sigmoid(ct)
    w = w_ref[...]
    for b in range(ct.shape[1]):
        o_ref[b:b + 1, :] = jnp.sum(act[:, b:b + 1] * w, axis=0, keepdims=True) + b_ref[...]


def _adaln(c, w_ada, b_ada):
    bsz, d = c.shape
    n = w_ada.shape[1]
    tn = 1024
    return pl.pallas_call(
        _adaln_kernel,
        grid=(n // tn,),
        in_specs=[pl.BlockSpec((d, bsz), lambda j: (0, 0)),
                  pl.BlockSpec((d, tn), lambda j: (0, j)),
                  pl.BlockSpec((1, tn), lambda j: (0, j))],
        out_specs=pl.BlockSpec((bsz, tn), lambda j: (0, j)),
        out_shape=jax.ShapeDtypeStruct((bsz, n), F32),
        compiler_params=_vmem_params(("parallel",), 32),
        name="adaln",
    )(c.T, w_ada, b_ada.reshape(1, n))


def _inproj_kernel(x_ref, sh_ref, sc_ref, wrow_ref, wlane_ref, bg_ref,
                   q_ref, kt_ref, v_ref, o_ref, z_ref, gi_ref, gf_ref):
    qk, vw = q_ref.shape[1], v_ref.shape[1]
    n_chunks = x_ref.shape[0] // CHUNK
    cps = max(1, n_chunks // INPROJ_SUBTILES)
    for c0 in range(0, n_chunks, cps):
        rs = slice(c0 * CHUNK, (c0 + cps) * CHUNK)
        h = _ln_plain(x_ref[rs, :]) * (1.0 + sc_ref[...]) + sh_ref[...]
        hb = h.astype(BF16)
        rowp = _dot(hb, wrow_ref[...])
        q_ref[rs, :] = rowp[:, :qk].astype(BF16)
        v_ref[rs, :] = rowp[:, qk:qk + vw].astype(BF16)
        o_ref[rs, :] = rowp[:, qk + vw:qk + 2 * vw].astype(BF16)
        fz = rowp[:, qk + 2 * vw:]
        for j in range(cps):
            for g in range(N_GROUPS):
                for jb in range(CHUNK // FFT_NB):
                    r0 = j * CHUNK + jb * FFT_NB
                    z_ref[g, jb, (c0 + j) * FFT_NB:(c0 + j + 1) * FFT_NB, :] = (
                        fz[r0:r0 + FFT_NB, g * GROUP_DIM:(g + 1) * GROUP_DIM])
        lanep = _dot_nt(wlane_ref[...], hb)
        gates = lanep[qk:, :] + bg_ref[...]
        for j in range(cps):
            cl = slice(j * CHUNK, (j + 1) * CHUNK)
            kt_ref[c0 + j] = lanep[:qk, cl].astype(BF16)
            for d in range(2):
                gi_ref[d, (c0 + j) * 8:(c0 + j + 1) * 8, :] = gates[16 * d:16 * d + 8, cl]
                gf_ref[d, (c0 + j) * 8:(c0 + j + 1) * 8, :] = gates[16 * d + 8:16 * d + 16, cl]


def _inproj(x2d, mod4, w, seq, tm):
    t, d = x2d.shape
    spb = seq // tm
    qk = N_HEADS * QK_DIM
    vw = N_HEADS * V_DIM
    nblk = CHUNK // FFT_NB
    cpt = tm // CHUNK
    full = lambda a: pl.BlockSpec(a.shape, lambda i: (0,) * a.ndim)
    modspec = lambda j: pl.BlockSpec((None, None, 1, d), lambda i: (i // spb, j, 0, 0))
    names = ["wrow", "wlane", "bg"]
    return pl.pallas_call(
        _inproj_kernel,
        grid=(t // tm,),
        in_specs=[pl.BlockSpec((tm, d), lambda i: (i, 0)), modspec(0), modspec(1)] + [full(w[k]) for k in names],
        out_specs=[pl.BlockSpec((tm, qk), lambda i: (i, 0)),
                   pl.BlockSpec((cpt, qk, CHUNK), lambda i: (i, 0, 0)),
                   pl.BlockSpec((tm, vw), lambda i: (i, 0)),
                   pl.BlockSpec((tm, vw), lambda i: (i, 0)),
                   pl.BlockSpec((None, N_GROUPS, nblk, cpt * FFT_NB, GROUP_DIM),
                                lambda i: (i // spb, 0, 0, i % spb, 0)),
                   pl.BlockSpec((2, cpt * 8, CHUNK), lambda i: (0, i, 0)),
                   pl.BlockSpec((2, cpt * 8, CHUNK), lambda i: (0, i, 0))],
        out_shape=[jax.ShapeDtypeStruct((t, qk), BF16),
                   jax.ShapeDtypeStruct((t // CHUNK, qk, CHUNK), BF16),
                   jax.ShapeDtypeStruct((t, vw), BF16),
                   jax.ShapeDtypeStruct((t, vw), BF16),
                   jax.ShapeDtypeStruct((t // seq, N_GROUPS, nblk, (seq // CHUNK) * FFT_NB, GROUP_DIM), F32),
                   jax.ShapeDtypeStruct((2, t // CHUNK * 8, CHUNK), F32),
                   jax.ShapeDtypeStruct((2, t // CHUNK * 8, CHUNK), F32)],
        compiler_params=_vmem_params(("parallel",), 48),
        name="inproj",
    )(x2d, mod4, mod4, *[w[k] for k in names])


def _gate_prep_kernel(gi_ref, gf_ref, ra_ref, rb_ref, rc_ref, bend_scr, amax_scr, mprev_scr):
    rows = gi_ref.shape[1]
    n_chunks = rows // 8
    lane = lax.broadcasted_iota(jnp.int32, (rows, CHUNK), 1)
    first = (lax.broadcasted_iota(jnp.int32, (rows, CHUNK), 0) & 7) < N_HEADS
    src = lax.broadcasted_iota(jnp.int32, (CHUNK, 2 * CHUNK), 0)
    dst = lax.broadcasted_iota(jnp.int32, (CHUNK, 2 * CHUNK), 1)
    for d in range(2):
        feeds = (src <= dst) if d == 0 else (src >= dst)
        cum_and_total = jnp.where((dst >= CHUNK) | feeds, 1.0, 0.0).astype(BF16)
        bc = sum(_dot(p, cum_and_total) for p in _split3(_log_sigmoid(gf_ref[d])))
        b, b_end = bc[:, :CHUNK], bc[:, CHUNK:]
        a = gi_ref[d] - b
        cm = a
        for k in range(7):
            sh = 1 << k
            if d == 0:
                cm = jnp.maximum(cm, jnp.where(lane >= sh, pltpu.roll(cm, sh, 1), MASKED))
            else:
                cm = jnp.maximum(cm, jnp.where(lane < CHUNK - sh, pltpu.roll(cm, CHUNK - sh, 1), MASKED))
        a_max = jnp.broadcast_to(jnp.max(a, axis=1, keepdims=True), a.shape)
        bend_scr[...] = b_end
        amax_scr[...] = a_max

        def scan(c, m):
            r = pl.ds(pl.multiple_of((c if d == 0 else n_chunks - 1 - c) * 8, 8), 8)
            mprev_scr[r, :] = m
            return bend_scr[r, :] + jnp.maximum(m, amax_scr[r, :])

        lax.fori_loop(0, n_chunks, scan, jnp.zeros((8, CHUNK), F32))
        m_prev = mprev_scr[...]
        u = -jnp.maximum(m_prev, cm)
        m_new = b_end + jnp.maximum(m_prev, a_max)
        ra_ref[d] = jnp.where(first, a, jnp.exp(b_end + a - m_new))
        rb_ref[d] = jnp.where(first, u, jnp.exp(m_prev + u))
        rc_ref[d] = jnp.where(first, jnp.exp(u - b), jnp.exp(b_end + m_prev - m_new))


def _gate_prep(gi, gf, bsz):
    rows = gi.shape[1] // bsz
    spec = pl.BlockSpec((2, rows, CHUNK), lambda b: (0, b, 0))
    shape = jax.ShapeDtypeStruct(gi.shape, F32)
    return pl.pallas_call(
        _gate_prep_kernel,
        grid=(bsz,),
        in_specs=[spec, spec],
        out_specs=[spec, spec, spec],
        out_shape=[shape, shape, shape],
        scratch_shapes=[pltpu.VMEM((rows, CHUNK), F32)] * 3,
        compiler_params=_vmem_params(("parallel",), 32),
        name="gate_prep",
    )(gi, gf)


@functools.lru_cache(maxsize=None)
def _column_selector():
    sel = np.zeros((48, N_HEADS * 3 * CHUNK), np.float32)
    for hd in range(N_HEADS):
        for j, (arr, row) in enumerate(((0, hd), (0, N_HEADS + hd), (1, hd))):
            for piece in range(3):
                sel[arr * 24 + piece * 8 + row, (3 * hd + j) * CHUNK:(3 * hd + j + 1) * CHUNK] = 1.0
    return sel


def _mlstm_kernel(sel_ref, qf_ref, ktf_ref, vf_ref, raf_ref, rbf_ref, rcf_ref,
                  qb_ref, ktb_ref, vb_ref, rab_ref, rbb_ref, rcb_ref, hf_ref, hb_ref, c_scr):
    @pl.when(pl.program_id(1) == 0)
    def _():
        c_scr[...] = jnp.zeros_like(c_scr)

    cpb = ktf_ref.shape[0]
    t_i = lax.broadcasted_iota(jnp.int32, (CHUNK, CHUNK), 0)
    s_i = lax.broadcasted_iota(jnp.int32, (CHUNK, CHUNK), 1)
    visible = (s_i <= t_i, s_i >= t_i)
    ones = jnp.ones((CHUNK, V_DIM), BF16)
    kzero = jnp.zeros((QK_DIM, CHUNK), BF16)
    dirs = ((qf_ref, ktf_ref, vf_ref, raf_ref, rbf_ref, rcf_ref, hf_ref),
            (qb_ref, ktb_ref, vb_ref, rab_ref, rbb_ref, rcb_ref, hb_ref))

    def body(i, carry):
        units = []
        for d, (q_ref, kt_ref, v_ref, ra_ref, rb_ref, rc_ref, h_ref) in enumerate(dirs):
            li = i if d == 0 else cpb - 1 - i
            r8 = pl.ds(pl.multiple_of(li * 8, 8), 8)
            rl = pl.ds(pl.multiple_of(li * CHUNK, CHUNK), CHUNK)
            ra = ra_ref[r8, :]
            rc = rc_ref[r8, :]
            pieces = [p.astype(F32) for arr in (rb_ref[r8, :], rc) for p in _split3(arr)]
            colb = lax.dot_general(jnp.concatenate(pieces, axis=0).astype(BF16), sel_ref[...],
                                   (((0,), (0,)), ((), ())), preferred_element_type=F32)
            kt = kt_ref[li]
            for pair in range(N_HEADS // 2):
                k_a = kt[2 * pair * QK_DIM:(2 * pair + 1) * QK_DIM, :]
                k_b = kt[(2 * pair + 1) * QK_DIM:(2 * pair + 2) * QK_DIM, :]
                kdiag = jnp.concatenate([jnp.concatenate([k_a, kzero], axis=1),
                                         jnp.concatenate([kzero, k_b], axis=1)], axis=0)
                qk2 = _dot(q_ref[rl, 2 * pair * QK_DIM:2 * (pair + 1) * QK_DIM], kdiag)
                for hd in (2 * pair, 2 * pair + 1):
                    qh = q_ref[rl, hd * QK_DIM:(hd + 1) * QK_DIM]
                    kth = kt[hd * QK_DIM:(hd + 1) * QK_DIM, :]
                    units.append(dict(d=d, hd=hd, rl=rl, h_ref=h_ref, ra=ra, rc=rc, colb=colb, qh=qh, kth=kth,
                                      v_ref=v_ref, qk=qk2[:, (hd % 2) * CHUNK:(hd % 2 + 1) * CHUNK]))
        for u in units:
            d, hd, colb = u["d"], u["hd"], u["colb"]
            c0 = 3 * hd * CHUNK
            u_b = colb[:, c0:c0 + CHUNK]
            w_inter = colb[:, c0 + CHUNK:c0 + CHUNK + QK_DIM]
            w_intra = jnp.exp(jnp.where(visible[d], u_b + u["ra"][hd:hd + 1, :], MASKED))
            s = (u["qk"] * w_intra).astype(BF16)
            q_inter = (u["qh"].astype(F32) * w_inter).astype(BF16)
            u["c_prev"] = c_scr[d, hd]
            u["v_aug"] = jnp.concatenate([u["v_ref"][u["rl"], hd * V_DIM:(hd + 1) * V_DIM], ones], axis=1)
            u["nd"] = _dot(jnp.concatenate([s, q_inter], axis=1),
                           jnp.concatenate([u["v_aug"], u["c_prev"].astype(BF16)], axis=0))
        for u in units:
            hd, nd = u["hd"], u["nd"]
            exp_neg_m = u["colb"][:, (3 * hd + 2) * CHUNK:(3 * hd + 3) * CHUNK]
            den = jnp.maximum(jnp.abs(nd[:, V_DIM:]), exp_neg_m)
            u["h_ref"][u["rl"], hd * V_DIM:(hd + 1) * V_DIM] = (nd[:, :V_DIM] / den).astype(u["h_ref"].dtype)
            wk_row = u["ra"][N_HEADS + hd:N_HEADS + hd + 1, :]
            decay = jnp.broadcast_to(u["rc"][N_HEADS + hd:N_HEADS + hd + 1, :], (QK_DIM, CHUNK))
            kw = (u["kth"].astype(F32) * wk_row).astype(BF16)
            c_scr[u["d"], hd] = jnp.concatenate([decay, decay], axis=1) * u["c_prev"] + _dot(kw, u["v_aug"])
        return carry

    lax.fori_loop(0, cpb, body, 0, unroll=2)


def _mlstm(q, kt, v, ra, rb, rc, bsz, seq, cpb):
    t = q.shape[0]
    nblk = seq // (cpb * CHUNK)
    qk = N_HEADS * QK_DIM
    vw = N_HEADS * V_DIM
    fwd = lambda b, j: b * nblk + j
    bwd = lambda b, j: b * nblk + nblk - 1 - j

    def specs(blk, d):
        return [pl.BlockSpec((cpb * CHUNK, qk), lambda b, j: (blk(b, j), 0)),
                pl.BlockSpec((cpb, qk, CHUNK), lambda b, j: (blk(b, j), 0, 0)),
                pl.BlockSpec((cpb * CHUNK, vw), lambda b, j: (blk(b, j), 0))] + \
               [pl.BlockSpec((None, cpb * 8, CHUNK), lambda b, j: (d, blk(b, j), 0))] * 3

    sel = jnp.asarray(_column_selector()).astype(BF16)
    return pl.pallas_call(
        _mlstm_kernel,
        grid=(bsz, nblk),
        in_specs=[pl.BlockSpec(sel.shape, lambda b, j: (0, 0))] + specs(fwd, 0) + specs(bwd, 1),
        out_specs=[pl.BlockSpec((cpb * CHUNK, vw), lambda b, j: (fwd(b, j), 0)),
                   pl.BlockSpec((cpb * CHUNK, vw), lambda b, j: (bwd(b, j), 0))],
        out_shape=[jax.ShapeDtypeStruct((t, vw), BF16)] * 2,
        scratch_shapes=[pltpu.VMEM((2, N_HEADS, QK_DIM, 2 * V_DIM), F32)],
        compiler_params=_vmem_params(("parallel", "arbitrary"), 32),
        name="mlstm",
    )(sel, q, kt, v, ra, rb, rc, q, kt, v, ra, rb, rc)


def _fft_pitch(n1):
    return 8 * ((n1 // 4) | 1)


@functools.lru_cache(maxsize=None)
def _fft_tables(seq):
    n1 = seq // CHUNK
    two_pi = 2.0 * np.pi
    k1 = np.arange(n1, dtype=np.int64)
    n = 128 * np.arange(n1, dtype=np.int64)[None, None, :] + np.arange(128, dtype=np.int64)[:, None, None]
    ang = two_pi * ((k1[None, :, None] * n) % seq).astype(np.float64) / seq
    ga = np.concatenate([np.cos(ang), -np.sin(ang)], axis=1)
    j = np.arange(128, dtype=np.int64)
    a128 = two_pi * ((j[:, None] * j[None, :]) % 128).astype(np.float64) / 128.0
    ff = np.concatenate([np.cos(a128), -np.sin(a128)], axis=0)
    cs = np.concatenate([np.cos(a128), np.sin(a128)], axis=0) / np.sqrt(128.0 * seq)
    return tuple(np.asarray(a, dtype=np.float32) for a in (ga, ff, cs))


def _fft_kernel(z_ref, ga_ref, ff_ref, cs_ref, y_ref, scr):
    j = pl.program_id(2)
    n1 = ga_ref.shape[2]
    pitch = scr.shape[0] // CHUNK
    for sb in range(z_ref.shape[0]):
        for l in range(FFT_NB):
            n2 = (j * z_ref.shape[0] + sb) * FFT_NB + l
            zl = z_ref[sb, pl.ds(l, n1, stride=FFT_NB), :].astype(BF16)
            row0 = pl.multiple_of(n2 * pitch, 8)
            scr[pl.ds(row0, 2 * n1), :] = _dot(ga_ref[n2], zl)

    @pl.when(j == pl.num_programs(2) - 1)
    def _():
        ff = ff_ref[...]
        cs = cs_ref[...]

        def body(it, carry):
            k0 = it * FFT_KB
            p = [scr[pl.ds(ri * n1 + k0 + kk, CHUNK, stride=pitch), :]
                 for kk in range(FFT_KB) for ri in range(2)]
            m = _dot(ff, jnp.concatenate(p, axis=1).astype(BF16))
            x = []
            for kk in range(FFT_KB):
                c0 = 2 * kk * GROUP_DIM
                xr = m[:CHUNK, c0:c0 + GROUP_DIM] - m[CHUNK:, c0 + GROUP_DIM:c0 + 2 * GROUP_DIM]
                xi = m[:CHUNK, c0 + GROUP_DIM:c0 + 2 * GROUP_DIM] + m[CHUNK:, c0:c0 + GROUP_DIM]
                x.append(jnp.concatenate([xr, xi], axis=1))
            y = _dot(jnp.concatenate(x, axis=0).astype(BF16), cs)
            for kk in range(FFT_KB):
                row0 = pl.multiple_of((k0 + kk) * FFT_NB, 8)
                for kb in range(CHUNK // FFT_NB):
                    r0 = kk * CHUNK + kb * FFT_NB
                    y_ref[kb, pl.ds(row0, FFT_NB), :] = y[r0:r0 + FFT_NB, :]
            return carry

        lax.fori_loop(0, n1 // FFT_KB, body, 0, unroll=2)


def _fourier(z5, seq):
    bsz = z5.shape[0]
    n1 = seq // CHUNK
    nblk = CHUNK // FFT_NB
    pitch = _fft_pitch(n1)
    ga, ff, cs = (jnp.asarray(a).astype(BF16) for a in _fft_tables(seq))
    return pl.pallas_call(
        _fft_kernel,
        grid=(bsz, N_GROUPS, nblk // FFT_SB),
        in_specs=[pl.BlockSpec((None, None, FFT_SB, n1 * FFT_NB, GROUP_DIM), lambda b, g, j: (b, g, j, 0, 0)),
                  pl.BlockSpec(ga.shape, lambda b, g, j: (0, 0, 0), pipeline_mode=pl.Buffered(1)),
                  pl.BlockSpec(ff.shape, lambda b, g, j: (0, 0)),
                  pl.BlockSpec(cs.shape, lambda b, g, j: (0, 0))],
        out_specs=pl.BlockSpec((None, None, nblk, n1 * FFT_NB, GROUP_DIM), lambda b, g, j: (b, g, 0, 0, 0)),
        out_shape=jax.ShapeDtypeStruct(z5.shape, F32),
        scratch_shapes=[pltpu.VMEM((CHUNK * pitch, GROUP_DIM), F32)],
        compiler_params=_vmem_params(("parallel", "parallel", "arbitrary"), 48),
        name="fft",
    )(z5, ga, ff, cs)


def _gather_rows(y_refs, k2_off, n1, kpt):
    rows = [jnp.concatenate([ref[pl.ds(k2_off + kk, n1, stride=FFT_NB), :] for ref in y_refs], axis=1)
            for kk in range(kpt)]
    return jnp.concatenate(rows, axis=0)


def _outmlp_kernel(alpha, ff_chunk, spb, x_ref, hf_ref, hb_ref, o_ref, y0_ref, y1_ref, y2_ref, y3_ref,
                   g1_ref, sc2_ref, sh2_ref, g2_ref,
                   nw_ref, wout_ref, l1g_ref, l1b_ref, w1_ref, b1_ref, w2_ref, b2_ref, l2g_ref, l2b_ref,
                   out_ref):
    n1 = y0_ref.shape[0] // FFT_NB
    kpt = x_ref.shape[0] // n1
    k2_off = ((pl.program_id(0) % spb) * kpt) % FFT_NB
    kps = max(1, kpt // OUTMLP_SUBTILES)
    subs = [dict(k2l=k2l, rs=slice(k2l * n1, (k2l + kps) * n1)) for k2l in range(0, kpt, kps)]
    for s in subs:
        rs = s["rs"]
        yf = _gather_rows((y0_ref, y1_ref, y2_ref, y3_ref), k2_off + s["k2l"], n1, kps).astype(BF16)
        hsum = hf_ref[rs, :].astype(F32) + hb_ref[rs, :].astype(F32)
        heads = [_ln_plain(hsum[:, j * V_DIM:(j + 1) * V_DIM]) for j in range(N_HEADS)]
        ym = jnp.concatenate(heads, axis=1) * nw_ref[...] * jax.nn.sigmoid(o_ref[rs, :].astype(F32))
        s["mix"] = _dot(jnp.concatenate([ym.astype(BF16), yf], axis=1), wout_ref[...])
    for s in subs:
        x1 = _ln_plain(alpha * x_ref[s["rs"], :] + (1.0 + g1_ref[...]) * s["mix"]) * l1g_ref[...] + l1b_ref[...]
        s["x1"] = x1
        s["h2"] = (_ln_plain(x1) * (1.0 + sc2_ref[...]) + sh2_ref[...]).astype(BF16)
    for s in subs:
        ff = b2_ref[...]
        for j in range(w1_ref.shape[1] // ff_chunk):
            sl = slice(j * ff_chunk, (j + 1) * ff_chunk)
            hid = jnp.maximum(_dot(s["h2"], w1_ref[:, sl]) + b1_ref[:, sl], 0.0)
            ff = ff + _dot((hid * hid).astype(BF16), w2_ref[sl, :])
        s["ff"] = ff
    for s in subs:
        out_ref[s["rs"], :] = (_ln_plain(alpha * s["x1"] + (1.0 + g2_ref[...]) * s["ff"]) * l2g_ref[...]
                               + l2b_ref[...])


def _outmlp(x2d, hf, hb, o, y5, mod4, w, seq, tm, alpha, ff_chunk):
    t, d = x2d.shape
    spb = seq // tm
    n1 = seq // CHUNK
    kpt = tm // n1
    assert FFT_NB % kpt == 0
    vw = N_HEADS * V_DIM
    row = lambda width: pl.BlockSpec((tm, width), lambda i: (i, 0))
    yspec = lambda g: pl.BlockSpec((None, None, None, n1 * FFT_NB, GROUP_DIM),
                                   lambda i: (i // spb, g, ((i % spb) * kpt) // FFT_NB, 0, 0))
    modspec = lambda j: pl.BlockSpec((None, None, 1, d), lambda i: (i // spb, j, 0, 0))
    const = lambda a: pl.BlockSpec(a.shape, lambda i: (0,) * a.ndim, pipeline_mode=pl.Buffered(1))
    names = ["nw", "wout", "l1g", "l1b", "w1", "b1", "w2", "b2", "l2g", "l2b"]
    return pl.pallas_call(
        functools.partial(_outmlp_kernel, alpha, ff_chunk, spb),
        grid=(t // tm,),
        in_specs=[row(d), row(vw), row(vw), row(vw)]
                 + [yspec(g) for g in range(N_GROUPS)]
                 + [modspec(2), modspec(4), modspec(3), modspec(5)] + [const(w[k]) for k in names],
        out_specs=row(d),
        out_shape=jax.ShapeDtypeStruct((t, d), F32),
        compiler_params=_vmem_params(("parallel",), 56),
        name="outmlp",
    )(x2d, hf, hb, o, y5, y5, y5, y5, mod4, mod4, mod4, mod4, *[w[k] for k in names])


def _layer_weights(w_in, b_gate, w_out, w_ff1, w_ff2, b_ff1, b_ff2, mlstm_norm_w, ln1_g, ln1_b, ln2_g, ln2_b):
    qk = N_HEADS * QK_DIM
    vw = N_HEADS * V_DIM
    fw = N_GROUPS * GROUP_DIM
    o0, o1, o2, o3, o4 = qk, 2 * qk, 2 * qk + vw, 2 * qk + 2 * vw, 2 * qk + 2 * vw + fw
    wg = w_in[:, o4:].T.reshape(2, 2, N_HEADS, -1)
    bg = b_gate.astype(F32).reshape(2, 2, N_HEADS, 1)
    dup = lambda a: jnp.concatenate([a, a], axis=1)
    r = lambda a: a.astype(F32).reshape(1, -1)
    wq = w_in[:, :o0] * (QK_DIM ** -0.5)
    gate_rows = lambda a: jnp.concatenate([dup(a[:, 0]), dup(a[:, 1])], axis=1).reshape(32, -1)
    return {
        "wrow": jnp.concatenate([wq, w_in[:, o1:o4]], axis=1).astype(BF16),
        "wlane": jnp.concatenate([w_in[:, o0:o1].T, gate_rows(wg)], axis=0).astype(BF16),
        "bg": gate_rows(bg),
        "nw": r(mlstm_norm_w), "wout": w_out.astype(BF16), "l1g": r(ln1_g), "l1b": r(ln1_b),
        "w1": w_ff1.astype(BF16), "b1": r(b_ff1), "w2": w_ff2.astype(BF16), "b2": r(b_ff2),
        "l2g": r(ln2_g), "l2b": r(ln2_b),
    }


def kernel(x, c, w_ada, b_ada, w_in, b_gate, mlstm_norm_w, w_out, ln1_g, ln1_b,
           w_ff1, b_ff1, w_ff2, b_ff2, ln2_g, ln2_b):
    bsz, seq, d = x.shape
    depth = w_ada.shape[0]
    alpha = (2 * depth) ** 0.25
    assert seq % (MLSTM_CPB * CHUNK) == 0 and (seq // CHUNK) % 8 == 0
    assert d == N_HEADS * V_DIM + N_GROUPS * GROUP_DIM
    tm = min(512, FFT_NB * (seq // CHUNK))
    x2d = x.reshape(bsz * seq, d)
    for l in range(depth):
        w = _layer_weights(w_in[l], b_gate[l], w_out[l], w_ff1[l], w_ff2[l], b_ff1[l], b_ff2[l],
                           mlstm_norm_w[l], ln1_g[l], ln1_b[l], ln2_g[l], ln2_b[l])
        mod4 = _adaln(c, w_ada[l], b_ada[l]).reshape(bsz, N_MOD, 1, d)
        q, kt, v, o, z5, gi, gf = _inproj(x2d, mod4, w, seq, min(1024, seq))
        ra, rb, rc = _gate_prep(gi, gf, bsz)
        hf, hb = _mlstm(q, kt, v, ra, rb, rc, bsz, seq, MLSTM_CPB)
        y5 = _fourier(z5, seq)
        x2d = _outmlp(x2d, hf, hb, o, y5, mod4, w, seq, tm, alpha, ff_chunk=1024)
    return x2d.reshape(bsz, seq, d)
```

```python
import functools

import numpy as np
import jax
import jax.numpy as jnp
from jax import lax
from jax.experimental import pallas as pl
from jax.experimental.pallas import tpu as pltpu

F32 = jnp.float32
BF16 = jnp.bfloat16

CHUNK = 128
N_HEADS = 4
QK_DIM = 64
V_DIM = 128
N_GROUPS = 4
GROUP_DIM = 128
N_MOD = 6
LN_EPS = 1e-5
MASKED = -1e30
FFT_NB = 8
FFT_KB = 4
FFT_SB = 4
FFT_GB = 4
MLSTM_CPB = 8
OUTMLP_SUBTILES = 2
INPROJ_SUBTILES = 4

_NT = (((1,), (1,)), ((), ()))


def _dot(a, b):
    return jnp.dot(a, b, preferred_element_type=F32)


def _dot_nt(a, b):
    return lax.dot_general(a, b, _NT, preferred_element_type=F32)


def _ln_plain(x):
    mu = jnp.mean(x, axis=-1, keepdims=True)
    xc = x - mu
    var = jnp.mean(xc * xc, axis=-1, keepdims=True)
    return xc * lax.rsqrt(var + LN_EPS)


def _log_sigmoid(x):
    return jnp.minimum(x, 0.0) - jnp.log1p(jnp.exp(-jnp.abs(x)))


def _split3(x):
    hi = x.astype(BF16)
    r1 = x - hi.astype(F32)
    mid = r1.astype(BF16)
    lo = (r1 - mid.astype(F32)).astype(BF16)
    return hi, mid, lo


def _vmem_params(semantics, limit_mb):
    return pltpu.CompilerParams(dimension_semantics=semantics, vmem_limit_bytes=limit_mb * 1024 * 1024)


def _adaln_kernel(ct_ref, w_ref, b_ref, o_ref):
    ct = ct_ref[...]
    act = ct * jax.nn.sigmoid(ct)
    w = w_ref[...]
    for b in range(ct.shape[1]):
        o_ref[b:b + 1, :] = jnp.sum(act[:, b:b + 1] * w, axis=0, keepdims=True) + b_ref[...]


def _adaln(c, w_ada, b_ada):
    bsz, d = c.shape
    n = w_ada.shape[1]
    tn = 1024
    return pl.pallas_call(
        _adaln_kernel,
        grid=(n // tn,),
        in_specs=[pl.BlockSpec((d, bsz), lambda j: (0, 0)),
                  pl.BlockSpec((d, tn), lambda j: (0, j)),
                  pl.BlockSpec((1, tn), lambda j: (0, j))],
        out_specs=pl.BlockSpec((bsz, tn), lambda j: (0, j)),
        out_shape=jax.ShapeDtypeStruct((bsz, n), F32),
        compiler_params=_vmem_params(("parallel",), 32),
        name="adaln",
    )(c.T, w_ada, b_ada.reshape(1, n))


def _inproj_kernel(x_ref, sh_ref, sc_ref, wrow_ref, wlane_ref, bg_ref, wout_ref, w1_ref, w2_ref,
                   q_ref, kt_ref, v_ref, o_ref, z_ref, gi_ref, gf_ref, woutb_ref, w1b_ref, w2b_ref):
    woutb_ref[...] = wout_ref[...].astype(BF16)
    w1b_ref[...] = w1_ref[...].astype(BF16)
    w2b_ref[...] = w2_ref[...].astype(BF16)
    qk, vw = q_ref.shape[1], v_ref.shape[1]
    n_chunks = x_ref.shape[0] // CHUNK
    cps = max(1, n_chunks // INPROJ_SUBTILES)
    for c0 in range(0, n_chunks, cps):
        rs = slice(c0 * CHUNK, (c0 + cps) * CHUNK)
        h = _ln_plain(x_ref[rs, :]) * (1.0 + sc_ref[...]) + sh_ref[...]
        hb = h.astype(BF16)
        rowp = _dot(hb, wrow_ref[...])
        q_ref[rs, :] = rowp[:, :qk].astype(BF16)
        v_ref[rs, :] = rowp[:, qk:qk + vw].astype(BF16)
        o_ref[rs, :] = rowp[:, qk + vw:qk + 2 * vw].astype(BF16)
        fz = rowp[:, qk + 2 * vw:]
        for j in range(cps):
            for g in range(N_GROUPS):
                for jb in range(CHUNK // FFT_NB):
                    r0 = j * CHUNK + jb * FFT_NB
                    z_ref[g, jb, (c0 + j) * FFT_NB:(c0 + j + 1) * FFT_NB, :] = (
                        fz[r0:r0 + FFT_NB, g * GROUP_DIM:(g + 1) * GROUP_DIM])
        lanep = _dot_nt(wlane_ref[...], hb)
        gates = lanep[qk:, :] + bg_ref[...]
        for j in range(cps):
            cl = slice(j * CHUNK, (j + 1) * CHUNK)
            kt_ref[c0 + j] = lanep[:qk, cl].astype(BF16)
            for d in range(2):
                gi_ref[d, (c0 + j) * 8:(c0 + j + 1) * 8, :] = gates[16 * d:16 * d + 8, cl]
                gf_ref[d, (c0 + j) * 8:(c0 + j + 1) * 8, :] = gates[16 * d + 8:16 * d + 16, cl]


def _inproj(x2d, mod4, w, seq, tm):
    t, d = x2d.shape
    spb = seq // tm
    qk = N_HEADS * QK_DIM
    vw = N_HEADS * V_DIM
    nblk = CHUNK // FFT_NB
    cpt = tm // CHUNK
    full = lambda a: pl.BlockSpec(a.shape, lambda i: (0,) * a.ndim)
    modspec = lambda j: pl.BlockSpec((None, None, 1, d), lambda i: (i // spb, j, 0, 0))
    names = ["wrow", "wlane", "bg"]
    steps = t // tm
    casts = [w["wout"], w["w1"], w["w2"]]
    assert all(a.shape[0] % (16 * steps) == 0 for a in casts)
    slab = lambda a: pl.BlockSpec((a.shape[0] // steps, a.shape[1]), lambda i: (i, 0))
    return pl.pallas_call(
        _inproj_kernel,
        grid=(steps,),
        in_specs=[pl.BlockSpec((tm, d), lambda i: (i, 0)), modspec(0), modspec(1)] + [full(w[k]) for k in names]
                 + [slab(a) for a in casts],
        out_specs=[pl.BlockSpec((tm, qk), lambda i: (i, 0)),
                   pl.BlockSpec((cpt, qk, CHUNK), lambda i: (i, 0, 0)),
                   pl.BlockSpec((tm, vw), lambda i: (i, 0)),
                   pl.BlockSpec((tm, vw), lambda i: (i, 0)),
                   pl.BlockSpec((None, N_GROUPS, nblk, cpt * FFT_NB, GROUP_DIM),
                                lambda i: (i // spb, 0, 0, i % spb, 0)),
                   pl.BlockSpec((2, cpt * 8, CHUNK), lambda i: (0, i, 0)),
                   pl.BlockSpec((2, cpt * 8, CHUNK), lambda i: (0, i, 0))] + [slab(a) for a in casts],
        out_shape=[jax.ShapeDtypeStruct((t, qk), BF16),
                   jax.ShapeDtypeStruct((t // CHUNK, qk, CHUNK), BF16),
                   jax.ShapeDtypeStruct((t, vw), BF16),
                   jax.ShapeDtypeStruct((t, vw), BF16),
                   jax.ShapeDtypeStruct((t // seq, N_GROUPS, nblk, (seq // CHUNK) * FFT_NB, GROUP_DIM), F32),
                   jax.ShapeDtypeStruct((2, t // CHUNK * 8, CHUNK), F32),
                   jax.ShapeDtypeStruct((2, t // CHUNK * 8, CHUNK), F32)]
                  + [jax.ShapeDtypeStruct(a.shape, BF16) for a in casts],
        compiler_params=_vmem_params(("parallel",), 48),
        name="inproj",
    )(x2d, mod4, mod4, *[w[k] for k in names], *casts)


def _gate_prep_kernel(gi_ref, gf_ref, ra_ref, rb_ref, rc_ref, bend_scr, amax_scr, mprev_scr):
    rows = gi_ref.shape[1]
    n_chunks = rows // 8
    lane = lax.broadcasted_iota(jnp.int32, (rows, CHUNK), 1)
    first = (lax.broadcasted_iota(jnp.int32, (rows, CHUNK), 0) & 7) < N_HEADS
    src = lax.broadcasted_iota(jnp.int32, (CHUNK, 2 * CHUNK), 0)
    dst = lax.broadcasted_iota(jnp.int32, (CHUNK, 2 * CHUNK), 1)
    for d in range(2):
        feeds = (src <= dst) if d == 0 else (src >= dst)
        cum_and_total = jnp.where((dst >= CHUNK) | feeds, 1.0, 0.0).astype(BF16)
        bc = sum(_dot(p, cum_and_total) for p in _split3(_log_sigmoid(gf_ref[d])))
        b, b_end = bc[:, :CHUNK], bc[:, CHUNK:]
        a = gi_ref[d] - b
        cm = a
        for k in range(7):
            sh = 1 << k
            if d == 0:
                cm = jnp.maximum(cm, jnp.where(lane >= sh, pltpu.roll(cm, sh, 1), MASKED))
            else:
                cm = jnp.maximum(cm, jnp.where(lane < CHUNK - sh, pltpu.roll(cm, CHUNK - sh, 1), MASKED))
        a_max = jnp.broadcast_to(jnp.max(a, axis=1, keepdims=True), a.shape)
        bend_scr[...] = b_end
        amax_scr[...] = a_max

        def scan(c, m):
            r = pl.ds(pl.multiple_of((c if d == 0 else n_chunks - 1 - c) * 8, 8), 8)
            mprev_scr[r, :] = m
            return bend_scr[r, :] + jnp.maximum(m, amax_scr[r, :])

        lax.fori_loop(0, n_chunks, scan, jnp.zeros((8, CHUNK), F32))
        m_prev = mprev_scr[...]
        u = -jnp.maximum(m_prev, cm)
        m_new = b_end + jnp.maximum(m_prev, a_max)
        ra_ref[d] = jnp.where(first, a, jnp.exp(b_end + a - m_new))
        rb_ref[d] = jnp.where(first, u, jnp.exp(m_prev + u))
        rc_ref[d] = jnp.where(first, jnp.exp(u - b), jnp.exp(b_end + m_prev - m_new))


def _gate_prep(gi, gf, bsz):
    rows = gi.shape[1] // bsz
    spec = pl.BlockSpec((2, rows, CHUNK), lambda b: (0, b, 0))
    shape = jax.ShapeDtypeStruct(gi.shape, F32)
    return pl.pallas_call(
        _gate_prep_kernel,
        grid=(bsz,),
        in_specs=[spec, spec],
        out_specs=[spec, spec, spec],
        out_shape=[shape, shape, shape],
        scratch_shapes=[pltpu.VMEM((rows, CHUNK), F32)] * 3,
        compiler_params=_vmem_params(("parallel",), 32),
        name="gate_prep",
    )(gi, gf)


@functools.lru_cache(maxsize=None)
def _column_selector():
    sel = np.zeros((48, N_HEADS * 3 * CHUNK), np.float32)
    for hd in range(N_HEADS):
        for j, (arr, row) in enumerate(((0, hd), (0, N_HEADS + hd), (1, hd))):
            for piece in range(3):
                sel[arr * 24 + piece * 8 + row, (3 * hd + j) * CHUNK:(3 * hd + j + 1) * CHUNK] = 1.0
    return sel


def _mlstm_kernel(sel_ref, qf_ref, ktf_ref, vf_ref, raf_ref, rbf_ref, rcf_ref,
                  qb_ref, ktb_ref, vb_ref, rab_ref, rbb_ref, rcb_ref, hf_ref, hb_ref, c_scr):
    @pl.when(pl.program_id(1) == 0)
    def _():
        c_scr[...] = jnp.zeros_like(c_scr)

    cpb = ktf_ref.shape[0]
    t_i = lax.broadcasted_iota(jnp.int32, (CHUNK, CHUNK), 0)
    s_i = lax.broadcasted_iota(jnp.int32, (CHUNK, CHUNK), 1)
    visible = (s_i <= t_i, s_i >= t_i)
    ones = jnp.ones((CHUNK, V_DIM), BF16)
    kzero = jnp.zeros((QK_DIM, CHUNK), BF16)
    dirs = ((qf_ref, ktf_ref, vf_ref, raf_ref, rbf_ref, rcf_ref, hf_ref),
            (qb_ref, ktb_ref, vb_ref, rab_ref, rbb_ref, rcb_ref, hb_ref))

    def body(i, carry):
        units = []
        for d, (q_ref, kt_ref, v_ref, ra_ref, rb_ref, rc_ref, h_ref) in enumerate(dirs):
            li = i if d == 0 else cpb - 1 - i
            r8 = pl.ds(pl.multiple_of(li * 8, 8), 8)
            rl = pl.ds(pl.multiple_of(li * CHUNK, CHUNK), CHUNK)
            ra = ra_ref[r8, :]
            rc = rc_ref[r8, :]
            pieces = [p.astype(F32) for arr in (rb_ref[r8, :], rc) for p in _split3(arr)]
            colb = lax.dot_general(jnp.concatenate(pieces, axis=0).astype(BF16), sel_ref[...],
                                   (((0,), (0,)), ((), ())), preferred_element_type=F32)
            kt = kt_ref[li]
            for pair in range(N_HEADS // 2):
                k_a = kt[2 * pair * QK_DIM:(2 * pair + 1) * QK_DIM, :]
                k_b = kt[(2 * pair + 1) * QK_DIM:(2 * pair + 2) * QK_DIM, :]
                kdiag = jnp.concatenate([jnp.concatenate([k_a, kzero], axis=1),
                                         jnp.concatenate([kzero, k_b], axis=1)], axis=0)
                qk2 = _dot(q_ref[rl, 2 * pair * QK_DIM:2 * (pair + 1) * QK_DIM], kdiag)
                for hd in (2 * pair, 2 * pair + 1):
                    qh = q_ref[rl, hd * QK_DIM:(hd + 1) * QK_DIM]
                    kth = kt[hd * QK_DIM:(hd + 1) * QK_DIM, :]
                    units.append(dict(d=d, hd=hd, rl=rl, h_ref=h_ref, ra=ra, rc=rc, colb=colb, qh=qh, kth=kth,
                                      v_ref=v_ref, qk=qk2[:, (hd % 2) * CHUNK:(hd % 2 + 1) * CHUNK]))
        for u in units:
            d, hd, colb = u["d"], u["hd"], u["colb"]
            c0 = 3 * hd * CHUNK
            u_b = colb[:, c0:c0 + CHUNK]
            w_inter = colb[:, c0 + CHUNK:c0 + CHUNK + QK_DIM]
            w_intra = jnp.exp(jnp.where(visible[d], u_b + u["ra"][hd:hd + 1, :], MASKED))
            s = (u["qk"] * w_intra).astype(BF16)
            q_inter = (u["qh"].astype(F32) * w_inter).astype(BF16)
            u["c_prev"] = c_scr[d, hd]
            u["v_aug"] = jnp.concatenate([u["v_ref"][u["rl"], hd * V_DIM:(hd + 1) * V_DIM], ones], axis=1)
            u["nd"] = _dot(jnp.concatenate([s, q_inter], axis=1),
                           jnp.concatenate([u["v_aug"], u["c_prev"].astype(BF16)], axis=0))
        for u in units:
            hd, nd = u["hd"], u["nd"]
            exp_neg_m = u["colb"][:, (3 * hd + 2) * CHUNK:(3 * hd + 3) * CHUNK]
            den = jnp.maximum(jnp.abs(nd[:, V_DIM:]), exp_neg_m)
            u["h_ref"][u["rl"], hd * V_DIM:(hd + 1) * V_DIM] = (nd[:, :V_DIM] / den).astype(u["h_ref"].dtype)
            wk_row = u["ra"][N_HEADS + hd:N_HEADS + hd + 1, :]
            decay = jnp.broadcast_to(u["rc"][N_HEADS + hd:N_HEADS + hd + 1, :], (QK_DIM, CHUNK))
            kw = (u["kth"].astype(F32) * wk_row).astype(BF16)
            c_scr[u["d"], hd] = jnp.concatenate([decay, decay], axis=1) * u["c_prev"] + _dot(kw, u["v_aug"])
        return carry

    lax.fori_loop(0, cpb, body, 0, unroll=4)


def _mlstm(q, kt, v, ra, rb, rc, bsz, seq, cpb):
    t = q.shape[0]
    nblk = seq // (cpb * CHUNK)
    qk = N_HEADS * QK_DIM
    vw = N_HEADS * V_DIM
    fwd = lambda b, j: b * nblk + j
    bwd = lambda b, j: b * nblk + nblk - 1 - j

    def specs(blk, d):
        return [pl.BlockSpec((cpb * CHUNK, qk), lambda b, j: (blk(b, j), 0)),
                pl.BlockSpec((cpb, qk, CHUNK), lambda b, j: (blk(b, j), 0, 0)),
                pl.BlockSpec((cpb * CHUNK, vw), lambda b, j: (blk(b, j), 0))] + \
               [pl.BlockSpec((None, cpb * 8, CHUNK), lambda b, j: (d, blk(b, j), 0))] * 3

    sel = jnp.asarray(_column_selector()).astype(BF16)
    return pl.pallas_call(
        _mlstm_kernel,
        grid=(bsz, nblk),
        in_specs=[pl.BlockSpec(sel.shape, lambda b, j: (0, 0))] + specs(fwd, 0) + specs(bwd, 1),
        out_specs=[pl.BlockSpec((cpb * CHUNK, vw), lambda b, j: (fwd(b, j), 0)),
                   pl.BlockSpec((cpb * CHUNK, vw), lambda b, j: (bwd(b, j), 0))],
        out_shape=[jax.ShapeDtypeStruct((t, vw), BF16)] * 2,
        scratch_shapes=[pltpu.VMEM((2, N_HEADS, QK_DIM, 2 * V_DIM), F32)],
        compiler_params=_vmem_params(("parallel", "arbitrary"), 32),
        name="mlstm",
    )(sel, q, kt, v, ra, rb, rc, q, kt, v, ra, rb, rc)


def _fft_pitch(n1):
    return 8 * ((n1 // 4) | 1)


@functools.lru_cache(maxsize=None)
def _fft_tables(seq):
    n1 = seq // CHUNK
    two_pi = 2.0 * np.pi
    k1 = np.arange(n1, dtype=np.int64)
    n = 128 * np.arange(n1, dtype=np.int64)[None, None, :] + np.arange(128, dtype=np.int64)[:, None, None]
    ang = two_pi * ((k1[None, :, None] * n) % seq).astype(np.float64) / seq
    ga = np.concatenate([np.cos(ang), -np.sin(ang)], axis=1)
    j = np.arange(128, dtype=np.int64)
    a128 = two_pi * ((j[:, None] * j[None, :]) % 128).astype(np.float64) / 128.0
    ff = np.concatenate([np.cos(a128), -np.sin(a128)], axis=0)
    cs = np.concatenate([np.cos(a128), np.sin(a128)], axis=0) / np.sqrt(128.0 * seq)
    return tuple(np.asarray(a, dtype=np.float32) for a in (ga, ff, cs))


def _fft_kernel(z_ref, ga_ref, ff_ref, cs_ref, y_ref, scr):
    j = pl.program_id(2)
    n1 = ga_ref.shape[2]
    pitch = scr.shape[0] // CHUNK
    for sb in range(z_ref.shape[0]):
        for l in range(FFT_NB):
            n2 = (j * z_ref.shape[0] + sb) * FFT_NB + l
            zl = z_ref[sb, pl.ds(l, n1, stride=FFT_NB), :].astype(BF16)
            row0 = pl.multiple_of(n2 * pitch, 8)
            scr[pl.ds(row0, 2 * n1), :] = _dot(ga_ref[n2], zl)

    @pl.when(j == pl.num_programs(2) - 1)
    def _():
        ff = ff_ref[...]
        cs = cs_ref[...]

        def body(it, carry):
            k0s = [(it * FFT_GB + gi) * FFT_KB for gi in range(FFT_GB)]
            ms = []
            for k0 in k0s:
                p = [scr[pl.ds(ri * n1 + k0 + kk, CHUNK, stride=pitch), :]
                     for kk in range(FFT_KB) for ri in range(2)]
                ms.append(_dot(ff, jnp.concatenate(p, axis=1).astype(BF16)))
            ys = []
            for m in ms:
                x = []
                for kk in range(FFT_KB):
                    c0 = 2 * kk * GROUP_DIM
                    xr = m[:CHUNK, c0:c0 + GROUP_DIM] - m[CHUNK:, c0 + GROUP_DIM:c0 + 2 * GROUP_DIM]
                    xi = m[:CHUNK, c0 + GROUP_DIM:c0 + 2 * GROUP_DIM] + m[CHUNK:, c0:c0 + GROUP_DIM]
                    x.append(jnp.concatenate([xr, xi], axis=1))
                ys.append(_dot(jnp.concatenate(x, axis=0).astype(BF16), cs))
            for k0, y in zip(k0s, ys):
                for kk in range(FFT_KB):
                    row0 = pl.multiple_of((k0 + kk) * FFT_NB, 8)
                    for kb in range(CHUNK // FFT_NB):
                        r0 = kk * CHUNK + kb * FFT_NB
                        y_ref[kb, pl.ds(row0, FFT_NB), :] = y[r0:r0 + FFT_NB, :]
            return carry

        lax.fori_loop(0, n1 // (FFT_KB * FFT_GB), body, 0)


def _fourier(z5, seq):
    bsz = z5.shape[0]
    n1 = seq // CHUNK
    nblk = CHUNK // FFT_NB
    pitch = _fft_pitch(n1)
    ga, ff, cs = (jnp.asarray(a).astype(BF16) for a in _fft_tables(seq))
    return pl.pallas_call(
        _fft_kernel,
        grid=(bsz, N_GROUPS, nblk // FFT_SB),
        in_specs=[pl.BlockSpec((None, None, FFT_SB, n1 * FFT_NB, GROUP_DIM), lambda b, g, j: (b, g, j, 0, 0)),
                  pl.BlockSpec(ga.shape, lambda b, g, j: (0, 0, 0), pipeline_mode=pl.Buffered(1)),
                  pl.BlockSpec(ff.shape, lambda b, g, j: (0, 0)),
                  pl.BlockSpec(cs.shape, lambda b, g, j: (0, 0))],
        out_specs=pl.BlockSpec((None, None, nblk, n1 * FFT_NB, GROUP_DIM), lambda b, g, j: (b, g, 0, 0, 0)),
        out_shape=jax.ShapeDtypeStruct(z5.shape, F32),
        scratch_shapes=[pltpu.VMEM((CHUNK * pitch, GROUP_DIM), F32)],
        compiler_params=_vmem_params(("parallel", "parallel", "arbitrary"), 48),
        name="fft",
    )(z5, ga, ff, cs)


def _gather_rows(y_refs, k2_off, n1, kpt):
    rows = [jnp.concatenate([ref[pl.ds(k2_off + kk, n1, stride=FFT_NB), :] for ref in y_refs], axis=1)
            for kk in range(kpt)]
    return jnp.concatenate(rows, axis=0)


def _outmlp_kernel(alpha, ff_chunk, spb, x_ref, hf_ref, hb_ref, o_ref, y0_ref, y1_ref, y2_ref, y3_ref,
                   g1_ref, sc2_ref, sh2_ref, g2_ref,
                   nw_ref, wout_ref, l1g_ref, l1b_ref, w1_ref, b1_ref, w2_ref, b2_ref, l2g_ref, l2b_ref,
                   out_ref):
    n1 = y0_ref.shape[0] // FFT_NB
    kpt = x_ref.shape[0] // n1
    k2_off = ((pl.program_id(0) % spb) * kpt) % FFT_NB
    kps = max(1, kpt // OUTMLP_SUBTILES)
    subs = [dict(k2l=k2l, rs=slice(k2l * n1, (k2l + kps) * n1)) for k2l in range(0, kpt, kps)]
    for s in subs:
        rs = s["rs"]
        yf = _gather_rows((y0_ref, y1_ref, y2_ref, y3_ref), k2_off + s["k2l"], n1, kps).astype(BF16)
        hsum = hf_ref[rs, :].astype(F32) + hb_ref[rs, :].astype(F32)
        heads = [_ln_plain(hsum[:, j * V_DIM:(j + 1) * V_DIM]) for j in range(N_HEADS)]
        ym = jnp.concatenate(heads, axis=1) * nw_ref[...] * jax.nn.sigmoid(o_ref[rs, :].astype(F32))
        s["mix"] = _dot(jnp.concatenate([ym.astype(BF16), yf], axis=1), wout_ref[...])
    for s in subs:
        x1 = _ln_plain(alpha * x_ref[s["rs"], :] + (1.0 + g1_ref[...]) * s["mix"]) * l1g_ref[...] + l1b_ref[...]
        s["x1"] = x1
        s["h2"] = (_ln_plain(x1) * (1.0 + sc2_ref[...]) + sh2_ref[...]).astype(BF16)
    for s in subs:
        s["ff"] = b2_ref[...]
    for j in range(w1_ref.shape[1] // ff_chunk):
        sl = slice(j * ff_chunk, (j + 1) * ff_chunk)
        for s in subs:
            s["pre"] = _dot(s["h2"], w1_ref[:, sl])
        for s in subs:
            hid = jnp.maximum(s["pre"] + b1_ref[:, sl], 0.0)
            s["ff"] = s["ff"] + _dot((hid * hid).astype(BF16), w2_ref[sl, :])
    for s in subs:
        out_ref[s["rs"], :] = (_ln_plain(alpha * s["x1"] + (1.0 + g2_ref[...]) * s["ff"]) * l2g_ref[...]
                               + l2b_ref[...])


def _outmlp(x2d, hf, hb, o, y5, mod4, w, seq, tm, alpha, ff_chunk):
    t, d = x2d.shape
    spb = seq // tm
    n1 = seq // CHUNK
    kpt = tm // n1
    assert FFT_NB % kpt == 0
    vw = N_HEADS * V_DIM
    row = lambda width: pl.BlockSpec((tm, width), lambda i: (i, 0))
    yspec = lambda g: pl.BlockSpec((None, None, None, n1 * FFT_NB, GROUP_DIM),
                                   lambda i: (i // spb, g, ((i % spb) * kpt) // FFT_NB, 0, 0))
    modspec = lambda j: pl.BlockSpec((None, None, 1, d), lambda i: (i // spb, j, 0, 0))
    const = lambda a: pl.BlockSpec(a.shape, lambda i: (0,) * a.ndim, pipeline_mode=pl.Buffered(1))
    names = ["nw", "wout", "l1g", "l1b", "w1", "b1", "w2", "b2", "l2g", "l2b"]
    return pl.pallas_call(
        functools.partial(_outmlp_kernel, alpha, ff_chunk, spb),
        grid=(t // tm,),
        in_specs=[row(d), row(vw), row(vw), row(vw)]
                 + [yspec(g) for g in range(N_GROUPS)]
                 + [modspec(2), modspec(4), modspec(3), modspec(5)] + [const(w[k]) for k in names],
        out_specs=row(d),
        out_shape=jax.ShapeDtypeStruct((t, d), F32),
        compiler_params=_vmem_params(("parallel",), 56),
        name="outmlp",
    )(x2d, hf, hb, o, y5, y5, y5, y5, mod4, mod4, mod4, mod4, *[w[k] for k in names])


def _layer_weights(w_in, b_gate, w_out, w_ff1, w_ff2, b_ff1, b_ff2, mlstm_norm_w, ln1_g, ln1_b, ln2_g, ln2_b):
    qk = N_HEADS * QK_DIM
    vw = N_HEADS * V_DIM
    fw = N_GROUPS * GROUP_DIM
    o0, o1, o2, o3, o4 = qk, 2 * qk, 2 * qk + vw, 2 * qk + 2 * vw, 2 * qk + 2 * vw + fw
    wg = w_in[:, o4:].T.reshape(2, 2, N_HEADS, -1)
    bg = b_gate.astype(F32).reshape(2, 2, N_HEADS, 1)
    dup = lambda a: jnp.concatenate([a, a], axis=1)
    r = lambda a: a.astype(F32).reshape(1, -1)
    wq = w_in[:, :o0] * (QK_DIM ** -0.5)
    gate_rows = lambda a: jnp.concatenate([dup(a[:, 0]), dup(a[:, 1])], axis=1).reshape(32, -1)
    return {
        "wrow": jnp.concatenate([wq, w_in[:, o1:o4]], axis=1).astype(BF16),
        "wlane": jnp.concatenate([w_in[:, o0:o1].T, gate_rows(wg)], axis=0).astype(BF16),
        "bg": gate_rows(bg),
        "nw": r(mlstm_norm_w), "wout": w_out, "l1g": r(ln1_g), "l1b": r(ln1_b),
        "w1": w_ff1, "b1": r(b_ff1), "w2": w_ff2, "b2": r(b_ff2),
        "l2g": r(ln2_g), "l2b": r(ln2_b),
    }


def kernel(x, c, w_ada, b_ada, w_in, b_gate, mlstm_norm_w, w_out, ln1_g, ln1_b,
           w_ff1, b_ff1, w_ff2, b_ff2, ln2_g, ln2_b):
    bsz, seq, d = x.shape
    depth = w_ada.shape[0]
    alpha = (2 * depth) ** 0.25
    assert seq % (MLSTM_CPB * CHUNK) == 0 and (seq // CHUNK) % 8 == 0
    assert d == N_HEADS * V_DIM + N_GROUPS * GROUP_DIM
    tm = min(512, FFT_NB * (seq // CHUNK))
    x2d = x.reshape(bsz * seq, d)
    for l in range(depth):
        w = _layer_weights(w_in[l], b_gate[l], w_out[l], w_ff1[l], w_ff2[l], b_ff1[l], b_ff2[l],
                           mlstm_norm_w[l], ln1_g[l], ln1_b[l], ln2_g[l], ln2_b[l])
        mod4 = _adaln(c, w_ada[l], b_ada[l]).reshape(bsz, N_MOD, 1, d)
        q, kt, v, o, z5, gi, gf, w["wout"], w["w1"], w["w2"] = _inproj(x2d, mod4, w, seq, min(1024, seq))
        ra, rb, rc = _gate_prep(gi, gf, bsz)
        hf, hb = _mlstm(q, kt, v, ra, rb, rc, bsz, seq, MLSTM_CPB)
        y5 = _fourier(z5, seq)
        x2d = _outmlp(x2d, hf, hb, o, y5, mod4, w, seq, tm, alpha, ff_chunk=1024)
    return x2d.reshape(bsz, seq, d)
```

```python
import functools

import numpy as np
import jax
import jax.numpy as jnp
from jax import lax
from jax.experimental import pallas as pl
from jax.experimental.pallas import tpu as pltpu

F32 = jnp.float32
BF16 = jnp.bfloat16

CHUNK = 128
N_HEADS = 4
QK_DIM = 64
V_DIM = 128
N_GROUPS = 4
GROUP_DIM = 128
N_MOD = 6
LN_EPS = 1e-5
MASKED = -1e30
FFT_NB = 8
FFT_KB = 4
FFT_SB = 8
FFT_GB = 4
MLSTM_CPB = 8
OUTMLP_SUBTILE_K2 = (2, 2)
INPROJ_SUBTILES = 4

_NT = (((1,), (1,)), ((), ()))


def _dot(a, b):
    return jnp.dot(a, b, preferred_element_type=F32)


def _dot_nt(a, b):
    return lax.dot_general(a, b, _NT, preferred_element_type=F32)


def _ln_plain(x):
    mu = jnp.mean(x, axis=-1, keepdims=True)
    xc = x - mu
    var = jnp.mean(xc * xc, axis=-1, keepdims=True)
    return xc * lax.rsqrt(var + LN_EPS)


def _log_sigmoid(x):
    return jnp.minimum(x, 0.0) - jnp.log1p(jnp.exp(-jnp.abs(x)))


def _split3(x):
    hi = x.astype(BF16)
    r1 = x - hi.astype(F32)
    mid = r1.astype(BF16)
    lo = (r1 - mid.astype(F32)).astype(BF16)
    return hi, mid, lo


def _vmem_params(semantics, limit_mb):
    return pltpu.CompilerParams(dimension_semantics=semantics, vmem_limit_bytes=limit_mb * 1024 * 1024)


def _adaln_kernel(ct_ref, w_ref, b_ref, o_ref):
    ct = ct_ref[...]
    act = ct * jax.nn.sigmoid(ct)
    w = w_ref[...]
    for b in range(ct.shape[1]):
        o_ref[b:b + 1, :] = jnp.sum(act[:, b:b + 1] * w, axis=0, keepdims=True) + b_ref[...]


def _adaln(c, w_ada, b_ada):
    bsz, d = c.shape
    n = w_ada.shape[1]
    tn = 1024
    return pl.pallas_call(
        _adaln_kernel,
        grid=(n // tn,),
        in_specs=[pl.BlockSpec((d, bsz), lambda j: (0, 0)),
                  pl.BlockSpec((d, tn), lambda j: (0, j)),
                  pl.BlockSpec((1, tn), lambda j: (0, j))],
        out_specs=pl.BlockSpec((bsz, tn), lambda j: (0, j)),
        out_shape=jax.ShapeDtypeStruct((bsz, n), F32),
        compiler_params=_vmem_params(("parallel",), 32),
        name="adaln",
    )(c.T, w_ada, b_ada.reshape(1, n))


def _inproj_kernel(x_ref, sh_ref, sc_ref, wrow_ref, wlane_ref, bg_ref, wout_ref, w1_ref, w2_ref,
                   q_ref, kt_ref, v_ref, o_ref, z_ref, gi_ref, gf_ref, woutb_ref, w1b_ref, w2b_ref):
    woutb_ref[...] = wout_ref[...].astype(BF16)
    w1b_ref[...] = w1_ref[...].astype(BF16)
    w2b_ref[...] = w2_ref[...].astype(BF16)
    qk, vw = q_ref.shape[1], v_ref.shape[1]
    n_chunks = x_ref.shape[0] // CHUNK
    cps = max(1, n_chunks // INPROJ_SUBTILES)
    for c0 in range(0, n_chunks, cps):
        rs = slice(c0 * CHUNK, (c0 + cps) * CHUNK)
        h = _ln_plain(x_ref[rs, :]) * (1.0 + sc_ref[...]) + sh_ref[...]
        hb = h.astype(BF16)
        rowp = _dot(hb, wrow_ref[...])
        q_ref[rs, :] = rowp[:, :qk].astype(BF16)
        v_ref[rs, :] = rowp[:, qk:qk + vw].astype(BF16)
        o_ref[rs, :] = rowp[:, qk + vw:qk + 2 * vw].astype(BF16)
        fz = rowp[:, qk + 2 * vw:]
        for j in range(cps):
            for g in range(N_GROUPS):
                for jb in range(CHUNK // FFT_NB):
                    r0 = j * CHUNK + jb * FFT_NB
                    z_ref[g, jb, (c0 + j) * FFT_NB:(c0 + j + 1) * FFT_NB, :] = (
                        fz[r0:r0 + FFT_NB, g * GROUP_DIM:(g + 1) * GROUP_DIM])
        lanep = _dot_nt(wlane_ref[...], hb)
        gates = lanep[qk:, :] + bg_ref[...]
        for j in range(cps):
            cl = slice(j * CHUNK, (j + 1) * CHUNK)
            kt_ref[c0 + j] = lanep[:qk, cl].astype(BF16)
            for d in range(2):
                gi_ref[d, (c0 + j) * 8:(c0 + j + 1) * 8, :] = gates[16 * d:16 * d + 8, cl]
                gf_ref[d, (c0 + j) * 8:(c0 + j + 1) * 8, :] = gates[16 * d + 8:16 * d + 16, cl]


def _inproj(x2d, mod4, w, seq, tm):
    t, d = x2d.shape
    spb = seq // tm
    qk = N_HEADS * QK_DIM
    vw = N_HEADS * V_DIM
    nblk = CHUNK // FFT_NB
    cpt = tm // CHUNK
    full = lambda a: pl.BlockSpec(a.shape, lambda i: (0,) * a.ndim)
    modspec = lambda j: pl.BlockSpec((None, None, 1, d), lambda i: (i // spb, j, 0, 0))
    names = ["wrow", "wlane", "bg"]
    steps = t // tm
    casts = [w["wout"], w["w1"], w["w2"]]
    assert all(a.shape[0] % (16 * steps) == 0 for a in casts)
    slab = lambda a: pl.BlockSpec((a.shape[0] // steps, a.shape[1]), lambda i: (i, 0))
    return pl.pallas_call(
        _inproj_kernel,
        grid=(steps,),
        in_specs=[pl.BlockSpec((tm, d), lambda i: (i, 0)), modspec(0), modspec(1)] + [full(w[k]) for k in names]
                 + [slab(a) for a in casts],
        out_specs=[pl.BlockSpec((tm, qk), lambda i: (i, 0)),
                   pl.BlockSpec((cpt, qk, CHUNK), lambda i: (i, 0, 0)),
                   pl.BlockSpec((tm, vw), lambda i: (i, 0)),
                   pl.BlockSpec((tm, vw), lambda i: (i, 0)),
                   pl.BlockSpec((None, N_GROUPS, nblk, cpt * FFT_NB, GROUP_DIM),
                                lambda i: (i // spb, 0, 0, i % spb, 0)),
                   pl.BlockSpec((2, cpt * 8, CHUNK), lambda i: (0, i, 0)),
                   pl.BlockSpec((2, cpt * 8, CHUNK), lambda i: (0, i, 0))] + [slab(a) for a in casts],
        out_shape=[jax.ShapeDtypeStruct((t, qk), BF16),
                   jax.ShapeDtypeStruct((t // CHUNK, qk, CHUNK), BF16),
                   jax.ShapeDtypeStruct((t, vw), BF16),
                   jax.ShapeDtypeStruct((t, vw), BF16),
                   jax.ShapeDtypeStruct((t // seq, N_GROUPS, nblk, (seq // CHUNK) * FFT_NB, GROUP_DIM), F32),
                   jax.ShapeDtypeStruct((2, t // CHUNK * 8, CHUNK), F32),
                   jax.ShapeDtypeStruct((2, t // CHUNK * 8, CHUNK), F32)]
                  + [jax.ShapeDtypeStruct(a.shape, BF16) for a in casts],
        compiler_params=_vmem_params(("parallel",), 48),
        name="inproj",
    )(x2d, mod4, mod4, *[w[k] for k in names], *casts)


def _gate_prep_kernel(gi_ref, gf_ref, ra_ref, rb_ref, rc_ref, bend_scr, amax_scr, mprev_scr):
    rows = gi_ref.shape[1]
    n_chunks = rows // 8
    lane = lax.broadcasted_iota(jnp.int32, (rows, CHUNK), 1)
    first = (lax.broadcasted_iota(jnp.int32, (rows, CHUNK), 0) & 7) < N_HEADS
    src = lax.broadcasted_iota(jnp.int32, (CHUNK, 2 * CHUNK), 0)
    dst = lax.broadcasted_iota(jnp.int32, (CHUNK, 2 * CHUNK), 1)
    for d in range(2):
        feeds = (src <= dst) if d == 0 else (src >= dst)
        cum_and_total = jnp.where((dst >= CHUNK) | feeds, 1.0, 0.0).astype(BF16)
        bc = sum(_dot(p, cum_and_total) for p in _split3(_log_sigmoid(gf_ref[d])))
        b, b_end = bc[:, :CHUNK], bc[:, CHUNK:]
        a = gi_ref[d] - b
        cm = a
        for k in range(7):
            sh = 1 << k
            if d == 0:
                cm = jnp.maximum(cm, jnp.where(lane >= sh, pltpu.roll(cm, sh, 1), MASKED))
            else:
                cm = jnp.maximum(cm, jnp.where(lane < CHUNK - sh, pltpu.roll(cm, CHUNK - sh, 1), MASKED))
        a_max = jnp.broadcast_to(jnp.max(a, axis=1, keepdims=True), a.shape)
        bend_scr[...] = b_end
        amax_scr[...] = a_max

        def scan(c, m):
            r = pl.ds(pl.multiple_of((c if d == 0 else n_chunks - 1 - c) * 8, 8), 8)
            mprev_scr[r, :] = m
            return bend_scr[r, :] + jnp.maximum(m, amax_scr[r, :])

        lax.fori_loop(0, n_chunks, scan, jnp.zeros((8, CHUNK), F32))
        m_prev = mprev_scr[...]
        u = -jnp.maximum(m_prev, cm)
        m_new = b_end + jnp.maximum(m_prev, a_max)
        ra_ref[d] = jnp.where(first, a, jnp.exp(b_end + a - m_new))
        rb_ref[d] = jnp.where(first, u, jnp.exp(m_prev + u))
        rc_ref[d] = jnp.where(first, jnp.exp(u - b), jnp.exp(b_end + m_prev - m_new))


def _gate_prep(gi, gf, bsz):
    rows = gi.shape[1] // bsz
    spec = pl.BlockSpec((2, rows, CHUNK), lambda b: (0, b, 0))
    shape = jax.ShapeDtypeStruct(gi.shape, F32)
    return pl.pallas_call(
        _gate_prep_kernel,
        grid=(bsz,),
        in_specs=[spec, spec],
        out_specs=[spec, spec, spec],
        out_shape=[shape, shape, shape],
        scratch_shapes=[pltpu.VMEM((rows, CHUNK), F32)] * 3,
        compiler_params=_vmem_params(("parallel",), 32),
        name="gate_prep",
    )(gi, gf)


@functools.lru_cache(maxsize=None)
def _column_selector():
    sel = np.zeros((48, N_HEADS * 3 * CHUNK), np.float32)
    for hd in range(N_HEADS):
        for j, (arr, row) in enumerate(((0, hd), (0, N_HEADS + hd), (1, hd))):
            for piece in range(3):
                sel[arr * 24 + piece * 8 + row, (3 * hd + j) * CHUNK:(3 * hd + j + 1) * CHUNK] = 1.0
    return sel


def _mlstm_kernel(sel_ref, qf_ref, ktf_ref, vf_ref, raf_ref, rbf_ref, rcf_ref,
                  qb_ref, ktb_ref, vb_ref, rab_ref, rbb_ref, rcb_ref, hf_ref, hb_ref, c_scr):
    @pl.when(pl.program_id(1) == 0)
    def _():
        c_scr[...] = jnp.zeros_like(c_scr)

    cpb = ktf_ref.shape[0]
    t_i = lax.broadcasted_iota(jnp.int32, (CHUNK, CHUNK), 0)
    s_i = lax.broadcasted_iota(jnp.int32, (CHUNK, CHUNK), 1)
    visible = (s_i <= t_i, s_i >= t_i)
    ones = jnp.ones((CHUNK, V_DIM), BF16)
    kzero = jnp.zeros((QK_DIM, CHUNK), BF16)
    dirs = ((qf_ref, ktf_ref, vf_ref, raf_ref, rbf_ref, rcf_ref, hf_ref),
            (qb_ref, ktb_ref, vb_ref, rab_ref, rbb_ref, rcb_ref, hb_ref))

    def body(i, carry):
        units = []
        for d, (q_ref, kt_ref, v_ref, ra_ref, rb_ref, rc_ref, h_ref) in enumerate(dirs):
            li = i if d == 0 else cpb - 1 - i
            r8 = pl.ds(pl.multiple_of(li * 8, 8), 8)
            rl = pl.ds(pl.multiple_of(li * CHUNK, CHUNK), CHUNK)
            ra = ra_ref[r8, :]
            rc = rc_ref[r8, :]
            pieces = [p.astype(F32) for arr in (rb_ref[r8, :], rc) for p in _split3(arr)]
            colb = lax.dot_general(jnp.concatenate(pieces, axis=0).astype(BF16), sel_ref[...],
                                   (((0,), (0,)), ((), ())), preferred_element_type=F32)
            kt = kt_ref[li]
            for pair in range(N_HEADS // 2):
                k_a = kt[2 * pair * QK_DIM:(2 * pair + 1) * QK_DIM, :]
                k_b = kt[(2 * pair + 1) * QK_DIM:(2 * pair + 2) * QK_DIM, :]
                kdiag = jnp.concatenate([jnp.concatenate([k_a, kzero], axis=1),
                                         jnp.concatenate([kzero, k_b], axis=1)], axis=0)
                qk2 = _dot(q_ref[rl, 2 * pair * QK_DIM:2 * (pair + 1) * QK_DIM], kdiag)
                for hd in (2 * pair, 2 * pair + 1):
                    qh = q_ref[rl, hd * QK_DIM:(hd + 1) * QK_DIM]
                    kth = kt[hd * QK_DIM:(hd + 1) * QK_DIM, :]
                    units.append(dict(d=d, hd=hd, rl=rl, h_ref=h_ref, ra=ra, rc=rc, colb=colb, qh=qh, kth=kth,
                                      v_ref=v_ref, qk=qk2[:, (hd % 2) * CHUNK:(hd % 2 + 1) * CHUNK]))
        for u in units:
            d, hd, colb = u["d"], u["hd"], u["colb"]
            c0 = 3 * hd * CHUNK
            u_b = colb[:, c0:c0 + CHUNK]
            w_inter = colb[:, c0 + CHUNK:c0 + CHUNK + QK_DIM]
            w_intra = jnp.exp(jnp.where(visible[d], u_b + u["ra"][hd:hd + 1, :], MASKED))
            s = (u["qk"] * w_intra).astype(BF16)
            q_inter = (u["qh"].astype(F32) * w_inter).astype(BF16)
            u["c_prev"] = c_scr[d, hd]
            u["v_aug"] = jnp.concatenate([u["v_ref"][u["rl"], hd * V_DIM:(hd + 1) * V_DIM], ones], axis=1)
            u["nd"] = _dot(jnp.concatenate([s, q_inter], axis=1),
                           jnp.concatenate([u["v_aug"], u["c_prev"].astype(BF16)], axis=0))
        for u in units:
            hd, nd = u["hd"], u["nd"]
            exp_neg_m = u["colb"][:, (3 * hd + 2) * CHUNK:(3 * hd + 3) * CHUNK]
            den = jnp.maximum(jnp.abs(nd[:, V_DIM:]), exp_neg_m)
            u["h_ref"][u["rl"], hd * V_DIM:(hd + 1) * V_DIM] = (nd[:, :V_DIM] / den).astype(u["h_ref"].dtype)
            wk_row = u["ra"][N_HEADS + hd:N_HEADS + hd + 1, :]
            decay = jnp.broadcast_to(u["rc"][N_HEADS + hd:N_HEADS + hd + 1, :], (QK_DIM, CHUNK))
            kw = (u["kth"].astype(F32) * wk_row).astype(BF16)
            c_scr[u["d"], hd] = jnp.concatenate([decay, decay], axis=1) * u["c_prev"] + _dot(kw, u["v_aug"])
        return carry

    lax.fori_loop(0, cpb, body, 0, unroll=4)


def _mlstm(q, kt, v, ra, rb, rc, bsz, seq, cpb):
    t = q.shape[0]
    nblk = seq // (cpb * CHUNK)
    qk = N_HEADS * QK_DIM
    vw = N_HEADS * V_DIM
    fwd = lambda b, j: b * nblk + j
    bwd = lambda b, j: b * nblk + nblk - 1 - j

    def specs(blk, d):
        return [pl.BlockSpec((cpb * CHUNK, qk), lambda b, j: (blk(b, j), 0)),
                pl.BlockSpec((cpb, qk, CHUNK), lambda b, j: (blk(b, j), 0, 0)),
                pl.BlockSpec((cpb * CHUNK, vw), lambda b, j: (blk(b, j), 0))] + \
               [pl.BlockSpec((None, cpb * 8, CHUNK), lambda b, j: (d, blk(b, j), 0))] * 3

    sel = jnp.asarray(_column_selector()).astype(BF16)
    return pl.pallas_call(
        _mlstm_kernel,
        grid=(bsz, nblk),
        in_specs=[pl.BlockSpec(sel.shape, lambda b, j: (0, 0))] + specs(fwd, 0) + specs(bwd, 1),
        out_specs=[pl.BlockSpec((cpb * CHUNK, vw), lambda b, j: (fwd(b, j), 0)),
                   pl.BlockSpec((cpb * CHUNK, vw), lambda b, j: (bwd(b, j), 0))],
        out_shape=[jax.ShapeDtypeStruct((t, vw), BF16)] * 2,
        scratch_shapes=[pltpu.VMEM((2, N_HEADS, QK_DIM, 2 * V_DIM), F32)],
        compiler_params=_vmem_params(("parallel", "arbitrary"), 32),
        name="mlstm",
    )(sel, q, kt, v, ra, rb, rc, q, kt, v, ra, rb, rc)


def _fft_pitch(n1):
    return 8 * ((n1 // 4) | 1)


@functools.lru_cache(maxsize=None)
def _fft_tables(seq):
    n1 = seq // CHUNK
    two_pi = 2.0 * np.pi
    k1 = np.arange(n1, dtype=np.int64)
    n = 128 * np.arange(n1, dtype=np.int64)[None, None, :] + np.arange(128, dtype=np.int64)[:, None, None]
    ang = two_pi * ((k1[None, :, None] * n) % seq).astype(np.float64) / seq
    ga = np.concatenate([np.cos(ang), -np.sin(ang)], axis=1)
    j = np.arange(128, dtype=np.int64)
    a128 = two_pi * ((j[:, None] * j[None, :]) % 128).astype(np.float64) / 128.0
    ff = np.concatenate([np.cos(a128), -np.sin(a128)], axis=0)
    cs = np.concatenate([np.cos(a128), np.sin(a128)], axis=0) / np.sqrt(128.0 * seq)
    return tuple(np.asarray(a, dtype=np.float32) for a in (ga, ff, cs))


def _fft_kernel(z_ref, ga_ref, ff_ref, cs_ref, y_ref, scr):
    j = pl.program_id(2)
    n1 = ga_ref.shape[2]
    pitch = scr.shape[0] // CHUNK
    for sb in range(z_ref.shape[0]):
        for l in range(FFT_NB):
            n2 = (j * z_ref.shape[0] + sb) * FFT_NB + l
            zl = z_ref[sb, pl.ds(l, n1, stride=FFT_NB), :].astype(BF16)
            row0 = pl.multiple_of(n2 * pitch, 8)
            scr[pl.ds(row0, 2 * n1), :] = _dot(ga_ref[n2], zl)

    @pl.when(j == pl.num_programs(2) - 1)
    def _():
        ff = ff_ref[...]
        cs = cs_ref[...]

        def body(it, carry):
            k0s = [(it * FFT_GB + gi) * FFT_KB for gi in range(FFT_GB)]
            ms = []
            for k0 in k0s:
                p = [scr[pl.ds(ri * n1 + k0 + kk, CHUNK, stride=pitch), :]
                     for kk in range(FFT_KB) for ri in range(2)]
                ms.append(_dot(ff, jnp.concatenate(p, axis=1).astype(BF16)))
            ys = []
            for m in ms:
                x = []
                for kk in range(FFT_KB):
                    c0 = 2 * kk * GROUP_DIM
                    xr = m[:CHUNK, c0:c0 + GROUP_DIM] - m[CHUNK:, c0 + GROUP_DIM:c0 + 2 * GROUP_DIM]
                    xi = m[:CHUNK, c0 + GROUP_DIM:c0 + 2 * GROUP_DIM] + m[CHUNK:, c0:c0 + GROUP_DIM]
                    x.append(jnp.concatenate([xr, xi], axis=1))
                ys.append(_dot(jnp.concatenate(x, axis=0).astype(BF16), cs))
            for k0, y in zip(k0s, ys):
                for kk in range(FFT_KB):
                    row0 = pl.multiple_of((k0 + kk) * FFT_NB, 8)
                    for kb in range(CHUNK // FFT_NB):
                        r0 = kk * CHUNK + kb * FFT_NB
                        y_ref[kb, pl.ds(row0, FFT_NB), :] = y[r0:r0 + FFT_NB, :]
            return carry

        lax.fori_loop(0, n1 // (FFT_KB * FFT_GB), body, 0)


def _fourier(z5, seq):
    bsz = z5.shape[0]
    n1 = seq // CHUNK
    nblk = CHUNK // FFT_NB
    pitch = _fft_pitch(n1)
    ga, ff, cs = (jnp.asarray(a).astype(BF16) for a in _fft_tables(seq))
    return pl.pallas_call(
        _fft_kernel,
        grid=(bsz, N_GROUPS, nblk // FFT_SB),
        in_specs=[pl.BlockSpec((None, None, FFT_SB, n1 * FFT_NB, GROUP_DIM), lambda b, g, j: (b, g, j, 0, 0)),
                  pl.BlockSpec(ga.shape, lambda b, g, j: (0, 0, 0), pipeline_mode=pl.Buffered(1)),
                  pl.BlockSpec(ff.shape, lambda b, g, j: (0, 0)),
                  pl.BlockSpec(cs.shape, lambda b, g, j: (0, 0))],
        out_specs=pl.BlockSpec((None, None, nblk, n1 * FFT_NB, GROUP_DIM), lambda b, g, j: (b, g, 0, 0, 0)),
        out_shape=jax.ShapeDtypeStruct(z5.shape, F32),
        scratch_shapes=[pltpu.VMEM((CHUNK * pitch, GROUP_DIM), F32)],
        compiler_params=_vmem_params(("parallel", "parallel", "arbitrary"), 56),
        name="fft",
    )(z5, ga, ff, cs)


def _gather_rows(y_refs, k2_off, n1, kpt):
    rows = [jnp.concatenate([ref[pl.ds(k2_off + kk, n1, stride=FFT_NB), :] for ref in y_refs], axis=1)
            for kk in range(kpt)]
    return jnp.concatenate(rows, axis=0)


def _outmlp_kernel(alpha, ff_chunk, spb, x_ref, hf_ref, hb_ref, o_ref, y0_ref, y1_ref, y2_ref, y3_ref,
                   g1_ref, sc2_ref, sh2_ref, g2_ref,
                   nw_ref, wout_ref, l1g_ref, l1b_ref, w1_ref, b1_ref, w2_ref, b2_ref, l2g_ref, l2b_ref,
                   out_ref):
    n1 = y0_ref.shape[0] // FFT_NB
    kpt = x_ref.shape[0] // n1
    k2_off = ((pl.program_id(0) % spb) * kpt) % FFT_NB
    sizes = OUTMLP_SUBTILE_K2 if sum(OUTMLP_SUBTILE_K2) == kpt else (kpt,)
    subs, k2l = [], 0
    for kps in sizes:
        subs.append(dict(k2l=k2l, kps=kps, rs=slice(k2l * n1, (k2l + kps) * n1)))
        k2l += kps
    for s in subs:
        rs = s["rs"]
        yf = _gather_rows((y0_ref, y1_ref, y2_ref, y3_ref), k2_off + s["k2l"], n1, s["kps"]).astype(BF16)
        hsum = hf_ref[rs, :].astype(F32) + hb_ref[rs, :].astype(F32)
        heads = [_ln_plain(hsum[:, j * V_DIM:(j + 1) * V_DIM]) for j in range(N_HEADS)]
        ym = jnp.concatenate(heads, axis=1) * nw_ref[...] * jax.nn.sigmoid(o_ref[rs, :].astype(F32))
        s["mix"] = _dot(jnp.concatenate([ym.astype(BF16), yf], axis=1), wout_ref[...])
    for s in subs:
        x1 = _ln_plain(alpha * x_ref[s["rs"], :] + (1.0 + g1_ref[...]) * s["mix"]) * l1g_ref[...] + l1b_ref[...]
        s["x1"] = x1
        s["h2"] = (_ln_plain(x1) * (1.0 + sc2_ref[...]) + sh2_ref[...]).astype(BF16)
    for s in subs:
        ff = b2_ref[...]
        for j in range(w1_ref.shape[1] // ff_chunk):
            sl = slice(j * ff_chunk, (j + 1) * ff_chunk)
            hid = jnp.maximum(_dot(s["h2"], w1_ref[:, sl]) + b1_ref[:, sl], 0.0)
            ff = ff + _dot((hid * hid).astype(BF16), w2_ref[sl, :])
        s["ff"] = ff
    for s in subs:
        out_ref[s["rs"], :] = (_ln_plain(alpha * s["x1"] + (1.0 + g2_ref[...]) * s["ff"]) * l2g_ref[...]
                               + l2b_ref[...])


def _outmlp(x2d, hf, hb, o, y5, mod4, w, seq, tm, alpha, ff_chunk):
    t, d = x2d.shape
    spb = seq // tm
    n1 = seq // CHUNK
    kpt = tm // n1
    assert FFT_NB % kpt == 0
    vw = N_HEADS * V_DIM
    row = lambda width: pl.BlockSpec((tm, width), lambda i: (i, 0))
    yspec = lambda g: pl.BlockSpec((None, None, None, n1 * FFT_NB, GROUP_DIM),
                                   lambda i: (i // spb, g, ((i % spb) * kpt) // FFT_NB, 0, 0))
    modspec = lambda j: pl.BlockSpec((None, None, 1, d), lambda i: (i // spb, j, 0, 0))
    const = lambda a: pl.BlockSpec(a.shape, lambda i: (0,) * a.ndim, pipeline_mode=pl.Buffered(1))
    names = ["nw", "wout", "l1g", "l1b", "w1", "b1", "w2", "b2", "l2g", "l2b"]
    return pl.pallas_call(
        functools.partial(_outmlp_kernel, alpha, ff_chunk, spb),
        grid=(t // tm,),
        in_specs=[row(d), row(vw), row(vw), row(vw)]
                 + [yspec(g) for g in range(N_GROUPS)]
                 + [modspec(2), modspec(4), modspec(3), modspec(5)] + [const(w[k]) for k in names],
        out_specs=row(d),
        out_shape=jax.ShapeDtypeStruct((t, d), F32),
        compiler_params=_vmem_params(("parallel",), 56),
        name="outmlp",
    )(x2d, hf, hb, o, y5, y5, y5, y5, mod4, mod4, mod4, mod4, *[w[k] for k in names])


def _layer_weights(w_in, b_gate, w_out, w_ff1, w_ff2, b_ff1, b_ff2, mlstm_norm_w, ln1_g, ln1_b, ln2_g, ln2_b):
    qk = N_HEADS * QK_DIM
    vw = N_HEADS * V_DIM
    fw = N_GROUPS * GROUP_DIM
    o0, o1, o2, o3, o4 = qk, 2 * qk, 2 * qk + vw, 2 * qk + 2 * vw, 2 * qk + 2 * vw + fw
    wg = w_in[:, o4:].T.reshape(2, 2, N_HEADS, -1)
    bg = b_gate.astype(F32).reshape(2, 2, N_HEADS, 1)
    dup = lambda a: jnp.concatenate([a, a], axis=1)
    r = lambda a: a.astype(F32).reshape(1, -1)
    wq = w_in[:, :o0] * (QK_DIM ** -0.5)
    gate_rows = lambda a: jnp.concatenate([dup(a[:, 0]), dup(a[:, 1])], axis=1).reshape(32, -1)
    return {
        "wrow": jnp.concatenate([wq, w_in[:, o1:o4]], axis=1).astype(BF16),
        "wlane": jnp.concatenate([w_in[:, o0:o1].T, gate_rows(wg)], axis=0).astype(BF16),
        "bg": gate_rows(bg),
        "nw": r(mlstm_norm_w), "wout": w_out, "l1g": r(ln1_g), "l1b": r(ln1_b),
        "w1": w_ff1, "b1": r(b_ff1), "w2": w_ff2, "b2": r(b_ff2),
        "l2g": r(ln2_g), "l2b": r(ln2_b),
    }


def kernel(x, c, w_ada, b_ada, w_in, b_gate, mlstm_norm_w, w_out, ln1_g, ln1_b,
           w_ff1, b_ff1, w_ff2, b_ff2, ln2_g, ln2_b):
    bsz, seq, d = x.shape
    depth = w_ada.shape[0]
    alpha = (2 * depth) ** 0.25
    assert seq % (MLSTM_CPB * CHUNK) == 0 and (seq // CHUNK) % 8 == 0
    assert d == N_HEADS * V_DIM + N_GROUPS * GROUP_DIM
    tm = min(512, FFT_NB * (seq // CHUNK))
    x2d = x.reshape(bsz * seq, d)
    for l in range(depth):
        w = _layer_weights(w_in[l], b_gate[l], w_out[l], w_ff1[l], w_ff2[l], b_ff1[l], b_ff2[l],
                           mlstm_norm_w[l], ln1_g[l], ln1_b[l], ln2_g[l], ln2_b[l])
        mod4 = _adaln(c, w_ada[l], b_ada[l]).reshape(bsz, N_MOD, 1, d)
        q, kt, v, o, z5, gi, gf, w["wout"], w["w1"], w["w2"] = _inproj(x2d, mod4, w, seq, min(1024, seq))
        ra, rb, rc = _gate_prep(gi, gf, bsz)
        hf, hb = _mlstm(q, kt, v, ra, rb, rc, bsz, seq, MLSTM_CPB)
        y5 = _fourier(z5, seq)
        x2d = _outmlp(x2d, hf, hb, o, y5, mod4, w, seq, tm, alpha, ff_chunk=1024)
    return x2d.reshape(bsz, seq, d)
```

```python
import functools

import numpy as np
import jax
import jax.numpy as jnp
from jax import lax
from jax.experimental import pallas as pl
from jax.experimental.pallas import tpu as pltpu

F32 = jnp.float32
BF16 = jnp.bfloat16

CHUNK = 128
N_HEADS = 4
QK_DIM = 64
V_DIM = 128
N_GROUPS = 4
GROUP_DIM = 128
N_MOD = 6
LN_EPS = 1e-5
MASKED = -1e30
FFT_NB = 8
FFT_KB = 4
FFT_SB = 8
FFT_GB = 4
MLSTM_CPB = 8
OUTMLP_SUBTILE_K2 = (2, 2)
INPROJ_SUBTILES = 4

_NT = (((1,), (1,)), ((), ()))


def _dot(a, b):
    return jnp.dot(a, b, preferred_element_type=F32)


def _dot_nt(a, b):
    return lax.dot_general(a, b, _NT, preferred_element_type=F32)


def _ln_plain(x):
    mu = jnp.mean(x, axis=-1, keepdims=True)
    xc = x - mu
    var = jnp.mean(xc * xc, axis=-1, keepdims=True)
    return xc * lax.rsqrt(var + LN_EPS)


def _log_sigmoid(x):
    return jnp.minimum(x, 0.0) - jnp.log1p(jnp.exp(-jnp.abs(x)))


def _split3(x):
    hi = x.astype(BF16)
    r1 = x - hi.astype(F32)
    mid = r1.astype(BF16)
    lo = (r1 - mid.astype(F32)).astype(BF16)
    return hi, mid, lo


def _vmem_params(semantics, limit_mb):
    return pltpu.CompilerParams(dimension_semantics=semantics, vmem_limit_bytes=limit_mb * 1024 * 1024)


def _adaln_kernel(ct_ref, w_ref, b_ref, o_ref):
    ct = ct_ref[...]
    act = ct * jax.nn.sigmoid(ct)
    w = w_ref[...]
    for b in range(ct.shape[1]):
        o_ref[b:b + 1, :] = jnp.sum(act[:, b:b + 1] * w, axis=0, keepdims=True) + b_ref[...]


def _adaln(c, w_ada, b_ada):
    bsz, d = c.shape
    n = w_ada.shape[1]
    tn = 1024
    return pl.pallas_call(
        _adaln_kernel,
        grid=(n // tn,),
        in_specs=[pl.BlockSpec((d, bsz), lambda j: (0, 0)),
                  pl.BlockSpec((d, tn), lambda j: (0, j)),
                  pl.BlockSpec((1, tn), lambda j: (0, j))],
        out_specs=pl.BlockSpec((bsz, tn), lambda j: (0, j)),
        out_shape=jax.ShapeDtypeStruct((bsz, n), F32),
        compiler_params=_vmem_params(("parallel",), 32),
        name="adaln",
    )(c.T, w_ada, b_ada.reshape(1, n))


def _inproj_kernel(x_ref, sh_ref, sc_ref, wrow_ref, wlane_ref, bg_ref, wout_ref, w1_ref, w2_ref,
                   q_ref, kt_ref, v_ref, o_ref, z_ref, gi_ref, gf_ref, woutb_ref, w1b_ref, w2b_ref):
    woutb_ref[...] = wout_ref[...].astype(BF16)
    w1b_ref[...] = w1_ref[...].astype(BF16)
    w2b_ref[...] = w2_ref[...].astype(BF16)
    qk, vw = q_ref.shape[1], v_ref.shape[1]
    n_chunks = x_ref.shape[0] // CHUNK
    cps = max(1, n_chunks // INPROJ_SUBTILES)
    for c0 in range(0, n_chunks, cps):
        rs = slice(c0 * CHUNK, (c0 + cps) * CHUNK)
        h = _ln_plain(x_ref[rs, :]) * (1.0 + sc_ref[...]) + sh_ref[...]
        hb = h.astype(BF16)
        rowp = _dot(hb, wrow_ref[...])
        q_ref[rs, :] = rowp[:, :qk].astype(BF16)
        v_ref[rs, :] = rowp[:, qk:qk + vw].astype(BF16)
        o_ref[rs, :] = rowp[:, qk + vw:qk + 2 * vw].astype(BF16)
        fz = rowp[:, qk + 2 * vw:]
        for j in range(cps):
            for g in range(N_GROUPS):
                for jb in range(CHUNK // FFT_NB):
                    r0 = j * CHUNK + jb * FFT_NB
                    z_ref[g, jb, (c0 + j) * FFT_NB:(c0 + j + 1) * FFT_NB, :] = (
                        fz[r0:r0 + FFT_NB, g * GROUP_DIM:(g + 1) * GROUP_DIM])
        lanep = _dot_nt(wlane_ref[...], hb)
        gates = lanep[qk:, :] + bg_ref[...]
        for j in range(cps):
            cl = slice(j * CHUNK, (j + 1) * CHUNK)
            kt_ref[c0 + j] = lanep[:qk, cl].astype(BF16)
            for d in range(2):
                gi_ref[d, (c0 + j) * 8:(c0 + j + 1) * 8, :] = gates[16 * d:16 * d + 8, cl]
                gf_ref[d, (c0 + j) * 8:(c0 + j + 1) * 8, :] = gates[16 * d + 8:16 * d + 16, cl]


def _inproj(x2d, mod4, w, seq, tm):
    t, d = x2d.shape
    spb = seq // tm
    qk = N_HEADS * QK_DIM
    vw = N_HEADS * V_DIM
    nblk = CHUNK // FFT_NB
    cpt = tm // CHUNK
    full = lambda a: pl.BlockSpec(a.shape, lambda i: (0,) * a.ndim)
    modspec = lambda j: pl.BlockSpec((None, None, 1, d), lambda i: (i // spb, j, 0, 0))
    names = ["wrow", "wlane", "bg"]
    steps = t // tm
    casts = [w["wout"], w["w1"], w["w2"]]
    assert all(a.shape[0] % (16 * steps) == 0 for a in casts)
    slab = lambda a: pl.BlockSpec((a.shape[0] // steps, a.shape[1]), lambda i: (i, 0))
    return pl.pallas_call(
        _inproj_kernel,
        grid=(steps,),
        in_specs=[pl.BlockSpec((tm, d), lambda i: (i, 0)), modspec(0), modspec(1)] + [full(w[k]) for k in names]
                 + [slab(a) for a in casts],
        out_specs=[pl.BlockSpec((tm, qk), lambda i: (i, 0)),
                   pl.BlockSpec((cpt, qk, CHUNK), lambda i: (i, 0, 0)),
                   pl.BlockSpec((tm, vw), lambda i: (i, 0)),
                   pl.BlockSpec((tm, vw), lambda i: (i, 0)),
                   pl.BlockSpec((None, N_GROUPS, nblk, cpt * FFT_NB, GROUP_DIM),
                                lambda i: (i // spb, 0, 0, i % spb, 0)),
                   pl.BlockSpec((2, cpt * 8, CHUNK), lambda i: (0, i, 0)),
                   pl.BlockSpec((2, cpt * 8, CHUNK), lambda i: (0, i, 0))] + [slab(a) for a in casts],
        out_shape=[jax.ShapeDtypeStruct((t, qk), BF16),
                   jax.ShapeDtypeStruct((t // CHUNK, qk, CHUNK), BF16),
                   jax.ShapeDtypeStruct((t, vw), BF16),
                   jax.ShapeDtypeStruct((t, vw), BF16),
                   jax.ShapeDtypeStruct((t // seq, N_GROUPS, nblk, (seq // CHUNK) * FFT_NB, GROUP_DIM), F32),
                   jax.ShapeDtypeStruct((2, t // CHUNK * 8, CHUNK), F32),
                   jax.ShapeDtypeStruct((2, t // CHUNK * 8, CHUNK), F32)]
                  + [jax.ShapeDtypeStruct(a.shape, BF16) for a in casts],
        compiler_params=_vmem_params(("parallel",), 48),
        name="inproj",
    )(x2d, mod4, mod4, *[w[k] for k in names], *casts)


def _gate_prep_kernel(gi_ref, gf_ref, ra_ref, rb_ref, rc_ref, bend_scr, amax_scr, mprev_scr):
    rows = gi_ref.shape[1]
    n_chunks = rows // 8
    lane = lax.broadcasted_iota(jnp.int32, (rows, CHUNK), 1)
    first = (lax.broadcasted_iota(jnp.int32, (rows, CHUNK), 0) & 7) < N_HEADS
    src = lax.broadcasted_iota(jnp.int32, (CHUNK, 2 * CHUNK), 0)
    dst = lax.broadcasted_iota(jnp.int32, (CHUNK, 2 * CHUNK), 1)
    for d in range(2):
        feeds = (src <= dst) if d == 0 else (src >= dst)
        cum_and_total = jnp.where((dst >= CHUNK) | feeds, 1.0, 0.0).astype(BF16)
        bc = sum(_dot(p, cum_and_total) for p in _split3(_log_sigmoid(gf_ref[d])))
        b, b_end = bc[:, :CHUNK], bc[:, CHUNK:]
        a = gi_ref[d] - b
        cm = a
        for k in range(7):
            sh = 1 << k
            if d == 0:
                cm = jnp.maximum(cm, jnp.where(lane >= sh, pltpu.roll(cm, sh, 1), MASKED))
            else:
                cm = jnp.maximum(cm, jnp.where(lane < CHUNK - sh, pltpu.roll(cm, CHUNK - sh, 1), MASKED))
        a_max = jnp.broadcast_to(jnp.max(a, axis=1, keepdims=True), a.shape)
        bend_scr[...] = b_end
        amax_scr[...] = a_max

        def scan(c, m):
            r = pl.ds(pl.multiple_of((c if d == 0 else n_chunks - 1 - c) * 8, 8), 8)
            mprev_scr[r, :] = m
            return bend_scr[r, :] + jnp.maximum(m, amax_scr[r, :])

        lax.fori_loop(0, n_chunks, scan, jnp.zeros((8, CHUNK), F32))
        m_prev = mprev_scr[...]
        u = -jnp.maximum(m_prev, cm)
        m_new = b_end + jnp.maximum(m_prev, a_max)
        ra_ref[d] = jnp.where(first, a, jnp.exp(b_end + a - m_new))
        rb_ref[d] = jnp.where(first, u, jnp.exp(m_prev + u))
        rc_ref[d] = jnp.where(first, u - b, jnp.exp(b_end + m_prev - m_new))


def _gate_prep(gi, gf, bsz):
    rows = gi.shape[1] // bsz
    spec = pl.BlockSpec((2, rows, CHUNK), lambda b: (0, b, 0))
    shape = jax.ShapeDtypeStruct(gi.shape, F32)
    return pl.pallas_call(
        _gate_prep_kernel,
        grid=(bsz,),
        in_specs=[spec, spec],
        out_specs=[spec, spec, spec],
        out_shape=[shape, shape, shape],
        scratch_shapes=[pltpu.VMEM((rows, CHUNK), F32)] * 3,
        compiler_params=_vmem_params(("parallel",), 32),
        name="gate_prep",
    )(gi, gf)


@functools.lru_cache(maxsize=None)
def _column_selector():
    sel = np.zeros((48, N_HEADS * 3 * CHUNK), np.float32)
    for hd in range(N_HEADS):
        for j, (arr, row) in enumerate(((0, hd), (0, N_HEADS + hd), (1, hd))):
            for piece in range(3):
                sel[arr * 24 + piece * 8 + row, (3 * hd + j) * CHUNK:(3 * hd + j + 1) * CHUNK] = 1.0
    return sel


def _mlstm_kernel(sel_ref, qf_ref, ktf_ref, vf_ref, raf_ref, rbf_ref, rcf_ref,
                  qb_ref, ktb_ref, vb_ref, rab_ref, rbb_ref, rcb_ref, hf_ref, hb_ref, c_scr):
    @pl.when(pl.program_id(1) == 0)
    def _():
        c_scr[...] = jnp.zeros_like(c_scr)

    cpb = ktf_ref.shape[0]
    t_i = lax.broadcasted_iota(jnp.int32, (CHUNK, CHUNK), 0)
    s_i = lax.broadcasted_iota(jnp.int32, (CHUNK, CHUNK), 1)
    visible = (s_i <= t_i, s_i >= t_i)
    ones = jnp.ones((CHUNK, V_DIM), BF16)
    kzero = jnp.zeros((QK_DIM, CHUNK), BF16)
    dirs = ((qf_ref, ktf_ref, vf_ref, raf_ref, rbf_ref, rcf_ref, hf_ref),
            (qb_ref, ktb_ref, vb_ref, rab_ref, rbb_ref, rcb_ref, hb_ref))

    def body(i, carry):
        units = []
        for d, (q_ref, kt_ref, v_ref, ra_ref, rb_ref, rc_ref, h_ref) in enumerate(dirs):
            li = i if d == 0 else cpb - 1 - i
            r8 = pl.ds(pl.multiple_of(li * 8, 8), 8)
            rl = pl.ds(pl.multiple_of(li * CHUNK, CHUNK), CHUNK)
            ra = ra_ref[r8, :]
            rc = rc_ref[r8, :]
            pieces = [p.astype(F32) for arr in (rb_ref[r8, :], rc) for p in _split3(arr)]
            colb = lax.dot_general(jnp.concatenate(pieces, axis=0).astype(BF16), sel_ref[...],
                                   (((0,), (0,)), ((), ())), preferred_element_type=F32)
            kt = kt_ref[li]
            for pair in range(N_HEADS // 2):
                k_a = kt[2 * pair * QK_DIM:(2 * pair + 1) * QK_DIM, :]
                k_b = kt[(2 * pair + 1) * QK_DIM:(2 * pair + 2) * QK_DIM, :]
                kdiag = jnp.concatenate([jnp.concatenate([k_a, kzero], axis=1),
                                         jnp.concatenate([kzero, k_b], axis=1)], axis=0)
                qk2 = _dot(q_ref[rl, 2 * pair * QK_DIM:2 * (pair + 1) * QK_DIM], kdiag)
                for hd in (2 * pair, 2 * pair + 1):
                    qh = q_ref[rl, hd * QK_DIM:(hd + 1) * QK_DIM]
                    kth = kt[hd * QK_DIM:(hd + 1) * QK_DIM, :]
                    units.append(dict(d=d, hd=hd, rl=rl, h_ref=h_ref, ra=ra, rc=rc, colb=colb, qh=qh, kth=kth,
                                      v_ref=v_ref, qk=qk2[:, (hd % 2) * CHUNK:(hd % 2 + 1) * CHUNK]))
        for u in units:
            d, hd, colb = u["d"], u["hd"], u["colb"]
            c0 = 3 * hd * CHUNK
            u_b = colb[:, c0:c0 + CHUNK]
            w_inter = colb[:, c0 + CHUNK:c0 + CHUNK + QK_DIM]
            w_intra = jnp.exp(jnp.where(visible[d], u_b + u["ra"][hd:hd + 1, :], MASKED))
            s = (u["qk"] * w_intra).astype(BF16)
            q_inter = (u["qh"].astype(F32) * w_inter).astype(BF16)
            u["c_prev"] = c_scr[d, hd]
            u["v_aug"] = jnp.concatenate([u["v_ref"][u["rl"], hd * V_DIM:(hd + 1) * V_DIM], ones], axis=1)
            u["nd"] = _dot(jnp.concatenate([s, q_inter], axis=1),
                           jnp.concatenate([u["v_aug"], u["c_prev"].astype(BF16)], axis=0))
        for u in units:
            hd, nd = u["hd"], u["nd"]
            exp_neg_m = jnp.exp(u["colb"][:, (3 * hd + 2) * CHUNK:(3 * hd + 3) * CHUNK])
            den = jnp.maximum(jnp.abs(nd[:, V_DIM:]), exp_neg_m)
            u["h_ref"][u["rl"], hd * V_DIM:(hd + 1) * V_DIM] = (nd[:, :V_DIM] / den).astype(u["h_ref"].dtype)
            wk_row = u["ra"][N_HEADS + hd:N_HEADS + hd + 1, :]
            decay = jnp.broadcast_to(u["rc"][N_HEADS + hd:N_HEADS + hd + 1, :], (QK_DIM, CHUNK))
            kw = (u["kth"].astype(F32) * wk_row).astype(BF16)
            c_scr[u["d"], hd] = jnp.concatenate([decay, decay], axis=1) * u["c_prev"] + _dot(kw, u["v_aug"])
        return carry

    lax.fori_loop(0, cpb, body, 0, unroll=4)


def _mlstm(q, kt, v, ra, rb, rc, bsz, seq, cpb):
    t = q.shape[0]
    nblk = seq // (cpb * CHUNK)
    qk = N_HEADS * QK_DIM
    vw = N_HEADS * V_DIM
    fwd = lambda b, j: b * nblk + j
    bwd = lambda b, j: b * nblk + nblk - 1 - j

    def specs(blk, d):
        return [pl.BlockSpec((cpb * CHUNK, qk), lambda b, j: (blk(b, j), 0)),
                pl.BlockSpec((cpb, qk, CHUNK), lambda b, j: (blk(b, j), 0, 0)),
                pl.BlockSpec((cpb * CHUNK, vw), lambda b, j: (blk(b, j), 0))] + \
               [pl.BlockSpec((None, cpb * 8, CHUNK), lambda b, j: (d, blk(b, j), 0))] * 3

    sel = jnp.asarray(_column_selector()).astype(BF16)
    return pl.pallas_call(
        _mlstm_kernel,
        grid=(bsz, nblk),
        in_specs=[pl.BlockSpec(sel.shape, lambda b, j: (0, 0))] + specs(fwd, 0) + specs(bwd, 1),
        out_specs=[pl.BlockSpec((cpb * CHUNK, vw), lambda b, j: (fwd(b, j), 0)),
                   pl.BlockSpec((cpb * CHUNK, vw), lambda b, j: (bwd(b, j), 0))],
        out_shape=[jax.ShapeDtypeStruct((t, vw), BF16)] * 2,
        scratch_shapes=[pltpu.VMEM((2, N_HEADS, QK_DIM, 2 * V_DIM), F32)],
        compiler_params=_vmem_params(("parallel", "arbitrary"), 32),
        name="mlstm",
    )(sel, q, kt, v, ra, rb, rc, q, kt, v, ra, rb, rc)


def _fft_pitch(n1):
    return 8 * ((n1 // 4) | 1)


@functools.lru_cache(maxsize=None)
def _fft_tables(seq):
    n1 = seq // CHUNK
    two_pi = 2.0 * np.pi
    k1 = np.arange(n1, dtype=np.int64)
    n = 128 * np.arange(n1, dtype=np.int64)[None, None, :] + np.arange(128, dtype=np.int64)[:, None, None]
    ang = two_pi * ((k1[None, :, None] * n) % seq).astype(np.float64) / seq
    ga = np.concatenate([np.cos(ang), -np.sin(ang)], axis=1)
    j = np.arange(128, dtype=np.int64)
    a128 = two_pi * ((j[:, None] * j[None, :]) % 128).astype(np.float64) / 128.0
    ff = np.concatenate([np.cos(a128), -np.sin(a128)], axis=0)
    cs = np.concatenate([np.cos(a128), np.sin(a128)], axis=0) / np.sqrt(128.0 * seq)
    return tuple(np.asarray(a, dtype=np.float32) for a in (ga, ff, cs))


def _fft_kernel(z_ref, ga_ref, ff_ref, cs_ref, y_ref, scr):
    j = pl.program_id(2)
    n1 = ga_ref.shape[2]
    pitch = scr.shape[0] // CHUNK
    for sb in range(z_ref.shape[0]):
        for l in range(FFT_NB):
            n2 = (j * z_ref.shape[0] + sb) * FFT_NB + l
            zl = z_ref[sb, pl.ds(l, n1, stride=FFT_NB), :].astype(BF16)
            row0 = pl.multiple_of(n2 * pitch, 8)
            scr[pl.ds(row0, 2 * n1), :] = _dot(ga_ref[n2], zl)

    @pl.when(j == pl.num_programs(2) - 1)
    def _():
        ff = ff_ref[...]
        cs = cs_ref[...]

        def body(it, carry):
            k0s = [(it * FFT_GB + gi) * FFT_KB for gi in range(FFT_GB)]
            ms = []
            for k0 in k0s:
                p = [scr[pl.ds(ri * n1 + k0 + kk, CHUNK, stride=pitch), :]
                     for kk in range(FFT_KB) for ri in range(2)]
                ms.append(_dot(ff, jnp.concatenate(p, axis=1).astype(BF16)))
            ys = []
            for m in ms:
                x = []
                for kk in range(FFT_KB):
                    c0 = 2 * kk * GROUP_DIM
                    xr = m[:CHUNK, c0:c0 + GROUP_DIM] - m[CHUNK:, c0 + GROUP_DIM:c0 + 2 * GROUP_DIM]
                    xi = m[:CHUNK, c0 + GROUP_DIM:c0 + 2 * GROUP_DIM] + m[CHUNK:, c0:c0 + GROUP_DIM]
                    x.append(jnp.concatenate([xr, xi], axis=1))
                ys.append(_dot(jnp.concatenate(x, axis=0).astype(BF16), cs))
            for k0, y in zip(k0s, ys):
                for kk in range(FFT_KB):
                    row0 = pl.multiple_of((k0 + kk) * FFT_NB, 8)
                    for kb in range(CHUNK // FFT_NB):
                        r0 = kk * CHUNK + kb * FFT_NB
                        y_ref[kb, pl.ds(row0, FFT_NB), :] = y[r0:r0 + FFT_NB, :]
            return carry

        lax.fori_loop(0, n1 // (FFT_KB * FFT_GB), body, 0)


def _fourier(z5, seq):
    bsz = z5.shape[0]
    n1 = seq // CHUNK
    nblk = CHUNK // FFT_NB
    pitch = _fft_pitch(n1)
    ga, ff, cs = (jnp.asarray(a).astype(BF16) for a in _fft_tables(seq))
    return pl.pallas_call(
        _fft_kernel,
        grid=(bsz, N_GROUPS, nblk // FFT_SB),
        in_specs=[pl.BlockSpec((None, None, FFT_SB, n1 * FFT_NB, GROUP_DIM), lambda b, g, j: (b, g, j, 0, 0)),
                  pl.BlockSpec(ga.shape, lambda b, g, j: (0, 0, 0), pipeline_mode=pl.Buffered(1)),
                  pl.BlockSpec(ff.shape, lambda b, g, j: (0, 0)),
                  pl.BlockSpec(cs.shape, lambda b, g, j: (0, 0))],
        out_specs=pl.BlockSpec((None, None, nblk, n1 * FFT_NB, GROUP_DIM), lambda b, g, j: (b, g, 0, 0, 0)),
        out_shape=jax.ShapeDtypeStruct(z5.shape, F32),
        scratch_shapes=[pltpu.VMEM((CHUNK * pitch, GROUP_DIM), F32)],
        compiler_params=_vmem_params(("parallel", "parallel", "arbitrary"), 56),
        name="fft",
    )(z5, ga, ff, cs)


def _gather_rows(y_refs, k2_off, n1, kpt):
    rows = [jnp.concatenate([ref[pl.ds(k2_off + kk, n1, stride=FFT_NB), :] for ref in y_refs], axis=1)
            for kk in range(kpt)]
    return jnp.concatenate(rows, axis=0)


def _outmlp_kernel(alpha, ff_chunk, spb, x_ref, hf_ref, hb_ref, o_ref, y0_ref, y1_ref, y2_ref, y3_ref,
                   g1_ref, sc2_ref, sh2_ref, g2_ref,
                   nw_ref, wout_ref, l1g_ref, l1b_ref, w1_ref, b1_ref, w2_ref, b2_ref, l2g_ref, l2b_ref,
                   out_ref):
    n1 = y0_ref.shape[0] // FFT_NB
    kpt = x_ref.shape[0] // n1
    k2_off = ((pl.program_id(0) % spb) * kpt) % FFT_NB
    sizes = OUTMLP_SUBTILE_K2 if sum(OUTMLP_SUBTILE_K2) == kpt else (kpt,)
    subs, k2l = [], 0
    for kps in sizes:
        subs.append(dict(k2l=k2l, kps=kps, rs=slice(k2l * n1, (k2l + kps) * n1)))
        k2l += kps
    for s in subs:
        rs = s["rs"]
        yf = _gather_rows((y0_ref, y1_ref, y2_ref, y3_ref), k2_off + s["k2l"], n1, s["kps"]).astype(BF16)
        hsum = hf_ref[rs, :].astype(F32) + hb_ref[rs, :].astype(F32)
        heads = [_ln_plain(hsum[:, j * V_DIM:(j + 1) * V_DIM]) for j in range(N_HEADS)]
        ym = jnp.concatenate(heads, axis=1) * nw_ref[...] * jax.nn.sigmoid(o_ref[rs, :].astype(F32))
        s["mix"] = _dot(jnp.concatenate([ym.astype(BF16), yf], axis=1), wout_ref[...])
    for s in subs:
        x1 = _ln_plain(alpha * x_ref[s["rs"], :] + (1.0 + g1_ref[...]) * s["mix"]) * l1g_ref[...] + l1b_ref[...]
        s["x1"] = x1
        s["h2"] = (_ln_plain(x1) * (1.0 + sc2_ref[...]) + sh2_ref[...]).astype(BF16)
    for s in subs:
        ff = b2_ref[...]
        for j in range(w1_ref.shape[1] // ff_chunk):
            sl = slice(j * ff_chunk, (j + 1) * ff_chunk)
            hid = jnp.maximum(_dot(s["h2"], w1_ref[:, sl]) + b1_ref[:, sl], 0.0)
            ff = ff + _dot((hid * hid).astype(BF16), w2_ref[sl, :])
        s["ff"] = ff
    for s in subs:
        out_ref[s["rs"], :] = (_ln_plain(alpha * s["x1"] + (1.0 + g2_ref[...]) * s["ff"]) * l2g_ref[...]
                               + l2b_ref[...])


def _outmlp(x2d, hf, hb, o, y5, mod4, w, seq, tm, alpha, ff_chunk):
    t, d = x2d.shape
    spb = seq // tm
    n1 = seq // CHUNK
    kpt = tm // n1
    assert FFT_NB % kpt == 0
    vw = N_HEADS * V_DIM
    row = lambda width: pl.BlockSpec((tm, width), lambda i: (i, 0))
    yspec = lambda g: pl.BlockSpec((None, None, None, n1 * FFT_NB, GROUP_DIM),
                                   lambda i: (i // spb, g, ((i % spb) * kpt) // FFT_NB, 0, 0))
    modspec = lambda j: pl.BlockSpec((None, None, 1, d), lambda i: (i // spb, j, 0, 0))
    const = lambda a: pl.BlockSpec(a.shape, lambda i: (0,) * a.ndim, pipeline_mode=pl.Buffered(1))
    names = ["nw", "wout", "l1g", "l1b", "w1", "b1", "w2", "b2", "l2g", "l2b"]
    return pl.pallas_call(
        functools.partial(_outmlp_kernel, alpha, ff_chunk, spb),
        grid=(t // tm,),
        in_specs=[row(d), row(vw), row(vw), row(vw)]
                 + [yspec(g) for g in range(N_GROUPS)]
                 + [modspec(2), modspec(4), modspec(3), modspec(5)] + [const(w[k]) for k in names],
        out_specs=row(d),
        out_shape=jax.ShapeDtypeStruct((t, d), F32),
        compiler_params=_vmem_params(("parallel",), 56),
        name="outmlp",
    )(x2d, hf, hb, o, y5, y5, y5, y5, mod4, mod4, mod4, mod4, *[w[k] for k in names])


def _layer_weights(w_in, b_gate, w_out, w_ff1, w_ff2, b_ff1, b_ff2, mlstm_norm_w, ln1_g, ln1_b, ln2_g, ln2_b):
    qk = N_HEADS * QK_DIM
    vw = N_HEADS * V_DIM
    fw = N_GROUPS * GROUP_DIM
    o0, o1, o2, o3, o4 = qk, 2 * qk, 2 * qk + vw, 2 * qk + 2 * vw, 2 * qk + 2 * vw + fw
    wg = w_in[:, o4:].T.reshape(2, 2, N_HEADS, -1)
    bg = b_gate.astype(F32).reshape(2, 2, N_HEADS, 1)
    dup = lambda a: jnp.concatenate([a, a], axis=1)
    r = lambda a: a.astype(F32).reshape(1, -1)
    wq = w_in[:, :o0] * (QK_DIM ** -0.5)
    gate_rows = lambda a: jnp.concatenate([dup(a[:, 0]), dup(a[:, 1])], axis=1).reshape(32, -1)
    return {
        "wrow": jnp.concatenate([wq, w_in[:, o1:o4]], axis=1).astype(BF16),
        "wlane": jnp.concatenate([w_in[:, o0:o1].T, gate_rows(wg)], axis=0).astype(BF16),
        "bg": gate_rows(bg),
        "nw": r(mlstm_norm_w), "wout": w_out, "l1g": r(ln1_g), "l1b": r(ln1_b),
        "w1": w_ff1, "b1": r(b_ff1), "w2": w_ff2, "b2": r(b_ff2),
        "l2g": r(ln2_g), "l2b": r(ln2_b),
    }


def kernel(x, c, w_ada, b_ada, w_in, b_gate, mlstm_norm_w, w_out, ln1_g, ln1_b,
           w_ff1, b_ff1, w_ff2, b_ff2, ln2_g, ln2_b):
    bsz, seq, d = x.shape
    depth = w_ada.shape[0]
    alpha = (2 * depth) ** 0.25
    assert seq % (MLSTM_CPB * CHUNK) == 0 and (seq // CHUNK) % 8 == 0
    assert d == N_HEADS * V_DIM + N_GROUPS * GROUP_DIM
    tm = min(512, FFT_NB * (seq // CHUNK))
    x2d = x.reshape(bsz * seq, d)
    for l in range(depth):
        w = _layer_weights(w_in[l], b_gate[l], w_out[l], w_ff1[l], w_ff2[l], b_ff1[l], b_ff2[l],
                           mlstm_norm_w[l], ln1_g[l], ln1_b[l], ln2_g[l], ln2_b[l])
        mod4 = _adaln(c, w_ada[l], b_ada[l]).reshape(bsz, N_MOD, 1, d)
        q, kt, v, o, z5, gi, gf, w["wout"], w["w1"], w["w2"] = _inproj(x2d, mod4, w, seq, min(1024, seq))
        ra, rb, rc = _gate_prep(gi, gf, bsz)
        hf, hb = _mlstm(q, kt, v, ra, rb, rc, bsz, seq, MLSTM_CPB)
        y5 = _fourier(z5, seq)
        x2d = _outmlp(x2d, hf, hb, o, y5, mod4, w, seq, tm, alpha, ff_chunk=1024)
    return x2d.reshape(bsz, seq, d)
```

```python
import functools

import numpy as np
import jax
import jax.numpy as jnp
from jax import lax
from jax.experimental import pallas as pl
from jax.experimental.pallas import tpu as pltpu

F32 = jnp.float32
BF16 = jnp.bfloat16

CHUNK = 128
N_HEADS = 4
QK_DIM = 64
V_DIM = 128
N_GROUPS = 4
GROUP_DIM = 128
N_MOD = 6
LN_EPS = 1e-5
MASKED = -1e30
FFT_NB = 8
FFT_KB = 4
FFT_SB = 16
FFT_GB = 4
FFT_BSTEPS = 4
MLSTM_CPB = 8
OUTMLP_SUBTILE_K2 = (2, 2)
INPROJ_SUBTILES = 4

_NT = (((1,), (1,)), ((), ()))


def _dot(a, b):
    return jnp.dot(a, b, preferred_element_type=F32)


def _dot_nt(a, b):
    return lax.dot_general(a, b, _NT, preferred_element_type=F32)


def _ln_plain(x):
    mu = jnp.mean(x, axis=-1, keepdims=True)
    xc = x - mu
    var = jnp.mean(xc * xc, axis=-1, keepdims=True)
    return xc * lax.rsqrt(var + LN_EPS)


def _log_sigmoid(x):
    return jnp.minimum(x, 0.0) - jnp.log1p(jnp.exp(-jnp.abs(x)))


def _split3(x):
    hi = x.astype(BF16)
    r1 = x - hi.astype(F32)
    mid = r1.astype(BF16)
    lo = (r1 - mid.astype(F32)).astype(BF16)
    return hi, mid, lo


def _vmem_params(semantics, limit_mb):
    return pltpu.CompilerParams(dimension_semantics=semantics, vmem_limit_bytes=limit_mb * 1024 * 1024)


def _adaln_kernel(ct_ref, w_ref, b_ref, o_ref):
    ct = ct_ref[...]
    act = ct * jax.nn.sigmoid(ct)
    w = w_ref[...]
    for b in range(ct.shape[1]):
        o_ref[b:b + 1, :] = jnp.sum(act[:, b:b + 1] * w, axis=0, keepdims=True) + b_ref[...]


def _adaln(c, w_ada, b_ada):
    bsz, d = c.shape
    n = w_ada.shape[1]
    tn = 1024
    return pl.pallas_call(
        _adaln_kernel,
        grid=(n // tn,),
        in_specs=[pl.BlockSpec((d, bsz), lambda j: (0, 0)),
                  pl.BlockSpec((d, tn), lambda j: (0, j)),
                  pl.BlockSpec((1, tn), lambda j: (0, j))],
        out_specs=pl.BlockSpec((bsz, tn), lambda j: (0, j)),
        out_shape=jax.ShapeDtypeStruct((bsz, n), F32),
        compiler_params=_vmem_params(("parallel",), 32),
        name="adaln",
    )(c.T, w_ada, b_ada.reshape(1, n))


def _inproj_kernel(x_ref, sh_ref, sc_ref, wrow_ref, wlane_ref, bg_ref, wout_ref, w1_ref, w2_ref,
                   q_ref, kt_ref, v_ref, o_ref, z_ref, gi_ref, gf_ref, woutb_ref, w1b_ref, w2b_ref):
    woutb_ref[...] = wout_ref[...].astype(BF16)
    w1b_ref[...] = w1_ref[...].astype(BF16)
    w2b_ref[...] = w2_ref[...].astype(BF16)
    qk, vw = q_ref.shape[1], v_ref.shape[1]
    n_chunks = x_ref.shape[0] // CHUNK
    cps = max(1, n_chunks // INPROJ_SUBTILES)
    for c0 in range(0, n_chunks, cps):
        rs = slice(c0 * CHUNK, (c0 + cps) * CHUNK)
        h = _ln_plain(x_ref[rs, :]) * (1.0 + sc_ref[...]) + sh_ref[...]
        hb = h.astype(BF16)
        rowp = _dot(hb, wrow_ref[...])
        q_ref[rs, :] = rowp[:, :qk].astype(BF16)
        v_ref[rs, :] = rowp[:, qk:qk + vw].astype(BF16)
        o_ref[rs, :] = rowp[:, qk + vw:qk + 2 * vw].astype(BF16)
        fz = rowp[:, qk + 2 * vw:]
        for j in range(cps):
            for g in range(N_GROUPS):
                for jb in range(CHUNK // FFT_NB):
                    r0 = j * CHUNK + jb * FFT_NB
                    z_ref[g, jb, (c0 + j) * FFT_NB:(c0 + j + 1) * FFT_NB, :] = (
                        fz[r0:r0 + FFT_NB, g * GROUP_DIM:(g + 1) * GROUP_DIM])
        lanep = _dot_nt(wlane_ref[...], hb)
        gates = lanep[qk:, :] + bg_ref[...]
        for j in range(cps):
            cl = slice(j * CHUNK, (j + 1) * CHUNK)
            kt_ref[c0 + j] = lanep[:qk, cl].astype(BF16)
            for d in range(2):
                gi_ref[d, (c0 + j) * 8:(c0 + j + 1) * 8, :] = gates[16 * d:16 * d + 8, cl]
                gf_ref[d, (c0 + j) * 8:(c0 + j + 1) * 8, :] = gates[16 * d + 8:16 * d + 16, cl]


def _inproj(x2d, mod4, w, seq, tm):
    t, d = x2d.shape
    spb = seq // tm
    qk = N_HEADS * QK_DIM
    vw = N_HEADS * V_DIM
    nblk = CHUNK // FFT_NB
    cpt = tm // CHUNK
    full = lambda a: pl.BlockSpec(a.shape, lambda i: (0,) * a.ndim)
    modspec = lambda j: pl.BlockSpec((None, None, 1, d), lambda i: (i // spb, j, 0, 0))
    names = ["wrow", "wlane", "bg"]
    steps = t // tm
    casts = [w["wout"], w["w1"], w["w2"]]
    assert all(a.shape[0] % (16 * steps) == 0 for a in casts)
    slab = lambda a: pl.BlockSpec((a.shape[0] // steps, a.shape[1]), lambda i: (i, 0))
    return pl.pallas_call(
        _inproj_kernel,
        grid=(steps,),
        in_specs=[pl.BlockSpec((tm, d), lambda i: (i, 0)), modspec(0), modspec(1)] + [full(w[k]) for k in names]
                 + [slab(a) for a in casts],
        out_specs=[pl.BlockSpec((tm, qk), lambda i: (i, 0)),
                   pl.BlockSpec((cpt, qk, CHUNK), lambda i: (i, 0, 0)),
                   pl.BlockSpec((tm, vw), lambda i: (i, 0)),
                   pl.BlockSpec((tm, vw), lambda i: (i, 0)),
                   pl.BlockSpec((None, N_GROUPS, nblk, cpt * FFT_NB, GROUP_DIM),
                                lambda i: (i // spb, 0, 0, i % spb, 0)),
                   pl.BlockSpec((2, cpt * 8, CHUNK), lambda i: (0, i, 0)),
                   pl.BlockSpec((2, cpt * 8, CHUNK), lambda i: (0, i, 0))] + [slab(a) for a in casts],
        out_shape=[jax.ShapeDtypeStruct((t, qk), BF16),
                   jax.ShapeDtypeStruct((t // CHUNK, qk, CHUNK), BF16),
                   jax.ShapeDtypeStruct((t, vw), BF16),
                   jax.ShapeDtypeStruct((t, vw), BF16),
                   jax.ShapeDtypeStruct((t // seq, N_GROUPS, nblk, (seq // CHUNK) * FFT_NB, GROUP_DIM), F32),
                   jax.ShapeDtypeStruct((2, t // CHUNK * 8, CHUNK), F32),
                   jax.ShapeDtypeStruct((2, t // CHUNK * 8, CHUNK), F32)]
                  + [jax.ShapeDtypeStruct(a.shape, BF16) for a in casts],
        compiler_params=_vmem_params(("parallel",), 48),
        name="inproj",
    )(x2d, mod4, mod4, *[w[k] for k in names], *casts)


def _gate_prep_kernel(gi_ref, gf_ref, ra_ref, rb_ref, rc_ref, bend_scr, amax_scr, mprev_scr):
    rows = gi_ref.shape[1]
    n_chunks = rows // 8
    lane = lax.broadcasted_iota(jnp.int32, (rows, CHUNK), 1)
    first = (lax.broadcasted_iota(jnp.int32, (rows, CHUNK), 0) & 7) < N_HEADS
    src = lax.broadcasted_iota(jnp.int32, (CHUNK, 2 * CHUNK), 0)
    dst = lax.broadcasted_iota(jnp.int32, (CHUNK, 2 * CHUNK), 1)
    for d in range(2):
        feeds = (src <= dst) if d == 0 else (src >= dst)
        cum_and_total = jnp.where((dst >= CHUNK) | feeds, 1.0, 0.0).astype(BF16)
        bc = sum(_dot(p, cum_and_total) for p in _split3(_log_sigmoid(gf_ref[d])))
        b, b_end = bc[:, :CHUNK], bc[:, CHUNK:]
        a = gi_ref[d] - b
        cm = a
        for k in range(7):
            sh = 1 << k
            if d == 0:
                cm = jnp.maximum(cm, jnp.where(lane >= sh, pltpu.roll(cm, sh, 1), MASKED))
            else:
                cm = jnp.maximum(cm, jnp.where(lane < CHUNK - sh, pltpu.roll(cm, CHUNK - sh, 1), MASKED))
        a_max = jnp.broadcast_to(jnp.max(a, axis=1, keepdims=True), a.shape)
        bend_scr[...] = b_end
        amax_scr[...] = a_max

        def scan(c, m):
            r = pl.ds(pl.multiple_of((c if d == 0 else n_chunks - 1 - c) * 8, 8), 8)
            mprev_scr[r, :] = m
            return bend_scr[r, :] + jnp.maximum(m, amax_scr[r, :])

        lax.fori_loop(0, n_chunks, scan, jnp.zeros((8, CHUNK), F32))
        m_prev = mprev_scr[...]
        u = -jnp.maximum(m_prev, cm)
        m_new = b_end + jnp.maximum(m_prev, a_max)
        ra_ref[d] = jnp.where(first, a, jnp.exp(b_end + a - m_new))
        rb_ref[d] = jnp.where(first, u, jnp.exp(m_prev + u))
        rc_ref[d] = jnp.where(first, u - b, jnp.exp(b_end + m_prev - m_new))


def _gate_prep(gi, gf, bsz):
    rows = gi.shape[1] // bsz
    spec = pl.BlockSpec((2, rows, CHUNK), lambda b: (0, b, 0))
    shape = jax.ShapeDtypeStruct(gi.shape, F32)
    return pl.pallas_call(
        _gate_prep_kernel,
        grid=(bsz,),
        in_specs=[spec, spec],
        out_specs=[spec, spec, spec],
        out_shape=[shape, shape, shape],
        scratch_shapes=[pltpu.VMEM((rows, CHUNK), F32)] * 3,
        compiler_params=_vmem_params(("parallel",), 32),
        name="gate_prep",
    )(gi, gf)


@functools.lru_cache(maxsize=None)
def _column_selector():
    sel = np.zeros((48, N_HEADS * 3 * CHUNK), np.float32)
    for hd in range(N_HEADS):
        for j, (arr, row) in enumerate(((0, hd), (0, N_HEADS + hd), (1, hd))):
            for piece in range(3):
                sel[arr * 24 + piece * 8 + row, (3 * hd + j) * CHUNK:(3 * hd + j + 1) * CHUNK] = 1.0
    return sel


def _mlstm_kernel(sel_ref, qf_ref, ktf_ref, vf_ref, raf_ref, rbf_ref, rcf_ref,
                  qb_ref, ktb_ref, vb_ref, rab_ref, rbb_ref, rcb_ref, hf_ref, hb_ref, c_scr):
    @pl.when(pl.program_id(1) == 0)
    def _():
        c_scr[...] = jnp.zeros_like(c_scr)

    cpb = ktf_ref.shape[0]
    t_i = lax.broadcasted_iota(jnp.int32, (CHUNK, CHUNK), 0)
    s_i = lax.broadcasted_iota(jnp.int32, (CHUNK, CHUNK), 1)
    visible = (s_i <= t_i, s_i >= t_i)
    ones = jnp.ones((CHUNK, V_DIM), BF16)
    kzero = jnp.zeros((QK_DIM, CHUNK), BF16)
    dirs = ((qf_ref, ktf_ref, vf_ref, raf_ref, rbf_ref, rcf_ref, hf_ref),
            (qb_ref, ktb_ref, vb_ref, rab_ref, rbb_ref, rcb_ref, hb_ref))

    def body(i, carry):
        units = []
        for d, (q_ref, kt_ref, v_ref, ra_ref, rb_ref, rc_ref, h_ref) in enumerate(dirs):
            li = i if d == 0 else cpb - 1 - i
            r8 = pl.ds(pl.multiple_of(li * 8, 8), 8)
            rl = pl.ds(pl.multiple_of(li * CHUNK, CHUNK), CHUNK)
            ra = ra_ref[r8, :]
            rc = rc_ref[r8, :]
            pieces = [p.astype(F32) for arr in (rb_ref[r8, :], rc) for p in _split3(arr)]
            colb = lax.dot_general(jnp.concatenate(pieces, axis=0).astype(BF16), sel_ref[...],
                                   (((0,), (0,)), ((), ())), preferred_element_type=F32)
            kt = kt_ref[li]
            for pair in range(N_HEADS // 2):
                k_a = kt[2 * pair * QK_DIM:(2 * pair + 1) * QK_DIM, :]
                k_b = kt[(2 * pair + 1) * QK_DIM:(2 * pair + 2) * QK_DIM, :]
                kdiag = jnp.concatenate([jnp.concatenate([k_a, kzero], axis=1),
                                         jnp.concatenate([kzero, k_b], axis=1)], axis=0)
                qk2 = _dot(q_ref[rl, 2 * pair * QK_DIM:2 * (pair + 1) * QK_DIM], kdiag)
                for hd in (2 * pair, 2 * pair + 1):
                    qh = q_ref[rl, hd * QK_DIM:(hd + 1) * QK_DIM]
                    kth = kt[hd * QK_DIM:(hd + 1) * QK_DIM, :]
                    units.append(dict(d=d, hd=hd, rl=rl, h_ref=h_ref, ra=ra, rc=rc, colb=colb, qh=qh, kth=kth,
                                      v_ref=v_ref, qk=qk2[:, (hd % 2) * CHUNK:(hd % 2 + 1) * CHUNK]))
        for u in units:
            d, hd, colb = u["d"], u["hd"], u["colb"]
            c0 = 3 * hd * CHUNK
            u_b = colb[:, c0:c0 + CHUNK]
            w_inter = colb[:, c0 + CHUNK:c0 + CHUNK + QK_DIM]
            w_intra = jnp.exp(jnp.where(visible[d], u_b + u["ra"][hd:hd + 1, :], MASKED))
            s = (u["qk"] * w_intra).astype(BF16)
            q_inter = (u["qh"].astype(F32) * w_inter).astype(BF16)
            u["c_prev"] = c_scr[d, hd]
            u["v_aug"] = jnp.concatenate([u["v_ref"][u["rl"], hd * V_DIM:(hd + 1) * V_DIM], ones], axis=1)
            u["nd"] = _dot(jnp.concatenate([s, q_inter], axis=1),
                           jnp.concatenate([u["v_aug"], u["c_prev"].astype(BF16)], axis=0))
        for u in units:
            hd, nd = u["hd"], u["nd"]
            exp_neg_m = jnp.exp(u["colb"][:, (3 * hd + 2) * CHUNK:(3 * hd + 3) * CHUNK])
            den = jnp.maximum(jnp.abs(nd[:, V_DIM:]), exp_neg_m)
            u["h_ref"][u["rl"], hd * V_DIM:(hd + 1) * V_DIM] = (nd[:, :V_DIM] / den).astype(u["h_ref"].dtype)
            wk_row = u["ra"][N_HEADS + hd:N_HEADS + hd + 1, :]
            decay = jnp.broadcast_to(u["rc"][N_HEADS + hd:N_HEADS + hd + 1, :], (QK_DIM, CHUNK))
            kw = (u["kth"].astype(F32) * wk_row).astype(BF16)
            c_scr[u["d"], hd] = jnp.concatenate([decay, decay], axis=1) * u["c_prev"] + _dot(kw, u["v_aug"])
        return carry

    lax.fori_loop(0, cpb, body, 0, unroll=4)


def _mlstm(q, kt, v, ra, rb, rc, bsz, seq, cpb):
    t = q.shape[0]
    nblk = seq // (cpb * CHUNK)
    qk = N_HEADS * QK_DIM
    vw = N_HEADS * V_DIM
    fwd = lambda b, j: b * nblk + j
    bwd = lambda b, j: b * nblk + nblk - 1 - j

    def specs(blk, d):
        return [pl.BlockSpec((cpb * CHUNK, qk), lambda b, j: (blk(b, j), 0)),
                pl.BlockSpec((cpb, qk, CHUNK), lambda b, j: (blk(b, j), 0, 0)),
                pl.BlockSpec((cpb * CHUNK, vw), lambda b, j: (blk(b, j), 0))] + \
               [pl.BlockSpec((None, cpb * 8, CHUNK), lambda b, j: (d, blk(b, j), 0))] * 3

    sel = jnp.asarray(_column_selector()).astype(BF16)
    return pl.pallas_call(
        _mlstm_kernel,
        grid=(bsz, nblk),
        in_specs=[pl.BlockSpec(sel.shape, lambda b, j: (0, 0))] + specs(fwd, 0) + specs(bwd, 1),
        out_specs=[pl.BlockSpec((cpb * CHUNK, vw), lambda b, j: (fwd(b, j), 0)),
                   pl.BlockSpec((cpb * CHUNK, vw), lambda b, j: (bwd(b, j), 0))],
        out_shape=[jax.ShapeDtypeStruct((t, vw), BF16)] * 2,
        scratch_shapes=[pltpu.VMEM((2, N_HEADS, QK_DIM, 2 * V_DIM), F32)],
        compiler_params=_vmem_params(("parallel", "arbitrary"), 32),
        name="mlstm",
    )(sel, q, kt, v, ra, rb, rc, q, kt, v, ra, rb, rc)


def _fft_pitch(n1):
    return 8 * ((n1 // 4) | 1)


@functools.lru_cache(maxsize=None)
def _fft_tables(seq):
    n1 = seq // CHUNK
    two_pi = 2.0 * np.pi
    k1 = np.arange(n1, dtype=np.int64)
    n = 128 * np.arange(n1, dtype=np.int64)[None, None, :] + np.arange(128, dtype=np.int64)[:, None, None]
    ang = two_pi * ((k1[None, :, None] * n) % seq).astype(np.float64) / seq
    ga = np.concatenate([np.cos(ang), -np.sin(ang)], axis=1)
    j = np.arange(128, dtype=np.int64)
    a128 = two_pi * ((j[:, None] * j[None, :]) % 128).astype(np.float64) / 128.0
    ff = np.concatenate([np.cos(a128), -np.sin(a128)], axis=0)
    cs = np.concatenate([np.cos(a128), np.sin(a128)], axis=0) / np.sqrt(128.0 * seq)
    return tuple(np.asarray(a, dtype=np.float32) for a in (ga, ff, cs))


def _fft_kernel(z_ref, ga_ref, ff_ref, cs_ref, y_ref, scr):
    j = pl.program_id(2)
    n1 = ga_ref.shape[2]
    pitch = scr.shape[0] // CHUNK
    a_steps = CHUNK // (FFT_NB * z_ref.shape[0])
    k1_per_step = y_ref.shape[1] // FFT_NB

    @pl.when(j < a_steps)
    def _():
        for sb in range(z_ref.shape[0]):
            for l in range(FFT_NB):
                n2 = (j * z_ref.shape[0] + sb) * FFT_NB + l
                zl = z_ref[sb, pl.ds(l, n1, stride=FFT_NB), :].astype(BF16)
                row0 = pl.multiple_of(n2 * pitch, 8)
                scr[pl.ds(row0, 2 * n1), :] = _dot(ga_ref[n2], zl)

    @pl.when(j >= a_steps)
    def _():
        ff = ff_ref[...]
        cs = cs_ref[...]
        gb = max(1, min(FFT_GB, k1_per_step // FFT_KB))

        def body(it, carry):
            l0s = [(it * gb + gi) * FFT_KB for gi in range(gb)]
            ms = []
            for l0 in l0s:
                k0 = (j - a_steps) * k1_per_step + l0
                p = [scr[pl.ds(ri * n1 + k0 + kk, CHUNK, stride=pitch), :]
                     for kk in range(FFT_KB) for ri in range(2)]
                ms.append(_dot(ff, jnp.concatenate(p, axis=1).astype(BF16)))
            ys = []
            for m in ms:
                x = []
                for kk in range(FFT_KB):
                    c0 = 2 * kk * GROUP_DIM
                    xr = m[:CHUNK, c0:c0 + GROUP_DIM] - m[CHUNK:, c0 + GROUP_DIM:c0 + 2 * GROUP_DIM]
                    xi = m[:CHUNK, c0 + GROUP_DIM:c0 + 2 * GROUP_DIM] + m[CHUNK:, c0:c0 + GROUP_DIM]
                    x.append(jnp.concatenate([xr, xi], axis=1))
                ys.append(_dot(jnp.concatenate(x, axis=0).astype(BF16), cs))
            for l0, y in zip(l0s, ys):
                for kk in range(FFT_KB):
                    row0 = pl.multiple_of((l0 + kk) * FFT_NB, 8)
                    for kb in range(CHUNK // FFT_NB):
                        r0 = kk * CHUNK + kb * FFT_NB
                        y_ref[kb, pl.ds(row0, FFT_NB), :] = y[r0:r0 + FFT_NB, :]
            return carry

        lax.fori_loop(0, k1_per_step // (FFT_KB * gb), body, 0)


def _fourier(z5, seq):
    bsz = z5.shape[0]
    n1 = seq // CHUNK
    nblk = CHUNK // FFT_NB
    a_steps = nblk // FFT_SB
    pitch = _fft_pitch(n1)
    ga, ff, cs = (jnp.asarray(a).astype(BF16) for a in _fft_tables(seq))
    return pl.pallas_call(
        _fft_kernel,
        grid=(bsz, N_GROUPS, a_steps + FFT_BSTEPS),
        in_specs=[pl.BlockSpec((None, None, FFT_SB, n1 * FFT_NB, GROUP_DIM),
                               lambda b, g, j: (b, g, jnp.minimum(j, a_steps - 1), 0, 0)),
                  pl.BlockSpec(ga.shape, lambda b, g, j: (0, 0, 0), pipeline_mode=pl.Buffered(1)),
                  pl.BlockSpec(ff.shape, lambda b, g, j: (0, 0)),
                  pl.BlockSpec(cs.shape, lambda b, g, j: (0, 0))],
        out_specs=pl.BlockSpec((None, None, nblk, n1 * FFT_NB // FFT_BSTEPS, GROUP_DIM),
                               lambda b, g, j: (b, g, 0, jnp.maximum(j - a_steps, 0), 0)),
        out_shape=jax.ShapeDtypeStruct(z5.shape, F32),
        scratch_shapes=[pltpu.VMEM((CHUNK * pitch, GROUP_DIM), F32)],
        compiler_params=_vmem_params(("parallel", "parallel", "arbitrary"), 56),
        name="fft",
    )(z5, ga, ff, cs)


def _gather_rows(y_refs, k2_off, n1, kpt):
    rows = [jnp.concatenate([ref[pl.ds(k2_off + kk, n1, stride=FFT_NB), :] for ref in y_refs], axis=1)
            for kk in range(kpt)]
    return jnp.concatenate(rows, axis=0)


def _outmlp_kernel(alpha, ff_chunk, spb, x_ref, hf_ref, hb_ref, o_ref, y0_ref, y1_ref, y2_ref, y3_ref,
                   g1_ref, sc2_ref, sh2_ref, g2_ref,
                   nw_ref, wout_ref, l1g_ref, l1b_ref, w1_ref, b1_ref, w2_ref, b2_ref, l2g_ref, l2b_ref,
                   out_ref):
    n1 = y0_ref.shape[0] // FFT_NB
    kpt = x_ref.shape[0] // n1
    k2_off = ((pl.program_id(0) % spb) * kpt) % FFT_NB
    sizes = OUTMLP_SUBTILE_K2 if sum(OUTMLP_SUBTILE_K2) == kpt else (kpt,)
    subs, k2l = [], 0
    for kps in sizes:
        subs.append(dict(k2l=k2l, kps=kps, rs=slice(k2l * n1, (k2l + kps) * n1)))
        k2l += kps
    for s in subs:
        rs = s["rs"]
        yf = _gather_rows((y0_ref, y1_ref, y2_ref, y3_ref), k2_off + s["k2l"], n1, s["kps"]).astype(BF16)
        hsum = hf_ref[rs, :].astype(F32) + hb_ref[rs, :].astype(F32)
        heads = [_ln_plain(hsum[:, j * V_DIM:(j + 1) * V_DIM]) for j in range(N_HEADS)]
        ym = jnp.concatenate(heads, axis=1) * nw_ref[...] * jax.nn.sigmoid(o_ref[rs, :].astype(F32))
        s["mix"] = _dot(jnp.concatenate([ym.astype(BF16), yf], axis=1), wout_ref[...])
    for s in subs:
        x1 = _ln_plain(alpha * x_ref[s["rs"], :] + (1.0 + g1_ref[...]) * s["mix"]) * l1g_ref[...] + l1b_ref[...]
        s["x1"] = x1
        s["h2"] = (_ln_plain(x1) * (1.0 + sc2_ref[...]) + sh2_ref[...]).astype(BF16)
    for s in subs:
        ff = b2_ref[...]
        for j in range(w1_ref.shape[1] // ff_chunk):
            sl = slice(j * ff_chunk, (j + 1) * ff_chunk)
            hid = jnp.maximum(_dot(s["h2"], w1_ref[:, sl]) + b1_ref[:, sl], 0.0)
            ff = ff + _dot((hid * hid).astype(BF16), w2_ref[sl, :])
        s["ff"] = ff
    for s in subs:
        out_ref[s["rs"], :] = (_ln_plain(alpha * s["x1"] + (1.0 + g2_ref[...]) * s["ff"]) * l2g_ref[...]
                               + l2b_ref[...])


def _outmlp(x2d, hf, hb, o, y5, mod4, w, seq, tm, alpha, ff_chunk):
    t, d = x2d.shape
    spb = seq // tm
    n1 = seq // CHUNK
    kpt = tm // n1
    assert FFT_NB % kpt == 0
    vw = N_HEADS * V_DIM
    row = lambda width: pl.BlockSpec((tm, width), lambda i: (i, 0))
    yspec = lambda g: pl.BlockSpec((None, None, None, n1 * FFT_NB, GROUP_DIM),
                                   lambda i: (i // spb, g, ((i % spb) * kpt) // FFT_NB, 0, 0))
    modspec = lambda j: pl.BlockSpec((None, None, 1, d), lambda i: (i // spb, j, 0, 0))
    const = lambda a: pl.BlockSpec(a.shape, lambda i: (0,) * a.ndim, pipeline_mode=pl.Buffered(1))
    names = ["nw", "wout", "l1g", "l1b", "w1", "b1", "w2", "b2", "l2g", "l2b"]
    return pl.pallas_call(
        functools.partial(_outmlp_kernel, alpha, ff_chunk, spb),
        grid=(t // tm,),
        in_specs=[row(d), row(vw), row(vw), row(vw)]
                 + [yspec(g) for g in range(N_GROUPS)]
                 + [modspec(2), modspec(4), modspec(3), modspec(5)] + [const(w[k]) for k in names],
        out_specs=row(d),
        out_shape=jax.ShapeDtypeStruct((t, d), F32),
        compiler_params=_vmem_params(("parallel",), 56),
        name="outmlp",
    )(x2d, hf, hb, o, y5, y5, y5, y5, mod4, mod4, mod4, mod4, *[w[k] for k in names])


def _layer_weights(w_in, b_gate, w_out, w_ff1, w_ff2, b_ff1, b_ff2, mlstm_norm_w, ln1_g, ln1_b, ln2_g, ln2_b):
    qk = N_HEADS * QK_DIM
    vw = N_HEADS * V_DIM
    fw = N_GROUPS * GROUP_DIM
    o0, o1, o2, o3, o4 = qk, 2 * qk, 2 * qk + vw, 2 * qk + 2 * vw, 2 * qk + 2 * vw + fw
    wg = w_in[:, o4:].T.reshape(2, 2, N_HEADS, -1)
    bg = b_gate.astype(F32).reshape(2, 2, N_HEADS, 1)
    dup = lambda a: jnp.concatenate([a, a], axis=1)
    r = lambda a: a.astype(F32).reshape(1, -1)
    wq = w_in[:, :o0] * (QK_DIM ** -0.5)
    gate_rows = lambda a: jnp.concatenate([dup(a[:, 0]), dup(a[:, 1])], axis=1).reshape(32, -1)
    return {
        "wrow": jnp.concatenate([wq, w_in[:, o1:o4]], axis=1).astype(BF16),
        "wlane": jnp.concatenate([w_in[:, o0:o1].T, gate_rows(wg)], axis=0).astype(BF16),
        "bg": gate_rows(bg),
        "nw": r(mlstm_norm_w), "wout": w_out, "l1g": r(ln1_g), "l1b": r(ln1_b),
        "w1": w_ff1, "b1": r(b_ff1), "w2": w_ff2, "b2": r(b_ff2),
        "l2g": r(ln2_g), "l2b": r(ln2_b),
    }


def kernel(x, c, w_ada, b_ada, w_in, b_gate, mlstm_norm_w, w_out, ln1_g, ln1_b,
           w_ff1, b_ff1, w_ff2, b_ff2, ln2_g, ln2_b):
    bsz, seq, d = x.shape
    depth = w_ada.shape[0]
    alpha = (2 * depth) ** 0.25
    assert seq % (MLSTM_CPB * CHUNK) == 0 and (seq // CHUNK) % 8 == 0
    assert d == N_HEADS * V_DIM + N_GROUPS * GROUP_DIM
    tm = min(512, FFT_NB * (seq // CHUNK))
    x2d = x.reshape(bsz * seq, d)
    for l in range(depth):
        w = _layer_weights(w_in[l], b_gate[l], w_out[l], w_ff1[l], w_ff2[l], b_ff1[l], b_ff2[l],
                           mlstm_norm_w[l], ln1_g[l], ln1_b[l], ln2_g[l], ln2_b[l])
        mod4 = _adaln(c, w_ada[l], b_ada[l]).reshape(bsz, N_MOD, 1, d)
        q, kt, v, o, z5, gi, gf, w["wout"], w["w1"], w["w2"] = _inproj(x2d, mod4, w, seq, min(1024, seq))
        ra, rb, rc = _gate_prep(gi, gf, bsz)
        hf, hb = _mlstm(q, kt, v, ra, rb, rc, bsz, seq, MLSTM_CPB)
        y5 = _fourier(z5, seq)
        x2d = _outmlp(x2d, hf, hb, o, y5, mod4, w, seq, tm, alpha, ff_chunk=1024)
    return x2d.reshape(bsz, seq, d)
```

```python
import functools

import numpy as np
import jax
import jax.numpy as jnp
from jax import lax
from jax.experimental import pallas as pl
from jax.experimental.pallas import tpu as pltpu

F32 = jnp.float32
BF16 = jnp.bfloat16

CHUNK = 128
N_HEADS = 4
QK_DIM = 64
V_DIM = 128
N_GROUPS = 4
GROUP_DIM = 128
N_MOD = 6
LN_EPS = 1e-5
MASKED = -1e30
FFT_NB = 8
FFT_KB = 4
FFT_SB = 16
FFT_GB = 4
FFT_BSTEPS = 4
MLSTM_CPB = 8
OUTMLP_SUBTILE_K2 = (2, 2)
INPROJ_SUBTILES = 4

_NT = (((1,), (1,)), ((), ()))


def _dot(a, b):
    return jnp.dot(a, b, preferred_element_type=F32)


def _dot_nt(a, b):
    return lax.dot_general(a, b, _NT, preferred_element_type=F32)


def _ln_plain(x):
    mu = jnp.mean(x, axis=-1, keepdims=True)
    xc = x - mu
    var = jnp.mean(xc * xc, axis=-1, keepdims=True)
    return xc * lax.rsqrt(var + LN_EPS)


def _log_sigmoid(x):
    return jnp.minimum(x, 0.0) - jnp.log1p(jnp.exp(-jnp.abs(x)))


def _split3(x):
    hi = x.astype(BF16)
    r1 = x - hi.astype(F32)
    mid = r1.astype(BF16)
    lo = (r1 - mid.astype(F32)).astype(BF16)
    return hi, mid, lo


def _vmem_params(semantics, limit_mb):
    return pltpu.CompilerParams(dimension_semantics=semantics, vmem_limit_bytes=limit_mb * 1024 * 1024)


def _adaln_kernel(ct_ref, w_ref, b_ref, o_ref):
    ct = ct_ref[...]
    act = ct * jax.nn.sigmoid(ct)
    w = w_ref[...]
    for b in range(ct.shape[1]):
        o_ref[b:b + 1, :] = jnp.sum(act[:, b:b + 1] * w, axis=0, keepdims=True) + b_ref[...]


def _adaln(c, w_ada, b_ada):
    bsz, d = c.shape
    n = w_ada.shape[1]
    tn = 1024
    return pl.pallas_call(
        _adaln_kernel,
        grid=(n // tn,),
        in_specs=[pl.BlockSpec((d, bsz), lambda j: (0, 0)),
                  pl.BlockSpec((d, tn), lambda j: (0, j)),
                  pl.BlockSpec((1, tn), lambda j: (0, j))],
        out_specs=pl.BlockSpec((bsz, tn), lambda j: (0, j)),
        out_shape=jax.ShapeDtypeStruct((bsz, n), F32),
        compiler_params=_vmem_params(("parallel",), 32),
        name="adaln",
    )(c.T, w_ada, b_ada.reshape(1, n))


def _inproj_kernel(x_ref, sh_ref, sc_ref, wrow_ref, wlane_ref, bg_ref, wout_ref, w1_ref, w2_ref,
                   q_ref, kt_ref, v_ref, o_ref, z_ref, gi_ref, gf_ref, woutb_ref, w1b_ref, w2b_ref):
    woutb_ref[...] = wout_ref[...].astype(BF16)
    w1b_ref[...] = w1_ref[...].astype(BF16)
    w2b_ref[...] = w2_ref[...].astype(BF16)
    qk, vw = q_ref.shape[1], v_ref.shape[1]
    n_chunks = x_ref.shape[0] // CHUNK
    cps = max(1, n_chunks // INPROJ_SUBTILES)
    for c0 in range(0, n_chunks, cps):
        rs = slice(c0 * CHUNK, (c0 + cps) * CHUNK)
        h = _ln_plain(x_ref[rs, :]) * (1.0 + sc_ref[...]) + sh_ref[...]
        hb = h.astype(BF16)
        rowp = _dot(hb, wrow_ref[...])
        q_ref[rs, :] = rowp[:, :qk].astype(BF16)
        v_ref[rs, :] = rowp[:, qk:qk + vw].astype(BF16)
        o_ref[rs, :] = rowp[:, qk + vw:qk + 2 * vw].astype(BF16)
        fz = rowp[:, qk + 2 * vw:]
        for j in range(cps):
            for g in range(N_GROUPS):
                for jb in range(CHUNK // FFT_NB):
                    r0 = j * CHUNK + jb * FFT_NB
                    z_ref[g, jb, (c0 + j) * FFT_NB:(c0 + j + 1) * FFT_NB, :] = (
                        fz[r0:r0 + FFT_NB, g * GROUP_DIM:(g + 1) * GROUP_DIM])
        lanep = _dot_nt(wlane_ref[...], hb)
        gates = lanep[qk:, :] + bg_ref[...]
        for j in range(cps):
            cl = slice(j * CHUNK, (j + 1) * CHUNK)
            kt_ref[c0 + j] = lanep[:qk, cl].astype(BF16)
            for d in range(2):
                gi_ref[d, (c0 + j) * 8:(c0 + j + 1) * 8, :] = gates[16 * d:16 * d + 8, cl]
                gf_ref[d, (c0 + j) * 8:(c0 + j + 1) * 8, :] = gates[16 * d + 8:16 * d + 16, cl]


def _inproj(x2d, mod4, w, seq, tm):
    t, d = x2d.shape
    spb = seq // tm
    qk = N_HEADS * QK_DIM
    vw = N_HEADS * V_DIM
    nblk = CHUNK // FFT_NB
    cpt = tm // CHUNK
    full = lambda a: pl.BlockSpec(a.shape, lambda i: (0,) * a.ndim)
    modspec = lambda j: pl.BlockSpec((None, None, 1, d), lambda i: (i // spb, j, 0, 0))
    names = ["wrow", "wlane", "bg"]
    steps = t // tm
    casts = [w["wout"], w["w1"], w["w2"]]
    assert all(a.shape[0] % (16 * steps) == 0 for a in casts)
    slab = lambda a: pl.BlockSpec((a.shape[0] // steps, a.shape[1]), lambda i: (i, 0))
    return pl.pallas_call(
        _inproj_kernel,
        grid=(steps,),
        in_specs=[pl.BlockSpec((tm, d), lambda i: (i, 0)), modspec(0), modspec(1)] + [full(w[k]) for k in names]
                 + [slab(a) for a in casts],
        out_specs=[pl.BlockSpec((tm, qk), lambda i: (i, 0)),
                   pl.BlockSpec((cpt, qk, CHUNK), lambda i: (i, 0, 0)),
                   pl.BlockSpec((tm, vw), lambda i: (i, 0)),
                   pl.BlockSpec((tm, vw), lambda i: (i, 0)),
                   pl.BlockSpec((None, N_GROUPS, nblk, cpt * FFT_NB, GROUP_DIM),
                                lambda i: (i // spb, 0, 0, i % spb, 0)),
                   pl.BlockSpec((2, cpt * 8, CHUNK), lambda i: (0, i, 0)),
                   pl.BlockSpec((2, cpt * 8, CHUNK), lambda i: (0, i, 0))] + [slab(a) for a in casts],
        out_shape=[jax.ShapeDtypeStruct((t, qk), BF16),
                   jax.ShapeDtypeStruct((t // CHUNK, qk, CHUNK), BF16),
                   jax.ShapeDtypeStruct((t, vw), BF16),
                   jax.ShapeDtypeStruct((t, vw), BF16),
                   jax.ShapeDtypeStruct((t // seq, N_GROUPS, nblk, (seq // CHUNK) * FFT_NB, GROUP_DIM), F32),
                   jax.ShapeDtypeStruct((2, t // CHUNK * 8, CHUNK), F32),
                   jax.ShapeDtypeStruct((2, t // CHUNK * 8, CHUNK), F32)]
                  + [jax.ShapeDtypeStruct(a.shape, BF16) for a in casts],
        compiler_params=_vmem_params(("parallel",), 48),
        name="inproj",
    )(x2d, mod4, mod4, *[w[k] for k in names], *casts)


def _gate_prep_kernel(gi_ref, gf_ref, ra_ref, rb_ref, rc_ref, bend_scr, amax_scr, mprev_scr):
    rows = gi_ref.shape[1]
    n_chunks = rows // 8
    lane = lax.broadcasted_iota(jnp.int32, (rows, CHUNK), 1)
    first = (lax.broadcasted_iota(jnp.int32, (rows, CHUNK), 0) & 7) < N_HEADS
    src = lax.broadcasted_iota(jnp.int32, (CHUNK, 2 * CHUNK), 0)
    dst = lax.broadcasted_iota(jnp.int32, (CHUNK, 2 * CHUNK), 1)
    for d in range(2):
        feeds = (src <= dst) if d == 0 else (src >= dst)
        cum_and_total = jnp.where((dst >= CHUNK) | feeds, 1.0, 0.0).astype(BF16)
        bc = sum(_dot(p, cum_and_total) for p in _split3(_log_sigmoid(gf_ref[d])))
        b, b_end = bc[:, :CHUNK], bc[:, CHUNK:]
        a = gi_ref[d] - b
        cm = a
        for k in range(7):
            sh = 1 << k
            if d == 0:
                cm = jnp.maximum(cm, jnp.where(lane >= sh, pltpu.roll(cm, sh, 1), MASKED))
            else:
                cm = jnp.maximum(cm, jnp.where(lane < CHUNK - sh, pltpu.roll(cm, CHUNK - sh, 1), MASKED))
        a_max = jnp.broadcast_to(jnp.max(a, axis=1, keepdims=True), a.shape)
        bend_scr[...] = b_end
        amax_scr[...] = a_max

        def scan(c, m):
            r = pl.ds(pl.multiple_of((c if d == 0 else n_chunks - 1 - c) * 8, 8), 8)
            mprev_scr[r, :] = m
            return bend_scr[r, :] + jnp.maximum(m, amax_scr[r, :])

        lax.fori_loop(0, n_chunks, scan, jnp.zeros((8, CHUNK), F32))
        m_prev = mprev_scr[...]
        u = -jnp.maximum(m_prev, cm)
        m_new = b_end + jnp.maximum(m_prev, a_max)
        ra_ref[d] = jnp.where(first, a, jnp.exp(b_end + a - m_new))
        rb_ref[d] = jnp.where(first, u, jnp.exp(m_prev + u))
        rc_ref[d] = jnp.where(first, u - b, jnp.exp(b_end + m_prev - m_new))


def _gate_prep(gi, gf, bsz):
    rows = gi.shape[1] // bsz
    spec = pl.BlockSpec((2, rows, CHUNK), lambda b: (0, b, 0))
    shape = jax.ShapeDtypeStruct(gi.shape, F32)
    return pl.pallas_call(
        _gate_prep_kernel,
        grid=(bsz,),
        in_specs=[spec, spec],
        out_specs=[spec, spec, spec],
        out_shape=[shape, shape, shape],
        scratch_shapes=[pltpu.VMEM((rows, CHUNK), F32)] * 3,
        compiler_params=_vmem_params(("parallel",), 32),
        name="gate_prep",
    )(gi, gf)


@functools.lru_cache(maxsize=None)
def _column_selector():
    sel = np.zeros((48, (N_HEADS // 2) * 5 * CHUNK), np.float32)

    def pick(arr, row, lane0, width):
        for piece in range(3):
            sel[arr * 24 + piece * 8 + row, lane0:lane0 + width] = 1.0

    for hd in range(N_HEADS):
        base = 5 * (hd // 2) * CHUNK
        pick(0, hd, base + (hd % 2) * CHUNK, CHUNK)
        pick(1, hd, base + (2 + hd % 2) * CHUNK, CHUNK)
        pick(0, N_HEADS + hd, base + 4 * CHUNK + (hd % 2) * QK_DIM, QK_DIM)
    return sel


def _mlstm_kernel(sel_ref, qf_ref, ktf_ref, vf_ref, raf_ref, rbf_ref, rcf_ref,
                  qb_ref, ktb_ref, vb_ref, rab_ref, rbb_ref, rcb_ref, hf_ref, hb_ref, c_scr):
    @pl.when(pl.program_id(1) == 0)
    def _():
        c_scr[...] = jnp.zeros_like(c_scr)

    cpb = ktf_ref.shape[0]
    t_i = lax.broadcasted_iota(jnp.int32, (CHUNK, CHUNK), 0)
    s_i = lax.broadcasted_iota(jnp.int32, (CHUNK, CHUNK), 1)
    visible = (s_i <= t_i, s_i >= t_i)
    ones = jnp.ones((CHUNK, V_DIM), BF16)
    kzero = jnp.zeros((QK_DIM, CHUNK), BF16)
    czero = jnp.zeros((QK_DIM, 2 * V_DIM), BF16)
    dirs = ((qf_ref, ktf_ref, vf_ref, raf_ref, rbf_ref, rcf_ref, hf_ref),
            (qb_ref, ktb_ref, vb_ref, rab_ref, rbb_ref, rcb_ref, hb_ref))

    def body(i, carry):
        units = []
        for d, (q_ref, kt_ref, v_ref, ra_ref, rb_ref, rc_ref, h_ref) in enumerate(dirs):
            li = i if d == 0 else cpb - 1 - i
            r8 = pl.ds(pl.multiple_of(li * 8, 8), 8)
            rl = pl.ds(pl.multiple_of(li * CHUNK, CHUNK), CHUNK)
            ra = ra_ref[r8, :]
            rc = rc_ref[r8, :]
            pieces = [p.astype(F32) for arr in (rb_ref[r8, :], rc) for p in _split3(arr)]
            colb = lax.dot_general(jnp.concatenate(pieces, axis=0).astype(BF16), sel_ref[...],
                                   (((0,), (0,)), ((), ())), preferred_element_type=F32)
            kt = kt_ref[li]
            for pair in range(N_HEADS // 2):
                k_a = kt[2 * pair * QK_DIM:(2 * pair + 1) * QK_DIM, :]
                k_b = kt[(2 * pair + 1) * QK_DIM:(2 * pair + 2) * QK_DIM, :]
                kdiag = jnp.concatenate([jnp.concatenate([k_a, kzero], axis=1),
                                         jnp.concatenate([kzero, k_b], axis=1)], axis=0)
                q2 = q_ref[rl, 2 * pair * QK_DIM:2 * (pair + 1) * QK_DIM]
                qk2 = _dot(q2, kdiag)
                w_inter2 = colb[:, (5 * pair + 4) * CHUNK:(5 * pair + 5) * CHUNK]
                q_inter2 = (q2.astype(F32) * w_inter2).astype(BF16)
                for hd in (2 * pair, 2 * pair + 1):
                    kth = kt[hd * QK_DIM:(hd + 1) * QK_DIM, :]
                    units.append(dict(d=d, hd=hd, rl=rl, h_ref=h_ref, ra=ra, rc=rc, colb=colb, kth=kth,
                                      q_inter2=q_inter2, v_ref=v_ref,
                                      qk=qk2[:, (hd % 2) * CHUNK:(hd % 2 + 1) * CHUNK]))
        for u in units:
            d, hd, colb = u["d"], u["hd"], u["colb"]
            c0 = (5 * (hd // 2) + hd % 2) * CHUNK
            u_b = colb[:, c0:c0 + CHUNK]
            w_intra = jnp.exp(jnp.where(visible[d], u_b + u["ra"][hd:hd + 1, :], MASKED))
            s = (u["qk"] * w_intra).astype(BF16)
            u["c_prev"] = c_scr[d, hd]
            c_bf = u["c_prev"].astype(BF16)
            c_rows = [c_bf, czero] if hd % 2 == 0 else [czero, c_bf]
            u["v_aug"] = jnp.concatenate([u["v_ref"][u["rl"], hd * V_DIM:(hd + 1) * V_DIM], ones], axis=1)
            u["nd"] = _dot(jnp.concatenate([s, u["q_inter2"]], axis=1),
                           jnp.concatenate([u["v_aug"]] + c_rows, axis=0))
        for u in units:
            hd, nd = u["hd"], u["nd"]
            e0 = (5 * (hd // 2) + 2 + hd % 2) * CHUNK
            exp_neg_m = jnp.exp(u["colb"][:, e0:e0 + CHUNK])
            den = jnp.maximum(jnp.abs(nd[:, V_DIM:]), exp_neg_m)
            u["h_ref"][u["rl"], hd * V_DIM:(hd + 1) * V_DIM] = (nd[:, :V_DIM] / den).astype(u["h_ref"].dtype)
            wk_row = u["ra"][N_HEADS + hd:N_HEADS + hd + 1, :]
            decay = jnp.broadcast_to(u["rc"][N_HEADS + hd:N_HEADS + hd + 1, :], (QK_DIM, CHUNK))
            kw = (u["kth"].astype(F32) * wk_row).astype(BF16)
            c_scr[u["d"], hd] = jnp.concatenate([decay, decay], axis=1) * u["c_prev"] + _dot(kw, u["v_aug"])
        return carry

    lax.fori_loop(0, cpb, body, 0, unroll=4)


def _mlstm(q, kt, v, ra, rb, rc, bsz, seq, cpb):
    t = q.shape[0]
    nblk = seq // (cpb * CHUNK)
    qk = N_HEADS * QK_DIM
    vw = N_HEADS * V_DIM
    fwd = lambda b, j: b * nblk + j
    bwd = lambda b, j: b * nblk + nblk - 1 - j

    def specs(blk, d):
        return [pl.BlockSpec((cpb * CHUNK, qk), lambda b, j: (blk(b, j), 0)),
                pl.BlockSpec((cpb, qk, CHUNK), lambda b, j: (blk(b, j), 0, 0)),
                pl.BlockSpec((cpb * CHUNK, vw), lambda b, j: (blk(b, j), 0))] + \
               [pl.BlockSpec((None, cpb * 8, CHUNK), lambda b, j: (d, blk(b, j), 0))] * 3

    sel = jnp.asarray(_column_selector()).astype(BF16)
    return pl.pallas_call(
        _mlstm_kernel,
        grid=(bsz, nblk),
        in_specs=[pl.BlockSpec(sel.shape, lambda b, j: (0, 0))] + specs(fwd, 0) + specs(bwd, 1),
        out_specs=[pl.BlockSpec((cpb * CHUNK, vw), lambda b, j: (fwd(b, j), 0)),
                   pl.BlockSpec((cpb * CHUNK, vw), lambda b, j: (bwd(b, j), 0))],
        out_shape=[jax.ShapeDtypeStruct((t, vw), BF16)] * 2,
        scratch_shapes=[pltpu.VMEM((2, N_HEADS, QK_DIM, 2 * V_DIM), F32)],
        compiler_params=_vmem_params(("parallel", "arbitrary"), 32),
        name="mlstm",
    )(sel, q, kt, v, ra, rb, rc, q, kt, v, ra, rb, rc)


def _fft_pitch(n1):
    return 8 * ((n1 // 4) | 1)


@functools.lru_cache(maxsize=None)
def _fft_tables(seq):
    n1 = seq // CHUNK
    two_pi = 2.0 * np.pi
    k1 = np.arange(n1, dtype=np.int64)
    n = 128 * np.arange(n1, dtype=np.int64)[None, None, :] + np.arange(128, dtype=np.int64)[:, None, None]
    ang = two_pi * ((k1[None, :, None] * n) % seq).astype(np.float64) / seq
    ga = np.concatenate([np.cos(ang), -np.sin(ang)], axis=1)
    j = np.arange(128, dtype=np.int64)
    a128 = two_pi * ((j[:, None] * j[None, :]) % 128).astype(np.float64) / 128.0
    ff = np.concatenate([np.cos(a128), -np.sin(a128)], axis=0)
    cs = np.concatenate([np.cos(a128), np.sin(a128)], axis=0) / np.sqrt(128.0 * seq)
    return tuple(np.asarray(a, dtype=np.float32) for a in (ga, ff, cs))


def _fft_kernel(z_ref, ga_ref, ff_ref, cs_ref, y_ref, scr):
    j = pl.program_id(2)
    n1 = ga_ref.shape[2]
    pitch = scr.shape[0] // CHUNK
    a_steps = CHUNK // (FFT_NB * z_ref.shape[0])
    k1_per_step = y_ref.shape[1] // FFT_NB

    @pl.when(j < a_steps)
    def _():
        for sb in range(z_ref.shape[0]):
            for l in range(FFT_NB):
                n2 = (j * z_ref.shape[0] + sb) * FFT_NB + l
                zl = z_ref[sb, pl.ds(l, n1, stride=FFT_NB), :].astype(BF16)
                row0 = pl.multiple_of(n2 * pitch, 8)
                scr[pl.ds(row0, 2 * n1), :] = _dot(ga_ref[n2], zl)

    @pl.when(j >= a_steps)
    def _():
        ff = ff_ref[...]
        cs = cs_ref[...]
        gb = max(1, min(FFT_GB, k1_per_step // FFT_KB))

        def body(it, carry):
            l0s = [(it * gb + gi) * FFT_KB for gi in range(gb)]
            ms = []
            for l0 in l0s:
                k0 = (j - a_steps) * k1_per_step + l0
                p = [scr[pl.ds(ri * n1 + k0 + kk, CHUNK, stride=pitch), :]
                     for kk in range(FFT_KB) for ri in range(2)]
                ms.append(_dot(ff, jnp.concatenate(p, axis=1).astype(BF16)))
            ys = []
            for m in ms:
                x = []
                for kk in range(FFT_KB):
                    c0 = 2 * kk * GROUP_DIM
                    xr = m[:CHUNK, c0:c0 + GROUP_DIM] - m[CHUNK:, c0 + GROUP_DIM:c0 + 2 * GROUP_DIM]
                    xi = m[:CHUNK, c0 + GROUP_DIM:c0 + 2 * GROUP_DIM] + m[CHUNK:, c0:c0 + GROUP_DIM]
                    x.append(jnp.concatenate([xr, xi], axis=1))
                ys.append(_dot(jnp.concatenate(x, axis=0).astype(BF16), cs))
            for l0, y in zip(l0s, ys):
                for kk in range(FFT_KB):
                    row0 = pl.multiple_of((l0 + kk) * FFT_NB, 8)
                    for kb in range(CHUNK // FFT_NB):
                        r0 = kk * CHUNK + kb * FFT_NB
                        y_ref[kb, pl.ds(row0, FFT_NB), :] = y[r0:r0 + FFT_NB, :]
            return carry

        lax.fori_loop(0, k1_per_step // (FFT_KB * gb), body, 0)


def _fourier(z5, seq):
    bsz = z5.shape[0]
    n1 = seq // CHUNK
    nblk = CHUNK // FFT_NB
    a_steps = nblk // FFT_SB
    pitch = _fft_pitch(n1)
    ga, ff, cs = (jnp.asarray(a).astype(BF16) for a in _fft_tables(seq))
    return pl.pallas_call(
        _fft_kernel,
        grid=(bsz, N_GROUPS, a_steps + FFT_BSTEPS),
        in_specs=[pl.BlockSpec((None, None, FFT_SB, n1 * FFT_NB, GROUP_DIM),
                               lambda b, g, j: (b, g, jnp.minimum(j, a_steps - 1), 0, 0)),
                  pl.BlockSpec(ga.shape, lambda b, g, j: (0, 0, 0), pipeline_mode=pl.Buffered(1)),
                  pl.BlockSpec(ff.shape, lambda b, g, j: (0, 0)),
                  pl.BlockSpec(cs.shape, lambda b, g, j: (0, 0))],
        out_specs=pl.BlockSpec((None, None, nblk, n1 * FFT_NB // FFT_BSTEPS, GROUP_DIM),
                               lambda b, g, j: (b, g, 0, jnp.maximum(j - a_steps, 0), 0)),
        out_shape=jax.ShapeDtypeStruct(z5.shape, F32),
        scratch_shapes=[pltpu.VMEM((CHUNK * pitch, GROUP_DIM), F32)],
        compiler_params=_vmem_params(("parallel", "parallel", "arbitrary"), 56),
        name="fft",
    )(z5, ga, ff, cs)


def _gather_rows(y_refs, k2_off, n1, kpt):
    rows = [jnp.concatenate([ref[pl.ds(k2_off + kk, n1, stride=FFT_NB), :] for ref in y_refs], axis=1)
            for kk in range(kpt)]
    return jnp.concatenate(rows, axis=0)


def _outmlp_kernel(alpha, ff_chunk, spb, x_ref, hf_ref, hb_ref, o_ref, y0_ref, y1_ref, y2_ref, y3_ref,
                   g1_ref, sc2_ref, sh2_ref, g2_ref,
                   nw_ref, wout_ref, l1g_ref, l1b_ref, w1_ref, b1_ref, w2_ref, b2_ref, l2g_ref, l2b_ref,
                   out_ref):
    n1 = y0_ref.shape[0] // FFT_NB
    kpt = x_ref.shape[0] // n1
    k2_off = ((pl.program_id(0) % spb) * kpt) % FFT_NB
    sizes = OUTMLP_SUBTILE_K2 if sum(OUTMLP_SUBTILE_K2) == kpt else (kpt,)
    subs, k2l = [], 0
    for kps in sizes:
        subs.append(dict(k2l=k2l, kps=kps, rs=slice(k2l * n1, (k2l + kps) * n1)))
        k2l += kps
    for s in subs:
        rs = s["rs"]
        yf = _gather_rows((y0_ref, y1_ref, y2_ref, y3_ref), k2_off + s["k2l"], n1, s["kps"]).astype(BF16)
        hsum = hf_ref[rs, :].astype(F32) + hb_ref[rs, :].astype(F32)
        heads = [_ln_plain(hsum[:, j * V_DIM:(j + 1) * V_DIM]) for j in range(N_HEADS)]
        ym = jnp.concatenate(heads, axis=1) * nw_ref[...] * jax.nn.sigmoid(o_ref[rs, :].astype(F32))
        s["mix"] = _dot(jnp.concatenate([ym.astype(BF16), yf], axis=1), wout_ref[...])
    for s in subs:
        x1 = _ln_plain(alpha * x_ref[s["rs"], :] + (1.0 + g1_ref[...]) * s["mix"]) * l1g_ref[...] + l1b_ref[...]
        s["x1"] = x1
        s["h2"] = (_ln_plain(x1) * (1.0 + sc2_ref[...]) + sh2_ref[...]).astype(BF16)
    for s in subs:
        ff = b2_ref[...]
        for j in range(w1_ref.shape[1] // ff_chunk):
            sl = slice(j * ff_chunk, (j + 1) * ff_chunk)
            hid = jnp.maximum(_dot(s["h2"], w1_ref[:, sl]) + b1_ref[:, sl], 0.0)
            ff = ff + _dot((hid * hid).astype(BF16), w2_ref[sl, :])
        s["ff"] = ff
    for s in subs:
        out_ref[s["rs"], :] = (_ln_plain(alpha * s["x1"] + (1.0 + g2_ref[...]) * s["ff"]) * l2g_ref[...]
                               + l2b_ref[...])


def _outmlp(x2d, hf, hb, o, y5, mod4, w, seq, tm, alpha, ff_chunk):
    t, d = x2d.shape
    spb = seq // tm
    n1 = seq // CHUNK
    kpt = tm // n1
    assert FFT_NB % kpt == 0
    vw = N_HEADS * V_DIM
    row = lambda width: pl.BlockSpec((tm, width), lambda i: (i, 0))
    yspec = lambda g: pl.BlockSpec((None, None, None, n1 * FFT_NB, GROUP_DIM),
                                   lambda i: (i // spb, g, ((i % spb) * kpt) // FFT_NB, 0, 0))
    modspec = lambda j: pl.BlockSpec((None, None, 1, d), lambda i: (i // spb, j, 0, 0))
    const = lambda a: pl.BlockSpec(a.shape, lambda i: (0,) * a.ndim, pipeline_mode=pl.Buffered(1))
    names = ["nw", "wout", "l1g", "l1b", "w1", "b1", "w2", "b2", "l2g", "l2b"]
    return pl.pallas_call(
        functools.partial(_outmlp_kernel, alpha, ff_chunk, spb),
        grid=(t // tm,),
        in_specs=[row(d), row(vw), row(vw), row(vw)]
                 + [yspec(g) for g in range(N_GROUPS)]
                 + [modspec(2), modspec(4), modspec(3), modspec(5)] + [const(w[k]) for k in names],
        out_specs=row(d),
        out_shape=jax.ShapeDtypeStruct((t, d), F32),
        compiler_params=_vmem_params(("parallel",), 56),
        name="outmlp",
    )(x2d, hf, hb, o, y5, y5, y5, y5, mod4, mod4, mod4, mod4, *[w[k] for k in names])


def _layer_weights(w_in, b_gate, w_out, w_ff1, w_ff2, b_ff1, b_ff2, mlstm_norm_w, ln1_g, ln1_b, ln2_g, ln2_b):
    qk = N_HEADS * QK_DIM
    vw = N_HEADS * V_DIM
    fw = N_GROUPS * GROUP_DIM
    o0, o1, o2, o3, o4 = qk, 2 * qk, 2 * qk + vw, 2 * qk + 2 * vw, 2 * qk + 2 * vw + fw
    wg = w_in[:, o4:].T.reshape(2, 2, N_HEADS, -1)
    bg = b_gate.astype(F32).reshape(2, 2, N_HEADS, 1)
    dup = lambda a: jnp.concatenate([a, a], axis=1)
    r = lambda a: a.astype(F32).reshape(1, -1)
    wq = w_in[:, :o0] * (QK_DIM ** -0.5)
    gate_rows = lambda a: jnp.concatenate([dup(a[:, 0]), dup(a[:, 1])], axis=1).reshape(32, -1)
    return {
        "wrow": jnp.concatenate([wq, w_in[:, o1:o4]], axis=1).astype(BF16),
        "wlane": jnp.concatenate([w_in[:, o0:o1].T, gate_rows(wg)], axis=0).astype(BF16),
        "bg": gate_rows(bg),
        "nw": r(mlstm_norm_w), "wout": w_out, "l1g": r(ln1_g), "l1b": r(ln1_b),
        "w1": w_ff1, "b1": r(b_ff1), "w2": w_ff2, "b2": r(b_ff2),
        "l2g": r(ln2_g), "l2b": r(ln2_b),
    }


def kernel(x, c, w_ada, b_ada, w_in, b_gate, mlstm_norm_w, w_out, ln1_g, ln1_b,
           w_ff1, b_ff1, w_ff2, b_ff2, ln2_g, ln2_b):
    bsz, seq, d = x.shape
    depth = w_ada.shape[0]
    alpha = (2 * depth) ** 0.25
    assert seq % (MLSTM_CPB * CHUNK) == 0 and (seq // CHUNK) % 8 == 0
    assert d == N_HEADS * V_DIM + N_GROUPS * GROUP_DIM
    tm = min(512, FFT_NB * (seq // CHUNK))
    x2d = x.reshape(bsz * seq, d)
    for l in range(depth):
        w = _layer_weights(w_in[l], b_gate[l], w_out[l], w_ff1[l], w_ff2[l], b_ff1[l], b_ff2[l],
                           mlstm_norm_w[l], ln1_g[l], ln1_b[l], ln2_g[l], ln2_b[l])
        mod4 = _adaln(c, w_ada[l], b_ada[l]).reshape(bsz, N_MOD, 1, d)
        q, kt, v, o, z5, gi, gf, w["wout"], w["w1"], w["w2"] = _inproj(x2d, mod4, w, seq, min(1024, seq))
        ra, rb, rc = _gate_prep(gi, gf, bsz)
        hf, hb = _mlstm(q, kt, v, ra, rb, rc, bsz, seq, MLSTM_CPB)
        y5 = _fourier(z5, seq)
        x2d = _outmlp(x2d, hf, hb, o, y5, mod4, w, seq, tm, alpha, ff_chunk=1024)
    return x2d.reshape(bsz, seq, d)
```

```python
import functools

import numpy as np
import jax
import jax.numpy as jnp
from jax import lax
from jax.experimental import pallas as pl
from jax.experimental.pallas import tpu as pltpu

F32 = jnp.float32
BF16 = jnp.bfloat16

CHUNK = 128
N_HEADS = 4
QK_DIM = 64
V_DIM = 128
N_GROUPS = 4
GROUP_DIM = 128
N_MOD = 6
LN_EPS = 1e-5
MASKED = -1e30
FFT_NB = 8
FFT_KB = 4
FFT_SB = 16
FFT_GB = 4
FFT_BSTEPS = 4
MLSTM_CPB = 8
OUTMLP_SUBTILE_K2 = (2, 2)
INPROJ_SUBTILES = 4

_NT = (((1,), (1,)), ((), ()))


def _dot(a, b):
    return jnp.dot(a, b, preferred_element_type=F32)


def _dot_nt(a, b):
    return lax.dot_general(a, b, _NT, preferred_element_type=F32)


def _ln_plain(x):
    mu = jnp.mean(x, axis=-1, keepdims=True)
    xc = x - mu
    var = jnp.mean(xc * xc, axis=-1, keepdims=True)
    return xc * lax.rsqrt(var + LN_EPS)


def _log_sigmoid(x):
    return jnp.minimum(x, 0.0) - jnp.log1p(jnp.exp(-jnp.abs(x)))


def _split3(x):
    hi = x.astype(BF16)
    r1 = x - hi.astype(F32)
    mid = r1.astype(BF16)
    lo = (r1 - mid.astype(F32)).astype(BF16)
    return hi, mid, lo


def _vmem_params(semantics, limit_mb):
    return pltpu.CompilerParams(dimension_semantics=semantics, vmem_limit_bytes=limit_mb * 1024 * 1024)


def _adaln_kernel(ct_ref, w_ref, b_ref, o_ref):
    ct = ct_ref[...]
    act = ct * jax.nn.sigmoid(ct)
    w = w_ref[...]
    for b in range(ct.shape[1]):
        o_ref[b:b + 1, :] = jnp.sum(act[:, b:b + 1] * w, axis=0, keepdims=True) + b_ref[...]


def _adaln(c, w_ada, b_ada):
    bsz, d = c.shape
    n = w_ada.shape[1]
    tn = 1024
    return pl.pallas_call(
        _adaln_kernel,
        grid=(n // tn,),
        in_specs=[pl.BlockSpec((d, bsz), lambda j: (0, 0)),
                  pl.BlockSpec((d, tn), lambda j: (0, j)),
                  pl.BlockSpec((1, tn), lambda j: (0, j))],
        out_specs=pl.BlockSpec((bsz, tn), lambda j: (0, j)),
        out_shape=jax.ShapeDtypeStruct((bsz, n), F32),
        compiler_params=_vmem_params(("parallel",), 32),
        name="adaln",
    )(c.T, w_ada, b_ada.reshape(1, n))


def _inproj_kernel(x_ref, sh_ref, sc_ref, wrow_ref, wlane_ref, bg_ref, wout_ref, w1_ref, w2_ref,
                   q_ref, kt_ref, v_ref, o_ref, z_ref, gi_ref, gf_ref, woutb_ref, w1b_ref, w2b_ref):
    woutb_ref[...] = wout_ref[...].astype(BF16)
    w1b_ref[...] = w1_ref[...].astype(BF16)
    w2b_ref[...] = w2_ref[...].astype(BF16)
    qk, vw = q_ref.shape[1], v_ref.shape[1]
    n_chunks = x_ref.shape[0] // CHUNK
    cps = max(1, n_chunks // INPROJ_SUBTILES)
    for c0 in range(0, n_chunks, cps):
        rs = slice(c0 * CHUNK, (c0 + cps) * CHUNK)
        h = _ln_plain(x_ref[rs, :]) * (1.0 + sc_ref[...]) + sh_ref[...]
        hb = h.astype(BF16)
        rowp = _dot(hb, wrow_ref[...])
        q_ref[rs, :] = rowp[:, :qk].astype(BF16)
        v_ref[rs, :] = rowp[:, qk:qk + vw].astype(BF16)
        o_ref[rs, :] = rowp[:, qk + vw:qk + 2 * vw].astype(BF16)
        fz = rowp[:, qk + 2 * vw:]
        for j in range(cps):
            for g in range(N_GROUPS):
                for jb in range(CHUNK // FFT_NB):
                    r0 = j * CHUNK + jb * FFT_NB
                    z_ref[g, jb, (c0 + j) * FFT_NB:(c0 + j + 1) * FFT_NB, :] = (
                        fz[r0:r0 + FFT_NB, g * GROUP_DIM:(g + 1) * GROUP_DIM])
        lanep = _dot_nt(wlane_ref[...], hb)
        gates = lanep[qk:, :] + bg_ref[...]
        for j in range(cps):
            cl = slice(j * CHUNK, (j + 1) * CHUNK)
            kt_ref[c0 + j] = lanep[:qk, cl].astype(BF16)
            for d in range(2):
                gi_ref[d, (c0 + j) * 8:(c0 + j + 1) * 8, :] = gates[16 * d:16 * d + 8, cl]
                gf_ref[d, (c0 + j) * 8:(c0 + j + 1) * 8, :] = gates[16 * d + 8:16 * d + 16, cl]


def _inproj(x2d, mod4, w, seq, tm):
    t, d = x2d.shape
    spb = seq // tm
    qk = N_HEADS * QK_DIM
    vw = N_HEADS * V_DIM
    nblk = CHUNK // FFT_NB
    cpt = tm // CHUNK
    full = lambda a: pl.BlockSpec(a.shape, lambda i: (0,) * a.ndim)
    modspec = lambda j: pl.BlockSpec((None, None, 1, d), lambda i: (i // spb, j, 0, 0))
    names = ["wrow", "wlane", "bg"]
    steps = t // tm
    casts = [w["wout"], w["w1"], w["w2"]]
    assert all(a.shape[0] % (16 * steps) == 0 for a in casts)
    slab = lambda a: pl.BlockSpec((a.shape[0] // steps, a.shape[1]), lambda i: (i, 0))
    return pl.pallas_call(
        _inproj_kernel,
        grid=(steps,),
        in_specs=[pl.BlockSpec((tm, d), lambda i: (i, 0)), modspec(0), modspec(1)] + [full(w[k]) for k in names]
                 + [slab(a) for a in casts],
        out_specs=[pl.BlockSpec((tm, qk), lambda i: (i, 0)),
                   pl.BlockSpec((cpt, qk, CHUNK), lambda i: (i, 0, 0)),
                   pl.BlockSpec((tm, vw), lambda i: (i, 0)),
                   pl.BlockSpec((tm, vw), lambda i: (i, 0)),
                   pl.BlockSpec((None, N_GROUPS, nblk, cpt * FFT_NB, GROUP_DIM),
                                lambda i: (i // spb, 0, 0, i % spb, 0)),
                   pl.BlockSpec((2, cpt * 8, CHUNK), lambda i: (0, i, 0)),
                   pl.BlockSpec((2, cpt * 8, CHUNK), lambda i: (0, i, 0))] + [slab(a) for a in casts],
        out_shape=[jax.ShapeDtypeStruct((t, qk), BF16),
                   jax.ShapeDtypeStruct((t // CHUNK, qk, CHUNK), BF16),
                   jax.ShapeDtypeStruct((t, vw), BF16),
                   jax.ShapeDtypeStruct((t, vw), BF16),
                   jax.ShapeDtypeStruct((t // seq, N_GROUPS, nblk, (seq // CHUNK) * FFT_NB, GROUP_DIM), F32),
                   jax.ShapeDtypeStruct((2, t // CHUNK * 8, CHUNK), F32),
                   jax.ShapeDtypeStruct((2, t // CHUNK * 8, CHUNK), F32)]
                  + [jax.ShapeDtypeStruct(a.shape, BF16) for a in casts],
        compiler_params=_vmem_params(("parallel",), 48),
        name="inproj",
    )(x2d, mod4, mod4, *[w[k] for k in names], *casts)


def _gate_prep_kernel(gi_ref, gf_ref, ra_ref, rb_ref, rc_ref, bend_scr, amax_scr, mprev_scr):
    rows = gi_ref.shape[1]
    n_chunks = rows // 8
    lane = lax.broadcasted_iota(jnp.int32, (rows, CHUNK), 1)
    first = (lax.broadcasted_iota(jnp.int32, (rows, CHUNK), 0) & 7) < N_HEADS
    src = lax.broadcasted_iota(jnp.int32, (CHUNK, 2 * CHUNK), 0)
    dst = lax.broadcasted_iota(jnp.int32, (CHUNK, 2 * CHUNK), 1)
    for d in range(2):
        feeds = (src <= dst) if d == 0 else (src >= dst)
        cum_and_total = jnp.where((dst >= CHUNK) | feeds, 1.0, 0.0).astype(BF16)
        bc = sum(_dot(p, cum_and_total) for p in _split3(_log_sigmoid(gf_ref[d])))
        b, b_end = bc[:, :CHUNK], bc[:, CHUNK:]
        a = gi_ref[d] - b
        cm = a
        for k in range(7):
            sh = 1 << k
            if d == 0:
                cm = jnp.maximum(cm, jnp.where(lane >= sh, pltpu.roll(cm, sh, 1), MASKED))
            else:
                cm = jnp.maximum(cm, jnp.where(lane < CHUNK - sh, pltpu.roll(cm, CHUNK - sh, 1), MASKED))
        a_max = jnp.broadcast_to(jnp.max(a, axis=1, keepdims=True), a.shape)
        bend_scr[...] = b_end
        amax_scr[...] = a_max

        def scan(c, m):
            r = pl.ds(pl.multiple_of((c if d == 0 else n_chunks - 1 - c) * 8, 8), 8)
            mprev_scr[r, :] = m
            return bend_scr[r, :] + jnp.maximum(m, amax_scr[r, :])

        lax.fori_loop(0, n_chunks, scan, jnp.zeros((8, CHUNK), F32))
        m_prev = mprev_scr[...]
        u = -jnp.maximum(m_prev, cm)
        m_new = b_end + jnp.maximum(m_prev, a_max)
        ra_ref[d] = jnp.where(first, a, jnp.exp(b_end + a - m_new))
        rb_ref[d] = jnp.where(first, u, jnp.exp(m_prev + u))
        rc_ref[d] = jnp.where(first, u - b, jnp.exp(b_end + m_prev - m_new))


def _gate_prep(gi, gf, bsz):
    rows = gi.shape[1] // bsz
    spec = pl.BlockSpec((2, rows, CHUNK), lambda b: (0, b, 0))
    shape = jax.ShapeDtypeStruct(gi.shape, F32)
    return pl.pallas_call(
        _gate_prep_kernel,
        grid=(bsz,),
        in_specs=[spec, spec],
        out_specs=[spec, spec, spec],
        out_shape=[shape, shape, shape],
        scratch_shapes=[pltpu.VMEM((rows, CHUNK), F32)] * 3,
        compiler_params=_vmem_params(("parallel",), 32),
        name="gate_prep",
    )(gi, gf)


@functools.lru_cache(maxsize=None)
def _column_selector():
    sel = np.zeros((48, (N_HEADS // 2) * 5 * CHUNK), np.float32)

    def pick(arr, row, lane0, width):
        for piece in range(3):
            sel[arr * 24 + piece * 8 + row, lane0:lane0 + width] = 1.0

    for hd in range(N_HEADS):
        base = 5 * (hd // 2) * CHUNK
        pick(0, hd, base + (hd % 2) * CHUNK, CHUNK)
        pick(1, hd, base + (2 + hd % 2) * CHUNK, CHUNK)
        pick(0, N_HEADS + hd, base + 4 * CHUNK + (hd % 2) * QK_DIM, QK_DIM)
    return sel


def _mlstm_kernel(sel_ref, qf_ref, ktf_ref, vf_ref, raf_ref, rbf_ref, rcf_ref,
                  qb_ref, ktb_ref, vb_ref, rab_ref, rbb_ref, rcb_ref, hf_ref, hb_ref, c_scr):
    @pl.when(pl.program_id(1) == 0)
    def _():
        c_scr[...] = jnp.zeros_like(c_scr)

    cpb = ktf_ref.shape[0]
    t_i = lax.broadcasted_iota(jnp.int32, (CHUNK, CHUNK), 0)
    s_i = lax.broadcasted_iota(jnp.int32, (CHUNK, CHUNK), 1)
    visible = (s_i <= t_i, s_i >= t_i)
    ones = jnp.ones((CHUNK, V_DIM), BF16)
    kzero = jnp.zeros((QK_DIM, CHUNK), BF16)
    czero = jnp.zeros((QK_DIM, 2 * V_DIM), BF16)
    dirs = ((qf_ref, ktf_ref, vf_ref, raf_ref, rbf_ref, rcf_ref, hf_ref),
            (qb_ref, ktb_ref, vb_ref, rab_ref, rbb_ref, rcb_ref, hb_ref))

    def body(i, carry):
        units = []
        for d, (q_ref, kt_ref, v_ref, ra_ref, rb_ref, rc_ref, h_ref) in enumerate(dirs):
            li = i if d == 0 else cpb - 1 - i
            r8 = pl.ds(pl.multiple_of(li * 8, 8), 8)
            rl = pl.ds(pl.multiple_of(li * CHUNK, CHUNK), CHUNK)
            ra = ra_ref[r8, :]
            rc = rc_ref[r8, :]
            pieces = [p.astype(F32) for arr in (rb_ref[r8, :], rc) for p in _split3(arr)]
            colb = lax.dot_general(jnp.concatenate(pieces, axis=0).astype(BF16), sel_ref[...],
                                   (((0,), (0,)), ((), ())), preferred_element_type=F32)
            kt = kt_ref[li]
            for pair in range(N_HEADS // 2):
                k_a = kt[2 * pair * QK_DIM:(2 * pair + 1) * QK_DIM, :]
                k_b = kt[(2 * pair + 1) * QK_DIM:(2 * pair + 2) * QK_DIM, :]
                kdiag = jnp.concatenate([jnp.concatenate([k_a, kzero], axis=1),
                                         jnp.concatenate([kzero, k_b], axis=1)], axis=0)
                q2 = q_ref[rl, 2 * pair * QK_DIM:2 * (pair + 1) * QK_DIM]
                qk2 = _dot(q2, kdiag)
                w_inter2 = colb[:, (5 * pair + 4) * CHUNK:(5 * pair + 5) * CHUNK]
                q_inter2 = (q2.astype(F32) * w_inter2).astype(BF16)
                for hd in (2 * pair, 2 * pair + 1):
                    kth = kt[hd * QK_DIM:(hd + 1) * QK_DIM, :]
                    units.append(dict(d=d, hd=hd, rl=rl, h_ref=h_ref, ra=ra, rc=rc, colb=colb, kth=kth,
                                      q_inter2=q_inter2, v_ref=v_ref,
                                      qk=qk2[:, (hd % 2) * CHUNK:(hd % 2 + 1) * CHUNK]))
        for u in units:
            d, hd, colb = u["d"], u["hd"], u["colb"]
            c0 = (5 * (hd // 2) + hd % 2) * CHUNK
            u_b = colb[:, c0:c0 + CHUNK]
            w_intra = jnp.exp(jnp.where(visible[d], u_b + u["ra"][hd:hd + 1, :], MASKED))
            s = (u["qk"] * w_intra).astype(BF16)
            u["c_prev"] = c_scr[d, hd]
            c_bf = u["c_prev"].astype(BF16)
            c_rows = [c_bf, czero] if hd % 2 == 0 else [czero, c_bf]
            u["v_aug"] = jnp.concatenate([u["v_ref"][u["rl"], hd * V_DIM:(hd + 1) * V_DIM], ones], axis=1)
            u["nd"] = _dot(jnp.concatenate([s, u["q_inter2"]], axis=1),
                           jnp.concatenate([u["v_aug"]] + c_rows, axis=0))
        for u in units:
            hd, nd = u["hd"], u["nd"]
            e0 = (5 * (hd // 2) + 2 + hd % 2) * CHUNK
            exp_neg_m = jnp.exp(u["colb"][:, e0:e0 + CHUNK])
            den = jnp.maximum(jnp.abs(nd[:, V_DIM:]), exp_neg_m)
            u["h_ref"][u["rl"], hd * V_DIM:(hd + 1) * V_DIM] = (nd[:, :V_DIM] / den).astype(u["h_ref"].dtype)
            wk_row = u["ra"][N_HEADS + hd:N_HEADS + hd + 1, :]
            decay = jnp.broadcast_to(u["rc"][N_HEADS + hd:N_HEADS + hd + 1, :], (QK_DIM, CHUNK))
            kw = (u["kth"].astype(F32) * wk_row).astype(BF16)
            c_scr[u["d"], hd] = jnp.concatenate([decay, decay], axis=1) * u["c_prev"] + _dot(kw, u["v_aug"])
        return carry

    lax.fori_loop(0, cpb, body, 0, unroll=4)


def _mlstm(q, kt, v, ra, rb, rc, bsz, seq, cpb):
    t = q.shape[0]
    nblk = seq // (cpb * CHUNK)
    qk = N_HEADS * QK_DIM
    vw = N_HEADS * V_DIM
    fwd = lambda b, j: b * nblk + j
    bwd = lambda b, j: b * nblk + nblk - 1 - j

    def specs(blk, d):
        return [pl.BlockSpec((cpb * CHUNK, qk), lambda b, j: (blk(b, j), 0)),
                pl.BlockSpec((cpb, qk, CHUNK), lambda b, j: (blk(b, j), 0, 0)),
                pl.BlockSpec((cpb * CHUNK, vw), lambda b, j: (blk(b, j), 0))] + \
               [pl.BlockSpec((None, cpb * 8, CHUNK), lambda b, j: (d, blk(b, j), 0))] * 3

    sel = jnp.asarray(_column_selector()).astype(BF16)
    return pl.pallas_call(
        _mlstm_kernel,
        grid=(bsz, nblk),
        in_specs=[pl.BlockSpec(sel.shape, lambda b, j: (0, 0))] + specs(fwd, 0) + specs(bwd, 1),
        out_specs=[pl.BlockSpec((cpb * CHUNK, vw), lambda b, j: (fwd(b, j), 0)),
                   pl.BlockSpec((cpb * CHUNK, vw), lambda b, j: (bwd(b, j), 0))],
        out_shape=[jax.ShapeDtypeStruct((t, vw), BF16)] * 2,
        scratch_shapes=[pltpu.VMEM((2, N_HEADS, QK_DIM, 2 * V_DIM), F32)],
        compiler_params=_vmem_params(("parallel", "arbitrary"), 32),
        name="mlstm",
    )(sel, q, kt, v, ra, rb, rc, q, kt, v, ra, rb, rc)


def _fft_kh(n1):
    return n1 // 2 + FFT_KB


@functools.lru_cache(maxsize=None)
def _fft_tables(seq):
    n1 = seq // CHUNK
    kh = _fft_kh(n1)
    two_pi = 2.0 * np.pi
    k1 = np.arange(kh, dtype=np.int64)
    n = 128 * np.arange(n1, dtype=np.int64)[None, None, :] + np.arange(128, dtype=np.int64)[:, None, None]
    ang = two_pi * ((k1[None, :, None] * n) % seq).astype(np.float64) / seq
    ga = np.concatenate([np.cos(ang), -np.sin(ang)], axis=1)
    j = np.arange(128, dtype=np.int64)
    a128 = two_pi * ((j[:, None] * j[None, :]) % 128).astype(np.float64) / 128.0
    ff = np.concatenate([np.cos(a128), -np.sin(a128)], axis=0)
    cc, sc = np.cos(a128), np.sin(a128)
    cs = np.block([[cc, cc], [sc, -sc]]) / np.sqrt(128.0 * seq)
    return tuple(np.asarray(a, dtype=np.float32) for a in (ga, ff, cs))


def _fft_kernel(z_ref, ga_ref, ff_ref, cs_ref, y_ref, scr):
    j = pl.program_id(2)
    n1 = ga_ref.shape[2]
    kh2 = ga_ref.shape[1]
    half = n1 // 2
    a_steps = CHUNK // (FFT_NB * z_ref.shape[0])

    @pl.when(j < a_steps)
    def _():
        for sb in range(z_ref.shape[0]):
            for l in range(FFT_NB):
                n2 = (j * z_ref.shape[0] + sb) * FFT_NB + l
                zl = z_ref[sb, pl.ds(l, n1, stride=FFT_NB), :].astype(BF16)
                row0 = pl.multiple_of(n2 * kh2, 8)
                scr[pl.ds(row0, kh2), :] = _dot(ga_ref[n2], zl)

    def spectra(k0s):
        ms = []
        for k0 in k0s:
            p = [scr[pl.ds(ri * (kh2 // 2) + k0 + kk, CHUNK, stride=kh2), :]
                 for kk in range(FFT_KB) for ri in range(2)]
            ms.append(_dot(ff_ref[...], jnp.concatenate(p, axis=1).astype(BF16)))
        ys = []
        for m in ms:
            x = []
            for kk in range(FFT_KB):
                c0 = 2 * kk * GROUP_DIM
                xr = m[:CHUNK, c0:c0 + GROUP_DIM] - m[CHUNK:, c0 + GROUP_DIM:c0 + 2 * GROUP_DIM]
                xi = m[:CHUNK, c0 + GROUP_DIM:c0 + 2 * GROUP_DIM] + m[CHUNK:, c0:c0 + GROUP_DIM]
                x.append(jnp.concatenate([xr, xi], axis=1))
            ys.append(_dot(jnp.concatenate(x, axis=0).astype(BF16), cs_ref[...]))
        return ys

    def store(y, kk, k1, lane0):
        row0 = pl.multiple_of(k1 * FFT_NB, 8)
        for kb in range(CHUNK // FFT_NB):
            r0 = kk * CHUNK + kb * FFT_NB
            y_ref[kb, pl.ds(row0, FFT_NB), :] = y[r0:r0 + FFT_NB, lane0:lane0 + GROUP_DIM]

    k1_step = half // FFT_BSTEPS
    gb = max(1, min(FFT_GB, k1_step // FFT_KB))

    @pl.when(j >= a_steps)
    def _():
        def body(it, carry):
            k0s = [(j - a_steps) * k1_step + (it * gb + gi) * FFT_KB for gi in range(gb)]
            for k0, y in zip(k0s, spectra(k0s)):
                for kk in range(FFT_KB):
                    k1 = k0 + kk
                    store(y, kk, jnp.where(k1 == 0, 0, n1 - k1), GROUP_DIM)
                    store(y, kk, k1, 0)
            return carry

        lax.fori_loop(0, k1_step // (FFT_KB * gb), body, 0)

    @pl.when(j == a_steps + FFT_BSTEPS - 1)
    def _():
        (y,) = spectra([half])
        store(y, 0, half, GROUP_DIM)


def _fourier(z5, seq):
    bsz = z5.shape[0]
    n1 = seq // CHUNK
    nblk = CHUNK // FFT_NB
    a_steps = nblk // FFT_SB
    ga, ff, cs = (jnp.asarray(a).astype(BF16) for a in _fft_tables(seq))
    return pl.pallas_call(
        _fft_kernel,
        grid=(bsz, N_GROUPS, a_steps + FFT_BSTEPS),
        in_specs=[pl.BlockSpec((None, None, FFT_SB, n1 * FFT_NB, GROUP_DIM),
                               lambda b, g, j: (b, g, jnp.minimum(j, a_steps - 1), 0, 0)),
                  pl.BlockSpec(ga.shape, lambda b, g, j: (0, 0, 0), pipeline_mode=pl.Buffered(1)),
                  pl.BlockSpec(ff.shape, lambda b, g, j: (0, 0)),
                  pl.BlockSpec(cs.shape, lambda b, g, j: (0, 0))],
        out_specs=pl.BlockSpec((None, None, nblk, n1 * FFT_NB, GROUP_DIM), lambda b, g, j: (b, g, 0, 0, 0)),
        out_shape=jax.ShapeDtypeStruct(z5.shape, F32),
        scratch_shapes=[pltpu.VMEM((CHUNK * 2 * _fft_kh(n1), GROUP_DIM), F32)],
        compiler_params=_vmem_params(("parallel", "parallel", "arbitrary"), 56),
        name="fft",
    )(z5, ga, ff, cs)


def _gather_rows(y_refs, k2l, n1):
    half = n1 // 2
    low = jnp.concatenate([a[pl.ds(k2l, half, stride=FFT_NB), :] for a, _ in y_refs], axis=1)
    high = jnp.concatenate([b[pl.ds(half * FFT_NB + FFT_NB - 1 - k2l, half, stride=FFT_NB), :] for _, b in y_refs],
                           axis=1)
    return jnp.concatenate([low, high], axis=0)


def _outmlp_kernel(alpha, ff_chunk, spb, x_ref, hf_ref, hb_ref, o_ref,
                   y0_ref, y1_ref, y2_ref, y3_ref, ym0_ref, ym1_ref, ym2_ref, ym3_ref,
                   g1_ref, sc2_ref, sh2_ref, g2_ref,
                   nw_ref, wout_ref, l1g_ref, l1b_ref, w1_ref, b1_ref, w2_ref, b2_ref, l2g_ref, l2b_ref,
                   out_ref):
    n1 = y0_ref.shape[0] // FFT_NB
    kpt = x_ref.shape[0] // n1
    k2_off = ((pl.program_id(0) % spb) * kpt) % FFT_NB
    sizes = OUTMLP_SUBTILE_K2 if sum(OUTMLP_SUBTILE_K2) == kpt else (kpt,)
    subs, k2l = [], 0
    for kps in sizes:
        subs.append(dict(k2l=k2l, kps=kps, rs=slice(k2l * n1, (k2l + kps) * n1)))
        k2l += kps
    for s in subs:
        rs = s["rs"]
        y_refs = ((y0_ref, ym0_ref), (y1_ref, ym1_ref), (y2_ref, ym2_ref), (y3_ref, ym3_ref))
        yf = jnp.concatenate([_gather_rows(y_refs, k2_off + s["k2l"] + kk, n1) for kk in range(s["kps"])],
                             axis=0).astype(BF16)
        hsum = hf_ref[rs, :].astype(F32) + hb_ref[rs, :].astype(F32)
        heads = [_ln_plain(hsum[:, j * V_DIM:(j + 1) * V_DIM]) for j in range(N_HEADS)]
        ym = jnp.concatenate(heads, axis=1) * nw_ref[...] * jax.nn.sigmoid(o_ref[rs, :].astype(F32))
        s["mix"] = _dot(jnp.concatenate([ym.astype(BF16), yf], axis=1), wout_ref[...])
    for s in subs:
        x1 = _ln_plain(alpha * x_ref[s["rs"], :] + (1.0 + g1_ref[...]) * s["mix"]) * l1g_ref[...] + l1b_ref[...]
        s["x1"] = x1
        s["h2"] = (_ln_plain(x1) * (1.0 + sc2_ref[...]) + sh2_ref[...]).astype(BF16)
    for s in subs:
        ff = b2_ref[...]
        for j in range(w1_ref.shape[1] // ff_chunk):
            sl = slice(j * ff_chunk, (j + 1) * ff_chunk)
            hid = jnp.maximum(_dot(s["h2"], w1_ref[:, sl]) + b1_ref[:, sl], 0.0)
            ff = ff + _dot((hid * hid).astype(BF16), w2_ref[sl, :])
        s["ff"] = ff
    for s in subs:
        out_ref[s["rs"], :] = (_ln_plain(alpha * s["x1"] + (1.0 + g2_ref[...]) * s["ff"]) * l2g_ref[...]
                               + l2b_ref[...])


def _outmlp(x2d, hf, hb, o, y5, mod4, w, seq, tm, alpha, ff_chunk):
    t, d = x2d.shape
    spb = seq // tm
    n1 = seq // CHUNK
    kpt = tm // n1
    assert FFT_NB % kpt == 0
    vw = N_HEADS * V_DIM
    row = lambda width: pl.BlockSpec((tm, width), lambda i: (i, 0))
    nblk = CHUNK // FFT_NB
    kblk = lambda i: ((i % spb) * kpt) // FFT_NB
    yspec = lambda g: pl.BlockSpec((None, None, None, n1 * FFT_NB, GROUP_DIM),
                                   lambda i: (i // spb, g, kblk(i), 0, 0))
    ymspec = lambda g: pl.BlockSpec((None, None, None, n1 * FFT_NB, GROUP_DIM),
                                    lambda i: (i // spb, g, nblk - 1 - kblk(i), 0, 0))
    modspec = lambda j: pl.BlockSpec((None, None, 1, d), lambda i: (i // spb, j, 0, 0))
    const = lambda a: pl.BlockSpec(a.shape, lambda i: (0,) * a.ndim, pipeline_mode=pl.Buffered(1))
    names = ["nw", "wout", "l1g", "l1b", "w1", "b1", "w2", "b2", "l2g", "l2b"]
    return pl.pallas_call(
        functools.partial(_outmlp_kernel, alpha, ff_chunk, spb),
        grid=(t // tm,),
        in_specs=[row(d), row(vw), row(vw), row(vw)]
                 + [yspec(g) for g in range(N_GROUPS)] + [ymspec(g) for g in range(N_GROUPS)]
                 + [modspec(2), modspec(4), modspec(3), modspec(5)] + [const(w[k]) for k in names],
        out_specs=row(d),
        out_shape=jax.ShapeDtypeStruct((t, d), F32),
        compiler_params=_vmem_params(("parallel",), 60),
        name="outmlp",
    )(x2d, hf, hb, o, *([y5] * (2 * N_GROUPS)), mod4, mod4, mod4, mod4, *[w[k] for k in names])


def _layer_weights(w_in, b_gate, w_out, w_ff1, w_ff2, b_ff1, b_ff2, mlstm_norm_w, ln1_g, ln1_b, ln2_g, ln2_b):
    qk = N_HEADS * QK_DIM
    vw = N_HEADS * V_DIM
    fw = N_GROUPS * GROUP_DIM
    o0, o1, o2, o3, o4 = qk, 2 * qk, 2 * qk + vw, 2 * qk + 2 * vw, 2 * qk + 2 * vw + fw
    wg = w_in[:, o4:].T.reshape(2, 2, N_HEADS, -1)
    bg = b_gate.astype(F32).reshape(2, 2, N_HEADS, 1)
    dup = lambda a: jnp.concatenate([a, a], axis=1)
    r = lambda a: a.astype(F32).reshape(1, -1)
    wq = w_in[:, :o0] * (QK_DIM ** -0.5)
    gate_rows = lambda a: jnp.concatenate([dup(a[:, 0]), dup(a[:, 1])], axis=1).reshape(32, -1)
    return {
        "wrow": jnp.concatenate([wq, w_in[:, o1:o4]], axis=1).astype(BF16),
        "wlane": jnp.concatenate([w_in[:, o0:o1].T, gate_rows(wg)], axis=0).astype(BF16),
        "bg": gate_rows(bg),
        "nw": r(mlstm_norm_w), "wout": w_out, "l1g": r(ln1_g), "l1b": r(ln1_b),
        "w1": w_ff1, "b1": r(b_ff1), "w2": w_ff2, "b2": r(b_ff2),
        "l2g": r(ln2_g), "l2b": r(ln2_b),
    }


def kernel(x, c, w_ada, b_ada, w_in, b_gate, mlstm_norm_w, w_out, ln1_g, ln1_b,
           w_ff1, b_ff1, w_ff2, b_ff2, ln2_g, ln2_b):
    bsz, seq, d = x.shape
    depth = w_ada.shape[0]
    alpha = (2 * depth) ** 0.25
    assert seq % (MLSTM_CPB * CHUNK) == 0 and (seq // CHUNK) % 8 == 0
    assert d == N_HEADS * V_DIM + N_GROUPS * GROUP_DIM
    tm = min(512, FFT_NB * (seq // CHUNK))
    x2d = x.reshape(bsz * seq, d)
    for l in range(depth):
        w = _layer_weights(w_in[l], b_gate[l], w_out[l], w_ff1[l], w_ff2[l], b_ff1[l], b_ff2[l],
                           mlstm_norm_w[l], ln1_g[l], ln1_b[l], ln2_g[l], ln2_b[l])
        mod4 = _adaln(c, w_ada[l], b_ada[l]).reshape(bsz, N_MOD, 1, d)
        q, kt, v, o, z5, gi, gf, w["wout"], w["w1"], w["w2"] = _inproj(x2d, mod4, w, seq, min(1024, seq))
        ra, rb, rc = _gate_prep(gi, gf, bsz)
        hf, hb = _mlstm(q, kt, v, ra, rb, rc, bsz, seq, MLSTM_CPB)
        y5 = _fourier(z5, seq)
        x2d = _outmlp(x2d, hf, hb, o, y5, mod4, w, seq, tm, alpha, ff_chunk=1024)
    return x2d.reshape(bsz, seq, d)
```

```python
import functools

import numpy as np
import jax
import jax.numpy as jnp
from jax import lax
from jax.experimental import pallas as pl
from jax.experimental.pallas import tpu as pltpu

F32 = jnp.float32
BF16 = jnp.bfloat16

CHUNK = 128
N_HEADS = 4
QK_DIM = 64
V_DIM = 128
N_GROUPS = 4
GROUP_DIM = 128
N_MOD = 6
LN_EPS = 1e-5
MASKED = -1e30
FFT_NB = 8
FFT_KB = 4
FFT_SB = 16
FFT_GB = 4
FFT_BSTEPS = 4
MLSTM_CPB = 16
OUTMLP_SUBTILE_K2 = (2, 2)
INPROJ_SUBTILES = 8

_NT = (((1,), (1,)), ((), ()))


def _dot(a, b):
    return jnp.dot(a, b, preferred_element_type=F32)


def _dot_nt(a, b):
    return lax.dot_general(a, b, _NT, preferred_element_type=F32)


def _ln_plain(x):
    mu = jnp.mean(x, axis=-1, keepdims=True)
    xc = x - mu
    var = jnp.mean(xc * xc, axis=-1, keepdims=True)
    return xc * lax.rsqrt(var + LN_EPS)


def _log_sigmoid(x):
    return jnp.minimum(x, 0.0) - jnp.log1p(jnp.exp(-jnp.abs(x)))


def _split3(x):
    hi = x.astype(BF16)
    r1 = x - hi.astype(F32)
    mid = r1.astype(BF16)
    lo = (r1 - mid.astype(F32)).astype(BF16)
    return hi, mid, lo


INPROJ_ROWS = 2048
OUTMLP_ROWS = 512
VMEM_LIMIT_MIB = dict(adaln=32, inproj=60, gate_prep=32, mlstm=32, fft=56, outmlp=60)


def _vmem_params(semantics, call):
    return pltpu.CompilerParams(dimension_semantics=semantics, vmem_limit_bytes=VMEM_LIMIT_MIB[call] * 1024 * 1024)


def _adaln_kernel(ct_ref, w_ref, b_ref, o_ref):
    ct = ct_ref[...]
    act = ct * jax.nn.sigmoid(ct)
    w = w_ref[...]
    for b in range(ct.shape[1]):
        o_ref[b:b + 1, :] = jnp.sum(act[:, b:b + 1] * w, axis=0, keepdims=True) + b_ref[...]


def _adaln(c, w_ada, b_ada):
    bsz, d = c.shape
    n = w_ada.shape[1]
    tn = 1024
    return pl.pallas_call(
        _adaln_kernel,
        grid=(n // tn,),
        in_specs=[pl.BlockSpec((d, bsz), lambda j: (0, 0)),
                  pl.BlockSpec((d, tn), lambda j: (0, j)),
                  pl.BlockSpec((1, tn), lambda j: (0, j))],
        out_specs=pl.BlockSpec((bsz, tn), lambda j: (0, j)),
        out_shape=jax.ShapeDtypeStruct((bsz, n), F32),
        compiler_params=_vmem_params(("parallel",), "adaln"),
        name="adaln",
    )(c.T, w_ada, b_ada.reshape(1, n))


def _inproj_kernel(x_ref, sh_ref, sc_ref, wrow_ref, wlane_ref, bg_ref, wout_ref, w1_ref, w2_ref,
                   q_ref, kt_ref, v_ref, o_ref, z_ref, gi_ref, gf_ref, woutb_ref, w1b_ref, w2b_ref):
    woutb_ref[...] = wout_ref[...].astype(BF16)
    w1b_ref[...] = w1_ref[...].astype(BF16)
    w2b_ref[...] = w2_ref[...].astype(BF16)
    qk, vw = q_ref.shape[1], v_ref.shape[1]
    n_chunks = x_ref.shape[0] // CHUNK
    cps = max(1, n_chunks // INPROJ_SUBTILES)
    for c0 in range(0, n_chunks, cps):
        rs = slice(c0 * CHUNK, (c0 + cps) * CHUNK)
        h = _ln_plain(x_ref[rs, :]) * (1.0 + sc_ref[...]) + sh_ref[...]
        hb = h.astype(BF16)
        rowp = _dot(hb, wrow_ref[...])
        q_ref[rs, :] = rowp[:, :qk].astype(BF16)
        v_ref[rs, :] = rowp[:, qk:qk + vw].astype(BF16)
        o_ref[rs, :] = rowp[:, qk + vw:qk + 2 * vw].astype(BF16)
        fz = rowp[:, qk + 2 * vw:]
        for j in range(cps):
            for g in range(N_GROUPS):
                for jb in range(CHUNK // FFT_NB):
                    r0 = j * CHUNK + jb * FFT_NB
                    z_ref[g, jb, (c0 + j) * FFT_NB:(c0 + j + 1) * FFT_NB, :] = (
                        fz[r0:r0 + FFT_NB, g * GROUP_DIM:(g + 1) * GROUP_DIM])
        lanep = _dot_nt(wlane_ref[...], hb)
        gates = lanep[qk:, :] + bg_ref[...]
        for j in range(cps):
            cl = slice(j * CHUNK, (j + 1) * CHUNK)
            kt_ref[c0 + j] = lanep[:qk, cl].astype(BF16)
            for d in range(2):
                gi_ref[d, (c0 + j) * 8:(c0 + j + 1) * 8, :] = gates[16 * d:16 * d + 8, cl]
                gf_ref[d, (c0 + j) * 8:(c0 + j + 1) * 8, :] = gates[16 * d + 8:16 * d + 16, cl]


def _inproj(x2d, mod4, w, seq, tm):
    t, d = x2d.shape
    spb = seq // tm
    qk = N_HEADS * QK_DIM
    vw = N_HEADS * V_DIM
    nblk = CHUNK // FFT_NB
    cpt = tm // CHUNK
    full = lambda a: pl.BlockSpec(a.shape, lambda i: (0,) * a.ndim)
    modspec = lambda j: pl.BlockSpec((None, None, 1, d), lambda i: (i // spb, j, 0, 0))
    names = ["wrow", "wlane", "bg"]
    steps = t // tm
    casts = [w["wout"], w["w1"], w["w2"]]
    assert all(a.shape[0] % (16 * steps) == 0 for a in casts)
    slab = lambda a: pl.BlockSpec((a.shape[0] // steps, a.shape[1]), lambda i: (i, 0))
    return pl.pallas_call(
        _inproj_kernel,
        grid=(steps,),
        in_specs=[pl.BlockSpec((tm, d), lambda i: (i, 0)), modspec(0), modspec(1)] + [full(w[k]) for k in names]
                 + [slab(a) for a in casts],
        out_specs=[pl.BlockSpec((tm, qk), lambda i: (i, 0)),
                   pl.BlockSpec((cpt, qk, CHUNK), lambda i: (i, 0, 0)),
                   pl.BlockSpec((tm, vw), lambda i: (i, 0)),
                   pl.BlockSpec((tm, vw), lambda i: (i, 0)),
                   pl.BlockSpec((None, N_GROUPS, nblk, cpt * FFT_NB, GROUP_DIM),
                                lambda i: (i // spb, 0, 0, i % spb, 0)),
                   pl.BlockSpec((2, cpt * 8, CHUNK), lambda i: (0, i, 0)),
                   pl.BlockSpec((2, cpt * 8, CHUNK), lambda i: (0, i, 0))] + [slab(a) for a in casts],
        out_shape=[jax.ShapeDtypeStruct((t, qk), BF16),
                   jax.ShapeDtypeStruct((t // CHUNK, qk, CHUNK), BF16),
                   jax.ShapeDtypeStruct((t, vw), BF16),
                   jax.ShapeDtypeStruct((t, vw), BF16),
                   jax.ShapeDtypeStruct((t // seq, N_GROUPS, nblk, (seq // CHUNK) * FFT_NB, GROUP_DIM), F32),
                   jax.ShapeDtypeStruct((2, t // CHUNK * 8, CHUNK), F32),
                   jax.ShapeDtypeStruct((2, t // CHUNK * 8, CHUNK), F32)]
                  + [jax.ShapeDtypeStruct(a.shape, BF16) for a in casts],
        compiler_params=_vmem_params(("parallel",), "inproj"),
        name="inproj",
    )(x2d, mod4, mod4, *[w[k] for k in names], *casts)


def _gate_prep_kernel(gi_ref, gf_ref, ra_ref, rb_ref, rc_ref, bend_scr, amax_scr, mprev_scr):
    rows = gi_ref.shape[1]
    n_chunks = rows // 8
    lane = lax.broadcasted_iota(jnp.int32, (rows, CHUNK), 1)
    first = (lax.broadcasted_iota(jnp.int32, (rows, CHUNK), 0) & 7) < N_HEADS
    src = lax.broadcasted_iota(jnp.int32, (CHUNK, 2 * CHUNK), 0)
    dst = lax.broadcasted_iota(jnp.int32, (CHUNK, 2 * CHUNK), 1)
    for d in range(2):
        feeds = (src <= dst) if d == 0 else (src >= dst)
        cum_and_total = jnp.where((dst >= CHUNK) | feeds, 1.0, 0.0).astype(BF16)
        bc = sum(_dot(p, cum_and_total) for p in _split3(_log_sigmoid(gf_ref[d])))
        b, b_end = bc[:, :CHUNK], bc[:, CHUNK:]
        a = gi_ref[d] - b
        cm = a
        for k in range(7):
            sh = 1 << k
            if d == 0:
                cm = jnp.maximum(cm, jnp.where(lane >= sh, pltpu.roll(cm, sh, 1), MASKED))
            else:
                cm = jnp.maximum(cm, jnp.where(lane < CHUNK - sh, pltpu.roll(cm, CHUNK - sh, 1), MASKED))
        a_max = jnp.broadcast_to(jnp.max(a, axis=1, keepdims=True), a.shape)
        bend_scr[...] = b_end
        amax_scr[...] = a_max

        def scan(c, m):
            r = pl.ds(pl.multiple_of((c if d == 0 else n_chunks - 1 - c) * 8, 8), 8)
            mprev_scr[r, :] = m
            return bend_scr[r, :] + jnp.maximum(m, amax_scr[r, :])

        lax.fori_loop(0, n_chunks, scan, jnp.zeros((8, CHUNK), F32))
        m_prev = mprev_scr[...]
        u = -jnp.maximum(m_prev, cm)
        m_new = b_end + jnp.maximum(m_prev, a_max)
        ra_ref[d] = jnp.where(first, a, jnp.exp(b_end + a - m_new))
        rb_ref[d] = jnp.where(first, u, jnp.exp(m_prev + u))
        rc_ref[d] = jnp.where(first, u - b, jnp.exp(b_end + m_prev - m_new))


def _gate_prep(gi, gf, bsz):
    rows = gi.shape[1] // bsz
    spec = pl.BlockSpec((2, rows, CHUNK), lambda b: (0, b, 0))
    shape = jax.ShapeDtypeStruct(gi.shape, F32)
    return pl.pallas_call(
        _gate_prep_kernel,
        grid=(bsz,),
        in_specs=[spec, spec],
        out_specs=[spec, spec, spec],
        out_shape=[shape, shape, shape],
        scratch_shapes=[pltpu.VMEM((rows, CHUNK), F32)] * 3,
        compiler_params=_vmem_params(("parallel",), "gate_prep"),
        name="gate_prep",
    )(gi, gf)


@functools.lru_cache(maxsize=None)
def _column_selector():
    sel = np.zeros((48, (N_HEADS // 2) * 5 * CHUNK), np.float32)

    def pick(arr, row, lane0, width):
        for piece in range(3):
            sel[arr * 24 + piece * 8 + row, lane0:lane0 + width] = 1.0

    for hd in range(N_HEADS):
        base = 5 * (hd // 2) * CHUNK
        pick(0, hd, base + (hd % 2) * CHUNK, CHUNK)
        pick(1, hd, base + (2 + hd % 2) * CHUNK, CHUNK)
        pick(0, N_HEADS + hd, base + 4 * CHUNK + (hd % 2) * QK_DIM, QK_DIM)
    return sel


def _mlstm_kernel(sel_ref, qf_ref, ktf_ref, vf_ref, raf_ref, rbf_ref, rcf_ref,
                  qb_ref, ktb_ref, vb_ref, rab_ref, rbb_ref, rcb_ref, hf_ref, hb_ref, c_scr):
    @pl.when(pl.program_id(1) == 0)
    def _():
        c_scr[...] = jnp.zeros_like(c_scr)

    cpb = ktf_ref.shape[0]
    t_i = lax.broadcasted_iota(jnp.int32, (CHUNK, CHUNK), 0)
    s_i = lax.broadcasted_iota(jnp.int32, (CHUNK, CHUNK), 1)
    visible = (s_i <= t_i, s_i >= t_i)
    ones = jnp.ones((CHUNK, V_DIM), BF16)
    kzero = jnp.zeros((QK_DIM, CHUNK), BF16)
    czero = jnp.zeros((QK_DIM, 2 * V_DIM), BF16)
    dirs = ((qf_ref, ktf_ref, vf_ref, raf_ref, rbf_ref, rcf_ref, hf_ref),
            (qb_ref, ktb_ref, vb_ref, rab_ref, rbb_ref, rcb_ref, hb_ref))

    def body(i, carry):
        units = []
        for d, (q_ref, kt_ref, v_ref, ra_ref, rb_ref, rc_ref, h_ref) in enumerate(dirs):
            li = i if d == 0 else cpb - 1 - i
            r8 = pl.ds(pl.multiple_of(li * 8, 8), 8)
            rl = pl.ds(pl.multiple_of(li * CHUNK, CHUNK), CHUNK)
            ra = ra_ref[r8, :]
            rc = rc_ref[r8, :]
            pieces = [p.astype(F32) for arr in (rb_ref[r8, :], rc) for p in _split3(arr)]
            colb = lax.dot_general(jnp.concatenate(pieces, axis=0).astype(BF16), sel_ref[...],
                                   (((0,), (0,)), ((), ())), preferred_element_type=F32)
            kt = kt_ref[li]
            for pair in range(N_HEADS // 2):
                k_a = kt[2 * pair * QK_DIM:(2 * pair + 1) * QK_DIM, :]
                k_b = kt[(2 * pair + 1) * QK_DIM:(2 * pair + 2) * QK_DIM, :]
                kdiag = jnp.concatenate([jnp.concatenate([k_a, kzero], axis=1),
                                         jnp.concatenate([kzero, k_b], axis=1)], axis=0)
                q2 = q_ref[rl, 2 * pair * QK_DIM:2 * (pair + 1) * QK_DIM]
                qk2 = _dot(q2, kdiag)
                w_inter2 = colb[:, (5 * pair + 4) * CHUNK:(5 * pair + 5) * CHUNK]
                q_inter2 = (q2.astype(F32) * w_inter2).astype(BF16)
                for hd in (2 * pair, 2 * pair + 1):
                    kth = kt[hd * QK_DIM:(hd + 1) * QK_DIM, :]
                    units.append(dict(d=d, hd=hd, rl=rl, h_ref=h_ref, ra=ra, rc=rc, colb=colb, kth=kth,
                                      q_inter2=q_inter2, v_ref=v_ref,
                                      qk=qk2[:, (hd % 2) * CHUNK:(hd % 2 + 1) * CHUNK]))
        for u in units:
            d, hd, colb = u["d"], u["hd"], u["colb"]
            c0 = (5 * (hd // 2) + hd % 2) * CHUNK
            u_b = colb[:, c0:c0 + CHUNK]
            w_intra = jnp.exp(jnp.where(visible[d], u_b + u["ra"][hd:hd + 1, :], MASKED))
            s = (u["qk"] * w_intra).astype(BF16)
            u["c_prev"] = c_scr[d, hd]
            c_bf = u["c_prev"].astype(BF16)
            c_rows = [c_bf, czero] if hd % 2 == 0 else [czero, c_bf]
            u["v_aug"] = jnp.concatenate([u["v_ref"][u["rl"], hd * V_DIM:(hd + 1) * V_DIM], ones], axis=1)
            u["nd"] = _dot(jnp.concatenate([s, u["q_inter2"]], axis=1),
                           jnp.concatenate([u["v_aug"]] + c_rows, axis=0))
        for u in units:
            hd, nd = u["hd"], u["nd"]
            e0 = (5 * (hd // 2) + 2 + hd % 2) * CHUNK
            exp_neg_m = jnp.exp(u["colb"][:, e0:e0 + CHUNK])
            den = jnp.maximum(jnp.abs(nd[:, V_DIM:]), exp_neg_m)
            u["h_ref"][u["rl"], hd * V_DIM:(hd + 1) * V_DIM] = (nd[:, :V_DIM] / den).astype(u["h_ref"].dtype)
            wk_row = u["ra"][N_HEADS + hd:N_HEADS + hd + 1, :]
            decay = jnp.broadcast_to(u["rc"][N_HEADS + hd:N_HEADS + hd + 1, :], (QK_DIM, CHUNK))
            kw = (u["kth"].astype(F32) * wk_row).astype(BF16)
            c_scr[u["d"], hd] = jnp.concatenate([decay, decay], axis=1) * u["c_prev"] + _dot(kw, u["v_aug"])
        return carry

    lax.fori_loop(0, cpb, body, 0, unroll=4)


def _mlstm(q, kt, v, ra, rb, rc, bsz, seq, cpb):
    t = q.shape[0]
    nblk = seq // (cpb * CHUNK)
    qk = N_HEADS * QK_DIM
    vw = N_HEADS * V_DIM
    fwd = lambda b, j: b * nblk + j
    bwd = lambda b, j: b * nblk + nblk - 1 - j

    def specs(blk, d):
        return [pl.BlockSpec((cpb * CHUNK, qk), lambda b, j: (blk(b, j), 0)),
                pl.BlockSpec((cpb, qk, CHUNK), lambda b, j: (blk(b, j), 0, 0)),
                pl.BlockSpec((cpb * CHUNK, vw), lambda b, j: (blk(b, j), 0))] + \
               [pl.BlockSpec((None, cpb * 8, CHUNK), lambda b, j: (d, blk(b, j), 0))] * 3

    sel = jnp.asarray(_column_selector()).astype(BF16)
    return pl.pallas_call(
        _mlstm_kernel,
        grid=(bsz, nblk),
        in_specs=[pl.BlockSpec(sel.shape, lambda b, j: (0, 0))] + specs(fwd, 0) + specs(bwd, 1),
        out_specs=[pl.BlockSpec((cpb * CHUNK, vw), lambda b, j: (fwd(b, j), 0)),
                   pl.BlockSpec((cpb * CHUNK, vw), lambda b, j: (bwd(b, j), 0))],
        out_shape=[jax.ShapeDtypeStruct((t, vw), BF16)] * 2,
        scratch_shapes=[pltpu.VMEM((2, N_HEADS, QK_DIM, 2 * V_DIM), F32)],
        compiler_params=_vmem_params(("parallel", "arbitrary"), "mlstm"),
        name="mlstm",
    )(sel, q, kt, v, ra, rb, rc, q, kt, v, ra, rb, rc)


def _fft_kh(n1):
    return n1 // 2 + FFT_KB


@functools.lru_cache(maxsize=None)
def _fft_tables(seq):
    n1 = seq // CHUNK
    kh = _fft_kh(n1)
    two_pi = 2.0 * np.pi
    k1 = np.arange(kh, dtype=np.int64)
    n = 128 * np.arange(n1, dtype=np.int64)[None, None, :] + np.arange(128, dtype=np.int64)[:, None, None]
    ang = two_pi * ((k1[None, :, None] * n) % seq).astype(np.float64) / seq
    ga = np.concatenate([np.cos(ang), -np.sin(ang)], axis=1)
    j = np.arange(128, dtype=np.int64)
    a128 = two_pi * ((j[:, None] * j[None, :]) % 128).astype(np.float64) / 128.0
    ff = np.concatenate([np.cos(a128), -np.sin(a128)], axis=0)
    cc, sc = np.cos(a128), np.sin(a128)
    cs = np.block([[cc, cc], [sc, -sc]]) / np.sqrt(128.0 * seq)
    return tuple(np.asarray(a, dtype=np.float32) for a in (ga, ff, cs))


def _fft_kernel(z_ref, ga_ref, ff_ref, cs_ref, y_ref, scr):
    j = pl.program_id(2)
    n1 = ga_ref.shape[2]
    kh2 = ga_ref.shape[1]
    half = n1 // 2
    a_steps = CHUNK // (FFT_NB * z_ref.shape[0])

    @pl.when(j < a_steps)
    def _():
        for sb in range(z_ref.shape[0]):
            for l in range(FFT_NB):
                n2 = (j * z_ref.shape[0] + sb) * FFT_NB + l
                zl = z_ref[sb, pl.ds(l, n1, stride=FFT_NB), :].astype(BF16)
                row0 = pl.multiple_of(n2 * kh2, 8)
                scr[pl.ds(row0, kh2), :] = _dot(ga_ref[n2], zl)

    def spectra(k0s):
        ms = []
        for k0 in k0s:
            p = [scr[pl.ds(ri * (kh2 // 2) + k0 + kk, CHUNK, stride=kh2), :]
                 for kk in range(FFT_KB) for ri in range(2)]
            ms.append(_dot(ff_ref[...], jnp.concatenate(p, axis=1).astype(BF16)))
        ys = []
        for m in ms:
            x = []
            for kk in range(FFT_KB):
                c0 = 2 * kk * GROUP_DIM
                xr = m[:CHUNK, c0:c0 + GROUP_DIM] - m[CHUNK:, c0 + GROUP_DIM:c0 + 2 * GROUP_DIM]
                xi = m[:CHUNK, c0 + GROUP_DIM:c0 + 2 * GROUP_DIM] + m[CHUNK:, c0:c0 + GROUP_DIM]
                x.append(jnp.concatenate([xr, xi], axis=1))
            ys.append(_dot(jnp.concatenate(x, axis=0).astype(BF16), cs_ref[...]))
        return ys

    def store(y, kk, k1, lane0):
        row0 = pl.multiple_of(k1 * FFT_NB, 8)
        for kb in range(CHUNK // FFT_NB):
            r0 = kk * CHUNK + kb * FFT_NB
            y_ref[kb, pl.ds(row0, FFT_NB), :] = y[r0:r0 + FFT_NB, lane0:lane0 + GROUP_DIM]

    k1_step = half // FFT_BSTEPS
    gb = max(1, min(FFT_GB, k1_step // FFT_KB))

    @pl.when(j >= a_steps)
    def _():
        def body(it, carry):
            k0s = [(j - a_steps) * k1_step + (it * gb + gi) * FFT_KB for gi in range(gb)]
            for k0, y in zip(k0s, spectra(k0s)):
                for kk in range(FFT_KB):
                    k1 = k0 + kk
                    store(y, kk, jnp.where(k1 == 0, 0, n1 - k1), GROUP_DIM)
                    store(y, kk, k1, 0)
            return carry

        lax.fori_loop(0, k1_step // (FFT_KB * gb), body, 0)

    @pl.when(j == a_steps + FFT_BSTEPS - 1)
    def _():
        (y,) = spectra([half])
        store(y, 0, half, GROUP_DIM)


def _fourier(z5, seq):
    bsz = z5.shape[0]
    n1 = seq // CHUNK
    nblk = CHUNK // FFT_NB
    a_steps = nblk // FFT_SB
    ga, ff, cs = (jnp.asarray(a).astype(BF16) for a in _fft_tables(seq))
    return pl.pallas_call(
        _fft_kernel,
        grid=(bsz, N_GROUPS, a_steps + FFT_BSTEPS),
        in_specs=[pl.BlockSpec((None, None, FFT_SB, n1 * FFT_NB, GROUP_DIM),
                               lambda b, g, j: (b, g, jnp.minimum(j, a_steps - 1), 0, 0)),
                  pl.BlockSpec(ga.shape, lambda b, g, j: (0, 0, 0), pipeline_mode=pl.Buffered(1)),
                  pl.BlockSpec(ff.shape, lambda b, g, j: (0, 0)),
                  pl.BlockSpec(cs.shape, lambda b, g, j: (0, 0))],
        out_specs=pl.BlockSpec((None, None, nblk, n1 * FFT_NB, GROUP_DIM), lambda b, g, j: (b, g, 0, 0, 0)),
        out_shape=jax.ShapeDtypeStruct(z5.shape, F32),
        scratch_shapes=[pltpu.VMEM((CHUNK * 2 * _fft_kh(n1), GROUP_DIM), F32)],
        compiler_params=_vmem_params(("parallel", "parallel", "arbitrary"), "fft"),
        name="fft",
    )(z5, ga, ff, cs)


def _gather_rows(y_refs, k2l, n1):
    half = n1 // 2
    low = jnp.concatenate([a[pl.ds(k2l, half, stride=FFT_NB), :] for a, _ in y_refs], axis=1)
    high = jnp.concatenate([b[pl.ds(half * FFT_NB + FFT_NB - 1 - k2l, half, stride=FFT_NB), :] for _, b in y_refs],
                           axis=1)
    return jnp.concatenate([low, high], axis=0)


def _outmlp_kernel(alpha, ff_chunk, spb, x_ref, hf_ref, hb_ref, o_ref,
                   y0_ref, y1_ref, y2_ref, y3_ref, ym0_ref, ym1_ref, ym2_ref, ym3_ref,
                   g1_ref, sc2_ref, sh2_ref, g2_ref,
                   nw_ref, wout_ref, l1g_ref, l1b_ref, w1_ref, b1_ref, w2_ref, b2_ref, l2g_ref, l2b_ref,
                   out_ref):
    n1 = y0_ref.shape[0] // FFT_NB
    kpt = x_ref.shape[0] // n1
    k2_off = ((pl.program_id(0) % spb) * kpt) % FFT_NB
    sizes = OUTMLP_SUBTILE_K2 if sum(OUTMLP_SUBTILE_K2) == kpt else (kpt,)
    subs, k2l = [], 0
    for kps in sizes:
        subs.append(dict(k2l=k2l, kps=kps, rs=slice(k2l * n1, (k2l + kps) * n1)))
        k2l += kps
    for s in subs:
        rs = s["rs"]
        y_refs = ((y0_ref, ym0_ref), (y1_ref, ym1_ref), (y2_ref, ym2_ref), (y3_ref, ym3_ref))
        yf = jnp.concatenate([_gather_rows(y_refs, k2_off + s["k2l"] + kk, n1) for kk in range(s["kps"])],
                             axis=0).astype(BF16)
        hsum = hf_ref[rs, :].astype(F32) + hb_ref[rs, :].astype(F32)
        heads = [_ln_plain(hsum[:, j * V_DIM:(j + 1) * V_DIM]) for j in range(N_HEADS)]
        ym = jnp.concatenate(heads, axis=1) * nw_ref[...] * jax.nn.sigmoid(o_ref[rs, :].astype(F32))
        s["mix"] = _dot(jnp.concatenate([ym.astype(BF16), yf], axis=1), wout_ref[...])
    for s in subs:
        x1 = _ln_plain(alpha * x_ref[s["rs"], :] + (1.0 + g1_ref[...]) * s["mix"]) * l1g_ref[...] + l1b_ref[...]
        s["x1"] = x1
        s["h2"] = (_ln_plain(x1) * (1.0 + sc2_ref[...]) + sh2_ref[...]).astype(BF16)
    for s in subs:
        ff = b2_ref[...]
        for j in range(w1_ref.shape[1] // ff_chunk):
            sl = slice(j * ff_chunk, (j + 1) * ff_chunk)
            hid = jnp.maximum(_dot(s["h2"], w1_ref[:, sl]) + b1_ref[:, sl], 0.0)
            ff = ff + _dot((hid * hid).astype(BF16), w2_ref[sl, :])
        s["ff"] = ff
    for s in subs:
        out_ref[s["rs"], :] = (_ln_plain(alpha * s["x1"] + (1.0 + g2_ref[...]) * s["ff"]) * l2g_ref[...]
                               + l2b_ref[...])


def _outmlp(x2d, hf, hb, o, y5, mod4, w, seq, tm, alpha, ff_chunk):
    t, d = x2d.shape
    spb = seq // tm
    n1 = seq // CHUNK
    kpt = tm // n1
    assert FFT_NB % kpt == 0
    vw = N_HEADS * V_DIM
    row = lambda width: pl.BlockSpec((tm, width), lambda i: (i, 0))
    nblk = CHUNK // FFT_NB
    kblk = lambda i: ((i % spb) * kpt) // FFT_NB
    yspec = lambda g: pl.BlockSpec((None, None, None, n1 * FFT_NB, GROUP_DIM),
                                   lambda i: (i // spb, g, kblk(i), 0, 0))
    ymspec = lambda g: pl.BlockSpec((None, None, None, n1 * FFT_NB, GROUP_DIM),
                                    lambda i: (i // spb, g, nblk - 1 - kblk(i), 0, 0))
    modspec = lambda j: pl.BlockSpec((None, None, 1, d), lambda i: (i // spb, j, 0, 0))
    const = lambda a: pl.BlockSpec(a.shape, lambda i: (0,) * a.ndim, pipeline_mode=pl.Buffered(1))
    names = ["nw", "wout", "l1g", "l1b", "w1", "b1", "w2", "b2", "l2g", "l2b"]
    return pl.pallas_call(
        functools.partial(_outmlp_kernel, alpha, ff_chunk, spb),
        grid=(t // tm,),
        in_specs=[row(d), row(vw), row(vw), row(vw)]
                 + [yspec(g) for g in range(N_GROUPS)] + [ymspec(g) for g in range(N_GROUPS)]
                 + [modspec(2), modspec(4), modspec(3), modspec(5)] + [const(w[k]) for k in names],
        out_specs=row(d),
        out_shape=jax.ShapeDtypeStruct((t, d), F32),
        compiler_params=_vmem_params(("parallel",), "outmlp"),
        name="outmlp",
    )(x2d, hf, hb, o, *([y5] * (2 * N_GROUPS)), mod4, mod4, mod4, mod4, *[w[k] for k in names])


def _layer_weights(w_in, b_gate, w_out, w_ff1, w_ff2, b_ff1, b_ff2, mlstm_norm_w, ln1_g, ln1_b, ln2_g, ln2_b):
    qk = N_HEADS * QK_DIM
    vw = N_HEADS * V_DIM
    fw = N_GROUPS * GROUP_DIM
    o0, o1, o2, o3, o4 = qk, 2 * qk, 2 * qk + vw, 2 * qk + 2 * vw, 2 * qk + 2 * vw + fw
    wg = w_in[:, o4:].T.reshape(2, 2, N_HEADS, -1)
    bg = b_gate.astype(F32).reshape(2, 2, N_HEADS, 1)
    dup = lambda a: jnp.concatenate([a, a], axis=1)
    r = lambda a: a.astype(F32).reshape(1, -1)
    wq = w_in[:, :o0] * (QK_DIM ** -0.5)
    gate_rows = lambda a: jnp.concatenate([dup(a[:, 0]), dup(a[:, 1])], axis=1).reshape(32, -1)
    return {
        "wrow": jnp.concatenate([wq, w_in[:, o1:o4]], axis=1).astype(BF16),
        "wlane": jnp.concatenate([w_in[:, o0:o1].T, gate_rows(wg)], axis=0).astype(BF16),
        "bg": gate_rows(bg),
        "nw": r(mlstm_norm_w), "wout": w_out, "l1g": r(ln1_g), "l1b": r(ln1_b),
        "w1": w_ff1, "b1": r(b_ff1), "w2": w_ff2, "b2": r(b_ff2),
        "l2g": r(ln2_g), "l2b": r(ln2_b),
    }


def kernel(x, c, w_ada, b_ada, w_in, b_gate, mlstm_norm_w, w_out, ln1_g, ln1_b,
           w_ff1, b_ff1, w_ff2, b_ff2, ln2_g, ln2_b):
    bsz, seq, d = x.shape
    depth = w_ada.shape[0]
    alpha = (2 * depth) ** 0.25
    assert seq % (MLSTM_CPB * CHUNK) == 0 and (seq // CHUNK) % 8 == 0
    assert d == N_HEADS * V_DIM + N_GROUPS * GROUP_DIM
    tm_out = min(OUTMLP_ROWS, FFT_NB * (seq // CHUNK))
    x2d = x.reshape(bsz * seq, d)
    for l in range(depth):
        w = _layer_weights(w_in[l], b_gate[l], w_out[l], w_ff1[l], w_ff2[l], b_ff1[l], b_ff2[l],
                           mlstm_norm_w[l], ln1_g[l], ln1_b[l], ln2_g[l], ln2_b[l])
        mod4 = _adaln(c, w_ada[l], b_ada[l]).reshape(bsz, N_MOD, 1, d)
        q, kt, v, o, z5, gi, gf, w["wout"], w["w1"], w["w2"] = _inproj(x2d, mod4, w, seq, min(INPROJ_ROWS, seq))
        ra, rb, rc = _gate_prep(gi, gf, bsz)
        hf, hb = _mlstm(q, kt, v, ra, rb, rc, bsz, seq, MLSTM_CPB)
        y5 = _fourier(z5, seq)
        x2d = _outmlp(x2d, hf, hb, o, y5, mod4, w, seq, tm_out, alpha, ff_chunk=1024)
    return x2d.reshape(bsz, seq, d)
```

```python
import functools

import numpy as np
import jax
import jax.numpy as jnp
from jax import lax
from jax.experimental import pallas as pl
from jax.experimental.pallas import tpu as pltpu

F32 = jnp.float32
BF16 = jnp.bfloat16

CHUNK = 128
N_HEADS = 4
QK_DIM = 64
V_DIM = 128
N_GROUPS = 4
GROUP_DIM = 128
N_MOD = 6
LN_EPS = 1e-5
MASKED = -1e30
FFT_NB = 8
FFT_KB = 4
FFT_SB = 16
FFT_GB = 4
FFT_BSTEPS = 4
MLSTM_CPB = 8
OUTMLP_SUBTILE_K2 = (2, 2)
INPROJ_SUBTILES = 4

_NT = (((1,), (1,)), ((), ()))


def _dot(a, b):
    return jnp.dot(a, b, preferred_element_type=F32)


def _dot_nt(a, b):
    return lax.dot_general(a, b, _NT, preferred_element_type=F32)


def _ln_plain(x):
    mu = jnp.mean(x, axis=-1, keepdims=True)
    xc = x - mu
    var = jnp.mean(xc * xc, axis=-1, keepdims=True)
    return xc * lax.rsqrt(var + LN_EPS)


def _log_sigmoid(x):
    return jnp.minimum(x, 0.0) - jnp.log1p(jnp.exp(-jnp.abs(x)))


def _split3(x):
    hi = x.astype(BF16)
    r1 = x - hi.astype(F32)
    mid = r1.astype(BF16)
    lo = (r1 - mid.astype(F32)).astype(BF16)
    return hi, mid, lo


INPROJ_ROWS = 1024
OUTMLP_ROWS = 512
VMEM_LIMIT_MIB = dict(adaln=32, inproj=48, gate_prep=32, mlstm=32, fft=56, outmlp=60)


def _vmem_params(semantics, call):
    return pltpu.CompilerParams(dimension_semantics=semantics, vmem_limit_bytes=VMEM_LIMIT_MIB[call] * 1024 * 1024)


def _adaln_kernel(ct_ref, w_ref, b_ref, o_ref):
    ct = ct_ref[...]
    act = ct * jax.nn.sigmoid(ct)
    w = w_ref[...]
    for b in range(ct.shape[1]):
        o_ref[b:b + 1, :] = jnp.sum(act[:, b:b + 1] * w, axis=0, keepdims=True) + b_ref[...]


def _adaln(c, w_ada, b_ada):
    bsz, d = c.shape
    n = w_ada.shape[1]
    tn = 1024
    return pl.pallas_call(
        _adaln_kernel,
        grid=(n // tn,),
        in_specs=[pl.BlockSpec((d, bsz), lambda j: (0, 0)),
                  pl.BlockSpec((d, tn), lambda j: (0, j)),
                  pl.BlockSpec((1, tn), lambda j: (0, j))],
        out_specs=pl.BlockSpec((bsz, tn), lambda j: (0, j)),
        out_shape=jax.ShapeDtypeStruct((bsz, n), F32),
        compiler_params=_vmem_params(("parallel",), "adaln"),
        name="adaln",
    )(c.T, w_ada, b_ada.reshape(1, n))


def _inproj_kernel(x_ref, sh_ref, sc_ref, wrow_ref, wlane_ref, bg_ref, wout_ref, w1_ref, w2_ref,
                   q_ref, kt_ref, v_ref, o_ref, z_ref, gi_ref, gf_ref, woutb_ref, w1b_ref, w2b_ref):
    woutb_ref[...] = wout_ref[...].astype(BF16)
    w1b_ref[...] = w1_ref[...].astype(BF16)
    w2b_ref[...] = w2_ref[...].astype(BF16)
    qk, vw = q_ref.shape[1], v_ref.shape[1]
    n_chunks = x_ref.shape[0] // CHUNK
    cps = max(1, n_chunks // INPROJ_SUBTILES)
    for c0 in range(0, n_chunks, cps):
        rs = slice(c0 * CHUNK, (c0 + cps) * CHUNK)
        h = _ln_plain(x_ref[rs, :]) * (1.0 + sc_ref[...]) + sh_ref[...]
        hb = h.astype(BF16)
        rowp = _dot(hb, wrow_ref[...])
        q_ref[rs, :] = rowp[:, :qk].astype(BF16)
        v_ref[rs, :] = rowp[:, qk:qk + vw].astype(BF16)
        o_ref[rs, :] = rowp[:, qk + vw:qk + 2 * vw].astype(BF16)
        fz = rowp[:, qk + 2 * vw:]
        for j in range(cps):
            for g in range(N_GROUPS):
                for jb in range(CHUNK // FFT_NB):
                    r0 = j * CHUNK + jb * FFT_NB
                    z_ref[g, jb, (c0 + j) * FFT_NB:(c0 + j + 1) * FFT_NB, :] = (
                        fz[r0:r0 + FFT_NB, g * GROUP_DIM:(g + 1) * GROUP_DIM])
        lanep = _dot_nt(wlane_ref[...], hb)
        gates = lanep[qk:, :] + bg_ref[...]
        for j in range(cps):
            cl = slice(j * CHUNK, (j + 1) * CHUNK)
            kt_ref[c0 + j] = lanep[:qk, cl].astype(BF16)
            for d in range(2):
                gi_ref[d, (c0 + j) * 8:(c0 + j + 1) * 8, :] = gates[16 * d:16 * d + 8, cl]
                gf_ref[d, (c0 + j) * 8:(c0 + j + 1) * 8, :] = gates[16 * d + 8:16 * d + 16, cl]


def _inproj(x2d, mod4, w, seq, tm):
    t, d = x2d.shape
    spb = seq // tm
    qk = N_HEADS * QK_DIM
    vw = N_HEADS * V_DIM
    nblk = CHUNK // FFT_NB
    cpt = tm // CHUNK
    full = lambda a: pl.BlockSpec(a.shape, lambda i: (0,) * a.ndim)
    modspec = lambda j: pl.BlockSpec((None, None, 1, d), lambda i: (i // spb, j, 0, 0))
    names = ["wrow", "wlane", "bg"]
    steps = t // tm
    casts = [w["wout"], w["w1"], w["w2"]]
    assert all(a.shape[0] % (16 * steps) == 0 for a in casts)
    slab = lambda a: pl.BlockSpec((a.shape[0] // steps, a.shape[1]), lambda i: (i, 0))
    return pl.pallas_call(
        _inproj_kernel,
        grid=(steps,),
        in_specs=[pl.BlockSpec((tm, d), lambda i: (i, 0)), modspec(0), modspec(1)] + [full(w[k]) for k in names]
                 + [slab(a) for a in casts],
        out_specs=[pl.BlockSpec((tm, qk), lambda i: (i, 0)),
                   pl.BlockSpec((cpt, qk, CHUNK), lambda i: (i, 0, 0)),
                   pl.BlockSpec((tm, vw), lambda i: (i, 0)),
                   pl.BlockSpec((tm, vw), lambda i: (i, 0)),
                   pl.BlockSpec((None, N_GROUPS, nblk, cpt * FFT_NB, GROUP_DIM),
                                lambda i: (i // spb, 0, 0, i % spb, 0)),
                   pl.BlockSpec((2, cpt * 8, CHUNK), lambda i: (0, i, 0)),
                   pl.BlockSpec((2, cpt * 8, CHUNK), lambda i: (0, i, 0))] + [slab(a) for a in casts],
        out_shape=[jax.ShapeDtypeStruct((t, qk), BF16),
                   jax.ShapeDtypeStruct((t // CHUNK, qk, CHUNK), BF16),
                   jax.ShapeDtypeStruct((t, vw), BF16),
                   jax.ShapeDtypeStruct((t, vw), BF16),
                   jax.ShapeDtypeStruct((t // seq, N_GROUPS, nblk, (seq // CHUNK) * FFT_NB, GROUP_DIM), F32),
                   jax.ShapeDtypeStruct((2, t // CHUNK * 8, CHUNK), F32),
                   jax.ShapeDtypeStruct((2, t // CHUNK * 8, CHUNK), F32)]
                  + [jax.ShapeDtypeStruct(a.shape, BF16) for a in casts],
        compiler_params=_vmem_params(("parallel",), "inproj"),
        name="inproj",
    )(x2d, mod4, mod4, *[w[k] for k in names], *casts)


def _gate_prep_kernel(gi_ref, gf_ref, ra_ref, rb_ref, rc_ref, bend_scr, amax_scr, mprev_scr):
    rows = gi_ref.shape[1]
    n_chunks = rows // 8
    lane = lax.broadcasted_iota(jnp.int32, (rows, CHUNK), 1)
    first = (lax.broadcasted_iota(jnp.int32, (rows, CHUNK), 0) & 7) < N_HEADS
    src = lax.broadcasted_iota(jnp.int32, (CHUNK, 2 * CHUNK), 0)
    dst = lax.broadcasted_iota(jnp.int32, (CHUNK, 2 * CHUNK), 1)
    per_dir = []
    for d in range(2):
        feeds = (src <= dst) if d == 0 else (src >= dst)
        cum_and_total = jnp.where((dst >= CHUNK) | feeds, 1.0, 0.0).astype(BF16)
        bc = sum(_dot(p, cum_and_total) for p in _split3(_log_sigmoid(gf_ref[d])))
        b, b_end = bc[:, :CHUNK], bc[:, CHUNK:]
        a = gi_ref[d] - b
        cm = a
        for k in range(7):
            sh = 1 << k
            if d == 0:
                cm = jnp.maximum(cm, jnp.where(lane >= sh, pltpu.roll(cm, sh, 1), MASKED))
            else:
                cm = jnp.maximum(cm, jnp.where(lane < CHUNK - sh, pltpu.roll(cm, CHUNK - sh, 1), MASKED))
        a_max = jnp.broadcast_to(jnp.max(a, axis=1, keepdims=True), a.shape)
        bend_scr[d] = b_end
        amax_scr[d] = a_max
        per_dir.append((a, b, b_end, cm, a_max))

    def scan(c, ms):
        out = []
        for d, m in enumerate(ms):
            r = pl.ds(pl.multiple_of((c if d == 0 else n_chunks - 1 - c) * 8, 8), 8)
            mprev_scr[d, r, :] = m
            out.append(bend_scr[d, r, :] + jnp.maximum(m, amax_scr[d, r, :]))
        return tuple(out)

    lax.fori_loop(0, n_chunks, scan, (jnp.zeros((8, CHUNK), F32),) * 2)
    for d, (a, b, b_end, cm, a_max) in enumerate(per_dir):
        m_prev = mprev_scr[d]
        u = -jnp.maximum(m_prev, cm)
        m_new = b_end + jnp.maximum(m_prev, a_max)
        ra_ref[d] = jnp.where(first, a, jnp.exp(b_end + a - m_new))
        rb_ref[d] = jnp.where(first, u, jnp.exp(m_prev + u))
        rc_ref[d] = jnp.where(first, u - b, jnp.exp(b_end + m_prev - m_new))


def _gate_prep(gi, gf, bsz):
    rows = gi.shape[1] // bsz
    spec = pl.BlockSpec((2, rows, CHUNK), lambda b: (0, b, 0))
    shape = jax.ShapeDtypeStruct(gi.shape, F32)
    return pl.pallas_call(
        _gate_prep_kernel,
        grid=(bsz,),
        in_specs=[spec, spec],
        out_specs=[spec, spec, spec],
        out_shape=[shape, shape, shape],
        scratch_shapes=[pltpu.VMEM((2, rows, CHUNK), F32)] * 3,
        compiler_params=_vmem_params(("parallel",), "gate_prep"),
        name="gate_prep",
    )(gi, gf)


@functools.lru_cache(maxsize=None)
def _column_selector():
    sel = np.zeros((48, (N_HEADS // 2) * 5 * CHUNK), np.float32)

    def pick(arr, row, lane0, width):
        for piece in range(3):
            sel[arr * 24 + piece * 8 + row, lane0:lane0 + width] = 1.0

    for hd in range(N_HEADS):
        base = 5 * (hd // 2) * CHUNK
        pick(0, hd, base + (hd % 2) * CHUNK, CHUNK)
        pick(1, hd, base + (2 + hd % 2) * CHUNK, CHUNK)
        pick(0, N_HEADS + hd, base + 4 * CHUNK + (hd % 2) * QK_DIM, QK_DIM)
    return sel


def _mlstm_kernel(sel_ref, qf_ref, ktf_ref, vf_ref, raf_ref, rbf_ref, rcf_ref,
                  qb_ref, ktb_ref, vb_ref, rab_ref, rbb_ref, rcb_ref, hf_ref, hb_ref, c_scr):
    @pl.when(pl.program_id(1) == 0)
    def _():
        c_scr[...] = jnp.zeros_like(c_scr)

    cpb = ktf_ref.shape[0]
    t_i = lax.broadcasted_iota(jnp.int32, (CHUNK, CHUNK), 0)
    s_i = lax.broadcasted_iota(jnp.int32, (CHUNK, CHUNK), 1)
    visible = (s_i <= t_i, s_i >= t_i)
    ones = jnp.ones((CHUNK, V_DIM), BF16)
    kzero = jnp.zeros((QK_DIM, CHUNK), BF16)
    czero = jnp.zeros((QK_DIM, 2 * V_DIM), BF16)
    dirs = ((qf_ref, ktf_ref, vf_ref, raf_ref, rbf_ref, rcf_ref, hf_ref),
            (qb_ref, ktb_ref, vb_ref, rab_ref, rbb_ref, rcb_ref, hb_ref))

    def body(i, carry):
        units = []
        for d, (q_ref, kt_ref, v_ref, ra_ref, rb_ref, rc_ref, h_ref) in enumerate(dirs):
            li = i if d == 0 else cpb - 1 - i
            r8 = pl.ds(pl.multiple_of(li * 8, 8), 8)
            rl = pl.ds(pl.multiple_of(li * CHUNK, CHUNK), CHUNK)
            ra = ra_ref[r8, :]
            rc = rc_ref[r8, :]
            pieces = [p.astype(F32) for arr in (rb_ref[r8, :], rc) for p in _split3(arr)]
            colb = lax.dot_general(jnp.concatenate(pieces, axis=0).astype(BF16), sel_ref[...],
                                   (((0,), (0,)), ((), ())), preferred_element_type=F32)
            kt = kt_ref[li]
            for pair in range(N_HEADS // 2):
                k_a = kt[2 * pair * QK_DIM:(2 * pair + 1) * QK_DIM, :]
                k_b = kt[(2 * pair + 1) * QK_DIM:(2 * pair + 2) * QK_DIM, :]
                kdiag = jnp.concatenate([jnp.concatenate([k_a, kzero], axis=1),
                                         jnp.concatenate([kzero, k_b], axis=1)], axis=0)
                q2 = q_ref[rl, 2 * pair * QK_DIM:2 * (pair + 1) * QK_DIM]
                qk2 = _dot(q2, kdiag)
                w_inter2 = colb[:, (5 * pair + 4) * CHUNK:(5 * pair + 5) * CHUNK]
                q_inter2 = (q2.astype(F32) * w_inter2).astype(BF16)
                for hd in (2 * pair, 2 * pair + 1):
                    kth = kt[hd * QK_DIM:(hd + 1) * QK_DIM, :]
                    units.append(dict(d=d, hd=hd, rl=rl, h_ref=h_ref, ra=ra, rc=rc, colb=colb, kth=kth,
                                      q_inter2=q_inter2, v_ref=v_ref,
                                      qk=qk2[:, (hd % 2) * CHUNK:(hd % 2 + 1) * CHUNK]))
        for u in units:
            d, hd, colb = u["d"], u["hd"], u["colb"]
            c0 = (5 * (hd // 2) + hd % 2) * CHUNK
            u_b = colb[:, c0:c0 + CHUNK]
            w_intra = jnp.exp(jnp.where(visible[d], u_b + u["ra"][hd:hd + 1, :], MASKED))
            s = (u["qk"] * w_intra).astype(BF16)
            u["c_prev"] = c_scr[d, hd]
            c_bf = u["c_prev"].astype(BF16)
            c_rows = [c_bf, czero] if hd % 2 == 0 else [czero, c_bf]
            u["v_aug"] = jnp.concatenate([u["v_ref"][u["rl"], hd * V_DIM:(hd + 1) * V_DIM], ones], axis=1)
            u["nd"] = _dot(jnp.concatenate([s, u["q_inter2"]], axis=1),
                           jnp.concatenate([u["v_aug"]] + c_rows, axis=0))
        for u in units:
            hd, nd = u["hd"], u["nd"]
            e0 = (5 * (hd // 2) + 2 + hd % 2) * CHUNK
            exp_neg_m = jnp.exp(u["colb"][:, e0:e0 + CHUNK])
            den = jnp.maximum(jnp.abs(nd[:, V_DIM:]), exp_neg_m)
            u["h_ref"][u["rl"], hd * V_DIM:(hd + 1) * V_DIM] = (nd[:, :V_DIM] / den).astype(u["h_ref"].dtype)
            wk_row = u["ra"][N_HEADS + hd:N_HEADS + hd + 1, :]
            decay = jnp.broadcast_to(u["rc"][N_HEADS + hd:N_HEADS + hd + 1, :], (QK_DIM, CHUNK))
            kw = (u["kth"].astype(F32) * wk_row).astype(BF16)
            c_scr[u["d"], hd] = jnp.concatenate([decay, decay], axis=1) * u["c_prev"] + _dot(kw, u["v_aug"])
        return carry

    lax.fori_loop(0, cpb, body, 0, unroll=4)


def _mlstm(q, kt, v, ra, rb, rc, bsz, seq, cpb):
    t = q.shape[0]
    nblk = seq // (cpb * CHUNK)
    qk = N_HEADS * QK_DIM
    vw = N_HEADS * V_DIM
    fwd = lambda b, j: b * nblk + j
    bwd = lambda b, j: b * nblk + nblk - 1 - j

    def specs(blk, d):
        return [pl.BlockSpec((cpb * CHUNK, qk), lambda b, j: (blk(b, j), 0)),
                pl.BlockSpec((cpb, qk, CHUNK), lambda b, j: (blk(b, j), 0, 0)),
                pl.BlockSpec((cpb * CHUNK, vw), lambda b, j: (blk(b, j), 0))] + \
               [pl.BlockSpec((None, cpb * 8, CHUNK), lambda b, j: (d, blk(b, j), 0))] * 3

    sel = jnp.asarray(_column_selector()).astype(BF16)
    return pl.pallas_call(
        _mlstm_kernel,
        grid=(bsz, nblk),
        in_specs=[pl.BlockSpec(sel.shape, lambda b, j: (0, 0))] + specs(fwd, 0) + specs(bwd, 1),
        out_specs=[pl.BlockSpec((cpb * CHUNK, vw), lambda b, j: (fwd(b, j), 0)),
                   pl.BlockSpec((cpb * CHUNK, vw), lambda b, j: (bwd(b, j), 0))],
        out_shape=[jax.ShapeDtypeStruct((t, vw), BF16)] * 2,
        scratch_shapes=[pltpu.VMEM((2, N_HEADS, QK_DIM, 2 * V_DIM), F32)],
        compiler_params=_vmem_params(("parallel", "arbitrary"), "mlstm"),
        name="mlstm",
    )(sel, q, kt, v, ra, rb, rc, q, kt, v, ra, rb, rc)


def _fft_kh(n1):
    return n1 // 2 + FFT_KB


@functools.lru_cache(maxsize=None)
def _fft_tables(seq):
    n1 = seq // CHUNK
    kh = _fft_kh(n1)
    two_pi = 2.0 * np.pi
    k1 = np.arange(kh, dtype=np.int64)
    n = 128 * np.arange(n1, dtype=np.int64)[None, None, :] + np.arange(128, dtype=np.int64)[:, None, None]
    ang = two_pi * ((k1[None, :, None] * n) % seq).astype(np.float64) / seq
    ga = np.concatenate([np.cos(ang), -np.sin(ang)], axis=1)
    j = np.arange(128, dtype=np.int64)
    a128 = two_pi * ((j[:, None] * j[None, :]) % 128).astype(np.float64) / 128.0
    ff = np.concatenate([np.cos(a128), -np.sin(a128)], axis=0)
    cc, sc = np.cos(a128), np.sin(a128)
    cs = np.block([[cc, cc], [sc, -sc]]) / np.sqrt(128.0 * seq)
    return tuple(np.asarray(a, dtype=np.float32) for a in (ga, ff, cs))


def _fft_kernel(z_ref, ga_ref, ff_ref, cs_ref, y_ref, scr):
    j = pl.program_id(2)
    n1 = ga_ref.shape[2]
    kh2 = ga_ref.shape[1]
    half = n1 // 2
    a_steps = CHUNK // (FFT_NB * z_ref.shape[0])

    @pl.when(j < a_steps)
    def _():
        for sb in range(z_ref.shape[0]):
            for l in range(FFT_NB):
                n2 = (j * z_ref.shape[0] + sb) * FFT_NB + l
                zl = z_ref[sb, pl.ds(l, n1, stride=FFT_NB), :].astype(BF16)
                row0 = pl.multiple_of(n2 * kh2, 8)
                scr[pl.ds(row0, kh2), :] = _dot(ga_ref[n2], zl)

    def spectra(k0s):
        ms = []
        for k0 in k0s:
            p = [scr[pl.ds(ri * (kh2 // 2) + k0 + kk, CHUNK, stride=kh2), :]
                 for kk in range(FFT_KB) for ri in range(2)]
            ms.append(_dot(ff_ref[...], jnp.concatenate(p, axis=1).astype(BF16)))
        ys = []
        for m in ms:
            x = []
            for kk in range(FFT_KB):
                c0 = 2 * kk * GROUP_DIM
                xr = m[:CHUNK, c0:c0 + GROUP_DIM] - m[CHUNK:, c0 + GROUP_DIM:c0 + 2 * GROUP_DIM]
                xi = m[:CHUNK, c0 + GROUP_DIM:c0 + 2 * GROUP_DIM] + m[CHUNK:, c0:c0 + GROUP_DIM]
                x.append(jnp.concatenate([xr, xi], axis=1))
            ys.append(_dot(jnp.concatenate(x, axis=0).astype(BF16), cs_ref[...]))
        return ys

    def store(y, kk, k1, lane0):
        row0 = pl.multiple_of(k1 * FFT_NB, 8)
        for kb in range(CHUNK // FFT_NB):
            r0 = kk * CHUNK + kb * FFT_NB
            y_ref[kb, pl.ds(row0, FFT_NB), :] = y[r0:r0 + FFT_NB, lane0:lane0 + GROUP_DIM]

    k1_step = half // FFT_BSTEPS
    gb = max(1, min(FFT_GB, k1_step // FFT_KB))

    @pl.when(j >= a_steps)
    def _():
        def body(it, carry):
            k0s = [(j - a_steps) * k1_step + (it * gb + gi) * FFT_KB for gi in range(gb)]
            for k0, y in zip(k0s, spectra(k0s)):
                for kk in range(FFT_KB):
                    k1 = k0 + kk
                    store(y, kk, jnp.where(k1 == 0, 0, n1 - k1), GROUP_DIM)
                    store(y, kk, k1, 0)
            return carry

        lax.fori_loop(0, k1_step // (FFT_KB * gb), body, 0)

    @pl.when(j == a_steps + FFT_BSTEPS - 1)
    def _():
        (y,) = spectra([half])
        store(y, 0, half, GROUP_DIM)


def _fourier(z5, seq):
    bsz = z5.shape[0]
    n1 = seq // CHUNK
    nblk = CHUNK // FFT_NB
    a_steps = nblk // FFT_SB
    ga, ff, cs = (jnp.asarray(a).astype(BF16) for a in _fft_tables(seq))
    return pl.pallas_call(
        _fft_kernel,
        grid=(bsz, N_GROUPS, a_steps + FFT_BSTEPS),
        in_specs=[pl.BlockSpec((None, None, FFT_SB, n1 * FFT_NB, GROUP_DIM),
                               lambda b, g, j: (b, g, jnp.minimum(j, a_steps - 1), 0, 0)),
                  pl.BlockSpec(ga.shape, lambda b, g, j: (0, 0, 0), pipeline_mode=pl.Buffered(1)),
                  pl.BlockSpec(ff.shape, lambda b, g, j: (0, 0)),
                  pl.BlockSpec(cs.shape, lambda b, g, j: (0, 0))],
        out_specs=pl.BlockSpec((None, None, nblk, n1 * FFT_NB, GROUP_DIM), lambda b, g, j: (b, g, 0, 0, 0)),
        out_shape=jax.ShapeDtypeStruct(z5.shape, F32),
        scratch_shapes=[pltpu.VMEM((CHUNK * 2 * _fft_kh(n1), GROUP_DIM), F32)],
        compiler_params=_vmem_params(("parallel", "parallel", "arbitrary"), "fft"),
        name="fft",
    )(z5, ga, ff, cs)


def _gather_rows(y_refs, k2l, n1):
    half = n1 // 2
    low = jnp.concatenate([a[pl.ds(k2l, half, stride=FFT_NB), :] for a, _ in y_refs], axis=1)
    high = jnp.concatenate([b[pl.ds(half * FFT_NB + FFT_NB - 1 - k2l, half, stride=FFT_NB), :] for _, b in y_refs],
                           axis=1)
    return jnp.concatenate([low, high], axis=0)


def _outmlp_kernel(alpha, ff_chunk, spb, x_ref, hf_ref, hb_ref, o_ref,
                   y0_ref, y1_ref, y2_ref, y3_ref, ym0_ref, ym1_ref, ym2_ref, ym3_ref,
                   g1_ref, sc2_ref, sh2_ref, g2_ref,
                   nw_ref, wout_ref, l1g_ref, l1b_ref, w1_ref, b1_ref, w2_ref, b2_ref, l2g_ref, l2b_ref,
                   out_ref):
    n1 = y0_ref.shape[0] // FFT_NB
    kpt = x_ref.shape[0] // n1
    k2_off = ((pl.program_id(0) % spb) * kpt) % FFT_NB
    sizes = OUTMLP_SUBTILE_K2 if sum(OUTMLP_SUBTILE_K2) == kpt else (kpt,)
    subs, k2l = [], 0
    for kps in sizes:
        subs.append(dict(k2l=k2l, kps=kps, rs=slice(k2l * n1, (k2l + kps) * n1)))
        k2l += kps
    for s in subs:
        rs = s["rs"]
        y_refs = ((y0_ref, ym0_ref), (y1_ref, ym1_ref), (y2_ref, ym2_ref), (y3_ref, ym3_ref))
        yf = jnp.concatenate([_gather_rows(y_refs, k2_off + s["k2l"] + kk, n1) for kk in range(s["kps"])],
                             axis=0).astype(BF16)
        hsum = hf_ref[rs, :].astype(F32) + hb_ref[rs, :].astype(F32)
        heads = [_ln_plain(hsum[:, j * V_DIM:(j + 1) * V_DIM]) for j in range(N_HEADS)]
        ym = jnp.concatenate(heads, axis=1) * nw_ref[...] * jax.nn.sigmoid(o_ref[rs, :].astype(F32))
        s["mix"] = _dot(jnp.concatenate([ym.astype(BF16), yf], axis=1), wout_ref[...])
    for s in subs:
        x1 = _ln_plain(alpha * x_ref[s["rs"], :] + (1.0 + g1_ref[...]) * s["mix"]) * l1g_ref[...] + l1b_ref[...]
        s["x1"] = x1
        s["h2"] = (_ln_plain(x1) * (1.0 + sc2_ref[...]) + sh2_ref[...]).astype(BF16)
    for s in subs:
        ff = b2_ref[...]
        for j in range(w1_ref.shape[1] // ff_chunk):
            sl = slice(j * ff_chunk, (j + 1) * ff_chunk)
            hid = jnp.maximum(_dot(s["h2"], w1_ref[:, sl]) + b1_ref[:, sl], 0.0)
            ff = ff + _dot((hid * hid).astype(BF16), w2_ref[sl, :])
        s["ff"] = ff
    for s in subs:
        out_ref[s["rs"], :] = (_ln_plain(alpha * s["x1"] + (1.0 + g2_ref[...]) * s["ff"]) * l2g_ref[...]
                               + l2b_ref[...])


def _outmlp(x2d, hf, hb, o, y5, mod4, w, seq, tm, alpha, ff_chunk):
    t, d = x2d.shape
    spb = seq // tm
    n1 = seq // CHUNK
    kpt = tm // n1
    assert FFT_NB % kpt == 0
    vw = N_HEADS * V_DIM
    row = lambda width: pl.BlockSpec((tm, width), lambda i: (i, 0))
    nblk = CHUNK // FFT_NB
    kblk = lambda i: ((i % spb) * kpt) // FFT_NB
    yspec = lambda g: pl.BlockSpec((None, None, None, n1 * FFT_NB, GROUP_DIM),
                                   lambda i: (i // spb, g, kblk(i), 0, 0))
    ymspec = lambda g: pl.BlockSpec((None, None, None, n1 * FFT_NB, GROUP_DIM),
                                    lambda i: (i // spb, g, nblk - 1 - kblk(i), 0, 0))
    modspec = lambda j: pl.BlockSpec((None, None, 1, d), lambda i: (i // spb, j, 0, 0))
    const = lambda a: pl.BlockSpec(a.shape, lambda i: (0,) * a.ndim, pipeline_mode=pl.Buffered(1))
    names = ["nw", "wout", "l1g", "l1b", "w1", "b1", "w2", "b2", "l2g", "l2b"]
    return pl.pallas_call(
        functools.partial(_outmlp_kernel, alpha, ff_chunk, spb),
        grid=(t // tm,),
        in_specs=[row(d), row(vw), row(vw), row(vw)]
                 + [yspec(g) for g in range(N_GROUPS)] + [ymspec(g) for g in range(N_GROUPS)]
                 + [modspec(2), modspec(4), modspec(3), modspec(5)] + [const(w[k]) for k in names],
        out_specs=row(d),
        out_shape=jax.ShapeDtypeStruct((t, d), F32),
        compiler_params=_vmem_params(("parallel",), "outmlp"),
        name="outmlp",
    )(x2d, hf, hb, o, *([y5] * (2 * N_GROUPS)), mod4, mod4, mod4, mod4, *[w[k] for k in names])


def _layer_weights(w_in, b_gate, w_out, w_ff1, w_ff2, b_ff1, b_ff2, mlstm_norm_w, ln1_g, ln1_b, ln2_g, ln2_b):
    qk = N_HEADS * QK_DIM
    vw = N_HEADS * V_DIM
    fw = N_GROUPS * GROUP_DIM
    o0, o1, o2, o3, o4 = qk, 2 * qk, 2 * qk + vw, 2 * qk + 2 * vw, 2 * qk + 2 * vw + fw
    wg = w_in[:, o4:].T.reshape(2, 2, N_HEADS, -1)
    bg = b_gate.astype(F32).reshape(2, 2, N_HEADS, 1)
    dup = lambda a: jnp.concatenate([a, a], axis=1)
    r = lambda a: a.astype(F32).reshape(1, -1)
    wq = w_in[:, :o0] * (QK_DIM ** -0.5)
    gate_rows = lambda a: jnp.concatenate([dup(a[:, 0]), dup(a[:, 1])], axis=1).reshape(32, -1)
    return {
        "wrow": jnp.concatenate([wq, w_in[:, o1:o4]], axis=1).astype(BF16),
        "wlane": jnp.concatenate([w_in[:, o0:o1].T, gate_rows(wg)], axis=0).astype(BF16),
        "bg": gate_rows(bg),
        "nw": r(mlstm_norm_w), "wout": w_out, "l1g": r(ln1_g), "l1b": r(ln1_b),
        "w1": w_ff1, "b1": r(b_ff1), "w2": w_ff2, "b2": r(b_ff2),
        "l2g": r(ln2_g), "l2b": r(ln2_b),
    }


def kernel(x, c, w_ada, b_ada, w_in, b_gate, mlstm_norm_w, w_out, ln1_g, ln1_b,
           w_ff1, b_ff1, w_ff2, b_ff2, ln2_g, ln2_b):
    bsz, seq, d = x.shape
    depth = w_ada.shape[0]
    alpha = (2 * depth) ** 0.25
    assert seq % (MLSTM_CPB * CHUNK) == 0 and (seq // CHUNK) % 8 == 0
    assert d == N_HEADS * V_DIM + N_GROUPS * GROUP_DIM
    tm_out = min(OUTMLP_ROWS, FFT_NB * (seq // CHUNK))
    x2d = x.reshape(bsz * seq, d)
    for l in range(depth):
        w = _layer_weights(w_in[l], b_gate[l], w_out[l], w_ff1[l], w_ff2[l], b_ff1[l], b_ff2[l],
                           mlstm_norm_w[l], ln1_g[l], ln1_b[l], ln2_g[l], ln2_b[l])
        mod4 = _adaln(c, w_ada[l], b_ada[l]).reshape(bsz, N_MOD, 1, d)
        q, kt, v, o, z5, gi, gf, w["wout"], w["w1"], w["w2"] = _inproj(x2d, mod4, w, seq, min(INPROJ_ROWS, seq))
        ra, rb, rc = _gate_prep(gi, gf, bsz)
        hf, hb = _mlstm(q, kt, v, ra, rb, rc, bsz, seq, MLSTM_CPB)
        y5 = _fourier(z5, seq)
        x2d = _outmlp(x2d, hf, hb, o, y5, mod4, w, seq, tm_out, alpha, ff_chunk=1024)
    return x2d.reshape(bsz, seq, d)
```

```python
import functools

import numpy as np
import jax
import jax.numpy as jnp
from jax import lax
from jax.experimental import pallas as pl
from jax.experimental.pallas import tpu as pltpu

F32 = jnp.float32
BF16 = jnp.bfloat16

CHUNK = 128
N_HEADS = 4
QK_DIM = 64
V_DIM = 128
N_GROUPS = 4
GROUP_DIM = 128
N_MOD = 6
LN_EPS = 1e-5
MASKED = -1e30
FFT_NB = 8
FFT_KB = 4
FFT_SB = 16
FFT_GB = 4
FFT_BSTEPS = 4
MLSTM_CPB = 8
OUTMLP_SUBTILE_K2 = (2, 2)
INPROJ_SUBTILES = 4

_NT = (((1,), (1,)), ((), ()))


def _dot(a, b):
    return jnp.dot(a, b, preferred_element_type=F32)


def _dot_nt(a, b):
    return lax.dot_general(a, b, _NT, preferred_element_type=F32)


def _ln_plain(x):
    mu = jnp.mean(x, axis=-1, keepdims=True)
    xc = x - mu
    var = jnp.mean(xc * xc, axis=-1, keepdims=True)
    return xc * lax.rsqrt(var + LN_EPS)


def _log_sigmoid(x):
    return jnp.minimum(x, 0.0) - jnp.log1p(jnp.exp(-jnp.abs(x)))


def _split3(x):
    hi = x.astype(BF16)
    r1 = x - hi.astype(F32)
    mid = r1.astype(BF16)
    lo = (r1 - mid.astype(F32)).astype(BF16)
    return hi, mid, lo


INPROJ_ROWS = 1024
OUTMLP_ROWS = 512
VMEM_LIMIT_MIB = dict(adaln=32, inproj=48, gate_prep=32, mlstm=32, fft=56, outmlp=60)


def _vmem_params(semantics, call):
    return pltpu.CompilerParams(dimension_semantics=semantics, vmem_limit_bytes=VMEM_LIMIT_MIB[call] * 1024 * 1024)


def _adaln_kernel(ct_ref, w_ref, b_ref, o_ref):
    ct = ct_ref[...]
    act = ct * jax.nn.sigmoid(ct)
    w = w_ref[...]
    for b in range(ct.shape[1]):
        o_ref[b:b + 1, :] = jnp.sum(act[:, b:b + 1] * w, axis=0, keepdims=True) + b_ref[...]


def _adaln(c, w_ada, b_ada):
    bsz, d = c.shape
    n = w_ada.shape[1]
    tn = 1024
    return pl.pallas_call(
        _adaln_kernel,
        grid=(n // tn,),
        in_specs=[pl.BlockSpec((d, bsz), lambda j: (0, 0)),
                  pl.BlockSpec((d, tn), lambda j: (0, j)),
                  pl.BlockSpec((1, tn), lambda j: (0, j))],
        out_specs=pl.BlockSpec((bsz, tn), lambda j: (0, j)),
        out_shape=jax.ShapeDtypeStruct((bsz, n), F32),
        compiler_params=_vmem_params(("parallel",), "adaln"),
        name="adaln",
    )(c.T, w_ada, b_ada.reshape(1, n))


def _inproj_kernel(x_ref, sh_ref, sc_ref, wrow_ref, wlane_ref, bg_ref, wout_ref, w1_ref, w2_ref,
                   q_ref, kt_ref, v_ref, o_ref, z_ref, gi_ref, gf_ref, woutb_ref, w1b_ref, w2b_ref):
    woutb_ref[...] = wout_ref[...].astype(BF16)
    w1b_ref[...] = w1_ref[...].astype(BF16)
    w2b_ref[...] = w2_ref[...].astype(BF16)
    qk, vw = q_ref.shape[1], v_ref.shape[1]
    n_chunks = x_ref.shape[0] // CHUNK
    cps = max(1, n_chunks // INPROJ_SUBTILES)
    for c0 in range(0, n_chunks, cps):
        rs = slice(c0 * CHUNK, (c0 + cps) * CHUNK)
        h = _ln_plain(x_ref[rs, :]) * (1.0 + sc_ref[...]) + sh_ref[...]
        hb = h.astype(BF16)
        rowp = _dot(hb, wrow_ref[...])
        q_ref[rs, :] = rowp[:, :qk].astype(BF16)
        v_ref[rs, :] = rowp[:, qk:qk + vw].astype(BF16)
        o_ref[rs, :] = rowp[:, qk + vw:qk + 2 * vw].astype(BF16)
        fz = rowp[:, qk + 2 * vw:]
        for j in range(cps):
            for g in range(N_GROUPS):
                for jb in range(CHUNK // FFT_NB):
                    r0 = j * CHUNK + jb * FFT_NB
                    z_ref[g, jb, (c0 + j) * FFT_NB:(c0 + j + 1) * FFT_NB, :] = (
                        fz[r0:r0 + FFT_NB, g * GROUP_DIM:(g + 1) * GROUP_DIM])
        lanep = _dot_nt(wlane_ref[...], hb)
        gates = lanep[qk:, :] + bg_ref[...]
        for j in range(cps):
            cl = slice(j * CHUNK, (j + 1) * CHUNK)
            kt_ref[c0 + j] = lanep[:qk, cl].astype(BF16)
            for d in range(2):
                gi_ref[d, (c0 + j) * 8:(c0 + j + 1) * 8, :] = gates[16 * d:16 * d + 8, cl]
                gf_ref[d, (c0 + j) * 8:(c0 + j + 1) * 8, :] = gates[16 * d + 8:16 * d + 16, cl]


def _inproj(x2d, mod4, w, seq, tm):
    t, d = x2d.shape
    spb = seq // tm
    qk = N_HEADS * QK_DIM
    vw = N_HEADS * V_DIM
    nblk = CHUNK // FFT_NB
    cpt = tm // CHUNK
    full = lambda a: pl.BlockSpec(a.shape, lambda i: (0,) * a.ndim)
    modspec = lambda j: pl.BlockSpec((None, None, 1, d), lambda i: (i // spb, j, 0, 0))
    names = ["wrow", "wlane", "bg"]
    steps = t // tm
    casts = [w["wout"], w["w1"], w["w2"]]
    assert all(a.shape[0] % (16 * steps) == 0 for a in casts)
    slab = lambda a: pl.BlockSpec((a.shape[0] // steps, a.shape[1]), lambda i: (i, 0))
    return pl.pallas_call(
        _inproj_kernel,
        grid=(steps,),
        in_specs=[pl.BlockSpec((tm, d), lambda i: (i, 0)), modspec(0), modspec(1)] + [full(w[k]) for k in names]
                 + [slab(a) for a in casts],
        out_specs=[pl.BlockSpec((tm, qk), lambda i: (i, 0)),
                   pl.BlockSpec((cpt, qk, CHUNK), lambda i: (i, 0, 0)),
                   pl.BlockSpec((tm, vw), lambda i: (i, 0)),
                   pl.BlockSpec((tm, vw), lambda i: (i, 0)),
                   pl.BlockSpec((None, N_GROUPS, nblk, cpt * FFT_NB, GROUP_DIM),
                                lambda i: (i // spb, 0, 0, i % spb, 0)),
                   pl.BlockSpec((2, cpt * 8, CHUNK), lambda i: (0, i, 0)),
                   pl.BlockSpec((2, cpt * 8, CHUNK), lambda i: (0, i, 0))] + [slab(a) for a in casts],
        out_shape=[jax.ShapeDtypeStruct((t, qk), BF16),
                   jax.ShapeDtypeStruct((t // CHUNK, qk, CHUNK), BF16),
                   jax.ShapeDtypeStruct((t, vw), BF16),
                   jax.ShapeDtypeStruct((t, vw), BF16),
                   jax.ShapeDtypeStruct((t // seq, N_GROUPS, nblk, (seq // CHUNK) * FFT_NB, GROUP_DIM), F32),
                   jax.ShapeDtypeStruct((2, t // CHUNK * 8, CHUNK), F32),
                   jax.ShapeDtypeStruct((2, t // CHUNK * 8, CHUNK), F32)]
                  + [jax.ShapeDtypeStruct(a.shape, BF16) for a in casts],
        compiler_params=_vmem_params(("parallel",), "inproj"),
        name="inproj",
    )(x2d, mod4, mod4, *[w[k] for k in names], *casts)


def _gate_prep_kernel(gi_ref, gf_ref, ra_ref, rb_ref, rc_ref, bend_scr, amax_scr, mprev_scr):
    rows = gi_ref.shape[1]
    n_chunks = rows // 8
    lane = lax.broadcasted_iota(jnp.int32, (rows, CHUNK), 1)
    first = (lax.broadcasted_iota(jnp.int32, (rows, CHUNK), 0) & 7) < N_HEADS
    src = lax.broadcasted_iota(jnp.int32, (CHUNK, 2 * CHUNK), 0)
    dst = lax.broadcasted_iota(jnp.int32, (CHUNK, 2 * CHUNK), 1)
    per_dir = []
    for d in range(2):
        feeds = (src <= dst) if d == 0 else (src >= dst)
        cum_and_total = jnp.where((dst >= CHUNK) | feeds, 1.0, 0.0).astype(BF16)
        bc = sum(_dot(p, cum_and_total) for p in _split3(_log_sigmoid(gf_ref[d])))
        b, b_end = bc[:, :CHUNK], bc[:, CHUNK:]
        a = gi_ref[d] - b
        cm = a
        for k in range(7):
            sh = 1 << k
            if d == 0:
                cm = jnp.maximum(cm, jnp.where(lane >= sh, pltpu.roll(cm, sh, 1), MASKED))
            else:
                cm = jnp.maximum(cm, jnp.where(lane < CHUNK - sh, pltpu.roll(cm, CHUNK - sh, 1), MASKED))
        a_max = jnp.broadcast_to(jnp.max(a, axis=1, keepdims=True), a.shape)
        bend_scr[d] = b_end
        amax_scr[d] = a_max
        per_dir.append((a, b, b_end, cm, a_max))

    def scan(c, ms):
        out = []
        for d, m in enumerate(ms):
            r = pl.ds(pl.multiple_of((c if d == 0 else n_chunks - 1 - c) * 8, 8), 8)
            mprev_scr[d, r, :] = m
            out.append(bend_scr[d, r, :] + jnp.maximum(m, amax_scr[d, r, :]))
        return tuple(out)

    lax.fori_loop(0, n_chunks, scan, (jnp.zeros((8, CHUNK), F32),) * 2)
    for d, (a, b, b_end, cm, a_max) in enumerate(per_dir):
        m_prev = mprev_scr[d]
        u = -jnp.maximum(m_prev, cm)
        m_new = b_end + jnp.maximum(m_prev, a_max)
        ra_ref[d] = jnp.where(first, a, jnp.exp(b_end + a - m_new))
        rb_ref[d] = jnp.where(first, u, jnp.exp(m_prev + u))
        rc_ref[d] = jnp.where(first, u - b, jnp.exp(b_end + m_prev - m_new))


def _gate_prep(gi, gf, bsz):
    rows = gi.shape[1] // bsz
    spec = pl.BlockSpec((2, rows, CHUNK), lambda b: (0, b, 0))
    shape = jax.ShapeDtypeStruct(gi.shape, F32)
    return pl.pallas_call(
        _gate_prep_kernel,
        grid=(bsz,),
        in_specs=[spec, spec],
        out_specs=[spec, spec, spec],
        out_shape=[shape, shape, shape],
        scratch_shapes=[pltpu.VMEM((2, rows, CHUNK), F32)] * 3,
        compiler_params=_vmem_params(("parallel",), "gate_prep"),
        name="gate_prep",
    )(gi, gf)


@functools.lru_cache(maxsize=None)
def _column_selector():
    sel = np.zeros((48, (N_HEADS // 2) * 4 * CHUNK), np.float32)

    def pick(arr, row, lane0, width):
        for piece in range(3):
            sel[arr * 24 + piece * 8 + row, lane0:lane0 + width] = 1.0

    for hd in range(N_HEADS):
        base = 4 * (hd // 2) * CHUNK
        pick(0, hd, base + (hd % 2) * CHUNK, CHUNK)
        pick(1, hd, base + 2 * CHUNK + (hd % 2) * QK_DIM, QK_DIM)
        pick(0, N_HEADS + hd, base + 3 * CHUNK + (hd % 2) * QK_DIM, QK_DIM)
    return sel


def _mlstm_kernel(sel_ref, qf_ref, ktf_ref, vf_ref, raf_ref, rbf_ref, rcf_ref,
                  qb_ref, ktb_ref, vb_ref, rab_ref, rbb_ref, rcb_ref, hf_ref, hb_ref, c_scr):
    @pl.when(pl.program_id(1) == 0)
    def _():
        c_scr[...] = jnp.zeros_like(c_scr)

    cpb = ktf_ref.shape[0]
    t_i = lax.broadcasted_iota(jnp.int32, (CHUNK, CHUNK), 0)
    s_i = lax.broadcasted_iota(jnp.int32, (CHUNK, CHUNK), 1)
    visible = (s_i <= t_i, s_i >= t_i)
    low_lanes = s_i < QK_DIM
    ones = jnp.ones((CHUNK, V_DIM), BF16)
    kzero = jnp.zeros((QK_DIM, CHUNK), BF16)
    czero = jnp.zeros((QK_DIM, 2 * V_DIM), BF16)
    dirs = ((qf_ref, ktf_ref, vf_ref, raf_ref, rbf_ref, rcf_ref, hf_ref),
            (qb_ref, ktb_ref, vb_ref, rab_ref, rbb_ref, rcb_ref, hb_ref))

    def body(i, carry):
        units = []
        for d, (q_ref, kt_ref, v_ref, ra_ref, rb_ref, rc_ref, h_ref) in enumerate(dirs):
            li = i if d == 0 else cpb - 1 - i
            r8 = pl.ds(pl.multiple_of(li * 8, 8), 8)
            rl = pl.ds(pl.multiple_of(li * CHUNK, CHUNK), CHUNK)
            ra = ra_ref[r8, :]
            rc = rc_ref[r8, :]
            pieces = [p.astype(F32) for arr in (rb_ref[r8, :], rc) for p in _split3(arr)]
            colb = lax.dot_general(jnp.concatenate(pieces, axis=0).astype(BF16), sel_ref[...],
                                   (((0,), (0,)), ((), ())), preferred_element_type=F32)
            kt = kt_ref[li]
            for pair in range(N_HEADS // 2):
                k_a = kt[2 * pair * QK_DIM:(2 * pair + 1) * QK_DIM, :]
                k_b = kt[(2 * pair + 1) * QK_DIM:(2 * pair + 2) * QK_DIM, :]
                kdiag = jnp.concatenate([jnp.concatenate([k_a, kzero], axis=1),
                                         jnp.concatenate([kzero, k_b], axis=1)], axis=0)
                q2 = q_ref[rl, 2 * pair * QK_DIM:2 * (pair + 1) * QK_DIM]
                qk2 = _dot(q2, kdiag)
                w_inter2 = colb[:, (4 * pair + 3) * CHUNK:(4 * pair + 4) * CHUNK]
                q_inter2 = (q2.astype(F32) * w_inter2).astype(BF16)
                for hd in (2 * pair, 2 * pair + 1):
                    kth = kt[hd * QK_DIM:(hd + 1) * QK_DIM, :]
                    units.append(dict(d=d, hd=hd, rl=rl, h_ref=h_ref, ra=ra, rc=rc, colb=colb, kth=kth,
                                      q_inter2=q_inter2, v_ref=v_ref,
                                      qk=qk2[:, (hd % 2) * CHUNK:(hd % 2 + 1) * CHUNK]))
        for u in units:
            d, hd, colb = u["d"], u["hd"], u["colb"]
            c0 = (4 * (hd // 2) + hd % 2) * CHUNK
            u_b = colb[:, c0:c0 + CHUNK]
            w_intra = jnp.exp(jnp.where(visible[d], u_b + u["ra"][hd:hd + 1, :], MASKED))
            s = (u["qk"] * w_intra).astype(BF16)
            u["c_prev"] = c_scr[d, hd]
            c_bf = u["c_prev"].astype(BF16)
            c_rows = [c_bf, czero] if hd % 2 == 0 else [czero, c_bf]
            u["v_aug"] = jnp.concatenate([u["v_ref"][u["rl"], hd * V_DIM:(hd + 1) * V_DIM], ones], axis=1)
            u["nd"] = _dot(jnp.concatenate([s, u["q_inter2"]], axis=1),
                           jnp.concatenate([u["v_aug"]] + c_rows, axis=0))
        for u in units:
            hd, nd = u["hd"], u["nd"]
            e0 = (4 * (hd // 2) + 2) * CHUNK
            e2 = jnp.exp(u["colb"][:, e0:e0 + CHUNK])
            own = low_lanes if hd % 2 == 0 else ~low_lanes
            exp_neg_m = jnp.where(own, e2, pltpu.roll(e2, QK_DIM, 1))
            den = jnp.maximum(jnp.abs(nd[:, V_DIM:]), exp_neg_m)
            u["h_ref"][u["rl"], hd * V_DIM:(hd + 1) * V_DIM] = (nd[:, :V_DIM] / den).astype(u["h_ref"].dtype)
            wk_row = u["ra"][N_HEADS + hd:N_HEADS + hd + 1, :]
            decay = jnp.broadcast_to(u["rc"][N_HEADS + hd:N_HEADS + hd + 1, :], (QK_DIM, CHUNK))
            kw = (u["kth"].astype(F32) * wk_row).astype(BF16)
            c_scr[u["d"], hd] = jnp.concatenate([decay, decay], axis=1) * u["c_prev"] + _dot(kw, u["v_aug"])
        return carry

    lax.fori_loop(0, cpb, body, 0, unroll=4)


def _mlstm(q, kt, v, ra, rb, rc, bsz, seq, cpb):
    t = q.shape[0]
    nblk = seq // (cpb * CHUNK)
    qk = N_HEADS * QK_DIM
    vw = N_HEADS * V_DIM
    fwd = lambda b, j: b * nblk + j
    bwd = lambda b, j: b * nblk + nblk - 1 - j

    def specs(blk, d):
        return [pl.BlockSpec((cpb * CHUNK, qk), lambda b, j: (blk(b, j), 0)),
                pl.BlockSpec((cpb, qk, CHUNK), lambda b, j: (blk(b, j), 0, 0)),
                pl.BlockSpec((cpb * CHUNK, vw), lambda b, j: (blk(b, j), 0))] + \
               [pl.BlockSpec((None, cpb * 8, CHUNK), lambda b, j: (d, blk(b, j), 0))] * 3

    sel = jnp.asarray(_column_selector()).astype(BF16)
    return pl.pallas_call(
        _mlstm_kernel,
        grid=(bsz, nblk),
        in_specs=[pl.BlockSpec(sel.shape, lambda b, j: (0, 0))] + specs(fwd, 0) + specs(bwd, 1),
        out_specs=[pl.BlockSpec((cpb * CHUNK, vw), lambda b, j: (fwd(b, j), 0)),
                   pl.BlockSpec((cpb * CHUNK, vw), lambda b, j: (bwd(b, j), 0))],
        out_shape=[jax.ShapeDtypeStruct((t, vw), BF16)] * 2,
        scratch_shapes=[pltpu.VMEM((2, N_HEADS, QK_DIM, 2 * V_DIM), F32)],
        compiler_params=_vmem_params(("parallel", "arbitrary"), "mlstm"),
        name="mlstm",
    )(sel, q, kt, v, ra, rb, rc, q, kt, v, ra, rb, rc)


def _fft_kh(n1):
    return n1 // 2 + FFT_KB


@functools.lru_cache(maxsize=None)
def _fft_tables(seq):
    n1 = seq // CHUNK
    kh = _fft_kh(n1)
    two_pi = 2.0 * np.pi
    k1 = np.arange(kh, dtype=np.int64)
    n = 128 * np.arange(n1, dtype=np.int64)[None, None, :] + np.arange(128, dtype=np.int64)[:, None, None]
    ang = two_pi * ((k1[None, :, None] * n) % seq).astype(np.float64) / seq
    ga = np.concatenate([np.cos(ang), -np.sin(ang)], axis=1)
    j = np.arange(128, dtype=np.int64)
    a128 = two_pi * ((j[:, None] * j[None, :]) % 128).astype(np.float64) / 128.0
    ff = np.concatenate([np.cos(a128), -np.sin(a128)], axis=0)
    cc, sc = np.cos(a128), np.sin(a128)
    cs = np.block([[cc, cc], [sc, -sc]]) / np.sqrt(128.0 * seq)
    return tuple(np.asarray(a, dtype=np.float32) for a in (ga, ff, cs))


def _fft_kernel(z_ref, ga_ref, ff_ref, cs_ref, y_ref, scr):
    j = pl.program_id(2)
    n1 = ga_ref.shape[2]
    kh2 = ga_ref.shape[1]
    half = n1 // 2
    a_steps = CHUNK // (FFT_NB * z_ref.shape[0])

    @pl.when(j < a_steps)
    def _():
        for sb in range(z_ref.shape[0]):
            for l in range(FFT_NB):
                n2 = (j * z_ref.shape[0] + sb) * FFT_NB + l
                zl = z_ref[sb, pl.ds(l, n1, stride=FFT_NB), :].astype(BF16)
                row0 = pl.multiple_of(n2 * kh2, 8)
                scr[pl.ds(row0, kh2), :] = _dot(ga_ref[n2], zl)

    def spectra(k0s):
        ms = []
        for k0 in k0s:
            p = [scr[pl.ds(ri * (kh2 // 2) + k0 + kk, CHUNK, stride=kh2), :]
                 for kk in range(FFT_KB) for ri in range(2)]
            ms.append(_dot(ff_ref[...], jnp.concatenate(p, axis=1).astype(BF16)))
        ys = []
        for m in ms:
            x = []
            for kk in range(FFT_KB):
                c0 = 2 * kk * GROUP_DIM
                xr = m[:CHUNK, c0:c0 + GROUP_DIM] - m[CHUNK:, c0 + GROUP_DIM:c0 + 2 * GROUP_DIM]
                xi = m[:CHUNK, c0 + GROUP_DIM:c0 + 2 * GROUP_DIM] + m[CHUNK:, c0:c0 + GROUP_DIM]
                x.append(jnp.concatenate([xr, xi], axis=1))
            ys.append(_dot(jnp.concatenate(x, axis=0).astype(BF16), cs_ref[...]))
        return ys

    def store(y, kk, k1, lane0):
        row0 = pl.multiple_of(k1 * FFT_NB, 8)
        for kb in range(CHUNK // FFT_NB):
            r0 = kk * CHUNK + kb * FFT_NB
            y_ref[kb, pl.ds(row0, FFT_NB), :] = y[r0:r0 + FFT_NB, lane0:lane0 + GROUP_DIM]

    k1_step = half // FFT_BSTEPS
    gb = max(1, min(FFT_GB, k1_step // FFT_KB))

    @pl.when(j >= a_steps)
    def _():
        def body(it, carry):
            k0s = [(j - a_steps) * k1_step + (it * gb + gi) * FFT_KB for gi in range(gb)]
            for k0, y in zip(k0s, spectra(k0s)):
                for kk in range(FFT_KB):
                    k1 = k0 + kk
                    store(y, kk, jnp.where(k1 == 0, 0, n1 - k1), GROUP_DIM)
                    store(y, kk, k1, 0)
            return carry

        lax.fori_loop(0, k1_step // (FFT_KB * gb), body, 0)

    @pl.when(j == a_steps + FFT_BSTEPS - 1)
    def _():
        (y,) = spectra([half])
        store(y, 0, half, GROUP_DIM)


def _fourier(z5, seq):
    bsz = z5.shape[0]
    n1 = seq // CHUNK
    nblk = CHUNK // FFT_NB
    a_steps = nblk // FFT_SB
    ga, ff, cs = (jnp.asarray(a).astype(BF16) for a in _fft_tables(seq))
    return pl.pallas_call(
        _fft_kernel,
        grid=(bsz, N_GROUPS, a_steps + FFT_BSTEPS),
        in_specs=[pl.BlockSpec((None, None, FFT_SB, n1 * FFT_NB, GROUP_DIM),
                               lambda b, g, j: (b, g, jnp.minimum(j, a_steps - 1), 0, 0)),
                  pl.BlockSpec(ga.shape, lambda b, g, j: (0, 0, 0), pipeline_mode=pl.Buffered(1)),
                  pl.BlockSpec(ff.shape, lambda b, g, j: (0, 0)),
                  pl.BlockSpec(cs.shape, lambda b, g, j: (0, 0))],
        out_specs=pl.BlockSpec((None, None, nblk, n1 * FFT_NB, GROUP_DIM), lambda b, g, j: (b, g, 0, 0, 0)),
        out_shape=jax.ShapeDtypeStruct(z5.shape, F32),
        scratch_shapes=[pltpu.VMEM((CHUNK * 2 * _fft_kh(n1), GROUP_DIM), F32)],
        compiler_params=_vmem_params(("parallel", "parallel", "arbitrary"), "fft"),
        name="fft",
    )(z5, ga, ff, cs)


def _gather_rows(y_refs, k2l, n1):
    half = n1 // 2
    low = jnp.concatenate([a[pl.ds(k2l, half, stride=FFT_NB), :] for a, _ in y_refs], axis=1)
    high = jnp.concatenate([b[pl.ds(half * FFT_NB + FFT_NB - 1 - k2l, half, stride=FFT_NB), :] for _, b in y_refs],
                           axis=1)
    return jnp.concatenate([low, high], axis=0)


def _outmlp_kernel(alpha, ff_chunk, spb, x_ref, hf_ref, hb_ref, o_ref,
                   y0_ref, y1_ref, y2_ref, y3_ref, ym0_ref, ym1_ref, ym2_ref, ym3_ref,
                   g1_ref, sc2_ref, sh2_ref, g2_ref,
                   nw_ref, wout_ref, l1g_ref, l1b_ref, w1_ref, b1_ref, w2_ref, b2_ref, l2g_ref, l2b_ref,
                   out_ref):
    n1 = y0_ref.shape[0] // FFT_NB
    kpt = x_ref.shape[0] // n1
    k2_off = ((pl.program_id(0) % spb) * kpt) % FFT_NB
    sizes = OUTMLP_SUBTILE_K2 if sum(OUTMLP_SUBTILE_K2) == kpt else (kpt,)
    subs, k2l = [], 0
    for kps in sizes:
        subs.append(dict(k2l=k2l, kps=kps, rs=slice(k2l * n1, (k2l + kps) * n1)))
        k2l += kps
    for s in subs:
        rs = s["rs"]
        y_refs = ((y0_ref, ym0_ref), (y1_ref, ym1_ref), (y2_ref, ym2_ref), (y3_ref, ym3_ref))
        yf = jnp.concatenate([_gather_rows(y_refs, k2_off + s["k2l"] + kk, n1) for kk in range(s["kps"])],
                             axis=0).astype(BF16)
        hsum = hf_ref[rs, :].astype(F32) + hb_ref[rs, :].astype(F32)
        heads = [_ln_plain(hsum[:, j * V_DIM:(j + 1) * V_DIM]) for j in range(N_HEADS)]
        ym = jnp.concatenate(heads, axis=1) * nw_ref[...] * jax.nn.sigmoid(o_ref[rs, :].astype(F32))
        s["mix"] = _dot(jnp.concatenate([ym.astype(BF16), yf], axis=1), wout_ref[...])
    for s in subs:
        x1 = _ln_plain(alpha * x_ref[s["rs"], :] + (1.0 + g1_ref[...]) * s["mix"]) * l1g_ref[...] + l1b_ref[...]
        s["x1"] = x1
        s["h2"] = (_ln_plain(x1) * (1.0 + sc2_ref[...]) + sh2_ref[...]).astype(BF16)
    for s in subs:
        ff = b2_ref[...]
        for j in range(w1_ref.shape[1] // ff_chunk):
            sl = slice(j * ff_chunk, (j + 1) * ff_chunk)
            hid = jnp.maximum(_dot(s["h2"], w1_ref[:, sl]) + b1_ref[:, sl], 0.0)
            ff = ff + _dot((hid * hid).astype(BF16), w2_ref[sl, :])
        s["ff"] = ff
    for s in subs:
        out_ref[s["rs"], :] = (_ln_plain(alpha * s["x1"] + (1.0 + g2_ref[...]) * s["ff"]) * l2g_ref[...]
                               + l2b_ref[...])


def _outmlp(x2d, hf, hb, o, y5, mod4, w, seq, tm, alpha, ff_chunk):
    t, d = x2d.shape
    spb = seq // tm
    n1 = seq // CHUNK
    kpt = tm // n1
    assert FFT_NB % kpt == 0
    vw = N_HEADS * V_DIM
    row = lambda width: pl.BlockSpec((tm, width), lambda i: (i, 0))
    nblk = CHUNK // FFT_NB
    kblk = lambda i: ((i % spb) * kpt) // FFT_NB
    yspec = lambda g: pl.BlockSpec((None, None, None, n1 * FFT_NB, GROUP_DIM),
                                   lambda i: (i // spb, g, kblk(i), 0, 0))
    ymspec = lambda g: pl.BlockSpec((None, None, None, n1 * FFT_NB, GROUP_DIM),
                                    lambda i: (i // spb, g, nblk - 1 - kblk(i), 0, 0))
    modspec = lambda j: pl.BlockSpec((None, None, 1, d), lambda i: (i // spb, j, 0, 0))
    const = lambda a: pl.BlockSpec(a.shape, lambda i: (0,) * a.ndim, pipeline_mode=pl.Buffered(1))
    names = ["nw", "wout", "l1g", "l1b", "w1", "b1", "w2", "b2", "l2g", "l2b"]
    return pl.pallas_call(
        functools.partial(_outmlp_kernel, alpha, ff_chunk, spb),
        grid=(t // tm,),
        in_specs=[row(d), row(vw), row(vw), row(vw)]
                 + [yspec(g) for g in range(N_GROUPS)] + [ymspec(g) for g in range(N_GROUPS)]
                 + [modspec(2), modspec(4), modspec(3), modspec(5)] + [const(w[k]) for k in names],
        out_specs=row(d),
        out_shape=jax.ShapeDtypeStruct((t, d), F32),
        compiler_params=_vmem_params(("parallel",), "outmlp"),
        name="outmlp",
    )(x2d, hf, hb, o, *([y5] * (2 * N_GROUPS)), mod4, mod4, mod4, mod4, *[w[k] for k in names])


def _layer_weights(w_in, b_gate, w_out, w_ff1, w_ff2, b_ff1, b_ff2, mlstm_norm_w, ln1_g, ln1_b, ln2_g, ln2_b):
    qk = N_HEADS * QK_DIM
    vw = N_HEADS * V_DIM
    fw = N_GROUPS * GROUP_DIM
    o0, o1, o2, o3, o4 = qk, 2 * qk, 2 * qk + vw, 2 * qk + 2 * vw, 2 * qk + 2 * vw + fw
    wg = w_in[:, o4:].T.reshape(2, 2, N_HEADS, -1)
    bg = b_gate.astype(F32).reshape(2, 2, N_HEADS, 1)
    dup = lambda a: jnp.concatenate([a, a], axis=1)
    r = lambda a: a.astype(F32).reshape(1, -1)
    wq = w_in[:, :o0] * (QK_DIM ** -0.5)
    gate_rows = lambda a: jnp.concatenate([dup(a[:, 0]), dup(a[:, 1])], axis=1).reshape(32, -1)
    return {
        "wrow": jnp.concatenate([wq, w_in[:, o1:o4]], axis=1).astype(BF16),
        "wlane": jnp.concatenate([w_in[:, o0:o1].T, gate_rows(wg)], axis=0).astype(BF16),
        "bg": gate_rows(bg),
        "nw": r(mlstm_norm_w), "wout": w_out, "l1g": r(ln1_g), "l1b": r(ln1_b),
        "w1": w_ff1, "b1": r(b_ff1), "w2": w_ff2, "b2": r(b_ff2),
        "l2g": r(ln2_g), "l2b": r(ln2_b),
    }


def kernel(x, c, w_ada, b_ada, w_in, b_gate, mlstm_norm_w, w_out, ln1_g, ln1_b,
           w_ff1, b_ff1, w_ff2, b_ff2, ln2_g, ln2_b):
    bsz, seq, d = x.shape
    depth = w_ada.shape[0]
    alpha = (2 * depth) ** 0.25
    assert seq % (MLSTM_CPB * CHUNK) == 0 and (seq // CHUNK) % 8 == 0
    assert (seq // CHUNK // 2) % (FFT_BSTEPS * FFT_KB) == 0
    assert d == N_HEADS * V_DIM + N_GROUPS * GROUP_DIM
    tm_out = min(OUTMLP_ROWS, FFT_NB * (seq // CHUNK))
    x2d = x.reshape(bsz * seq, d)
    for l in range(depth):
        w = _layer_weights(w_in[l], b_gate[l], w_out[l], w_ff1[l], w_ff2[l], b_ff1[l], b_ff2[l],
                           mlstm_norm_w[l], ln1_g[l], ln1_b[l], ln2_g[l], ln2_b[l])
        mod4 = _adaln(c, w_ada[l], b_ada[l]).reshape(bsz, N_MOD, 1, d)
        q, kt, v, o, z5, gi, gf, w["wout"], w["w1"], w["w2"] = _inproj(x2d, mod4, w, seq, min(INPROJ_ROWS, seq))
        ra, rb, rc = _gate_prep(gi, gf, bsz)
        hf, hb = _mlstm(q, kt, v, ra, rb, rc, bsz, seq, MLSTM_CPB)
        y5 = _fourier(z5, seq)
        x2d = _outmlp(x2d, hf, hb, o, y5, mod4, w, seq, tm_out, alpha, ff_chunk=1024)
    return x2d.reshape(bsz, seq, d)
```

```python
import functools

import numpy as np
import jax
import jax.numpy as jnp
from jax import lax
from jax.experimental import pallas as pl
from jax.experimental.pallas import tpu as pltpu

F32 = jnp.float32
BF16 = jnp.bfloat16

CHUNK = 128
N_HEADS = 4
QK_DIM = 64
V_DIM = 128
N_GROUPS = 4
GROUP_DIM = 128
N_MOD = 6
LN_EPS = 1e-5
MASKED = -1e30
FFT_NB = 8
FFT_KB = 4
FFT_SB = 16
FFT_GB = 4
FFT_BSTEPS = 1
MLSTM_CPB = 8
OUTMLP_SUBTILE_K2 = (2, 2)
INPROJ_SUBTILES = 4

_NT = (((1,), (1,)), ((), ()))


def _dot(a, b):
    return jnp.dot(a, b, preferred_element_type=F32)


def _dot_nt(a, b):
    return lax.dot_general(a, b, _NT, preferred_element_type=F32)


def _ln_plain(x):
    mu = jnp.mean(x, axis=-1, keepdims=True)
    xc = x - mu
    var = jnp.mean(xc * xc, axis=-1, keepdims=True)
    return xc * lax.rsqrt(var + LN_EPS)


def _log_sigmoid(x):
    return jnp.minimum(x, 0.0) - jnp.log1p(jnp.exp(-jnp.abs(x)))


def _split3(x):
    hi = x.astype(BF16)
    r1 = x - hi.astype(F32)
    mid = r1.astype(BF16)
    lo = (r1 - mid.astype(F32)).astype(BF16)
    return hi, mid, lo


INPROJ_ROWS = 1024
OUTMLP_ROWS = 512
VMEM_LIMIT_MIB = dict(adaln=32, inproj=48, gate_prep=32, mlstm=32, fft=56, outmlp=60)


def _vmem_params(semantics, call):
    return pltpu.CompilerParams(dimension_semantics=semantics, vmem_limit_bytes=VMEM_LIMIT_MIB[call] * 1024 * 1024)


def _adaln_kernel(ct_ref, w_ref, b_ref, o_ref):
    ct = ct_ref[...]
    act = ct * jax.nn.sigmoid(ct)
    w = w_ref[...]
    for b in range(ct.shape[1]):
        o_ref[b:b + 1, :] = jnp.sum(act[:, b:b + 1] * w, axis=0, keepdims=True) + b_ref[...]


def _adaln(c, w_ada, b_ada):
    bsz, d = c.shape
    n = w_ada.shape[1]
    tn = 1024
    return pl.pallas_call(
        _adaln_kernel,
        grid=(n // tn,),
        in_specs=[pl.BlockSpec((d, bsz), lambda j: (0, 0)),
                  pl.BlockSpec((d, tn), lambda j: (0, j)),
                  pl.BlockSpec((1, tn), lambda j: (0, j))],
        out_specs=pl.BlockSpec((bsz, tn), lambda j: (0, j)),
        out_shape=jax.ShapeDtypeStruct((bsz, n), F32),
        compiler_params=_vmem_params(("parallel",), "adaln"),
        name="adaln",
    )(c.T, w_ada, b_ada.reshape(1, n))


def _inproj_kernel(x_ref, sh_ref, sc_ref, wrow_ref, wlane_ref, bg_ref, wout_ref, w1_ref, w2_ref,
                   q_ref, kt_ref, v_ref, o_ref, z_ref, gi_ref, gf_ref, woutb_ref, w1b_ref, w2b_ref):
    woutb_ref[...] = wout_ref[...].astype(BF16)
    w1b_ref[...] = w1_ref[...].astype(BF16)
    w2b_ref[...] = w2_ref[...].astype(BF16)
    qk, vw = q_ref.shape[1], v_ref.shape[1]
    n_chunks = x_ref.shape[0] // CHUNK
    cps = max(1, n_chunks // INPROJ_SUBTILES)
    for c0 in range(0, n_chunks, cps):
        rs = slice(c0 * CHUNK, (c0 + cps) * CHUNK)
        h = _ln_plain(x_ref[rs, :]) * (1.0 + sc_ref[...]) + sh_ref[...]
        hb = h.astype(BF16)
        rowp = _dot(hb, wrow_ref[...])
        q_ref[rs, :] = rowp[:, :qk].astype(BF16)
        v_ref[rs, :] = rowp[:, qk:qk + vw].astype(BF16)
        o_ref[rs, :] = rowp[:, qk + vw:qk + 2 * vw].astype(BF16)
        fz = rowp[:, qk + 2 * vw:]
        for j in range(cps):
            for g in range(N_GROUPS):
                for jb in range(CHUNK // FFT_NB):
                    r0 = j * CHUNK + jb * FFT_NB
                    z_ref[g, jb, (c0 + j) * FFT_NB:(c0 + j + 1) * FFT_NB, :] = (
                        fz[r0:r0 + FFT_NB, g * GROUP_DIM:(g + 1) * GROUP_DIM])
        lanep = _dot_nt(wlane_ref[...], hb)
        gates = lanep[qk:, :] + bg_ref[...]
        for j in range(cps):
            cl = slice(j * CHUNK, (j + 1) * CHUNK)
            kt_ref[c0 + j] = lanep[:qk, cl].astype(BF16)
            for d in range(2):
                gi_ref[d, (c0 + j) * 8:(c0 + j + 1) * 8, :] = gates[16 * d:16 * d + 8, cl]
                gf_ref[d, (c0 + j) * 8:(c0 + j + 1) * 8, :] = gates[16 * d + 8:16 * d + 16, cl]


def _inproj(x2d, mod4, w, seq, tm):
    t, d = x2d.shape
    spb = seq // tm
    qk = N_HEADS * QK_DIM
    vw = N_HEADS * V_DIM
    nblk = CHUNK // FFT_NB
    cpt = tm // CHUNK
    full = lambda a: pl.BlockSpec(a.shape, lambda i: (0,) * a.ndim)
    modspec = lambda j: pl.BlockSpec((None, None, 1, d), lambda i: (i // spb, j, 0, 0))
    names = ["wrow", "wlane", "bg"]
    steps = t // tm
    casts = [w["wout"], w["w1"], w["w2"]]
    assert all(a.shape[0] % (16 * steps) == 0 for a in casts)
    slab = lambda a: pl.BlockSpec((a.shape[0] // steps, a.shape[1]), lambda i: (i, 0))
    return pl.pallas_call(
        _inproj_kernel,
        grid=(steps,),
        in_specs=[pl.BlockSpec((tm, d), lambda i: (i, 0)), modspec(0), modspec(1)] + [full(w[k]) for k in names]
                 + [slab(a) for a in casts],
        out_specs=[pl.BlockSpec((tm, qk), lambda i: (i, 0)),
                   pl.BlockSpec((cpt, qk, CHUNK), lambda i: (i, 0, 0)),
                   pl.BlockSpec((tm, vw), lambda i: (i, 0)),
                   pl.BlockSpec((tm, vw), lambda i: (i, 0)),
                   pl.BlockSpec((None, N_GROUPS, nblk, cpt * FFT_NB, GROUP_DIM),
                                lambda i: (i // spb, 0, 0, i % spb, 0)),
                   pl.BlockSpec((2, cpt * 8, CHUNK), lambda i: (0, i, 0)),
                   pl.BlockSpec((2, cpt * 8, CHUNK), lambda i: (0, i, 0))] + [slab(a) for a in casts],
        out_shape=[jax.ShapeDtypeStruct((t, qk), BF16),
                   jax.ShapeDtypeStruct((t // CHUNK, qk, CHUNK), BF16),
                   jax.ShapeDtypeStruct((t, vw), BF16),
                   jax.ShapeDtypeStruct((t, vw), BF16),
                   jax.ShapeDtypeStruct((t // seq, N_GROUPS, nblk, (seq // CHUNK) * FFT_NB, GROUP_DIM), F32),
                   jax.ShapeDtypeStruct((2, t // CHUNK * 8, CHUNK), F32),
                   jax.ShapeDtypeStruct((2, t // CHUNK * 8, CHUNK), F32)]
                  + [jax.ShapeDtypeStruct(a.shape, BF16) for a in casts],
        compiler_params=_vmem_params(("parallel",), "inproj"),
        name="inproj",
    )(x2d, mod4, mod4, *[w[k] for k in names], *casts)


def _gate_prep_kernel(gi_ref, gf_ref, ra_ref, rb_ref, rc_ref, bend_scr, amax_scr, mprev_scr):
    rows = gi_ref.shape[1]
    n_chunks = rows // 8
    lane = lax.broadcasted_iota(jnp.int32, (rows, CHUNK), 1)
    first = (lax.broadcasted_iota(jnp.int32, (rows, CHUNK), 0) & 7) < N_HEADS
    src = lax.broadcasted_iota(jnp.int32, (CHUNK, 2 * CHUNK), 0)
    dst = lax.broadcasted_iota(jnp.int32, (CHUNK, 2 * CHUNK), 1)
    per_dir = []
    for d in range(2):
        feeds = (src <= dst) if d == 0 else (src >= dst)
        cum_and_total = jnp.where((dst >= CHUNK) | feeds, 1.0, 0.0).astype(BF16)
        bc = sum(_dot(p, cum_and_total) for p in _split3(_log_sigmoid(gf_ref[d])))
        b, b_end = bc[:, :CHUNK], bc[:, CHUNK:]
        a = gi_ref[d] - b
        cm = a
        for k in range(7):
            sh = 1 << k
            if d == 0:
                cm = jnp.maximum(cm, jnp.where(lane >= sh, pltpu.roll(cm, sh, 1), MASKED))
            else:
                cm = jnp.maximum(cm, jnp.where(lane < CHUNK - sh, pltpu.roll(cm, CHUNK - sh, 1), MASKED))
        a_max = jnp.broadcast_to(jnp.max(a, axis=1, keepdims=True), a.shape)
        bend_scr[d] = b_end
        amax_scr[d] = a_max
        per_dir.append((a, b, b_end, cm, a_max))

    def scan(c, ms):
        out = []
        for d, m in enumerate(ms):
            r = pl.ds(pl.multiple_of((c if d == 0 else n_chunks - 1 - c) * 8, 8), 8)
            mprev_scr[d, r, :] = m
            out.append(bend_scr[d, r, :] + jnp.maximum(m, amax_scr[d, r, :]))
        return tuple(out)

    lax.fori_loop(0, n_chunks, scan, (jnp.zeros((8, CHUNK), F32),) * 2)
    for d, (a, b, b_end, cm, a_max) in enumerate(per_dir):
        m_prev = mprev_scr[d]
        u = -jnp.maximum(m_prev, cm)
        m_new = b_end + jnp.maximum(m_prev, a_max)
        ra_ref[d] = jnp.where(first, a, jnp.exp(b_end + a - m_new))
        rb_ref[d] = jnp.where(first, u, jnp.exp(m_prev + u))
        rc_ref[d] = jnp.where(first, u - b, jnp.exp(b_end + m_prev - m_new))


def _gate_prep(gi, gf, bsz):
    rows = gi.shape[1] // bsz
    spec = pl.BlockSpec((2, rows, CHUNK), lambda b: (0, b, 0))
    shape = jax.ShapeDtypeStruct(gi.shape, F32)
    return pl.pallas_call(
        _gate_prep_kernel,
        grid=(bsz,),
        in_specs=[spec, spec],
        out_specs=[spec, spec, spec],
        out_shape=[shape, shape, shape],
        scratch_shapes=[pltpu.VMEM((2, rows, CHUNK), F32)] * 3,
        compiler_params=_vmem_params(("parallel",), "gate_prep"),
        name="gate_prep",
    )(gi, gf)


@functools.lru_cache(maxsize=None)
def _column_selector():
    sel = np.zeros((48, (N_HEADS // 2) * 4 * CHUNK), np.float32)

    def pick(arr, row, lane0, width):
        for piece in range(3):
            sel[arr * 24 + piece * 8 + row, lane0:lane0 + width] = 1.0

    for hd in range(N_HEADS):
        base = 4 * (hd // 2) * CHUNK
        pick(0, hd, base + (hd % 2) * CHUNK, CHUNK)
        pick(1, hd, base + 2 * CHUNK + (hd % 2) * QK_DIM, QK_DIM)
        pick(0, N_HEADS + hd, base + 3 * CHUNK + (hd % 2) * QK_DIM, QK_DIM)
    return sel


def _mlstm_kernel(sel_ref, qf_ref, ktf_ref, vf_ref, raf_ref, rbf_ref, rcf_ref,
                  qb_ref, ktb_ref, vb_ref, rab_ref, rbb_ref, rcb_ref, hf_ref, hb_ref, c_scr):
    @pl.when(pl.program_id(1) == 0)
    def _():
        c_scr[...] = jnp.zeros_like(c_scr)

    cpb = ktf_ref.shape[0]
    t_i = lax.broadcasted_iota(jnp.int32, (CHUNK, CHUNK), 0)
    s_i = lax.broadcasted_iota(jnp.int32, (CHUNK, CHUNK), 1)
    visible = (s_i <= t_i, s_i >= t_i)
    low_lanes = s_i < QK_DIM
    ones = jnp.ones((CHUNK, V_DIM), BF16)
    kzero = jnp.zeros((QK_DIM, CHUNK), BF16)
    czero = jnp.zeros((QK_DIM, 2 * V_DIM), BF16)
    dirs = ((qf_ref, ktf_ref, vf_ref, raf_ref, rbf_ref, rcf_ref, hf_ref),
            (qb_ref, ktb_ref, vb_ref, rab_ref, rbb_ref, rcb_ref, hb_ref))

    def body(i, carry):
        units = []
        for d, (q_ref, kt_ref, v_ref, ra_ref, rb_ref, rc_ref, h_ref) in enumerate(dirs):
            li = i if d == 0 else cpb - 1 - i
            r8 = pl.ds(pl.multiple_of(li * 8, 8), 8)
            rl = pl.ds(pl.multiple_of(li * CHUNK, CHUNK), CHUNK)
            ra = ra_ref[r8, :]
            rc = rc_ref[r8, :]
            pieces = [p.astype(F32) for arr in (rb_ref[r8, :], rc) for p in _split3(arr)]
            colb = lax.dot_general(jnp.concatenate(pieces, axis=0).astype(BF16), sel_ref[...],
                                   (((0,), (0,)), ((), ())), preferred_element_type=F32)
            kt = kt_ref[li]
            for pair in range(N_HEADS // 2):
                k_a = kt[2 * pair * QK_DIM:(2 * pair + 1) * QK_DIM, :]
                k_b = kt[(2 * pair + 1) * QK_DIM:(2 * pair + 2) * QK_DIM, :]
                kdiag = jnp.concatenate([jnp.concatenate([k_a, kzero], axis=1),
                                         jnp.concatenate([kzero, k_b], axis=1)], axis=0)
                q2 = q_ref[rl, 2 * pair * QK_DIM:2 * (pair + 1) * QK_DIM]
                qk2 = _dot(q2, kdiag)
                w_inter2 = colb[:, (4 * pair + 3) * CHUNK:(4 * pair + 4) * CHUNK]
                q_inter2 = (q2.astype(F32) * w_inter2).astype(BF16)
                for hd in (2 * pair, 2 * pair + 1):
                    kth = kt[hd * QK_DIM:(hd + 1) * QK_DIM, :]
                    units.append(dict(d=d, hd=hd, rl=rl, h_ref=h_ref, ra=ra, rc=rc, colb=colb, kth=kth,
                                      q_inter2=q_inter2, v_ref=v_ref,
                                      qk=qk2[:, (hd % 2) * CHUNK:(hd % 2 + 1) * CHUNK]))
        for u in units:
            d, hd, colb = u["d"], u["hd"], u["colb"]
            c0 = (4 * (hd // 2) + hd % 2) * CHUNK
            u_b = colb[:, c0:c0 + CHUNK]
            w_intra = jnp.exp(jnp.where(visible[d], u_b + u["ra"][hd:hd + 1, :], MASKED))
            s = (u["qk"] * w_intra).astype(BF16)
            u["c_prev"] = c_scr[d, hd]
            c_bf = u["c_prev"].astype(BF16)
            c_rows = [c_bf, czero] if hd % 2 == 0 else [czero, c_bf]
            u["v_aug"] = jnp.concatenate([u["v_ref"][u["rl"], hd * V_DIM:(hd + 1) * V_DIM], ones], axis=1)
            u["nd"] = _dot(jnp.concatenate([s, u["q_inter2"]], axis=1),
                           jnp.concatenate([u["v_aug"]] + c_rows, axis=0))
        for u in units:
            hd, nd = u["hd"], u["nd"]
            e0 = (4 * (hd // 2) + 2) * CHUNK
            e2 = jnp.exp(u["colb"][:, e0:e0 + CHUNK])
            own = low_lanes if hd % 2 == 0 else ~low_lanes
            exp_neg_m = jnp.where(own, e2, pltpu.roll(e2, QK_DIM, 1))
            den = jnp.maximum(jnp.abs(nd[:, V_DIM:]), exp_neg_m)
            u["h_ref"][u["rl"], hd * V_DIM:(hd + 1) * V_DIM] = (nd[:, :V_DIM] / den).astype(u["h_ref"].dtype)
            wk_row = u["ra"][N_HEADS + hd:N_HEADS + hd + 1, :]
            decay = jnp.broadcast_to(u["rc"][N_HEADS + hd:N_HEADS + hd + 1, :], (QK_DIM, CHUNK))
            kw = (u["kth"].astype(F32) * wk_row).astype(BF16)
            c_scr[u["d"], hd] = jnp.concatenate([decay, decay], axis=1) * u["c_prev"] + _dot(kw, u["v_aug"])
        return carry

    lax.fori_loop(0, cpb, body, 0, unroll=4)


def _mlstm(q, kt, v, ra, rb, rc, bsz, seq, cpb):
    t = q.shape[0]
    nblk = seq // (cpb * CHUNK)
    qk = N_HEADS * QK_DIM
    vw = N_HEADS * V_DIM
    fwd = lambda b, j: b * nblk + j
    bwd = lambda b, j: b * nblk + nblk - 1 - j

    def specs(blk, d):
        return [pl.BlockSpec((cpb * CHUNK, qk), lambda b, j: (blk(b, j), 0)),
                pl.BlockSpec((cpb, qk, CHUNK), lambda b, j: (blk(b, j), 0, 0)),
                pl.BlockSpec((cpb * CHUNK, vw), lambda b, j: (blk(b, j), 0))] + \
               [pl.BlockSpec((None, cpb * 8, CHUNK), lambda b, j: (d, blk(b, j), 0))] * 3

    sel = jnp.asarray(_column_selector()).astype(BF16)
    return pl.pallas_call(
        _mlstm_kernel,
        grid=(bsz, nblk),
        in_specs=[pl.BlockSpec(sel.shape, lambda b, j: (0, 0))] + specs(fwd, 0) + specs(bwd, 1),
        out_specs=[pl.BlockSpec((cpb * CHUNK, vw), lambda b, j: (fwd(b, j), 0)),
                   pl.BlockSpec((cpb * CHUNK, vw), lambda b, j: (bwd(b, j), 0))],
        out_shape=[jax.ShapeDtypeStruct((t, vw), BF16)] * 2,
        scratch_shapes=[pltpu.VMEM((2, N_HEADS, QK_DIM, 2 * V_DIM), F32)],
        compiler_params=_vmem_params(("parallel", "arbitrary"), "mlstm"),
        name="mlstm",
    )(sel, q, kt, v, ra, rb, rc, q, kt, v, ra, rb, rc)


def _fft_kh(n1):
    return n1 // 2 + FFT_KB


@functools.lru_cache(maxsize=None)
def _fft_tables(seq):
    n1 = seq // CHUNK
    kh = _fft_kh(n1)
    two_pi = 2.0 * np.pi
    k1 = np.arange(kh, dtype=np.int64)
    n = 128 * np.arange(n1, dtype=np.int64)[None, None, :] + np.arange(128, dtype=np.int64)[:, None, None]
    ang = two_pi * ((k1[None, :, None] * n) % seq).astype(np.float64) / seq
    ga = np.concatenate([np.cos(ang), -np.sin(ang)], axis=1)
    ga = ga.reshape(64, 2, 2 * kh, n1).transpose(0, 2, 1, 3).reshape(64, 2 * kh, 2 * n1)
    j = np.arange(128, dtype=np.int64)
    a128 = two_pi * ((j[:, None] * j[None, :]) % 128).astype(np.float64) / 128.0
    ff = np.concatenate([np.cos(a128), -np.sin(a128)], axis=0)
    cc, sc = np.cos(a128), np.sin(a128)
    cs = np.block([[cc, cc], [sc, -sc]]) / np.sqrt(128.0 * seq)
    return tuple(np.asarray(a, dtype=np.float32) for a in (ga, ff, cs))


def _fft_kernel(z_ref, ga_ref, ff_ref, cs_ref, y_ref, scr):
    j = pl.program_id(2)
    n1 = ga_ref.shape[2] // 2
    kh2 = ga_ref.shape[1]
    half = n1 // 2
    a_steps = CHUNK // (FFT_NB * z_ref.shape[0])

    @pl.when(j < a_steps)
    def _():
        zero = jnp.zeros((n1, GROUP_DIM), BF16)
        for sb in range(z_ref.shape[0]):
            for l in range(0, FFT_NB, 2):
                n2 = (j * z_ref.shape[0] + sb) * FFT_NB + l
                z_a = z_ref[sb, pl.ds(l, n1, stride=FFT_NB), :].astype(BF16)
                z_b = z_ref[sb, pl.ds(l + 1, n1, stride=FFT_NB), :].astype(BF16)
                zd = jnp.concatenate([jnp.concatenate([z_a, zero], axis=1),
                                      jnp.concatenate([zero, z_b], axis=1)], axis=0)
                p2 = _dot(ga_ref[n2 // 2], zd)
                for h in range(2):
                    row0 = pl.multiple_of((n2 + h) * kh2, 8)
                    scr[pl.ds(row0, kh2), :] = p2[:, h * GROUP_DIM:(h + 1) * GROUP_DIM]

    def spectra(k0s):
        ms = []
        for k0 in k0s:
            p = [scr[pl.ds(ri * (kh2 // 2) + k0 + kk, CHUNK, stride=kh2), :]
                 for kk in range(FFT_KB) for ri in range(2)]
            ms.append(_dot(ff_ref[...], jnp.concatenate(p, axis=1).astype(BF16)))
        ys = []
        for m in ms:
            x = []
            for kk in range(FFT_KB):
                c0 = 2 * kk * GROUP_DIM
                xr = m[:CHUNK, c0:c0 + GROUP_DIM] - m[CHUNK:, c0 + GROUP_DIM:c0 + 2 * GROUP_DIM]
                xi = m[:CHUNK, c0 + GROUP_DIM:c0 + 2 * GROUP_DIM] + m[CHUNK:, c0:c0 + GROUP_DIM]
                x.append(jnp.concatenate([xr, xi], axis=1))
            ys.append(_dot(jnp.concatenate(x, axis=0).astype(BF16), cs_ref[...]))
        return ys

    def store(y, kk, k1, lane0):
        row0 = pl.multiple_of(k1 * FFT_NB, 8)
        for kb in range(CHUNK // FFT_NB):
            r0 = kk * CHUNK + kb * FFT_NB
            y_ref[kb, pl.ds(row0, FFT_NB), :] = y[r0:r0 + FFT_NB, lane0:lane0 + GROUP_DIM]

    k1_step = half // FFT_BSTEPS
    gb = max(1, min(FFT_GB, k1_step // FFT_KB))

    @pl.when(j >= a_steps)
    def _():
        def body(it, carry):
            k0s = [(j - a_steps) * k1_step + (it * gb + gi) * FFT_KB for gi in range(gb)]
            for k0, y in zip(k0s, spectra(k0s)):
                for kk in range(FFT_KB):
                    k1 = k0 + kk
                    store(y, kk, jnp.where(k1 == 0, 0, n1 - k1), GROUP_DIM)
                    store(y, kk, k1, 0)
            return carry

        lax.fori_loop(0, k1_step // (FFT_KB * gb), body, 0)

    @pl.when(j == a_steps + FFT_BSTEPS - 1)
    def _():
        (y,) = spectra([half])
        store(y, 0, half, GROUP_DIM)


def _fourier(z5, seq):
    bsz = z5.shape[0]
    n1 = seq // CHUNK
    nblk = CHUNK // FFT_NB
    a_steps = nblk // FFT_SB
    ga, ff, cs = (jnp.asarray(a).astype(BF16) for a in _fft_tables(seq))
    return pl.pallas_call(
        _fft_kernel,
        grid=(bsz, N_GROUPS, a_steps + FFT_BSTEPS),
        in_specs=[pl.BlockSpec((None, None, FFT_SB, n1 * FFT_NB, GROUP_DIM),
                               lambda b, g, j: (b, g, jnp.minimum(j, a_steps - 1), 0, 0)),
                  pl.BlockSpec(ga.shape, lambda b, g, j: (0, 0, 0), pipeline_mode=pl.Buffered(1)),
                  pl.BlockSpec(ff.shape, lambda b, g, j: (0, 0)),
                  pl.BlockSpec(cs.shape, lambda b, g, j: (0, 0))],
        out_specs=pl.BlockSpec((None, None, nblk, n1 * FFT_NB, GROUP_DIM), lambda b, g, j: (b, g, 0, 0, 0)),
        out_shape=jax.ShapeDtypeStruct(z5.shape, F32),
        scratch_shapes=[pltpu.VMEM((CHUNK * 2 * _fft_kh(n1), GROUP_DIM), F32)],
        compiler_params=_vmem_params(("parallel", "parallel", "arbitrary"), "fft"),
        name="fft",
    )(z5, ga, ff, cs)


def _gather_rows(y_refs, k2l, n1):
    half = n1 // 2
    low = jnp.concatenate([a[pl.ds(k2l, half, stride=FFT_NB), :] for a, _ in y_refs], axis=1)
    high = jnp.concatenate([b[pl.ds(half * FFT_NB + FFT_NB - 1 - k2l, half, stride=FFT_NB), :] for _, b in y_refs],
                           axis=1)
    return jnp.concatenate([low, high], axis=0)


def _outmlp_kernel(alpha, ff_chunk, spb, x_ref, hf_ref, hb_ref, o_ref,
                   y0_ref, y1_ref, y2_ref, y3_ref, ym0_ref, ym1_ref, ym2_ref, ym3_ref,
                   g1_ref, sc2_ref, sh2_ref, g2_ref,
                   nw_ref, wout_ref, l1g_ref, l1b_ref, w1_ref, b1_ref, w2_ref, b2_ref, l2g_ref, l2b_ref,
                   out_ref):
    n1 = y0_ref.shape[0] // FFT_NB
    kpt = x_ref.shape[0] // n1
    k2_off = ((pl.program_id(0) % spb) * kpt) % FFT_NB
    sizes = OUTMLP_SUBTILE_K2 if sum(OUTMLP_SUBTILE_K2) == kpt else (kpt,)
    subs, k2l = [], 0
    for kps in sizes:
        subs.append(dict(k2l=k2l, kps=kps, rs=slice(k2l * n1, (k2l + kps) * n1)))
        k2l += kps
    for s in subs:
        rs = s["rs"]
        y_refs = ((y0_ref, ym0_ref), (y1_ref, ym1_ref), (y2_ref, ym2_ref), (y3_ref, ym3_ref))
        yf = jnp.concatenate([_gather_rows(y_refs, k2_off + s["k2l"] + kk, n1) for kk in range(s["kps"])],
                             axis=0).astype(BF16)
        hsum = hf_ref[rs, :].astype(F32) + hb_ref[rs, :].astype(F32)
        heads = [_ln_plain(hsum[:, j * V_DIM:(j + 1) * V_DIM]) for j in range(N_HEADS)]
        ym = jnp.concatenate(heads, axis=1) * nw_ref[...] * jax.nn.sigmoid(o_ref[rs, :].astype(F32))
        s["mix"] = _dot(jnp.concatenate([ym.astype(BF16), yf], axis=1), wout_ref[...])
    for s in subs:
        x1 = _ln_plain(alpha * x_ref[s["rs"], :] + (1.0 + g1_ref[...]) * s["mix"]) * l1g_ref[...] + l1b_ref[...]
        s["x1"] = x1
        s["h2"] = (_ln_plain(x1) * (1.0 + sc2_ref[...]) + sh2_ref[...]).astype(BF16)
    for s in subs:
        ff = b2_ref[...]
        for j in range(w1_ref.shape[1] // ff_chunk):
            sl = slice(j * ff_chunk, (j + 1) * ff_chunk)
            hid = jnp.maximum(_dot(s["h2"], w1_ref[:, sl]) + b1_ref[:, sl], 0.0)
            ff = ff + _dot((hid * hid).astype(BF16), w2_ref[sl, :])
        s["ff"] = ff
    for s in subs:
        out_ref[s["rs"], :] = (_ln_plain(alpha * s["x1"] + (1.0 + g2_ref[...]) * s["ff"]) * l2g_ref[...]
                               + l2b_ref[...])


def _outmlp(x2d, hf, hb, o, y5, mod4, w, seq, tm, alpha, ff_chunk):
    t, d = x2d.shape
    spb = seq // tm
    n1 = seq // CHUNK
    kpt = tm // n1
    assert FFT_NB % kpt == 0
    vw = N_HEADS * V_DIM
    row = lambda width: pl.BlockSpec((tm, width), lambda i: (i, 0))
    nblk = CHUNK // FFT_NB
    kblk = lambda i: ((i % spb) * kpt) // FFT_NB
    yspec = lambda g: pl.BlockSpec((None, None, None, n1 * FFT_NB, GROUP_DIM),
                                   lambda i: (i // spb, g, kblk(i), 0, 0))
    ymspec = lambda g: pl.BlockSpec((None, None, None, n1 * FFT_NB, GROUP_DIM),
                                    lambda i: (i // spb, g, nblk - 1 - kblk(i), 0, 0))
    modspec = lambda j: pl.BlockSpec((None, None, 1, d), lambda i: (i // spb, j, 0, 0))
    const = lambda a: pl.BlockSpec(a.shape, lambda i: (0,) * a.ndim, pipeline_mode=pl.Buffered(1))
    names = ["nw", "wout", "l1g", "l1b", "w1", "b1", "w2", "b2", "l2g", "l2b"]
    return pl.pallas_call(
        functools.partial(_outmlp_kernel, alpha, ff_chunk, spb),
        grid=(t // tm,),
        in_specs=[row(d), row(vw), row(vw), row(vw)]
                 + [yspec(g) for g in range(N_GROUPS)] + [ymspec(g) for g in range(N_GROUPS)]
                 + [modspec(2), modspec(4), modspec(3), modspec(5)] + [const(w[k]) for k in names],
        out_specs=row(d),
        out_shape=jax.ShapeDtypeStruct((t, d), F32),
        compiler_params=_vmem_params(("parallel",), "outmlp"),
        name="outmlp",
    )(x2d, hf, hb, o, *([y5] * (2 * N_GROUPS)), mod4, mod4, mod4, mod4, *[w[k] for k in names])


def _layer_weights(w_in, b_gate, w_out, w_ff1, w_ff2, b_ff1, b_ff2, mlstm_norm_w, ln1_g, ln1_b, ln2_g, ln2_b):
    qk = N_HEADS * QK_DIM
    vw = N_HEADS * V_DIM
    fw = N_GROUPS * GROUP_DIM
    o0, o1, o2, o3, o4 = qk, 2 * qk, 2 * qk + vw, 2 * qk + 2 * vw, 2 * qk + 2 * vw + fw
    wg = w_in[:, o4:].T.reshape(2, 2, N_HEADS, -1)
    bg = b_gate.astype(F32).reshape(2, 2, N_HEADS, 1)
    dup = lambda a: jnp.concatenate([a, a], axis=1)
    r = lambda a: a.astype(F32).reshape(1, -1)
    wq = w_in[:, :o0] * (QK_DIM ** -0.5)
    gate_rows = lambda a: jnp.concatenate([dup(a[:, 0]), dup(a[:, 1])], axis=1).reshape(32, -1)
    return {
        "wrow": jnp.concatenate([wq, w_in[:, o1:o4]], axis=1).astype(BF16),
        "wlane": jnp.concatenate([w_in[:, o0:o1].T, gate_rows(wg)], axis=0).astype(BF16),
        "bg": gate_rows(bg),
        "nw": r(mlstm_norm_w), "wout": w_out, "l1g": r(ln1_g), "l1b": r(ln1_b),
        "w1": w_ff1, "b1": r(b_ff1), "w2": w_ff2, "b2": r(b_ff2),
        "l2g": r(ln2_g), "l2b": r(ln2_b),
    }


def kernel(x, c, w_ada, b_ada, w_in, b_gate, mlstm_norm_w, w_out, ln1_g, ln1_b,
           w_ff1, b_ff1, w_ff2, b_ff2, ln2_g, ln2_b):
    bsz, seq, d = x.shape
    depth = w_ada.shape[0]
    alpha = (2 * depth) ** 0.25
    assert seq % (MLSTM_CPB * CHUNK) == 0 and (seq // CHUNK) % 8 == 0
    assert (seq // CHUNK // 2) % (FFT_BSTEPS * FFT_KB) == 0
    assert d == N_HEADS * V_DIM + N_GROUPS * GROUP_DIM
    tm_out = min(OUTMLP_ROWS, FFT_NB * (seq // CHUNK))
    x2d = x.reshape(bsz * seq, d)
    for l in range(depth):
        w = _layer_weights(w_in[l], b_gate[l], w_out[l], w_ff1[l], w_ff2[l], b_ff1[l], b_ff2[l],
                           mlstm_norm_w[l], ln1_g[l], ln1_b[l], ln2_g[l], ln2_b[l])
        mod4 = _adaln(c, w_ada[l], b_ada[l]).reshape(bsz, N_MOD, 1, d)
        q, kt, v, o, z5, gi, gf, w["wout"], w["w1"], w["w2"] = _inproj(x2d, mod4, w, seq, min(INPROJ_ROWS, seq))
        ra, rb, rc = _gate_prep(gi, gf, bsz)
        hf, hb = _mlstm(q, kt, v, ra, rb, rc, bsz, seq, MLSTM_CPB)
        y5 = _fourier(z5, seq)
        x2d = _outmlp(x2d, hf, hb, o, y5, mod4, w, seq, tm_out, alpha, ff_chunk=1024)
    return x2d.reshape(bsz, seq, d)
```

```python
import functools

import numpy as np
import jax
import jax.numpy as jnp
from jax import lax
from jax.experimental import pallas as pl
from jax.experimental.pallas import tpu as pltpu

F32 = jnp.float32
BF16 = jnp.bfloat16

CHUNK = 128
N_HEADS = 4
QK_DIM = 64
V_DIM = 128
N_GROUPS = 4
GROUP_DIM = 128
N_MOD = 6
LN_EPS = 1e-5
MASKED = -1e30
FFT_NB = 8
FFT_KB = 4
FFT_SB = 16
FFT_GB = 4
FFT_BSTEPS = 1
MLSTM_CPB = 8
OUTMLP_SUBTILE_K2 = (2, 2)
INPROJ_SUBTILES = 4

_NT = (((1,), (1,)), ((), ()))


def _dot(a, b):
    return jnp.dot(a, b, preferred_element_type=F32)


def _dot_nt(a, b):
    return lax.dot_general(a, b, _NT, preferred_element_type=F32)


def _ln_plain(x):
    mu = jnp.mean(x, axis=-1, keepdims=True)
    xc = x - mu
    var = jnp.mean(xc * xc, axis=-1, keepdims=True)
    return xc * lax.rsqrt(var + LN_EPS)


def _log_sigmoid(x):
    return jnp.minimum(x, 0.0) - jnp.log1p(jnp.exp(-jnp.abs(x)))


def _split3(x):
    hi = x.astype(BF16)
    r1 = x - hi.astype(F32)
    mid = r1.astype(BF16)
    lo = (r1 - mid.astype(F32)).astype(BF16)
    return hi, mid, lo


INPROJ_ROWS = 1024
OUTMLP_ROWS = 512
VMEM_LIMIT_MIB = dict(adaln=32, inproj=48, gate_prep=32, mlstm=32, fft=56, outmlp=60)


def _vmem_params(semantics, call):
    return pltpu.CompilerParams(dimension_semantics=semantics, vmem_limit_bytes=VMEM_LIMIT_MIB[call] * 1024 * 1024)


def _adaln_kernel(ct_ref, w_ref, b_ref, o_ref):
    ct = ct_ref[...]
    act = ct * jax.nn.sigmoid(ct)
    w = w_ref[...]
    for b in range(ct.shape[1]):
        o_ref[b:b + 1, :] = jnp.sum(act[:, b:b + 1] * w, axis=0, keepdims=True) + b_ref[...]


def _adaln(c, w_ada, b_ada):
    bsz, d = c.shape
    n = w_ada.shape[1]
    tn = 1024
    return pl.pallas_call(
        _adaln_kernel,
        grid=(n // tn,),
        in_specs=[pl.BlockSpec((d, bsz), lambda j: (0, 0)),
                  pl.BlockSpec((d, tn), lambda j: (0, j)),
                  pl.BlockSpec((1, tn), lambda j: (0, j))],
        out_specs=pl.BlockSpec((bsz, tn), lambda j: (0, j)),
        out_shape=jax.ShapeDtypeStruct((bsz, n), F32),
        compiler_params=_vmem_params(("parallel",), "adaln"),
        name="adaln",
    )(c.T, w_ada, b_ada.reshape(1, n))


def _inproj_kernel(x_ref, sh_ref, sc_ref, wrow_ref, wlane_ref, bg_ref, wout_ref, w1_ref, w2_ref,
                   q_ref, kt_ref, v_ref, o_ref, z_ref, gi_ref, gf_ref, woutb_ref, w1b_ref, w2b_ref):
    woutb_ref[...] = wout_ref[...].astype(BF16)
    w1b_ref[...] = w1_ref[...].astype(BF16)
    w2b_ref[...] = w2_ref[...].astype(BF16)
    qk, vw = q_ref.shape[1], v_ref.shape[1]
    n_chunks = x_ref.shape[0] // CHUNK
    cps = max(1, n_chunks // INPROJ_SUBTILES)
    for c0 in range(0, n_chunks, cps):
        rs = slice(c0 * CHUNK, (c0 + cps) * CHUNK)
        h = _ln_plain(x_ref[rs, :]) * (1.0 + sc_ref[...]) + sh_ref[...]
        hb = h.astype(BF16)
        rowp = _dot(hb, wrow_ref[...])
        q_ref[rs, :] = rowp[:, :qk].astype(BF16)
        v_ref[rs, :] = rowp[:, qk:qk + vw].astype(BF16)
        o_ref[rs, :] = rowp[:, qk + vw:qk + 2 * vw].astype(BF16)
        fz = rowp[:, qk + 2 * vw:]
        for j in range(cps):
            for g in range(N_GROUPS):
                for jb in range(CHUNK // FFT_NB):
                    r0 = j * CHUNK + jb * FFT_NB
                    z_ref[g, jb, (c0 + j) * FFT_NB:(c0 + j + 1) * FFT_NB, :] = (
                        fz[r0:r0 + FFT_NB, g * GROUP_DIM:(g + 1) * GROUP_DIM])
        lanep = _dot_nt(wlane_ref[...], hb)
        gates = lanep[qk:, :] + bg_ref[...]
        for j in range(cps):
            cl = slice(j * CHUNK, (j + 1) * CHUNK)
            kt_ref[c0 + j] = lanep[:qk, cl].astype(BF16)
            for d in range(2):
                gi_ref[d, (c0 + j) * 8:(c0 + j + 1) * 8, :] = gates[16 * d:16 * d + 8, cl]
                gf_ref[d, (c0 + j) * 8:(c0 + j + 1) * 8, :] = gates[16 * d + 8:16 * d + 16, cl]


def _inproj(x2d, mod4, w, seq, tm):
    t, d = x2d.shape
    spb = seq // tm
    qk = N_HEADS * QK_DIM
    vw = N_HEADS * V_DIM
    nblk = CHUNK // FFT_NB
    cpt = tm // CHUNK
    full = lambda a: pl.BlockSpec(a.shape, lambda i: (0,) * a.ndim)
    modspec = lambda j: pl.BlockSpec((None, None, 1, d), lambda i: (i // spb, j, 0, 0))
    names = ["wrow", "wlane", "bg"]
    steps = t // tm
    casts = [w["wout"], w["w1"], w["w2"]]
    assert all(a.shape[0] % (16 * steps) == 0 for a in casts)
    slab = lambda a: pl.BlockSpec((a.shape[0] // steps, a.shape[1]), lambda i: (i, 0))
    return pl.pallas_call(
        _inproj_kernel,
        grid=(steps,),
        in_specs=[pl.BlockSpec((tm, d), lambda i: (i, 0)), modspec(0), modspec(1)] + [full(w[k]) for k in names]
                 + [slab(a) for a in casts],
        out_specs=[pl.BlockSpec((tm, qk), lambda i: (i, 0)),
                   pl.BlockSpec((cpt, qk, CHUNK), lambda i: (i, 0, 0)),
                   pl.BlockSpec((tm, vw), lambda i: (i, 0)),
                   pl.BlockSpec((tm, vw), lambda i: (i, 0)),
                   pl.BlockSpec((None, N_GROUPS, nblk, cpt * FFT_NB, GROUP_DIM),
                                lambda i: (i // spb, 0, 0, i % spb, 0)),
                   pl.BlockSpec((2, cpt * 8, CHUNK), lambda i: (0, i, 0)),
                   pl.BlockSpec((2, cpt * 8, CHUNK), lambda i: (0, i, 0))] + [slab(a) for a in casts],
        out_shape=[jax.ShapeDtypeStruct((t, qk), BF16),
                   jax.ShapeDtypeStruct((t // CHUNK, qk, CHUNK), BF16),
                   jax.ShapeDtypeStruct((t, vw), BF16),
                   jax.ShapeDtypeStruct((t, vw), BF16),
                   jax.ShapeDtypeStruct((t // seq, N_GROUPS, nblk, (seq // CHUNK) * FFT_NB, GROUP_DIM), F32),
                   jax.ShapeDtypeStruct((2, t // CHUNK * 8, CHUNK), F32),
                   jax.ShapeDtypeStruct((2, t // CHUNK * 8, CHUNK), F32)]
                  + [jax.ShapeDtypeStruct(a.shape, BF16) for a in casts],
        compiler_params=_vmem_params(("parallel",), "inproj"),
        name="inproj",
    )(x2d, mod4, mod4, *[w[k] for k in names], *casts)


def _gate_prep_kernel(gi_ref, gf_ref, ra_ref, rb_ref, rc_ref, bend_scr, amax_scr, mprev_scr):
    rows = gi_ref.shape[1]
    n_chunks = rows // 8
    lane = lax.broadcasted_iota(jnp.int32, (rows, CHUNK), 1)
    first = (lax.broadcasted_iota(jnp.int32, (rows, CHUNK), 0) & 7) < N_HEADS
    src = lax.broadcasted_iota(jnp.int32, (CHUNK, 2 * CHUNK), 0)
    dst = lax.broadcasted_iota(jnp.int32, (CHUNK, 2 * CHUNK), 1)
    per_dir = []
    for d in range(2):
        feeds = (src <= dst) if d == 0 else (src >= dst)
        cum_and_total = jnp.where((dst >= CHUNK) | feeds, 1.0, 0.0).astype(BF16)
        bc = sum(_dot(p, cum_and_total) for p in _split3(_log_sigmoid(gf_ref[d])))
        b, b_end = bc[:, :CHUNK], bc[:, CHUNK:]
        a = gi_ref[d] - b
        cm = a
        for k in range(7):
            sh = 1 << k
            if d == 0:
                cm = jnp.maximum(cm, jnp.where(lane >= sh, pltpu.roll(cm, sh, 1), MASKED))
            else:
                cm = jnp.maximum(cm, jnp.where(lane < CHUNK - sh, pltpu.roll(cm, CHUNK - sh, 1), MASKED))
        a_max = jnp.broadcast_to(jnp.max(a, axis=1, keepdims=True), a.shape)
        bend_scr[d] = b_end
        amax_scr[d] = a_max
        per_dir.append((a, b, b_end, cm, a_max))

    def scan(c, ms):
        out = []
        for d, m in enumerate(ms):
            r = pl.ds(pl.multiple_of((c if d == 0 else n_chunks - 1 - c) * 8, 8), 8)
            mprev_scr[d, r, :] = m
            out.append(bend_scr[d, r, :] + jnp.maximum(m, amax_scr[d, r, :]))
        return tuple(out)

    lax.fori_loop(0, n_chunks, scan, (jnp.zeros((8, CHUNK), F32),) * 2)
    for d, (a, b, b_end, cm, a_max) in enumerate(per_dir):
        m_prev = mprev_scr[d]
        u = -jnp.maximum(m_prev, cm)
        m_new = b_end + jnp.maximum(m_prev, a_max)
        ra_ref[d] = jnp.where(first, a, jnp.exp(b_end + a - m_new))
        rb_ref[d] = jnp.where(first, u, jnp.exp(m_prev + u))
        rc_ref[d] = jnp.where(first, u - b, jnp.exp(b_end + m_prev - m_new))


def _gate_prep(gi, gf, bsz):
    rows = gi.shape[1] // bsz
    spec = pl.BlockSpec((2, rows, CHUNK), lambda b: (0, b, 0))
    shape = jax.ShapeDtypeStruct(gi.shape, F32)
    return pl.pallas_call(
        _gate_prep_kernel,
        grid=(bsz,),
        in_specs=[spec, spec],
        out_specs=[spec, spec, spec],
        out_shape=[shape, shape, shape],
        scratch_shapes=[pltpu.VMEM((2, rows, CHUNK), F32)] * 3,
        compiler_params=_vmem_params(("parallel",), "gate_prep"),
        name="gate_prep",
    )(gi, gf)


@functools.lru_cache(maxsize=None)
def _column_selector():
    sel = np.zeros((48, (N_HEADS // 2) * 4 * CHUNK), np.float32)

    def pick(arr, row, lane0, width):
        for piece in range(3):
            sel[arr * 24 + piece * 8 + row, lane0:lane0 + width] = 1.0

    for hd in range(N_HEADS):
        base = 4 * (hd // 2) * CHUNK
        pick(0, hd, base + (hd % 2) * CHUNK, CHUNK)
        pick(1, hd, base + 2 * CHUNK + (hd % 2) * QK_DIM, QK_DIM)
        pick(0, N_HEADS + hd, base + 3 * CHUNK + (hd % 2) * QK_DIM, QK_DIM)
    return sel


def _mlstm_kernel(sel_ref, qf_ref, ktf_ref, vf_ref, raf_ref, rbf_ref, rcf_ref,
                  qb_ref, ktb_ref, vb_ref, rab_ref, rbb_ref, rcb_ref, hf_ref, hb_ref, c_scr):
    @pl.when(pl.program_id(1) == 0)
    def _():
        c_scr[...] = jnp.zeros_like(c_scr)

    cpb = ktf_ref.shape[0]
    t_i = lax.broadcasted_iota(jnp.int32, (CHUNK, CHUNK), 0)
    s_i = lax.broadcasted_iota(jnp.int32, (CHUNK, CHUNK), 1)
    visible = (s_i <= t_i, s_i >= t_i)
    low_lanes = s_i < QK_DIM
    ones = jnp.ones((CHUNK, V_DIM), BF16)
    kzero = jnp.zeros((QK_DIM, CHUNK), BF16)
    czero = jnp.zeros((QK_DIM, 2 * V_DIM), BF16)
    dirs = ((qf_ref, ktf_ref, vf_ref, raf_ref, rbf_ref, rcf_ref, hf_ref),
            (qb_ref, ktb_ref, vb_ref, rab_ref, rbb_ref, rcb_ref, hb_ref))

    def body(i, carry):
        units = []
        for d, (q_ref, kt_ref, v_ref, ra_ref, rb_ref, rc_ref, h_ref) in enumerate(dirs):
            li = i if d == 0 else cpb - 1 - i
            r8 = pl.ds(pl.multiple_of(li * 8, 8), 8)
            rl = pl.ds(pl.multiple_of(li * CHUNK, CHUNK), CHUNK)
            ra = ra_ref[r8, :]
            rc = rc_ref[r8, :]
            pieces = [p.astype(F32) for arr in (rb_ref[r8, :], rc) for p in _split3(arr)]
            colb = lax.dot_general(jnp.concatenate(pieces, axis=0).astype(BF16), sel_ref[...],
                                   (((0,), (0,)), ((), ())), preferred_element_type=F32)
            kt = kt_ref[li]
            for pair in range(N_HEADS // 2):
                k_a = kt[2 * pair * QK_DIM:(2 * pair + 1) * QK_DIM, :]
                k_b = kt[(2 * pair + 1) * QK_DIM:(2 * pair + 2) * QK_DIM, :]
                kdiag = jnp.concatenate([jnp.concatenate([k_a, kzero], axis=1),
                                         jnp.concatenate([kzero, k_b], axis=1)], axis=0)
                q2 = q_ref[rl, 2 * pair * QK_DIM:2 * (pair + 1) * QK_DIM]
                qk2 = _dot(q2, kdiag)
                w_inter2 = colb[:, (4 * pair + 3) * CHUNK:(4 * pair + 4) * CHUNK]
                q_inter2 = (q2.astype(F32) * w_inter2).astype(BF16)
                for hd in (2 * pair, 2 * pair + 1):
                    kth = kt[hd * QK_DIM:(hd + 1) * QK_DIM, :]
                    units.append(dict(d=d, hd=hd, rl=rl, h_ref=h_ref, ra=ra, rc=rc, colb=colb, kth=kth,
                                      q_inter2=q_inter2, v_ref=v_ref,
                                      qk=qk2[:, (hd % 2) * CHUNK:(hd % 2 + 1) * CHUNK]))
        for u in units:
            d, hd, colb = u["d"], u["hd"], u["colb"]
            c0 = (4 * (hd // 2) + hd % 2) * CHUNK
            u_b = colb[:, c0:c0 + CHUNK]
            w_intra = jnp.exp(jnp.where(visible[d], u_b + u["ra"][hd:hd + 1, :], MASKED))
            s = (u["qk"] * w_intra).astype(BF16)
            u["c_prev"] = c_scr[d, hd]
            c_bf = u["c_prev"].astype(BF16)
            c_rows = [c_bf, czero] if hd % 2 == 0 else [czero, c_bf]
            u["v_aug"] = jnp.concatenate([u["v_ref"][u["rl"], hd * V_DIM:(hd + 1) * V_DIM], ones], axis=1)
            u["nd"] = _dot(jnp.concatenate([s, u["q_inter2"]], axis=1),
                           jnp.concatenate([u["v_aug"]] + c_rows, axis=0))
        for u in units:
            hd, nd = u["hd"], u["nd"]
            e0 = (4 * (hd // 2) + 2) * CHUNK
            e2 = jnp.exp(u["colb"][:, e0:e0 + CHUNK])
            own = low_lanes if hd % 2 == 0 else ~low_lanes
            exp_neg_m = jnp.where(own, e2, pltpu.roll(e2, QK_DIM, 1))
            den = jnp.maximum(jnp.abs(nd[:, V_DIM:]), exp_neg_m)
            u["h_ref"][u["rl"], hd * V_DIM:(hd + 1) * V_DIM] = (nd[:, :V_DIM] / den).astype(u["h_ref"].dtype)
            wk_row = u["ra"][N_HEADS + hd:N_HEADS + hd + 1, :]
            decay = jnp.broadcast_to(u["rc"][N_HEADS + hd:N_HEADS + hd + 1, :], (QK_DIM, CHUNK))
            kw = (u["kth"].astype(F32) * wk_row).astype(BF16)
            c_scr[u["d"], hd] = jnp.concatenate([decay, decay], axis=1) * u["c_prev"] + _dot(kw, u["v_aug"])
        return carry

    lax.fori_loop(0, cpb, body, 0, unroll=4)


def _mlstm(q, kt, v, ra, rb, rc, bsz, seq, cpb):
    t = q.shape[0]
    nblk = seq // (cpb * CHUNK)
    qk = N_HEADS * QK_DIM
    vw = N_HEADS * V_DIM
    fwd = lambda b, j: b * nblk + j
    bwd = lambda b, j: b * nblk + nblk - 1 - j

    def specs(blk, d):
        return [pl.BlockSpec((cpb * CHUNK, qk), lambda b, j: (blk(b, j), 0)),
                pl.BlockSpec((cpb, qk, CHUNK), lambda b, j: (blk(b, j), 0, 0)),
                pl.BlockSpec((cpb * CHUNK, vw), lambda b, j: (blk(b, j), 0))] + \
               [pl.BlockSpec((None, cpb * 8, CHUNK), lambda b, j: (d, blk(b, j), 0))] * 3

    sel = jnp.asarray(_column_selector()).astype(BF16)
    return pl.pallas_call(
        _mlstm_kernel,
        grid=(bsz, nblk),
        in_specs=[pl.BlockSpec(sel.shape, lambda b, j: (0, 0))] + specs(fwd, 0) + specs(bwd, 1),
        out_specs=[pl.BlockSpec((cpb * CHUNK, vw), lambda b, j: (fwd(b, j), 0)),
                   pl.BlockSpec((cpb * CHUNK, vw), lambda b, j: (bwd(b, j), 0))],
        out_shape=[jax.ShapeDtypeStruct((t, vw), BF16)] * 2,
        scratch_shapes=[pltpu.VMEM((2, N_HEADS, QK_DIM, 2 * V_DIM), F32)],
        compiler_params=_vmem_params(("parallel", "arbitrary"), "mlstm"),
        name="mlstm",
    )(sel, q, kt, v, ra, rb, rc, q, kt, v, ra, rb, rc)


def _fft_kh(n1):
    return n1 // 2 + FFT_KB


@functools.lru_cache(maxsize=None)
def _fft_tables(seq):
    n1 = seq // CHUNK
    kh = _fft_kh(n1)
    two_pi = 2.0 * np.pi
    k1 = np.arange(kh, dtype=np.int64)
    n = 128 * np.arange(n1, dtype=np.int64)[None, None, :] + np.arange(128, dtype=np.int64)[:, None, None]
    ang = two_pi * ((k1[None, :, None] * n) % seq).astype(np.float64) / seq
    ga = np.concatenate([np.cos(ang), -np.sin(ang)], axis=1)
    ga = ga.reshape(64, 2, 2 * kh, n1).transpose(0, 2, 1, 3).reshape(64, 2 * kh, 2 * n1)
    j = np.arange(128, dtype=np.int64)
    a128 = two_pi * ((j[:, None] * j[None, :]) % 128).astype(np.float64) / 128.0
    ff = np.concatenate([np.cos(a128), -np.sin(a128)], axis=0)
    cc, sc = np.cos(a128), np.sin(a128)
    cs = np.block([[cc, cc], [sc, -sc]]) / np.sqrt(128.0 * seq)
    return tuple(np.asarray(a, dtype=np.float32) for a in (ga, ff, cs))


def _fft_kernel(z_ref, ga_ref, ff_ref, cs_ref, y_ref, scr):
    j = pl.program_id(2)
    n1 = ga_ref.shape[2] // 2
    kh2 = ga_ref.shape[1]
    half = n1 // 2
    a_steps = CHUNK // (FFT_NB * z_ref.shape[0])

    @pl.when(j < a_steps)
    def _():
        zero = jnp.zeros((n1, GROUP_DIM), BF16)
        for sb in range(z_ref.shape[0]):
            for l in range(0, FFT_NB, 2):
                n2 = (j * z_ref.shape[0] + sb) * FFT_NB + l
                z_a = z_ref[sb, pl.ds(l, n1, stride=FFT_NB), :].astype(BF16)
                z_b = z_ref[sb, pl.ds(l + 1, n1, stride=FFT_NB), :].astype(BF16)
                zd = jnp.concatenate([jnp.concatenate([z_a, zero], axis=1),
                                      jnp.concatenate([zero, z_b], axis=1)], axis=0)
                p2 = _dot(ga_ref[n2 // 2].astype(BF16), zd)
                for h in range(2):
                    row0 = pl.multiple_of((n2 + h) * kh2, 8)
                    scr[pl.ds(row0, kh2), :] = p2[:, h * GROUP_DIM:(h + 1) * GROUP_DIM]

    def spectra(k0s):
        ms = []
        for k0 in k0s:
            p = [scr[pl.ds(ri * (kh2 // 2) + k0 + kk, CHUNK, stride=kh2), :]
                 for kk in range(FFT_KB) for ri in range(2)]
            ms.append(_dot(ff_ref[...].astype(BF16), jnp.concatenate(p, axis=1).astype(BF16)))
        ys = []
        for m in ms:
            x = []
            for kk in range(FFT_KB):
                c0 = 2 * kk * GROUP_DIM
                xr = m[:CHUNK, c0:c0 + GROUP_DIM] - m[CHUNK:, c0 + GROUP_DIM:c0 + 2 * GROUP_DIM]
                xi = m[:CHUNK, c0 + GROUP_DIM:c0 + 2 * GROUP_DIM] + m[CHUNK:, c0:c0 + GROUP_DIM]
                x.append(jnp.concatenate([xr, xi], axis=1))
            ys.append(_dot(jnp.concatenate(x, axis=0).astype(BF16), cs_ref[...].astype(BF16)))
        return ys

    def store(y, kk, k1, lane0):
        row0 = pl.multiple_of(k1 * FFT_NB, 8)
        for kb in range(CHUNK // FFT_NB):
            r0 = kk * CHUNK + kb * FFT_NB
            y_ref[kb, pl.ds(row0, FFT_NB), :] = y[r0:r0 + FFT_NB, lane0:lane0 + GROUP_DIM]

    k1_step = half // FFT_BSTEPS
    gb = max(1, min(FFT_GB, k1_step // FFT_KB))

    @pl.when(j >= a_steps)
    def _():
        def body(it, carry):
            k0s = [(j - a_steps) * k1_step + (it * gb + gi) * FFT_KB for gi in range(gb)]
            for k0, y in zip(k0s, spectra(k0s)):
                for kk in range(FFT_KB):
                    k1 = k0 + kk
                    store(y, kk, jnp.where(k1 == 0, 0, n1 - k1), GROUP_DIM)
                    store(y, kk, k1, 0)
            return carry

        lax.fori_loop(0, k1_step // (FFT_KB * gb), body, 0)

    @pl.when(j == a_steps + FFT_BSTEPS - 1)
    def _():
        (y,) = spectra([half])
        store(y, 0, half, GROUP_DIM)


def _fourier(z5, seq):
    bsz = z5.shape[0]
    n1 = seq // CHUNK
    nblk = CHUNK // FFT_NB
    a_steps = nblk // FFT_SB
    ga, ff, cs = (jnp.asarray(a) for a in _fft_tables(seq))
    return pl.pallas_call(
        _fft_kernel,
        grid=(bsz, N_GROUPS, a_steps + FFT_BSTEPS),
        in_specs=[pl.BlockSpec((None, None, FFT_SB, n1 * FFT_NB, GROUP_DIM),
                               lambda b, g, j: (b, g, jnp.minimum(j, a_steps - 1), 0, 0)),
                  pl.BlockSpec(ga.shape, lambda b, g, j: (0, 0, 0), pipeline_mode=pl.Buffered(1)),
                  pl.BlockSpec(ff.shape, lambda b, g, j: (0, 0)),
                  pl.BlockSpec(cs.shape, lambda b, g, j: (0, 0))],
        out_specs=pl.BlockSpec((None, None, nblk, n1 * FFT_NB, GROUP_DIM), lambda b, g, j: (b, g, 0, 0, 0)),
        out_shape=jax.ShapeDtypeStruct(z5.shape, F32),
        scratch_shapes=[pltpu.VMEM((CHUNK * 2 * _fft_kh(n1), GROUP_DIM), F32)],
        compiler_params=_vmem_params(("parallel", "parallel", "arbitrary"), "fft"),
        name="fft",
    )(z5, ga, ff, cs)


def _gather_rows(y_refs, k2l, n1):
    half = n1 // 2
    low = jnp.concatenate([a[pl.ds(k2l, half, stride=FFT_NB), :] for a, _ in y_refs], axis=1)
    high = jnp.concatenate([b[pl.ds(half * FFT_NB + FFT_NB - 1 - k2l, half, stride=FFT_NB), :] for _, b in y_refs],
                           axis=1)
    return jnp.concatenate([low, high], axis=0)


def _outmlp_kernel(alpha, ff_chunk, spb, x_ref, hf_ref, hb_ref, o_ref,
                   y0_ref, y1_ref, y2_ref, y3_ref, ym0_ref, ym1_ref, ym2_ref, ym3_ref,
                   g1_ref, sc2_ref, sh2_ref, g2_ref,
                   nw_ref, wout_ref, l1g_ref, l1b_ref, w1_ref, b1_ref, w2_ref, b2_ref, l2g_ref, l2b_ref,
                   out_ref):
    n1 = y0_ref.shape[0] // FFT_NB
    kpt = x_ref.shape[0] // n1
    k2_off = ((pl.program_id(0) % spb) * kpt) % FFT_NB
    sizes = OUTMLP_SUBTILE_K2 if sum(OUTMLP_SUBTILE_K2) == kpt else (kpt,)
    subs, k2l = [], 0
    for kps in sizes:
        subs.append(dict(k2l=k2l, kps=kps, rs=slice(k2l * n1, (k2l + kps) * n1)))
        k2l += kps
    for s in subs:
        rs = s["rs"]
        y_refs = ((y0_ref, ym0_ref), (y1_ref, ym1_ref), (y2_ref, ym2_ref), (y3_ref, ym3_ref))
        yf = jnp.concatenate([_gather_rows(y_refs, k2_off + s["k2l"] + kk, n1) for kk in range(s["kps"])],
                             axis=0).astype(BF16)
        hsum = hf_ref[rs, :].astype(F32) + hb_ref[rs, :].astype(F32)
        heads = [_ln_plain(hsum[:, j * V_DIM:(j + 1) * V_DIM]) for j in range(N_HEADS)]
        ym = jnp.concatenate(heads, axis=1) * nw_ref[...] * jax.nn.sigmoid(o_ref[rs, :].astype(F32))
        s["mix"] = _dot(jnp.concatenate([ym.astype(BF16), yf], axis=1), wout_ref[...])
    for s in subs:
        x1 = _ln_plain(alpha * x_ref[s["rs"], :] + (1.0 + g1_ref[...]) * s["mix"]) * l1g_ref[...] + l1b_ref[...]
        s["x1"] = x1
        s["h2"] = (_ln_plain(x1) * (1.0 + sc2_ref[...]) + sh2_ref[...]).astype(BF16)
    for s in subs:
        ff = b2_ref[...]
        for j in range(w1_ref.shape[1] // ff_chunk):
            sl = slice(j * ff_chunk, (j + 1) * ff_chunk)
            hid = jnp.maximum(_dot(s["h2"], w1_ref[:, sl]) + b1_ref[:, sl], 0.0)
            ff = ff + _dot((hid * hid).astype(BF16), w2_ref[sl, :])
        s["ff"] = ff
    for s in subs:
        out_ref[s["rs"], :] = (_ln_plain(alpha * s["x1"] + (1.0 + g2_ref[...]) * s["ff"]) * l2g_ref[...]
                               + l2b_ref[...])


def _outmlp(x2d, hf, hb, o, y5, mod4, w, seq, tm, alpha, ff_chunk):
    t, d = x2d.shape
    spb = seq // tm
    n1 = seq // CHUNK
    kpt = tm // n1
    assert FFT_NB % kpt == 0
    vw = N_HEADS * V_DIM
    row = lambda width: pl.BlockSpec((tm, width), lambda i: (i, 0))
    nblk = CHUNK // FFT_NB
    kblk = lambda i: ((i % spb) * kpt) // FFT_NB
    yspec = lambda g: pl.BlockSpec((None, None, None, n1 * FFT_NB, GROUP_DIM),
                                   lambda i: (i // spb, g, kblk(i), 0, 0))
    ymspec = lambda g: pl.BlockSpec((None, None, None, n1 * FFT_NB, GROUP_DIM),
                                    lambda i: (i // spb, g, nblk - 1 - kblk(i), 0, 0))
    modspec = lambda j: pl.BlockSpec((None, None, 1, d), lambda i: (i // spb, j, 0, 0))
    const = lambda a: pl.BlockSpec(a.shape, lambda i: (0,) * a.ndim, pipeline_mode=pl.Buffered(1))
    names = ["nw", "wout", "l1g", "l1b", "w1", "b1", "w2", "b2", "l2g", "l2b"]
    return pl.pallas_call(
        functools.partial(_outmlp_kernel, alpha, ff_chunk, spb),
        grid=(t // tm,),
        in_specs=[row(d), row(vw), row(vw), row(vw)]
                 + [yspec(g) for g in range(N_GROUPS)] + [ymspec(g) for g in range(N_GROUPS)]
                 + [modspec(2), modspec(4), modspec(3), modspec(5)] + [const(w[k]) for k in names],
        out_specs=row(d),
        out_shape=jax.ShapeDtypeStruct((t, d), F32),
        compiler_params=_vmem_params(("parallel",), "outmlp"),
        name="outmlp",
    )(x2d, hf, hb, o, *([y5] * (2 * N_GROUPS)), mod4, mod4, mod4, mod4, *[w[k] for k in names])


def _layer_weights(w_in, b_gate, w_out, w_ff1, w_ff2, b_ff1, b_ff2, mlstm_norm_w, ln1_g, ln1_b, ln2_g, ln2_b):
    qk = N_HEADS * QK_DIM
    vw = N_HEADS * V_DIM
    fw = N_GROUPS * GROUP_DIM
    o0, o1, o2, o3, o4 = qk, 2 * qk, 2 * qk + vw, 2 * qk + 2 * vw, 2 * qk + 2 * vw + fw
    wg = w_in[:, o4:].T.reshape(2, 2, N_HEADS, -1)
    bg = b_gate.astype(F32).reshape(2, 2, N_HEADS, 1)
    dup = lambda a: jnp.concatenate([a, a], axis=1)
    r = lambda a: a.astype(F32).reshape(1, -1)
    wq = w_in[:, :o0] * (QK_DIM ** -0.5)
    gate_rows = lambda a: jnp.concatenate([dup(a[:, 0]), dup(a[:, 1])], axis=1).reshape(32, -1)
    return {
        "wrow": jnp.concatenate([wq, w_in[:, o1:o4]], axis=1).astype(BF16),
        "wlane": jnp.concatenate([w_in[:, o0:o1].T, gate_rows(wg)], axis=0).astype(BF16),
        "bg": gate_rows(bg),
        "nw": r(mlstm_norm_w), "wout": w_out, "l1g": r(ln1_g), "l1b": r(ln1_b),
        "w1": w_ff1, "b1": r(b_ff1), "w2": w_ff2, "b2": r(b_ff2),
        "l2g": r(ln2_g), "l2b": r(ln2_b),
    }


def kernel(x, c, w_ada, b_ada, w_in, b_gate, mlstm_norm_w, w_out, ln1_g, ln1_b,
           w_ff1, b_ff1, w_ff2, b_ff2, ln2_g, ln2_b):
    bsz, seq, d = x.shape
    depth = w_ada.shape[0]
    alpha = (2 * depth) ** 0.25
    assert seq % (MLSTM_CPB * CHUNK) == 0 and (seq // CHUNK) % 8 == 0
    assert (seq // CHUNK // 2) % (FFT_BSTEPS * FFT_KB) == 0
    assert d == N_HEADS * V_DIM + N_GROUPS * GROUP_DIM
    tm_out = min(OUTMLP_ROWS, FFT_NB * (seq // CHUNK))
    x2d = x.reshape(bsz * seq, d)
    for l in range(depth):
        w = _layer_weights(w_in[l], b_gate[l], w_out[l], w_ff1[l], w_ff2[l], b_ff1[l], b_ff2[l],
                           mlstm_norm_w[l], ln1_g[l], ln1_b[l], ln2_g[l], ln2_b[l])
        mod4 = _adaln(c, w_ada[l], b_ada[l]).reshape(bsz, N_MOD, 1, d)
        q, kt, v, o, z5, gi, gf, w["wout"], w["w1"], w["w2"] = _inproj(x2d, mod4, w, seq, min(INPROJ_ROWS, seq))
        ra, rb, rc = _gate_prep(gi, gf, bsz)
        hf, hb = _mlstm(q, kt, v, ra, rb, rc, bsz, seq, MLSTM_CPB)
        y5 = _fourier(z5, seq)
        x2d = _outmlp(x2d, hf, hb, o, y5, mod4, w, seq, tm_out, alpha, ff_chunk=1024)
    return x2d.reshape(bsz, seq, d)
```

```python
import functools

import numpy as np
import jax
import jax.numpy as jnp
from jax import lax
from jax.experimental import pallas as pl
from jax.experimental.pallas import tpu as pltpu

F32 = jnp.float32
BF16 = jnp.bfloat16

CHUNK = 128
N_HEADS = 4
QK_DIM = 64
V_DIM = 128
N_GROUPS = 4
GROUP_DIM = 128
N_MOD = 6
LN_EPS = 1e-5
MASKED = -1e30
FFT_NB = 8
FFT_KB = 4
FFT_SB = 16
FFT_GB = 4
FFT_BSTEPS = 1
MLSTM_CPB = 8
OUTMLP_SUBTILE_K2 = (2, 2)
INPROJ_SUBTILES = 4

_NT = (((1,), (1,)), ((), ()))


def _dot(a, b):
    return jnp.dot(a, b, preferred_element_type=F32)


def _dot_nt(a, b):
    return lax.dot_general(a, b, _NT, preferred_element_type=F32)


def _ln_plain(x):
    mu = jnp.mean(x, axis=-1, keepdims=True)
    xc = x - mu
    var = jnp.mean(xc * xc, axis=-1, keepdims=True)
    return xc * lax.rsqrt(var + LN_EPS)


def _log_sigmoid(x):
    return jnp.minimum(x, 0.0) - jnp.log1p(jnp.exp(-jnp.abs(x)))


def _split3(x):
    hi = x.astype(BF16)
    r1 = x - hi.astype(F32)
    mid = r1.astype(BF16)
    lo = (r1 - mid.astype(F32)).astype(BF16)
    return hi, mid, lo


INPROJ_ROWS = 1024
OUTMLP_ROWS = 512
VMEM_LIMIT_MIB = dict(adaln=32, inproj=48, gate_prep=32, mlstm=32, fft=56, outmlp=60)


def _vmem_params(semantics, call):
    return pltpu.CompilerParams(dimension_semantics=semantics, vmem_limit_bytes=VMEM_LIMIT_MIB[call] * 1024 * 1024)


def _adaln_kernel(ct_ref, w_ref, b_ref, o_ref):
    ct = ct_ref[...]
    act = ct * jax.nn.sigmoid(ct)
    w = w_ref[...]
    for b in range(ct.shape[1]):
        o_ref[b:b + 1, :] = jnp.sum(act[:, b:b + 1] * w, axis=0, keepdims=True) + b_ref[...]


def _adaln(c, w_ada, b_ada):
    bsz, d = c.shape
    n = w_ada.shape[1]
    tn = 1024
    return pl.pallas_call(
        _adaln_kernel,
        grid=(n // tn,),
        in_specs=[pl.BlockSpec((d, bsz), lambda j: (0, 0)),
                  pl.BlockSpec((d, tn), lambda j: (0, j)),
                  pl.BlockSpec((1, tn), lambda j: (0, j))],
        out_specs=pl.BlockSpec((bsz, tn), lambda j: (0, j)),
        out_shape=jax.ShapeDtypeStruct((bsz, n), F32),
        compiler_params=_vmem_params(("parallel",), "adaln"),
        name="adaln",
    )(c.T, w_ada, b_ada.reshape(1, n))


def _inproj_kernel(x_ref, sh_ref, sc_ref, wrow_ref, wlane_ref, bg_ref, wout_ref, w1_ref, w2_ref,
                   q_ref, kt_ref, v_ref, o_ref, z_ref, gi_ref, gf_ref, woutb_ref, w1b_ref, w2b_ref):
    woutb_ref[...] = wout_ref[...].astype(BF16)
    w1b_ref[...] = w1_ref[...].astype(BF16)
    w2b_ref[...] = w2_ref[...].astype(BF16)
    qk, vw = q_ref.shape[1], v_ref.shape[1]
    n_chunks = x_ref.shape[0] // CHUNK
    cps = max(1, n_chunks // INPROJ_SUBTILES)
    for c0 in range(0, n_chunks, cps):
        rs = slice(c0 * CHUNK, (c0 + cps) * CHUNK)
        h = _ln_plain(x_ref[rs, :]) * (1.0 + sc_ref[...]) + sh_ref[...]
        hb = h.astype(BF16)
        rowp = _dot(hb, wrow_ref[...])
        q_ref[rs, :] = rowp[:, :qk].astype(BF16)
        v_ref[rs, :] = rowp[:, qk:qk + vw].astype(BF16)
        o_ref[rs, :] = rowp[:, qk + vw:qk + 2 * vw].astype(BF16)
        fz = rowp[:, qk + 2 * vw:]
        for j in range(cps):
            for g in range(N_GROUPS):
                for jb in range(CHUNK // FFT_NB):
                    r0 = j * CHUNK + jb * FFT_NB
                    z_ref[g, jb, (c0 + j) * FFT_NB:(c0 + j + 1) * FFT_NB, :] = (
                        fz[r0:r0 + FFT_NB, g * GROUP_DIM:(g + 1) * GROUP_DIM])
        lanep = _dot_nt(wlane_ref[...], hb)
        gates = lanep[qk:, :] + bg_ref[...]
        for j in range(cps):
            cl = slice(j * CHUNK, (j + 1) * CHUNK)
            kt_ref[c0 + j] = lanep[:qk, cl].astype(BF16)
            for d in range(2):
                gi_ref[d, (c0 + j) * 8:(c0 + j + 1) * 8, :] = gates[16 * d:16 * d + 8, cl]
                gf_ref[d, (c0 + j) * 8:(c0 + j + 1) * 8, :] = gates[16 * d + 8:16 * d + 16, cl]


def _inproj(x2d, mod4, w, seq, tm):
    t, d = x2d.shape
    spb = seq // tm
    qk = N_HEADS * QK_DIM
    vw = N_HEADS * V_DIM
    nblk = CHUNK // FFT_NB
    cpt = tm // CHUNK
    full = lambda a: pl.BlockSpec(a.shape, lambda i: (0,) * a.ndim)
    modspec = lambda j: pl.BlockSpec((None, None, 1, d), lambda i: (i // spb, j, 0, 0))
    names = ["wrow", "wlane", "bg"]
    steps = t // tm
    casts = [w["wout"], w["w1"], w["w2"]]
    assert all(a.shape[0] % (16 * steps) == 0 for a in casts)
    slab = lambda a: pl.BlockSpec((a.shape[0] // steps, a.shape[1]), lambda i: (i, 0))
    return pl.pallas_call(
        _inproj_kernel,
        grid=(steps,),
        in_specs=[pl.BlockSpec((tm, d), lambda i: (i, 0)), modspec(0), modspec(1)] + [full(w[k]) for k in names]
                 + [slab(a) for a in casts],
        out_specs=[pl.BlockSpec((tm, qk), lambda i: (i, 0)),
                   pl.BlockSpec((cpt, qk, CHUNK), lambda i: (i, 0, 0)),
                   pl.BlockSpec((tm, vw), lambda i: (i, 0)),
                   pl.BlockSpec((tm, vw), lambda i: (i, 0)),
                   pl.BlockSpec((None, N_GROUPS, nblk, cpt * FFT_NB, GROUP_DIM),
                                lambda i: (i // spb, 0, 0, i % spb, 0)),
                   pl.BlockSpec((2, cpt * 8, CHUNK), lambda i: (0, i, 0)),
                   pl.BlockSpec((2, cpt * 8, CHUNK), lambda i: (0, i, 0))] + [slab(a) for a in casts],
        out_shape=[jax.ShapeDtypeStruct((t, qk), BF16),
                   jax.ShapeDtypeStruct((t // CHUNK, qk, CHUNK), BF16),
                   jax.ShapeDtypeStruct((t, vw), BF16),
                   jax.ShapeDtypeStruct((t, vw), BF16),
                   jax.ShapeDtypeStruct((t // seq, N_GROUPS, nblk, (seq // CHUNK) * FFT_NB, GROUP_DIM), F32),
                   jax.ShapeDtypeStruct((2, t // CHUNK * 8, CHUNK), F32),
                   jax.ShapeDtypeStruct((2, t // CHUNK * 8, CHUNK), F32)]
                  + [jax.ShapeDtypeStruct(a.shape, BF16) for a in casts],
        compiler_params=_vmem_params(("parallel",), "inproj"),
        name="inproj",
    )(x2d, mod4, mod4, *[w[k] for k in names], *casts)


def _gate_prep_kernel(gi_ref, gf_ref, ra_ref, rb_ref, rc_ref, bend_scr, amax_scr, mprev_scr):
    rows = gi_ref.shape[1]
    n_chunks = rows // 8
    lane = lax.broadcasted_iota(jnp.int32, (rows, CHUNK), 1)
    first = (lax.broadcasted_iota(jnp.int32, (rows, CHUNK), 0) & 7) < N_HEADS
    src = lax.broadcasted_iota(jnp.int32, (CHUNK, 2 * CHUNK), 0)
    dst = lax.broadcasted_iota(jnp.int32, (CHUNK, 2 * CHUNK), 1)
    per_dir = []
    for d in range(2):
        feeds = (src <= dst) if d == 0 else (src >= dst)
        cum_and_total = jnp.where((dst >= CHUNK) | feeds, 1.0, 0.0).astype(BF16)
        bc = sum(_dot(p, cum_and_total) for p in _split3(_log_sigmoid(gf_ref[d])))
        b, b_end = bc[:, :CHUNK], bc[:, CHUNK:]
        a = gi_ref[d] - b
        cm = a
        for k in range(7):
            sh = 1 << k
            if d == 0:
                cm = jnp.maximum(cm, jnp.where(lane >= sh, pltpu.roll(cm, sh, 1), MASKED))
            else:
                cm = jnp.maximum(cm, jnp.where(lane < CHUNK - sh, pltpu.roll(cm, CHUNK - sh, 1), MASKED))
        a_max = jnp.broadcast_to(jnp.max(a, axis=1, keepdims=True), a.shape)
        bend_scr[d] = b_end
        amax_scr[d] = a_max
        per_dir.append((a, b, b_end, cm, a_max))

    def scan(c, ms):
        out = []
        for d, m in enumerate(ms):
            r = pl.ds(pl.multiple_of((c if d == 0 else n_chunks - 1 - c) * 8, 8), 8)
            mprev_scr[d, r, :] = m
            out.append(bend_scr[d, r, :] + jnp.maximum(m, amax_scr[d, r, :]))
        return tuple(out)

    lax.fori_loop(0, n_chunks, scan, (jnp.zeros((8, CHUNK), F32),) * 2)
    for d, (a, b, b_end, cm, a_max) in enumerate(per_dir):
        m_prev = mprev_scr[d]
        u = -jnp.maximum(m_prev, cm)
        m_new = b_end + jnp.maximum(m_prev, a_max)
        ra_ref[d] = jnp.where(first, a, jnp.exp(b_end + a - m_new))
        rb_ref[d] = jnp.where(first, u, jnp.exp(m_prev + u))
        rc_ref[d] = jnp.where(first, u - b, jnp.exp(b_end + m_prev - m_new))


def _gate_prep(gi, gf, bsz):
    rows = gi.shape[1] // bsz
    spec = pl.BlockSpec((2, rows, CHUNK), lambda b: (0, b, 0))
    shape = jax.ShapeDtypeStruct(gi.shape, F32)
    return pl.pallas_call(
        _gate_prep_kernel,
        grid=(bsz,),
        in_specs=[spec, spec],
        out_specs=[spec, spec, spec],
        out_shape=[shape, shape, shape],
        scratch_shapes=[pltpu.VMEM((2, rows, CHUNK), F32)] * 3,
        compiler_params=_vmem_params(("parallel",), "gate_prep"),
        name="gate_prep",
    )(gi, gf)


@functools.lru_cache(maxsize=None)
def _column_selector():
    sel = np.zeros((48, (N_HEADS // 2) * 4 * CHUNK), np.float32)

    def pick(arr, row, lane0, width):
        for piece in range(3):
            sel[arr * 24 + piece * 8 + row, lane0:lane0 + width] = 1.0

    for hd in range(N_HEADS):
        base = 4 * (hd // 2) * CHUNK
        pick(0, hd, base + (hd % 2) * CHUNK, CHUNK)
        pick(1, hd, base + 2 * CHUNK + (hd % 2) * QK_DIM, QK_DIM)
        pick(0, N_HEADS + hd, base + 3 * CHUNK + (hd % 2) * QK_DIM, QK_DIM)
    return sel


def _mlstm_kernel(sel_ref, qf_ref, ktf_ref, vf_ref, raf_ref, rbf_ref, rcf_ref,
                  qb_ref, ktb_ref, vb_ref, rab_ref, rbb_ref, rcb_ref, hf_ref, hb_ref, c_scr):
    @pl.when(pl.program_id(1) == 0)
    def _():
        c_scr[...] = jnp.zeros_like(c_scr)

    cpb = ktf_ref.shape[0]
    t_i = lax.broadcasted_iota(jnp.int32, (CHUNK, CHUNK), 0)
    s_i = lax.broadcasted_iota(jnp.int32, (CHUNK, CHUNK), 1)
    visible = (s_i <= t_i, s_i >= t_i)
    low_lanes = s_i < QK_DIM
    ones = jnp.ones((CHUNK, V_DIM), BF16)
    kzero = jnp.zeros((QK_DIM, CHUNK), BF16)
    czero = jnp.zeros((QK_DIM, 2 * V_DIM), BF16)
    dirs = ((qf_ref, ktf_ref, vf_ref, raf_ref, rbf_ref, rcf_ref, hf_ref),
            (qb_ref, ktb_ref, vb_ref, rab_ref, rbb_ref, rcb_ref, hb_ref))

    def body(i, carry):
        units = []
        for d, (q_ref, kt_ref, v_ref, ra_ref, rb_ref, rc_ref, h_ref) in enumerate(dirs):
            li = i if d == 0 else cpb - 1 - i
            r8 = pl.ds(pl.multiple_of(li * 8, 8), 8)
            rl = pl.ds(pl.multiple_of(li * CHUNK, CHUNK), CHUNK)
            ra = ra_ref[r8, :]
            rc = rc_ref[r8, :]
            pieces = [p.astype(F32) for arr in (rb_ref[r8, :], rc) for p in _split3(arr)]
            colb = lax.dot_general(jnp.concatenate(pieces, axis=0).astype(BF16), sel_ref[...],
                                   (((0,), (0,)), ((), ())), preferred_element_type=F32)
            kt = kt_ref[li]
            for pair in range(N_HEADS // 2):
                k_a = kt[2 * pair * QK_DIM:(2 * pair + 1) * QK_DIM, :]
                k_b = kt[(2 * pair + 1) * QK_DIM:(2 * pair + 2) * QK_DIM, :]
                kdiag = jnp.concatenate([jnp.concatenate([k_a, kzero], axis=1),
                                         jnp.concatenate([kzero, k_b], axis=1)], axis=0)
                q2 = q_ref[rl, 2 * pair * QK_DIM:2 * (pair + 1) * QK_DIM]
                qk2 = _dot(q2, kdiag)
                w_inter2 = colb[:, (4 * pair + 3) * CHUNK:(4 * pair + 4) * CHUNK]
                q_inter2 = (q2.astype(F32) * w_inter2).astype(BF16)
                for hd in (2 * pair, 2 * pair + 1):
                    kth = kt[hd * QK_DIM:(hd + 1) * QK_DIM, :]
                    units.append(dict(d=d, hd=hd, rl=rl, h_ref=h_ref, ra=ra, rc=rc, colb=colb, kth=kth,
                                      q_inter2=q_inter2, v_ref=v_ref,
                                      qk=qk2[:, (hd % 2) * CHUNK:(hd % 2 + 1) * CHUNK]))
        for u in units:
            d, hd, colb = u["d"], u["hd"], u["colb"]
            c0 = (4 * (hd // 2) + hd % 2) * CHUNK
            u_b = colb[:, c0:c0 + CHUNK]
            w_intra = jnp.exp(jnp.where(visible[d], u_b + u["ra"][hd:hd + 1, :], MASKED))
            s = (u["qk"] * w_intra).astype(BF16)
            u["c_prev"] = c_scr[d, hd]
            c_bf = u["c_prev"].astype(BF16)
            c_rows = [c_bf, czero] if hd % 2 == 0 else [czero, c_bf]
            u["v_aug"] = jnp.concatenate([u["v_ref"][u["rl"], hd * V_DIM:(hd + 1) * V_DIM], ones], axis=1)
            u["nd"] = _dot(jnp.concatenate([s, u["q_inter2"]], axis=1),
                           jnp.concatenate([u["v_aug"]] + c_rows, axis=0))
        for u in units:
            hd, nd = u["hd"], u["nd"]
            e0 = (4 * (hd // 2) + 2) * CHUNK
            e2 = jnp.exp(u["colb"][:, e0:e0 + CHUNK])
            own = low_lanes if hd % 2 == 0 else ~low_lanes
            exp_neg_m = jnp.where(own, e2, pltpu.roll(e2, QK_DIM, 1))
            den = jnp.maximum(jnp.abs(nd[:, V_DIM:]), exp_neg_m)
            u["h_ref"][u["rl"], hd * V_DIM:(hd + 1) * V_DIM] = (nd[:, :V_DIM] / den).astype(u["h_ref"].dtype)
            wk_row = u["ra"][N_HEADS + hd:N_HEADS + hd + 1, :]
            decay = jnp.broadcast_to(u["rc"][N_HEADS + hd:N_HEADS + hd + 1, :], (QK_DIM, CHUNK))
            kw = (u["kth"].astype(F32) * wk_row).astype(BF16)
            c_scr[u["d"], hd] = jnp.concatenate([decay, decay], axis=1) * u["c_prev"] + _dot(kw, u["v_aug"])
        return carry

    lax.fori_loop(0, cpb, body, 0, unroll=True)


def _mlstm(q, kt, v, ra, rb, rc, bsz, seq, cpb):
    t = q.shape[0]
    nblk = seq // (cpb * CHUNK)
    qk = N_HEADS * QK_DIM
    vw = N_HEADS * V_DIM
    fwd = lambda b, j: b * nblk + j
    bwd = lambda b, j: b * nblk + nblk - 1 - j

    def specs(blk, d):
        return [pl.BlockSpec((cpb * CHUNK, qk), lambda b, j: (blk(b, j), 0)),
                pl.BlockSpec((cpb, qk, CHUNK), lambda b, j: (blk(b, j), 0, 0)),
                pl.BlockSpec((cpb * CHUNK, vw), lambda b, j: (blk(b, j), 0))] + \
               [pl.BlockSpec((None, cpb * 8, CHUNK), lambda b, j: (d, blk(b, j), 0))] * 3

    sel = jnp.asarray(_column_selector()).astype(BF16)
    return pl.pallas_call(
        _mlstm_kernel,
        grid=(bsz, nblk),
        in_specs=[pl.BlockSpec(sel.shape, lambda b, j: (0, 0))] + specs(fwd, 0) + specs(bwd, 1),
        out_specs=[pl.BlockSpec((cpb * CHUNK, vw), lambda b, j: (fwd(b, j), 0)),
                   pl.BlockSpec((cpb * CHUNK, vw), lambda b, j: (bwd(b, j), 0))],
        out_shape=[jax.ShapeDtypeStruct((t, vw), BF16)] * 2,
        scratch_shapes=[pltpu.VMEM((2, N_HEADS, QK_DIM, 2 * V_DIM), F32)],
        compiler_params=_vmem_params(("parallel", "arbitrary"), "mlstm"),
        name="mlstm",
    )(sel, q, kt, v, ra, rb, rc, q, kt, v, ra, rb, rc)


def _fft_kh(n1):
    return n1 // 2 + FFT_KB


@functools.lru_cache(maxsize=None)
def _fft_tables(seq):
    n1 = seq // CHUNK
    kh = _fft_kh(n1)
    two_pi = 2.0 * np.pi
    k1 = np.arange(kh, dtype=np.int64)
    n = 128 * np.arange(n1, dtype=np.int64)[None, None, :] + np.arange(128, dtype=np.int64)[:, None, None]
    ang = two_pi * ((k1[None, :, None] * n) % seq).astype(np.float64) / seq
    ga = np.concatenate([np.cos(ang), -np.sin(ang)], axis=1)
    ga = ga.reshape(64, 2, 2 * kh, n1).transpose(0, 2, 1, 3).reshape(64, 2 * kh, 2 * n1)
    j = np.arange(128, dtype=np.int64)
    a128 = two_pi * ((j[:, None] * j[None, :]) % 128).astype(np.float64) / 128.0
    ff = np.concatenate([np.cos(a128), -np.sin(a128)], axis=0)
    cc, sc = np.cos(a128), np.sin(a128)
    cs = np.block([[cc, cc], [sc, -sc]]) / np.sqrt(128.0 * seq)
    return tuple(np.asarray(a, dtype=np.float32) for a in (ga, ff, cs))


def _fft_kernel(z_ref, ga_ref, ff_ref, cs_ref, y_ref, scr):
    j = pl.program_id(2)
    n1 = ga_ref.shape[2] // 2
    kh2 = ga_ref.shape[1]
    half = n1 // 2
    a_steps = CHUNK // (FFT_NB * z_ref.shape[0])

    @pl.when(j < a_steps)
    def _():
        zero = jnp.zeros((n1, GROUP_DIM), BF16)
        for sb in range(z_ref.shape[0]):
            for l in range(0, FFT_NB, 2):
                n2 = (j * z_ref.shape[0] + sb) * FFT_NB + l
                z_a = z_ref[sb, pl.ds(l, n1, stride=FFT_NB), :].astype(BF16)
                z_b = z_ref[sb, pl.ds(l + 1, n1, stride=FFT_NB), :].astype(BF16)
                zd = jnp.concatenate([jnp.concatenate([z_a, zero], axis=1),
                                      jnp.concatenate([zero, z_b], axis=1)], axis=0)
                p2 = _dot(ga_ref[n2 // 2], zd)
                for h in range(2):
                    row0 = pl.multiple_of((n2 + h) * kh2, 8)
                    scr[pl.ds(row0, kh2), :] = p2[:, h * GROUP_DIM:(h + 1) * GROUP_DIM]

    def spectra(k0s):
        ms = []
        for k0 in k0s:
            p = [scr[pl.ds(ri * (kh2 // 2) + k0 + kk, CHUNK, stride=kh2), :]
                 for kk in range(FFT_KB) for ri in range(2)]
            ms.append(_dot(ff_ref[...], jnp.concatenate(p, axis=1).astype(BF16)))
        ys = []
        for m in ms:
            x = []
            for kk in range(FFT_KB):
                c0 = 2 * kk * GROUP_DIM
                xr = m[:CHUNK, c0:c0 + GROUP_DIM] - m[CHUNK:, c0 + GROUP_DIM:c0 + 2 * GROUP_DIM]
                xi = m[:CHUNK, c0 + GROUP_DIM:c0 + 2 * GROUP_DIM] + m[CHUNK:, c0:c0 + GROUP_DIM]
                x.append(jnp.concatenate([xr, xi], axis=1))
            ys.append(_dot(jnp.concatenate(x, axis=0).astype(BF16), cs_ref[...]))
        return ys

    def store(y, kk, k1, lane0):
        row0 = pl.multiple_of(k1 * FFT_NB, 8)
        for kb in range(CHUNK // FFT_NB):
            r0 = kk * CHUNK + kb * FFT_NB
            y_ref[kb, pl.ds(row0, FFT_NB), :] = y[r0:r0 + FFT_NB, lane0:lane0 + GROUP_DIM]

    k1_step = half // FFT_BSTEPS
    gb = max(1, min(FFT_GB, k1_step // FFT_KB))

    @pl.when(j >= a_steps)
    def _():
        def body(it, carry):
            k0s = [(j - a_steps) * k1_step + (it * gb + gi) * FFT_KB for gi in range(gb)]
            for k0, y in zip(k0s, spectra(k0s)):
                for kk in range(FFT_KB):
                    k1 = k0 + kk
                    store(y, kk, jnp.where(k1 == 0, 0, n1 - k1), GROUP_DIM)
                    store(y, kk, k1, 0)
            return carry

        lax.fori_loop(0, k1_step // (FFT_KB * gb), body, 0)

    @pl.when(j == a_steps + FFT_BSTEPS - 1)
    def _():
        (y,) = spectra([half])
        store(y, 0, half, GROUP_DIM)


def _fourier(z5, seq):
    bsz = z5.shape[0]
    n1 = seq // CHUNK
    nblk = CHUNK // FFT_NB
    a_steps = nblk // FFT_SB
    ga, ff, cs = (jnp.asarray(a).astype(BF16) for a in _fft_tables(seq))
    return pl.pallas_call(
        _fft_kernel,
        grid=(bsz, N_GROUPS, a_steps + FFT_BSTEPS),
        in_specs=[pl.BlockSpec((None, None, FFT_SB, n1 * FFT_NB, GROUP_DIM),
                               lambda b, g, j: (b, g, jnp.minimum(j, a_steps - 1), 0, 0)),
                  pl.BlockSpec(ga.shape, lambda b, g, j: (0, 0, 0), pipeline_mode=pl.Buffered(1)),
                  pl.BlockSpec(ff.shape, lambda b, g, j: (0, 0)),
                  pl.BlockSpec(cs.shape, lambda b, g, j: (0, 0))],
        out_specs=pl.BlockSpec((None, None, nblk, n1 * FFT_NB, GROUP_DIM), lambda b, g, j: (b, g, 0, 0, 0)),
        out_shape=jax.ShapeDtypeStruct(z5.shape, F32),
        scratch_shapes=[pltpu.VMEM((CHUNK * 2 * _fft_kh(n1), GROUP_DIM), F32)],
        compiler_params=_vmem_params(("parallel", "parallel", "arbitrary"), "fft"),
        name="fft",
    )(z5, ga, ff, cs)


def _gather_rows(y_refs, k2l, n1):
    half = n1 // 2
    low = jnp.concatenate([a[pl.ds(k2l, half, stride=FFT_NB), :] for a, _ in y_refs], axis=1)
    high = jnp.concatenate([b[pl.ds(half * FFT_NB + FFT_NB - 1 - k2l, half, stride=FFT_NB), :] for _, b in y_refs],
                           axis=1)
    return jnp.concatenate([low, high], axis=0)


def _outmlp_kernel(alpha, ff_chunk, spb, x_ref, hf_ref, hb_ref, o_ref,
                   y0_ref, y1_ref, y2_ref, y3_ref, ym0_ref, ym1_ref, ym2_ref, ym3_ref,
                   g1_ref, sc2_ref, sh2_ref, g2_ref,
                   nw_ref, wout_ref, l1g_ref, l1b_ref, w1_ref, b1_ref, w2_ref, b2_ref, l2g_ref, l2b_ref,
                   out_ref):
    n1 = y0_ref.shape[0] // FFT_NB
    kpt = x_ref.shape[0] // n1
    k2_off = ((pl.program_id(0) % spb) * kpt) % FFT_NB
    sizes = OUTMLP_SUBTILE_K2 if sum(OUTMLP_SUBTILE_K2) == kpt else (kpt,)
    subs, k2l = [], 0
    for kps in sizes:
        subs.append(dict(k2l=k2l, kps=kps, rs=slice(k2l * n1, (k2l + kps) * n1)))
        k2l += kps
    for s in subs:
        rs = s["rs"]
        y_refs = ((y0_ref, ym0_ref), (y1_ref, ym1_ref), (y2_ref, ym2_ref), (y3_ref, ym3_ref))
        yf = jnp.concatenate([_gather_rows(y_refs, k2_off + s["k2l"] + kk, n1) for kk in range(s["kps"])],
                             axis=0).astype(BF16)
        hsum = hf_ref[rs, :].astype(F32) + hb_ref[rs, :].astype(F32)
        heads = [_ln_plain(hsum[:, j * V_DIM:(j + 1) * V_DIM]) for j in range(N_HEADS)]
        ym = jnp.concatenate(heads, axis=1) * nw_ref[...] * jax.nn.sigmoid(o_ref[rs, :].astype(F32))
        s["mix"] = _dot(jnp.concatenate([ym.astype(BF16), yf], axis=1), wout_ref[...])
    for s in subs:
        x1 = _ln_plain(alpha * x_ref[s["rs"], :] + (1.0 + g1_ref[...]) * s["mix"]) * l1g_ref[...] + l1b_ref[...]
        s["x1"] = x1
        s["h2"] = (_ln_plain(x1) * (1.0 + sc2_ref[...]) + sh2_ref[...]).astype(BF16)
    for s in subs:
        ff = b2_ref[...]
        for j in range(w1_ref.shape[1] // ff_chunk):
            sl = slice(j * ff_chunk, (j + 1) * ff_chunk)
            hid = jnp.maximum(_dot(s["h2"], w1_ref[:, sl]) + b1_ref[:, sl], 0.0)
            ff = ff + _dot((hid * hid).astype(BF16), w2_ref[sl, :])
        s["ff"] = ff
    for s in subs:
        out_ref[s["rs"], :] = (_ln_plain(alpha * s["x1"] + (1.0 + g2_ref[...]) * s["ff"]) * l2g_ref[...]
                               + l2b_ref[...])


def _outmlp(x2d, hf, hb, o, y5, mod4, w, seq, tm, alpha, ff_chunk):
    t, d = x2d.shape
    spb = seq // tm
    n1 = seq // CHUNK
    kpt = tm // n1
    assert FFT_NB % kpt == 0
    vw = N_HEADS * V_DIM
    row = lambda width: pl.BlockSpec((tm, width), lambda i: (i, 0))
    nblk = CHUNK // FFT_NB
    kblk = lambda i: ((i % spb) * kpt) // FFT_NB
    yspec = lambda g: pl.BlockSpec((None, None, None, n1 * FFT_NB, GROUP_DIM),
                                   lambda i: (i // spb, g, kblk(i), 0, 0))
    ymspec = lambda g: pl.BlockSpec((None, None, None, n1 * FFT_NB, GROUP_DIM),
                                    lambda i: (i // spb, g, nblk - 1 - kblk(i), 0, 0))
    modspec = lambda j: pl.BlockSpec((None, None, 1, d), lambda i: (i // spb, j, 0, 0))
    const = lambda a: pl.BlockSpec(a.shape, lambda i: (0,) * a.ndim, pipeline_mode=pl.Buffered(1))
    names = ["nw", "wout", "l1g", "l1b", "w1", "b1", "w2", "b2", "l2g", "l2b"]
    return pl.pallas_call(
        functools.partial(_outmlp_kernel, alpha, ff_chunk, spb),
        grid=(t // tm,),
        in_specs=[row(d), row(vw), row(vw), row(vw)]
                 + [yspec(g) for g in range(N_GROUPS)] + [ymspec(g) for g in range(N_GROUPS)]
                 + [modspec(2), modspec(4), modspec(3), modspec(5)] + [const(w[k]) for k in names],
        out_specs=row(d),
        out_shape=jax.ShapeDtypeStruct((t, d), F32),
        compiler_params=_vmem_params(("parallel",), "outmlp"),
        name="outmlp",
    )(x2d, hf, hb, o, *([y5] * (2 * N_GROUPS)), mod4, mod4, mod4, mod4, *[w[k] for k in names])


def _layer_weights(w_in, b_gate, w_out, w_ff1, w_ff2, b_ff1, b_ff2, mlstm_norm_w, ln1_g, ln1_b, ln2_g, ln2_b):
    qk = N_HEADS * QK_DIM
    vw = N_HEADS * V_DIM
    fw = N_GROUPS * GROUP_DIM
    o0, o1, o2, o3, o4 = qk, 2 * qk, 2 * qk + vw, 2 * qk + 2 * vw, 2 * qk + 2 * vw + fw
    wg = w_in[:, o4:].T.reshape(2, 2, N_HEADS, -1)
    bg = b_gate.astype(F32).reshape(2, 2, N_HEADS, 1)
    dup = lambda a: jnp.concatenate([a, a], axis=1)
    r = lambda a: a.astype(F32).reshape(1, -1)
    wq = w_in[:, :o0] * (QK_DIM ** -0.5)
    gate_rows = lambda a: jnp.concatenate([dup(a[:, 0]), dup(a[:, 1])], axis=1).reshape(32, -1)
    return {
        "wrow": jnp.concatenate([wq, w_in[:, o1:o4]], axis=1).astype(BF16),
        "wlane": jnp.concatenate([w_in[:, o0:o1].T, gate_rows(wg)], axis=0).astype(BF16),
        "bg": gate_rows(bg),
        "nw": r(mlstm_norm_w), "wout": w_out, "l1g": r(ln1_g), "l1b": r(ln1_b),
        "w1": w_ff1, "b1": r(b_ff1), "w2": w_ff2, "b2": r(b_ff2),
        "l2g": r(ln2_g), "l2b": r(ln2_b),
    }


def kernel(x, c, w_ada, b_ada, w_in, b_gate, mlstm_norm_w, w_out, ln1_g, ln1_b,
           w_ff1, b_ff1, w_ff2, b_ff2, ln2_g, ln2_b):
    bsz, seq, d = x.shape
    depth = w_ada.shape[0]
    alpha = (2 * depth) ** 0.25
    assert seq % (MLSTM_CPB * CHUNK) == 0 and (seq // CHUNK) % 8 == 0
    assert (seq // CHUNK // 2) % (FFT_BSTEPS * FFT_KB) == 0
    assert d == N_HEADS * V_DIM + N_GROUPS * GROUP_DIM
    tm_out = min(OUTMLP_ROWS, FFT_NB * (seq // CHUNK))
    x2d = x.reshape(bsz * seq, d)
    for l in range(depth):
        w = _layer_weights(w_in[l], b_gate[l], w_out[l], w_ff1[l], w_ff2[l], b_ff1[l], b_ff2[l],
                           mlstm_norm_w[l], ln1_g[l], ln1_b[l], ln2_g[l], ln2_b[l])
        mod4 = _adaln(c, w_ada[l], b_ada[l]).reshape(bsz, N_MOD, 1, d)
        q, kt, v, o, z5, gi, gf, w["wout"], w["w1"], w["w2"] = _inproj(x2d, mod4, w, seq, min(INPROJ_ROWS, seq))
        ra, rb, rc = _gate_prep(gi, gf, bsz)
        hf, hb = _mlstm(q, kt, v, ra, rb, rc, bsz, seq, MLSTM_CPB)
        y5 = _fourier(z5, seq)
        x2d = _outmlp(x2d, hf, hb, o, y5, mod4, w, seq, tm_out, alpha, ff_chunk=1024)
    return x2d.reshape(bsz, seq, d)
```

```python
import functools

import numpy as np
import jax
import jax.numpy as jnp
from jax import lax
from jax.experimental import pallas as pl
from jax.experimental.pallas import tpu as pltpu

F32 = jnp.float32
BF16 = jnp.bfloat16

CHUNK = 128
N_HEADS = 4
QK_DIM = 64
V_DIM = 128
N_GROUPS = 4
GROUP_DIM = 128
N_MOD = 6
LN_EPS = 1e-5
MASKED = -1e30
FFT_NB = 8
FFT_KB = 4
FFT_SB = 16
FFT_GB = 4
MLSTM_CPB = 8
OUTMLP_SUBTILE_K2 = (2, 2)
INPROJ_SUBTILES = 4

_NT = (((1,), (1,)), ((), ()))


def _dot(a, b):
    return jnp.dot(a, b, preferred_element_type=F32)


def _dot_nt(a, b):
    return lax.dot_general(a, b, _NT, preferred_element_type=F32)


def _ln_plain(x):
    mu = jnp.mean(x, axis=-1, keepdims=True)
    xc = x - mu
    var = jnp.mean(xc * xc, axis=-1, keepdims=True)
    return xc * lax.rsqrt(var + LN_EPS)


def _log_sigmoid(x):
    return jnp.minimum(x, 0.0) - jnp.log1p(jnp.exp(-jnp.abs(x)))


def _split3(x):
    hi = x.astype(BF16)
    r1 = x - hi.astype(F32)
    mid = r1.astype(BF16)
    lo = (r1 - mid.astype(F32)).astype(BF16)
    return hi, mid, lo


INPROJ_ROWS = 1024
OUTMLP_ROWS = 512
VMEM_LIMIT_MIB = dict(adaln=32, inproj=48, gate_prep=32, mlstm=32, fft=56, outmlp=60)


def _vmem_params(semantics, call):
    return pltpu.CompilerParams(dimension_semantics=semantics, vmem_limit_bytes=VMEM_LIMIT_MIB[call] * 1024 * 1024)


def _adaln_kernel(ct_ref, w_ref, b_ref, o_ref):
    ct = ct_ref[...]
    act = ct * jax.nn.sigmoid(ct)
    w = w_ref[...]
    for b in range(ct.shape[1]):
        o_ref[b:b + 1, :] = jnp.sum(act[:, b:b + 1] * w, axis=0, keepdims=True) + b_ref[...]


def _adaln(c, w_ada, b_ada):
    bsz, d = c.shape
    n = w_ada.shape[1]
    tn = 1024
    return pl.pallas_call(
        _adaln_kernel,
        grid=(n // tn,),
        in_specs=[pl.BlockSpec((d, bsz), lambda j: (0, 0)),
                  pl.BlockSpec((d, tn), lambda j: (0, j)),
                  pl.BlockSpec((1, tn), lambda j: (0, j))],
        out_specs=pl.BlockSpec((bsz, tn), lambda j: (0, j)),
        out_shape=jax.ShapeDtypeStruct((bsz, n), F32),
        compiler_params=_vmem_params(("parallel",), "adaln"),
        name="adaln",
    )(c.T, w_ada, b_ada.reshape(1, n))


def _inproj_kernel(x_ref, sh_ref, sc_ref, wrow_ref, wlane_ref, bg_ref, wout_ref, w1_ref, w2_ref,
                   q_ref, kt_ref, v_ref, o_ref, z_ref, gi_ref, gf_ref, woutb_ref, w1b_ref, w2b_ref):
    woutb_ref[...] = wout_ref[...].astype(BF16)
    w1b_ref[...] = w1_ref[...].astype(BF16)
    w2b_ref[...] = w2_ref[...].astype(BF16)
    qk, vw = q_ref.shape[1], v_ref.shape[1]
    n_chunks = x_ref.shape[0] // CHUNK
    cps = max(1, n_chunks // INPROJ_SUBTILES)
    for c0 in range(0, n_chunks, cps):
        rs = slice(c0 * CHUNK, (c0 + cps) * CHUNK)
        h = _ln_plain(x_ref[rs, :]) * (1.0 + sc_ref[...]) + sh_ref[...]
        hb = h.astype(BF16)
        rowp = _dot(hb, wrow_ref[...])
        q_ref[rs, :] = rowp[:, :qk].astype(BF16)
        v_ref[rs, :] = rowp[:, qk:qk + vw].astype(BF16)
        o_ref[rs, :] = rowp[:, qk + vw:qk + 2 * vw].astype(BF16)
        fz = rowp[:, qk + 2 * vw:]
        for j in range(cps):
            for g in range(N_GROUPS):
                for jb in range(CHUNK // FFT_NB):
                    r0 = j * CHUNK + jb * FFT_NB
                    z_ref[g, jb, (c0 + j) * FFT_NB:(c0 + j + 1) * FFT_NB, :] = (
                        fz[r0:r0 + FFT_NB, g * GROUP_DIM:(g + 1) * GROUP_DIM])
        lanep = _dot_nt(wlane_ref[...], hb)
        gates = lanep[qk:, :] + bg_ref[...]
        for j in range(cps):
            cl = slice(j * CHUNK, (j + 1) * CHUNK)
            kt_ref[c0 + j] = lanep[:qk, cl].astype(BF16)
            for d in range(2):
                gi_ref[d, (c0 + j) * 8:(c0 + j + 1) * 8, :] = gates[16 * d:16 * d + 8, cl]
                gf_ref[d, (c0 + j) * 8:(c0 + j + 1) * 8, :] = gates[16 * d + 8:16 * d + 16, cl]


def _inproj(x2d, mod4, w, seq, tm):
    t, d = x2d.shape
    spb = seq // tm
    qk = N_HEADS * QK_DIM
    vw = N_HEADS * V_DIM
    nblk = CHUNK // FFT_NB
    cpt = tm // CHUNK
    full = lambda a: pl.BlockSpec(a.shape, lambda i: (0,) * a.ndim)
    modspec = lambda j: pl.BlockSpec((None, None, 1, d), lambda i: (i // spb, j, 0, 0))
    names = ["wrow", "wlane", "bg"]
    steps = t // tm
    casts = [w["wout"], w["w1"], w["w2"]]
    assert all(a.shape[0] % (16 * steps) == 0 for a in casts)
    slab = lambda a: pl.BlockSpec((a.shape[0] // steps, a.shape[1]), lambda i: (i, 0))
    return pl.pallas_call(
        _inproj_kernel,
        grid=(steps,),
        in_specs=[pl.BlockSpec((tm, d), lambda i: (i, 0)), modspec(0), modspec(1)] + [full(w[k]) for k in names]
                 + [slab(a) for a in casts],
        out_specs=[pl.BlockSpec((tm, qk), lambda i: (i, 0)),
                   pl.BlockSpec((cpt, qk, CHUNK), lambda i: (i, 0, 0)),
                   pl.BlockSpec((tm, vw), lambda i: (i, 0)),
                   pl.BlockSpec((tm, vw), lambda i: (i, 0)),
                   pl.BlockSpec((None, N_GROUPS, nblk, cpt * FFT_NB, GROUP_DIM),
                                lambda i: (i // spb, 0, 0, i % spb, 0)),
                   pl.BlockSpec((2, cpt * 8, CHUNK), lambda i: (0, i, 0)),
                   pl.BlockSpec((2, cpt * 8, CHUNK), lambda i: (0, i, 0))] + [slab(a) for a in casts],
        out_shape=[jax.ShapeDtypeStruct((t, qk), BF16),
                   jax.ShapeDtypeStruct((t // CHUNK, qk, CHUNK), BF16),
                   jax.ShapeDtypeStruct((t, vw), BF16),
                   jax.ShapeDtypeStruct((t, vw), BF16),
                   jax.ShapeDtypeStruct((t // seq, N_GROUPS, nblk, (seq // CHUNK) * FFT_NB, GROUP_DIM), F32),
                   jax.ShapeDtypeStruct((2, t // CHUNK * 8, CHUNK), F32),
                   jax.ShapeDtypeStruct((2, t // CHUNK * 8, CHUNK), F32)]
                  + [jax.ShapeDtypeStruct(a.shape, BF16) for a in casts],
        compiler_params=_vmem_params(("parallel",), "inproj"),
        name="inproj",
    )(x2d, mod4, mod4, *[w[k] for k in names], *casts)


def _gate_prep_kernel(gi_ref, gf_ref, ra_ref, rb_ref, rc_ref, bend_scr, amax_scr, mprev_scr):
    rows = gi_ref.shape[1]
    n_chunks = rows // 8
    lane = lax.broadcasted_iota(jnp.int32, (rows, CHUNK), 1)
    first = (lax.broadcasted_iota(jnp.int32, (rows, CHUNK), 0) & 7) < N_HEADS
    src = lax.broadcasted_iota(jnp.int32, (CHUNK, 2 * CHUNK), 0)
    dst = lax.broadcasted_iota(jnp.int32, (CHUNK, 2 * CHUNK), 1)
    per_dir = []
    for d in range(2):
        feeds = (src <= dst) if d == 0 else (src >= dst)
        cum_and_total = jnp.where((dst >= CHUNK) | feeds, 1.0, 0.0).astype(BF16)
        bc = sum(_dot(p, cum_and_total) for p in _split3(_log_sigmoid(gf_ref[d])))
        b, b_end = bc[:, :CHUNK], bc[:, CHUNK:]
        a = gi_ref[d] - b
        cm = a
        for k in range(7):
            sh = 1 << k
            if d == 0:
                cm = jnp.maximum(cm, jnp.where(lane >= sh, pltpu.roll(cm, sh, 1), MASKED))
            else:
                cm = jnp.maximum(cm, jnp.where(lane < CHUNK - sh, pltpu.roll(cm, CHUNK - sh, 1), MASKED))
        a_max = jnp.broadcast_to(jnp.max(a, axis=1, keepdims=True), a.shape)
        bend_scr[d] = b_end
        amax_scr[d] = a_max
        per_dir.append((a, b, b_end, cm, a_max))

    def scan(c, ms):
        out = []
        for d, m in enumerate(ms):
            r = pl.ds(pl.multiple_of((c if d == 0 else n_chunks - 1 - c) * 8, 8), 8)
            mprev_scr[d, r, :] = m
            out.append(bend_scr[d, r, :] + jnp.maximum(m, amax_scr[d, r, :]))
        return tuple(out)

    lax.fori_loop(0, n_chunks, scan, (jnp.zeros((8, CHUNK), F32),) * 2)
    for d, (a, b, b_end, cm, a_max) in enumerate(per_dir):
        m_prev = mprev_scr[d]
        u = -jnp.maximum(m_prev, cm)
        m_new = b_end + jnp.maximum(m_prev, a_max)
        ra_ref[d] = jnp.where(first, a, jnp.exp(b_end + a - m_new))
        rb_ref[d] = jnp.where(first, u, jnp.exp(m_prev + u))
        rc_ref[d] = jnp.where(first, u - b, jnp.exp(b_end + m_prev - m_new))


def _gate_prep(gi, gf, bsz):
    rows = gi.shape[1] // bsz
    spec = pl.BlockSpec((2, rows, CHUNK), lambda b: (0, b, 0))
    shape = jax.ShapeDtypeStruct(gi.shape, F32)
    return pl.pallas_call(
        _gate_prep_kernel,
        grid=(bsz,),
        in_specs=[spec, spec],
        out_specs=[spec, spec, spec],
        out_shape=[shape, shape, shape],
        scratch_shapes=[pltpu.VMEM((2, rows, CHUNK), F32)] * 3,
        compiler_params=_vmem_params(("parallel",), "gate_prep"),
        name="gate_prep",
    )(gi, gf)


@functools.lru_cache(maxsize=None)
def _column_selector():
    sel = np.zeros((48, (N_HEADS // 2) * 4 * CHUNK), np.float32)

    def pick(arr, row, lane0, width):
        for piece in range(3):
            sel[arr * 24 + piece * 8 + row, lane0:lane0 + width] = 1.0

    for hd in range(N_HEADS):
        base = 4 * (hd // 2) * CHUNK
        pick(0, hd, base + (hd % 2) * CHUNK, CHUNK)
        pick(1, hd, base + 2 * CHUNK + (hd % 2) * QK_DIM, QK_DIM)
        pick(0, N_HEADS + hd, base + 3 * CHUNK + (hd % 2) * QK_DIM, QK_DIM)
    return sel


def _mlstm_kernel(sel_ref, qf_ref, ktf_ref, vf_ref, raf_ref, rbf_ref, rcf_ref,
                  qb_ref, ktb_ref, vb_ref, rab_ref, rbb_ref, rcb_ref, hf_ref, hb_ref, c_scr):
    @pl.when(pl.program_id(1) == 0)
    def _():
        c_scr[...] = jnp.zeros_like(c_scr)

    cpb = ktf_ref.shape[0]
    t_i = lax.broadcasted_iota(jnp.int32, (CHUNK, CHUNK), 0)
    s_i = lax.broadcasted_iota(jnp.int32, (CHUNK, CHUNK), 1)
    visible = (s_i <= t_i, s_i >= t_i)
    low_lanes = s_i < QK_DIM
    ones = jnp.ones((CHUNK, V_DIM), BF16)
    kzero = jnp.zeros((QK_DIM, CHUNK), BF16)
    czero = jnp.zeros((QK_DIM, 2 * V_DIM), BF16)
    dirs = ((qf_ref, ktf_ref, vf_ref, raf_ref, rbf_ref, rcf_ref, hf_ref),
            (qb_ref, ktb_ref, vb_ref, rab_ref, rbb_ref, rcb_ref, hb_ref))

    def body(i, carry):
        units = []
        for d, (q_ref, kt_ref, v_ref, ra_ref, rb_ref, rc_ref, h_ref) in enumerate(dirs):
            li = i if d == 0 else cpb - 1 - i
            r8 = pl.ds(pl.multiple_of(li * 8, 8), 8)
            rl = pl.ds(pl.multiple_of(li * CHUNK, CHUNK), CHUNK)
            ra = ra_ref[r8, :]
            rc = rc_ref[r8, :]
            pieces = [p.astype(F32) for arr in (rb_ref[r8, :], rc) for p in _split3(arr)]
            colb = lax.dot_general(jnp.concatenate(pieces, axis=0).astype(BF16), sel_ref[...],
                                   (((0,), (0,)), ((), ())), preferred_element_type=F32)
            kt = kt_ref[li]
            for pair in range(N_HEADS // 2):
                k_a = kt[2 * pair * QK_DIM:(2 * pair + 1) * QK_DIM, :]
                k_b = kt[(2 * pair + 1) * QK_DIM:(2 * pair + 2) * QK_DIM, :]
                kdiag = jnp.concatenate([jnp.concatenate([k_a, kzero], axis=1),
                                         jnp.concatenate([kzero, k_b], axis=1)], axis=0)
                q2 = q_ref[rl, 2 * pair * QK_DIM:2 * (pair + 1) * QK_DIM]
                qk2 = _dot(q2, kdiag)
                w_inter2 = colb[:, (4 * pair + 3) * CHUNK:(4 * pair + 4) * CHUNK]
                q_inter2 = (q2.astype(F32) * w_inter2).astype(BF16)
                for hd in (2 * pair, 2 * pair + 1):
                    kth = kt[hd * QK_DIM:(hd + 1) * QK_DIM, :]
                    units.append(dict(d=d, hd=hd, rl=rl, h_ref=h_ref, ra=ra, rc=rc, colb=colb, kth=kth,
                                      q_inter2=q_inter2, v_ref=v_ref,
                                      qk=qk2[:, (hd % 2) * CHUNK:(hd % 2 + 1) * CHUNK]))
        for u in units:
            d, hd, colb = u["d"], u["hd"], u["colb"]
            c0 = (4 * (hd // 2) + hd % 2) * CHUNK
            u_b = colb[:, c0:c0 + CHUNK]
            w_intra = jnp.exp(jnp.where(visible[d], u_b + u["ra"][hd:hd + 1, :], MASKED))
            s = (u["qk"] * w_intra).astype(BF16)
            u["c_prev"] = c_scr[d, hd]
            c_bf = u["c_prev"].astype(BF16)
            c_rows = [c_bf, czero] if hd % 2 == 0 else [czero, c_bf]
            u["v_aug"] = jnp.concatenate([u["v_ref"][u["rl"], hd * V_DIM:(hd + 1) * V_DIM], ones], axis=1)
            u["nd"] = _dot(jnp.concatenate([s, u["q_inter2"]], axis=1),
                           jnp.concatenate([u["v_aug"]] + c_rows, axis=0))
        for u in units:
            hd, nd = u["hd"], u["nd"]
            e0 = (4 * (hd // 2) + 2) * CHUNK
            e2 = jnp.exp(u["colb"][:, e0:e0 + CHUNK])
            own = low_lanes if hd % 2 == 0 else ~low_lanes
            exp_neg_m = jnp.where(own, e2, pltpu.roll(e2, QK_DIM, 1))
            den = jnp.maximum(jnp.abs(nd[:, V_DIM:]), exp_neg_m)
            u["h_ref"][u["rl"], hd * V_DIM:(hd + 1) * V_DIM] = (nd[:, :V_DIM] / den).astype(u["h_ref"].dtype)
            wk_row = u["ra"][N_HEADS + hd:N_HEADS + hd + 1, :]
            decay = jnp.broadcast_to(u["rc"][N_HEADS + hd:N_HEADS + hd + 1, :], (QK_DIM, CHUNK))
            kw = (u["kth"].astype(F32) * wk_row).astype(BF16)
            c_scr[u["d"], hd] = jnp.concatenate([decay, decay], axis=1) * u["c_prev"] + _dot(kw, u["v_aug"])
        return carry

    lax.fori_loop(0, cpb, body, 0, unroll=True)


def _mlstm(q, kt, v, ra, rb, rc, bsz, seq, cpb):
    t = q.shape[0]
    nblk = seq // (cpb * CHUNK)
    qk = N_HEADS * QK_DIM
    vw = N_HEADS * V_DIM
    fwd = lambda b, j: b * nblk + j
    bwd = lambda b, j: b * nblk + nblk - 1 - j

    def specs(blk, d):
        return [pl.BlockSpec((cpb * CHUNK, qk), lambda b, j: (blk(b, j), 0)),
                pl.BlockSpec((cpb, qk, CHUNK), lambda b, j: (blk(b, j), 0, 0)),
                pl.BlockSpec((cpb * CHUNK, vw), lambda b, j: (blk(b, j), 0))] + \
               [pl.BlockSpec((None, cpb * 8, CHUNK), lambda b, j: (d, blk(b, j), 0))] * 3

    sel = jnp.asarray(_column_selector()).astype(BF16)
    return pl.pallas_call(
        _mlstm_kernel,
        grid=(bsz, nblk),
        in_specs=[pl.BlockSpec(sel.shape, lambda b, j: (0, 0))] + specs(fwd, 0) + specs(bwd, 1),
        out_specs=[pl.BlockSpec((cpb * CHUNK, vw), lambda b, j: (fwd(b, j), 0)),
                   pl.BlockSpec((cpb * CHUNK, vw), lambda b, j: (bwd(b, j), 0))],
        out_shape=[jax.ShapeDtypeStruct((t, vw), BF16)] * 2,
        scratch_shapes=[pltpu.VMEM((2, N_HEADS, QK_DIM, 2 * V_DIM), F32)],
        compiler_params=_vmem_params(("parallel", "arbitrary"), "mlstm"),
        name="mlstm",
    )(sel, q, kt, v, ra, rb, rc, q, kt, v, ra, rb, rc)


def _fft_kh(n1):
    return n1 // 2 + FFT_KB


@functools.lru_cache(maxsize=None)
def _fft_tables(seq):
    n1 = seq // CHUNK
    kh = _fft_kh(n1)
    two_pi = 2.0 * np.pi
    k1 = np.arange(kh, dtype=np.int64)
    n = 128 * np.arange(n1, dtype=np.int64)[None, None, :] + np.arange(128, dtype=np.int64)[:, None, None]
    ang = two_pi * ((k1[None, :, None] * n) % seq).astype(np.float64) / seq
    ga = np.concatenate([np.cos(ang), -np.sin(ang)], axis=1)
    ga = ga.reshape(64, 2, 2 * kh, n1).transpose(0, 2, 1, 3).reshape(64, 2 * kh, 2 * n1)
    j = np.arange(128, dtype=np.int64)
    a128 = two_pi * ((j[:, None] * j[None, :]) % 128).astype(np.float64) / 128.0
    ff = np.concatenate([np.cos(a128), -np.sin(a128)], axis=0)
    cc, sc = np.cos(a128), np.sin(a128)
    cs = np.block([[cc, cc], [sc, -sc]]) / np.sqrt(128.0 * seq)
    return tuple(np.asarray(a, dtype=np.float32) for a in (ga, ff, cs))


def _fft_kernel(z_ref, ga_ref, ff_ref, cs_ref, y_ref, scr):
    j = pl.program_id(2)
    n1 = ga_ref.shape[2] // 2
    kh2 = ga_ref.shape[1]
    half = n1 // 2
    a_steps = CHUNK // (FFT_NB * z_ref.shape[0])

    @pl.when(j < a_steps)
    def _():
        zero = jnp.zeros((n1, GROUP_DIM), BF16)
        for sb in range(z_ref.shape[0]):
            for l in range(0, FFT_NB, 2):
                n2 = (j * z_ref.shape[0] + sb) * FFT_NB + l
                z_a = z_ref[sb, pl.ds(l, n1, stride=FFT_NB), :].astype(BF16)
                z_b = z_ref[sb, pl.ds(l + 1, n1, stride=FFT_NB), :].astype(BF16)
                zd = jnp.concatenate([jnp.concatenate([z_a, zero], axis=1),
                                      jnp.concatenate([zero, z_b], axis=1)], axis=0)
                p2 = _dot(ga_ref[n2 // 2], zd)
                for h in range(2):
                    row0 = pl.multiple_of((n2 + h) * kh2, 8)
                    scr[pl.ds(row0, kh2), :] = p2[:, h * GROUP_DIM:(h + 1) * GROUP_DIM]

    def spectra(k0s):
        ms = []
        for k0 in k0s:
            p = [scr[pl.ds(ri * (kh2 // 2) + k0 + kk, CHUNK, stride=kh2), :]
                 for kk in range(FFT_KB) for ri in range(2)]
            ms.append(_dot(ff_ref[...], jnp.concatenate(p, axis=1).astype(BF16)))
        ys = []
        for m in ms:
            x = []
            for kk in range(FFT_KB):
                c0 = 2 * kk * GROUP_DIM
                xr = m[:CHUNK, c0:c0 + GROUP_DIM] - m[CHUNK:, c0 + GROUP_DIM:c0 + 2 * GROUP_DIM]
                xi = m[:CHUNK, c0 + GROUP_DIM:c0 + 2 * GROUP_DIM] + m[CHUNK:, c0:c0 + GROUP_DIM]
                x.append(jnp.concatenate([xr, xi], axis=1))
            ys.append(_dot(jnp.concatenate(x, axis=0).astype(BF16), cs_ref[...]))
        return ys

    def store(y, kk, k1, lane0):
        for kb in range(CHUNK // FFT_NB):
            r0 = kk * CHUNK + kb * FFT_NB
            y_ref[kb, k1 * FFT_NB:(k1 + 1) * FFT_NB, :] = y[r0:r0 + FFT_NB, lane0:lane0 + GROUP_DIM]

    gb = max(1, min(FFT_GB, half // FFT_KB))

    @pl.when(j == a_steps)
    def _():
        for it in range(half // (FFT_KB * gb)):
            k0s = [(it * gb + gi) * FFT_KB for gi in range(gb)]
            for k0, y in zip(k0s, spectra(k0s)):
                for kk in range(FFT_KB):
                    k1 = k0 + kk
                    if k1 > 0:
                        store(y, kk, n1 - k1, GROUP_DIM)
                    store(y, kk, k1, 0)
        (y,) = spectra([half])
        store(y, 0, half, GROUP_DIM)


def _fourier(z5, seq):
    bsz = z5.shape[0]
    n1 = seq // CHUNK
    nblk = CHUNK // FFT_NB
    a_steps = nblk // FFT_SB
    ga, ff, cs = (jnp.asarray(a).astype(BF16) for a in _fft_tables(seq))
    return pl.pallas_call(
        _fft_kernel,
        grid=(bsz, N_GROUPS, a_steps + 1),
        in_specs=[pl.BlockSpec((None, None, FFT_SB, n1 * FFT_NB, GROUP_DIM),
                               lambda b, g, j: (b, g, jnp.minimum(j, a_steps - 1), 0, 0)),
                  pl.BlockSpec(ga.shape, lambda b, g, j: (0, 0, 0), pipeline_mode=pl.Buffered(1)),
                  pl.BlockSpec(ff.shape, lambda b, g, j: (0, 0)),
                  pl.BlockSpec(cs.shape, lambda b, g, j: (0, 0))],
        out_specs=pl.BlockSpec((None, None, nblk, n1 * FFT_NB, GROUP_DIM), lambda b, g, j: (b, g, 0, 0, 0)),
        out_shape=jax.ShapeDtypeStruct(z5.shape, F32),
        scratch_shapes=[pltpu.VMEM((CHUNK * 2 * _fft_kh(n1), GROUP_DIM), F32)],
        compiler_params=_vmem_params(("parallel", "parallel", "arbitrary"), "fft"),
        name="fft",
    )(z5, ga, ff, cs)


def _gather_rows(y_refs, k2l, n1):
    half = n1 // 2
    low = jnp.concatenate([a[pl.ds(k2l, half, stride=FFT_NB), :] for a, _ in y_refs], axis=1)
    high = jnp.concatenate([b[pl.ds(half * FFT_NB + FFT_NB - 1 - k2l, half, stride=FFT_NB), :] for _, b in y_refs],
                           axis=1)
    return jnp.concatenate([low, high], axis=0)


def _outmlp_kernel(alpha, ff_chunk, spb, x_ref, hf_ref, hb_ref, o_ref,
                   y0_ref, y1_ref, y2_ref, y3_ref, ym0_ref, ym1_ref, ym2_ref, ym3_ref,
                   g1_ref, sc2_ref, sh2_ref, g2_ref,
                   nw_ref, wout_ref, l1g_ref, l1b_ref, w1_ref, b1_ref, w2_ref, b2_ref, l2g_ref, l2b_ref,
                   out_ref):
    n1 = y0_ref.shape[0] // FFT_NB
    kpt = x_ref.shape[0] // n1
    k2_off = ((pl.program_id(0) % spb) * kpt) % FFT_NB
    sizes = OUTMLP_SUBTILE_K2 if sum(OUTMLP_SUBTILE_K2) == kpt else (kpt,)
    subs, k2l = [], 0
    for kps in sizes:
        subs.append(dict(k2l=k2l, kps=kps, rs=slice(k2l * n1, (k2l + kps) * n1)))
        k2l += kps
    for s in subs:
        rs = s["rs"]
        y_refs = ((y0_ref, ym0_ref), (y1_ref, ym1_ref), (y2_ref, ym2_ref), (y3_ref, ym3_ref))
        yf = jnp.concatenate([_gather_rows(y_refs, k2_off + s["k2l"] + kk, n1) for kk in range(s["kps"])],
                             axis=0).astype(BF16)
        hsum = hf_ref[rs, :].astype(F32) + hb_ref[rs, :].astype(F32)
        heads = [_ln_plain(hsum[:, j * V_DIM:(j + 1) * V_DIM]) for j in range(N_HEADS)]
        ym = jnp.concatenate(heads, axis=1) * nw_ref[...] * jax.nn.sigmoid(o_ref[rs, :].astype(F32))
        s["mix"] = _dot(jnp.concatenate([ym.astype(BF16), yf], axis=1), wout_ref[...])
    for s in subs:
        x1 = _ln_plain(alpha * x_ref[s["rs"], :] + (1.0 + g1_ref[...]) * s["mix"]) * l1g_ref[...] + l1b_ref[...]
        s["x1"] = x1
        s["h2"] = (_ln_plain(x1) * (1.0 + sc2_ref[...]) + sh2_ref[...]).astype(BF16)
    for s in subs:
        ff = b2_ref[...]
        for j in range(w1_ref.shape[1] // ff_chunk):
            sl = slice(j * ff_chunk, (j + 1) * ff_chunk)
            hid = jnp.maximum(_dot(s["h2"], w1_ref[:, sl]) + b1_ref[:, sl], 0.0)
            ff = ff + _dot((hid * hid).astype(BF16), w2_ref[sl, :])
        s["ff"] = ff
    for s in subs:
        out_ref[s["rs"], :] = (_ln_plain(alpha * s["x1"] + (1.0 + g2_ref[...]) * s["ff"]) * l2g_ref[...]
                               + l2b_ref[...])


def _outmlp(x2d, hf, hb, o, y5, mod4, w, seq, tm, alpha, ff_chunk):
    t, d = x2d.shape
    spb = seq // tm
    n1 = seq // CHUNK
    kpt = tm // n1
    assert FFT_NB % kpt == 0
    vw = N_HEADS * V_DIM
    row = lambda width: pl.BlockSpec((tm, width), lambda i: (i, 0))
    nblk = CHUNK // FFT_NB
    kblk = lambda i: ((i % spb) * kpt) // FFT_NB
    yspec = lambda g: pl.BlockSpec((None, None, None, n1 * FFT_NB, GROUP_DIM),
                                   lambda i: (i // spb, g, kblk(i), 0, 0))
    ymspec = lambda g: pl.BlockSpec((None, None, None, n1 * FFT_NB, GROUP_DIM),
                                    lambda i: (i // spb, g, nblk - 1 - kblk(i), 0, 0))
    modspec = lambda j: pl.BlockSpec((None, None, 1, d), lambda i: (i // spb, j, 0, 0))
    const = lambda a: pl.BlockSpec(a.shape, lambda i: (0,) * a.ndim, pipeline_mode=pl.Buffered(1))
    names = ["nw", "wout", "l1g", "l1b", "w1", "b1", "w2", "b2", "l2g", "l2b"]
    return pl.pallas_call(
        functools.partial(_outmlp_kernel, alpha, ff_chunk, spb),
        grid=(t // tm,),
        in_specs=[row(d), row(vw), row(vw), row(vw)]
                 + [yspec(g) for g in range(N_GROUPS)] + [ymspec(g) for g in range(N_GROUPS)]
                 + [modspec(2), modspec(4), modspec(3), modspec(5)] + [const(w[k]) for k in names],
        out_specs=row(d),
        out_shape=jax.ShapeDtypeStruct((t, d), F32),
        compiler_params=_vmem_params(("parallel",), "outmlp"),
        name="outmlp",
    )(x2d, hf, hb, o, *([y5] * (2 * N_GROUPS)), mod4, mod4, mod4, mod4, *[w[k] for k in names])


def _layer_weights(w_in, b_gate, w_out, w_ff1, w_ff2, b_ff1, b_ff2, mlstm_norm_w, ln1_g, ln1_b, ln2_g, ln2_b):
    qk = N_HEADS * QK_DIM
    vw = N_HEADS * V_DIM
    fw = N_GROUPS * GROUP_DIM
    o0, o1, o2, o3, o4 = qk, 2 * qk, 2 * qk + vw, 2 * qk + 2 * vw, 2 * qk + 2 * vw + fw
    wg = w_in[:, o4:].T.reshape(2, 2, N_HEADS, -1)
    bg = b_gate.astype(F32).reshape(2, 2, N_HEADS, 1)
    dup = lambda a: jnp.concatenate([a, a], axis=1)
    r = lambda a: a.astype(F32).reshape(1, -1)
    wq = w_in[:, :o0] * (QK_DIM ** -0.5)
    gate_rows = lambda a: jnp.concatenate([dup(a[:, 0]), dup(a[:, 1])], axis=1).reshape(32, -1)
    return {
        "wrow": jnp.concatenate([wq, w_in[:, o1:o4]], axis=1).astype(BF16),
        "wlane": jnp.concatenate([w_in[:, o0:o1].T, gate_rows(wg)], axis=0).astype(BF16),
        "bg": gate_rows(bg),
        "nw": r(mlstm_norm_w), "wout": w_out, "l1g": r(ln1_g), "l1b": r(ln1_b),
        "w1": w_ff1, "b1": r(b_ff1), "w2": w_ff2, "b2": r(b_ff2),
        "l2g": r(ln2_g), "l2b": r(ln2_b),
    }


def kernel(x, c, w_ada, b_ada, w_in, b_gate, mlstm_norm_w, w_out, ln1_g, ln1_b,
           w_ff1, b_ff1, w_ff2, b_ff2, ln2_g, ln2_b):
    bsz, seq, d = x.shape
    depth = w_ada.shape[0]
    alpha = (2 * depth) ** 0.25
    assert seq % (MLSTM_CPB * CHUNK) == 0 and (seq // CHUNK) % 8 == 0
    assert (seq // CHUNK // 2) % (FFT_KB * FFT_GB) == 0
    assert d == N_HEADS * V_DIM + N_GROUPS * GROUP_DIM
    tm_out = min(OUTMLP_ROWS, FFT_NB * (seq // CHUNK))
    x2d = x.reshape(bsz * seq, d)
    for l in range(depth):
        w = _layer_weights(w_in[l], b_gate[l], w_out[l], w_ff1[l], w_ff2[l], b_ff1[l], b_ff2[l],
                           mlstm_norm_w[l], ln1_g[l], ln1_b[l], ln2_g[l], ln2_b[l])
        mod4 = _adaln(c, w_ada[l], b_ada[l]).reshape(bsz, N_MOD, 1, d)
        q, kt, v, o, z5, gi, gf, w["wout"], w["w1"], w["w2"] = _inproj(x2d, mod4, w, seq, min(INPROJ_ROWS, seq))
        ra, rb, rc = _gate_prep(gi, gf, bsz)
        hf, hb = _mlstm(q, kt, v, ra, rb, rc, bsz, seq, MLSTM_CPB)
        y5 = _fourier(z5, seq)
        x2d = _outmlp(x2d, hf, hb, o, y5, mod4, w, seq, tm_out, alpha, ff_chunk=1024)
    return x2d.reshape(bsz, seq, d)
```

```python
import functools

import numpy as np
import jax
import jax.numpy as jnp
from jax import lax
from jax.experimental import pallas as pl
from jax.experimental.pallas import tpu as pltpu

F32 = jnp.float32
BF16 = jnp.bfloat16

CHUNK = 128
N_HEADS = 4
QK_DIM = 64
V_DIM = 128
N_GROUPS = 4
GROUP_DIM = 128
N_MOD = 6
LN_EPS = 1e-5
MASKED = -1e30
FFT_NB = 8
FFT_KB = 4
FFT_SB = 16
FFT_GB = 4
MLSTM_CPB = 16
OUTMLP_SUBTILE_K2 = (2, 2)
INPROJ_SUBTILES = 4

_NT = (((1,), (1,)), ((), ()))


def _dot(a, b):
    return jnp.dot(a, b, preferred_element_type=F32)


def _dot_nt(a, b):
    return lax.dot_general(a, b, _NT, preferred_element_type=F32)


def _ln_plain(x):
    mu = jnp.mean(x, axis=-1, keepdims=True)
    xc = x - mu
    var = jnp.mean(xc * xc, axis=-1, keepdims=True)
    return xc * lax.rsqrt(var + LN_EPS)


def _log_sigmoid(x):
    return jnp.minimum(x, 0.0) - jnp.log1p(jnp.exp(-jnp.abs(x)))


def _split3(x):
    hi = x.astype(BF16)
    r1 = x - hi.astype(F32)
    mid = r1.astype(BF16)
    lo = (r1 - mid.astype(F32)).astype(BF16)
    return hi, mid, lo


INPROJ_ROWS = 1024
OUTMLP_ROWS = 512
VMEM_LIMIT_MIB = dict(adaln=32, inproj=48, gate_prep=32, mlstm=32, fft=56, outmlp=60)


def _vmem_params(semantics, call):
    return pltpu.CompilerParams(dimension_semantics=semantics, vmem_limit_bytes=VMEM_LIMIT_MIB[call] * 1024 * 1024)


def _adaln_kernel(ct_ref, w_ref, b_ref, o_ref):
    ct = ct_ref[...]
    act = ct * jax.nn.sigmoid(ct)
    w = w_ref[...]
    for b in range(ct.shape[1]):
        o_ref[b:b + 1, :] = jnp.sum(act[:, b:b + 1] * w, axis=0, keepdims=True) + b_ref[...]


def _adaln(c, w_ada, b_ada):
    bsz, d = c.shape
    n = w_ada.shape[1]
    tn = 1024
    return pl.pallas_call(
        _adaln_kernel,
        grid=(n // tn,),
        in_specs=[pl.BlockSpec((d, bsz), lambda j: (0, 0)),
                  pl.BlockSpec((d, tn), lambda j: (0, j)),
                  pl.BlockSpec((1, tn), lambda j: (0, j))],
        out_specs=pl.BlockSpec((bsz, tn), lambda j: (0, j)),
        out_shape=jax.ShapeDtypeStruct((bsz, n), F32),
        compiler_params=_vmem_params(("parallel",), "adaln"),
        name="adaln",
    )(c.T, w_ada, b_ada.reshape(1, n))


def _inproj_kernel(x_ref, sh_ref, sc_ref, wrow_ref, wlane_ref, bg_ref, wout_ref, w1_ref, w2_ref,
                   q_ref, kt_ref, v_ref, o_ref, z_ref, gi_ref, gf_ref, woutb_ref, w1b_ref, w2b_ref):
    woutb_ref[...] = wout_ref[...].astype(BF16)
    w1b_ref[...] = w1_ref[...].astype(BF16)
    w2b_ref[...] = w2_ref[...].astype(BF16)
    qk, vw = q_ref.shape[1], v_ref.shape[1]
    n_chunks = x_ref.shape[0] // CHUNK
    cps = max(1, n_chunks // INPROJ_SUBTILES)
    for c0 in range(0, n_chunks, cps):
        rs = slice(c0 * CHUNK, (c0 + cps) * CHUNK)
        h = _ln_plain(x_ref[rs, :]) * (1.0 + sc_ref[...]) + sh_ref[...]
        hb = h.astype(BF16)
        rowp = _dot(hb, wrow_ref[...])
        q_ref[rs, :] = rowp[:, :qk].astype(BF16)
        v_ref[rs, :] = rowp[:, qk:qk + vw].astype(BF16)
        o_ref[rs, :] = rowp[:, qk + vw:qk + 2 * vw].astype(BF16)
        fz = rowp[:, qk + 2 * vw:]
        for j in range(cps):
            for g in range(N_GROUPS):
                for jb in range(CHUNK // FFT_NB):
                    r0 = j * CHUNK + jb * FFT_NB
                    z_ref[g, jb, (c0 + j) * FFT_NB:(c0 + j + 1) * FFT_NB, :] = (
                        fz[r0:r0 + FFT_NB, g * GROUP_DIM:(g + 1) * GROUP_DIM])
        lanep = _dot_nt(wlane_ref[...], hb)
        gates = lanep[qk:, :] + bg_ref[...]
        for j in range(cps):
            cl = slice(j * CHUNK, (j + 1) * CHUNK)
            kt_ref[c0 + j] = lanep[:qk, cl].astype(BF16)
            for d in range(2):
                gi_ref[d, (c0 + j) * 8:(c0 + j + 1) * 8, :] = gates[16 * d:16 * d + 8, cl]
                gf_ref[d, (c0 + j) * 8:(c0 + j + 1) * 8, :] = gates[16 * d + 8:16 * d + 16, cl]


def _inproj(x2d, mod4, w, seq, tm):
    t, d = x2d.shape
    spb = seq // tm
    qk = N_HEADS * QK_DIM
    vw = N_HEADS * V_DIM
    nblk = CHUNK // FFT_NB
    cpt = tm // CHUNK
    full = lambda a: pl.BlockSpec(a.shape, lambda i: (0,) * a.ndim)
    modspec = lambda j: pl.BlockSpec((None, None, 1, d), lambda i: (i // spb, j, 0, 0))
    names = ["wrow", "wlane", "bg"]
    steps = t // tm
    casts = [w["wout"], w["w1"], w["w2"]]
    assert all(a.shape[0] % (16 * steps) == 0 for a in casts)
    slab = lambda a: pl.BlockSpec((a.shape[0] // steps, a.shape[1]), lambda i: (i, 0))
    return pl.pallas_call(
        _inproj_kernel,
        grid=(steps,),
        in_specs=[pl.BlockSpec((tm, d), lambda i: (i, 0)), modspec(0), modspec(1)] + [full(w[k]) for k in names]
                 + [slab(a) for a in casts],
        out_specs=[pl.BlockSpec((tm, qk), lambda i: (i, 0)),
                   pl.BlockSpec((cpt, qk, CHUNK), lambda i: (i, 0, 0)),
                   pl.BlockSpec((tm, vw), lambda i: (i, 0)),
                   pl.BlockSpec((tm, vw), lambda i: (i, 0)),
                   pl.BlockSpec((None, N_GROUPS, nblk, cpt * FFT_NB, GROUP_DIM),
                                lambda i: (i // spb, 0, 0, i % spb, 0)),
                   pl.BlockSpec((2, cpt * 8, CHUNK), lambda i: (0, i, 0)),
                   pl.BlockSpec((2, cpt * 8, CHUNK), lambda i: (0, i, 0))] + [slab(a) for a in casts],
        out_shape=[jax.ShapeDtypeStruct((t, qk), BF16),
                   jax.ShapeDtypeStruct((t // CHUNK, qk, CHUNK), BF16),
                   jax.ShapeDtypeStruct((t, vw), BF16),
                   jax.ShapeDtypeStruct((t, vw), BF16),
                   jax.ShapeDtypeStruct((t // seq, N_GROUPS, nblk, (seq // CHUNK) * FFT_NB, GROUP_DIM), F32),
                   jax.ShapeDtypeStruct((2, t // CHUNK * 8, CHUNK), F32),
                   jax.ShapeDtypeStruct((2, t // CHUNK * 8, CHUNK), F32)]
                  + [jax.ShapeDtypeStruct(a.shape, BF16) for a in casts],
        compiler_params=_vmem_params(("parallel",), "inproj"),
        name="inproj",
    )(x2d, mod4, mod4, *[w[k] for k in names], *casts)


def _gate_prep_kernel(gi_ref, gf_ref, ra_ref, rb_ref, rc_ref, bend_scr, amax_scr, mprev_scr):
    rows = gi_ref.shape[1]
    n_chunks = rows // 8
    lane = lax.broadcasted_iota(jnp.int32, (rows, CHUNK), 1)
    first = (lax.broadcasted_iota(jnp.int32, (rows, CHUNK), 0) & 7) < N_HEADS
    src = lax.broadcasted_iota(jnp.int32, (CHUNK, 2 * CHUNK), 0)
    dst = lax.broadcasted_iota(jnp.int32, (CHUNK, 2 * CHUNK), 1)
    per_dir = []
    for d in range(2):
        feeds = (src <= dst) if d == 0 else (src >= dst)
        cum_and_total = jnp.where((dst >= CHUNK) | feeds, 1.0, 0.0).astype(BF16)
        bc = sum(_dot(p, cum_and_total) for p in _split3(_log_sigmoid(gf_ref[d])))
        b, b_end = bc[:, :CHUNK], bc[:, CHUNK:]
        a = gi_ref[d] - b
        cm = a
        for k in range(7):
            sh = 1 << k
            if d == 0:
                cm = jnp.maximum(cm, jnp.where(lane >= sh, pltpu.roll(cm, sh, 1), MASKED))
            else:
                cm = jnp.maximum(cm, jnp.where(lane < CHUNK - sh, pltpu.roll(cm, CHUNK - sh, 1), MASKED))
        a_max = jnp.broadcast_to(jnp.max(a, axis=1, keepdims=True), a.shape)
        bend_scr[d] = b_end
        amax_scr[d] = a_max
        per_dir.append((a, b, b_end, cm, a_max))

    def scan(c, ms):
        out = []
        for d, m in enumerate(ms):
            r = pl.ds(pl.multiple_of((c if d == 0 else n_chunks - 1 - c) * 8, 8), 8)
            mprev_scr[d, r, :] = m
            out.append(bend_scr[d, r, :] + jnp.maximum(m, amax_scr[d, r, :]))
        return tuple(out)

    lax.fori_loop(0, n_chunks, scan, (jnp.zeros((8, CHUNK), F32),) * 2)
    for d, (a, b, b_end, cm, a_max) in enumerate(per_dir):
        m_prev = mprev_scr[d]
        u = -jnp.maximum(m_prev, cm)
        m_new = b_end + jnp.maximum(m_prev, a_max)
        ra_ref[d] = jnp.where(first, a, jnp.exp(b_end + a - m_new))
        rb_ref[d] = jnp.where(first, u, jnp.exp(m_prev + u))
        rc_ref[d] = jnp.where(first, u - b, jnp.exp(b_end + m_prev - m_new))


def _gate_prep(gi, gf, bsz):
    rows = gi.shape[1] // bsz
    spec = pl.BlockSpec((2, rows, CHUNK), lambda b: (0, b, 0))
    shape = jax.ShapeDtypeStruct(gi.shape, F32)
    return pl.pallas_call(
        _gate_prep_kernel,
        grid=(bsz,),
        in_specs=[spec, spec],
        out_specs=[spec, spec, spec],
        out_shape=[shape, shape, shape],
        scratch_shapes=[pltpu.VMEM((2, rows, CHUNK), F32)] * 3,
        compiler_params=_vmem_params(("parallel",), "gate_prep"),
        name="gate_prep",
    )(gi, gf)


@functools.lru_cache(maxsize=None)
def _column_selector():
    sel = np.zeros((48, (N_HEADS // 2) * 4 * CHUNK), np.float32)

    def pick(arr, row, lane0, width):
        for piece in range(3):
            sel[arr * 24 + piece * 8 + row, lane0:lane0 + width] = 1.0

    for hd in range(N_HEADS):
        base = 4 * (hd // 2) * CHUNK
        pick(0, hd, base + (hd % 2) * CHUNK, CHUNK)
        pick(1, hd, base + 2 * CHUNK + (hd % 2) * QK_DIM, QK_DIM)
        pick(0, N_HEADS + hd, base + 3 * CHUNK + (hd % 2) * QK_DIM, QK_DIM)
    return sel


def _mlstm_kernel(sel_ref, qf_ref, ktf_ref, vf_ref, raf_ref, rbf_ref, rcf_ref,
                  qb_ref, ktb_ref, vb_ref, rab_ref, rbb_ref, rcb_ref, hf_ref, hb_ref, c_scr):
    @pl.when(pl.program_id(1) == 0)
    def _():
        c_scr[...] = jnp.zeros_like(c_scr)

    cpb = ktf_ref.shape[0]
    t_i = lax.broadcasted_iota(jnp.int32, (CHUNK, CHUNK), 0)
    s_i = lax.broadcasted_iota(jnp.int32, (CHUNK, CHUNK), 1)
    visible = (s_i <= t_i, s_i >= t_i)
    low_lanes = s_i < QK_DIM
    ones = jnp.ones((CHUNK, V_DIM), BF16)
    kzero = jnp.zeros((QK_DIM, CHUNK), BF16)
    czero = jnp.zeros((QK_DIM, 2 * V_DIM), BF16)
    dirs = ((qf_ref, ktf_ref, vf_ref, raf_ref, rbf_ref, rcf_ref, hf_ref),
            (qb_ref, ktb_ref, vb_ref, rab_ref, rbb_ref, rcb_ref, hb_ref))

    def body(i, carry):
        units = []
        for d, (q_ref, kt_ref, v_ref, ra_ref, rb_ref, rc_ref, h_ref) in enumerate(dirs):
            li = i if d == 0 else cpb - 1 - i
            r8 = pl.ds(pl.multiple_of(li * 8, 8), 8)
            rl = pl.ds(pl.multiple_of(li * CHUNK, CHUNK), CHUNK)
            ra = ra_ref[r8, :]
            rc = rc_ref[r8, :]
            pieces = [p.astype(F32) for arr in (rb_ref[r8, :], rc) for p in _split3(arr)]
            colb = lax.dot_general(jnp.concatenate(pieces, axis=0).astype(BF16), sel_ref[...],
                                   (((0,), (0,)), ((), ())), preferred_element_type=F32)
            kt = kt_ref[li]
            for pair in range(N_HEADS // 2):
                k_a = kt[2 * pair * QK_DIM:(2 * pair + 1) * QK_DIM, :]
                k_b = kt[(2 * pair + 1) * QK_DIM:(2 * pair + 2) * QK_DIM, :]
                kdiag = jnp.concatenate([jnp.concatenate([k_a, kzero], axis=1),
                                         jnp.concatenate([kzero, k_b], axis=1)], axis=0)
                q2 = q_ref[rl, 2 * pair * QK_DIM:2 * (pair + 1) * QK_DIM]
                qk2 = _dot(q2, kdiag)
                w_inter2 = colb[:, (4 * pair + 3) * CHUNK:(4 * pair + 4) * CHUNK]
                q_inter2 = (q2.astype(F32) * w_inter2).astype(BF16)
                for hd in (2 * pair, 2 * pair + 1):
                    kth = kt[hd * QK_DIM:(hd + 1) * QK_DIM, :]
                    units.append(dict(d=d, hd=hd, rl=rl, h_ref=h_ref, ra=ra, rc=rc, colb=colb, kth=kth,
                                      q_inter2=q_inter2, v_ref=v_ref,
                                      qk=qk2[:, (hd % 2) * CHUNK:(hd % 2 + 1) * CHUNK]))
        for u in units:
            d, hd, colb = u["d"], u["hd"], u["colb"]
            c0 = (4 * (hd // 2) + hd % 2) * CHUNK
            u_b = colb[:, c0:c0 + CHUNK]
            w_intra = jnp.exp(jnp.where(visible[d], u_b + u["ra"][hd:hd + 1, :], MASKED))
            s = (u["qk"] * w_intra).astype(BF16)
            u["c_prev"] = c_scr[d, hd]
            c_bf = u["c_prev"].astype(BF16)
            c_rows = [c_bf, czero] if hd % 2 == 0 else [czero, c_bf]
            u["v_aug"] = jnp.concatenate([u["v_ref"][u["rl"], hd * V_DIM:(hd + 1) * V_DIM], ones], axis=1)
            u["nd"] = _dot(jnp.concatenate([s, u["q_inter2"]], axis=1),
                           jnp.concatenate([u["v_aug"]] + c_rows, axis=0))
        for u in units:
            hd, nd = u["hd"], u["nd"]
            e0 = (4 * (hd // 2) + 2) * CHUNK
            e2 = jnp.exp(u["colb"][:, e0:e0 + CHUNK])
            own = low_lanes if hd % 2 == 0 else ~low_lanes
            exp_neg_m = jnp.where(own, e2, pltpu.roll(e2, QK_DIM, 1))
            den = jnp.maximum(jnp.abs(nd[:, V_DIM:]), exp_neg_m)
            u["h_ref"][u["rl"], hd * V_DIM:(hd + 1) * V_DIM] = (nd[:, :V_DIM] / den).astype(u["h_ref"].dtype)
            wk_row = u["ra"][N_HEADS + hd:N_HEADS + hd + 1, :]
            decay = jnp.broadcast_to(u["rc"][N_HEADS + hd:N_HEADS + hd + 1, :], (QK_DIM, CHUNK))
            kw = (u["kth"].astype(F32) * wk_row).astype(BF16)
            c_scr[u["d"], hd] = jnp.concatenate([decay, decay], axis=1) * u["c_prev"] + _dot(kw, u["v_aug"])
        return carry

    lax.fori_loop(0, cpb, body, 0, unroll=True)


def _mlstm(q, kt, v, ra, rb, rc, bsz, seq, cpb):
    t = q.shape[0]
    nblk = seq // (cpb * CHUNK)
    qk = N_HEADS * QK_DIM
    vw = N_HEADS * V_DIM
    fwd = lambda b, j: b * nblk + j
    bwd = lambda b, j: b * nblk + nblk - 1 - j

    def specs(blk, d):
        return [pl.BlockSpec((cpb * CHUNK, qk), lambda b, j: (blk(b, j), 0)),
                pl.BlockSpec((cpb, qk, CHUNK), lambda b, j: (blk(b, j), 0, 0)),
                pl.BlockSpec((cpb * CHUNK, vw), lambda b, j: (blk(b, j), 0))] + \
               [pl.BlockSpec((None, cpb * 8, CHUNK), lambda b, j: (d, blk(b, j), 0))] * 3

    sel = jnp.asarray(_column_selector()).astype(BF16)
    return pl.pallas_call(
        _mlstm_kernel,
        grid=(bsz, nblk),
        in_specs=[pl.BlockSpec(sel.shape, lambda b, j: (0, 0))] + specs(fwd, 0) + specs(bwd, 1),
        out_specs=[pl.BlockSpec((cpb * CHUNK, vw), lambda b, j: (fwd(b, j), 0)),
                   pl.BlockSpec((cpb * CHUNK, vw), lambda b, j: (bwd(b, j), 0))],
        out_shape=[jax.ShapeDtypeStruct((t, vw), BF16)] * 2,
        scratch_shapes=[pltpu.VMEM((2, N_HEADS, QK_DIM, 2 * V_DIM), F32)],
        compiler_params=_vmem_params(("parallel", "arbitrary"), "mlstm"),
        name="mlstm",
    )(sel, q, kt, v, ra, rb, rc, q, kt, v, ra, rb, rc)


def _fft_kh(n1):
    return n1 // 2 + FFT_KB


@functools.lru_cache(maxsize=None)
def _fft_tables(seq):
    n1 = seq // CHUNK
    kh = _fft_kh(n1)
    two_pi = 2.0 * np.pi
    k1 = np.arange(kh, dtype=np.int64)
    n = 128 * np.arange(n1, dtype=np.int64)[None, None, :] + np.arange(128, dtype=np.int64)[:, None, None]
    ang = two_pi * ((k1[None, :, None] * n) % seq).astype(np.float64) / seq
    ga = np.concatenate([np.cos(ang), -np.sin(ang)], axis=1)
    ga = ga.reshape(64, 2, 2 * kh, n1).transpose(0, 2, 1, 3).reshape(64, 2 * kh, 2 * n1)
    j = np.arange(128, dtype=np.int64)
    a128 = two_pi * ((j[:, None] * j[None, :]) % 128).astype(np.float64) / 128.0
    ff = np.concatenate([np.cos(a128), -np.sin(a128)], axis=0)
    cc, sc = np.cos(a128), np.sin(a128)
    cs = np.block([[cc, cc], [sc, -sc]]) / np.sqrt(128.0 * seq)
    return tuple(np.asarray(a, dtype=np.float32) for a in (ga, ff, cs))


def _fft_kernel(z_ref, ga_ref, ff_ref, cs_ref, y_ref, scr):
    j = pl.program_id(2)
    n1 = ga_ref.shape[2] // 2
    kh2 = ga_ref.shape[1]
    half = n1 // 2
    a_steps = CHUNK // (FFT_NB * z_ref.shape[0])

    @pl.when(j < a_steps)
    def _():
        zero = jnp.zeros((n1, GROUP_DIM), BF16)
        for sb in range(z_ref.shape[0]):
            for l in range(0, FFT_NB, 2):
                n2 = (j * z_ref.shape[0] + sb) * FFT_NB + l
                z_a = z_ref[sb, pl.ds(l, n1, stride=FFT_NB), :].astype(BF16)
                z_b = z_ref[sb, pl.ds(l + 1, n1, stride=FFT_NB), :].astype(BF16)
                zd = jnp.concatenate([jnp.concatenate([z_a, zero], axis=1),
                                      jnp.concatenate([zero, z_b], axis=1)], axis=0)
                p2 = _dot(ga_ref[n2 // 2], zd)
                for h in range(2):
                    row0 = pl.multiple_of((n2 + h) * kh2, 8)
                    scr[pl.ds(row0, kh2), :] = p2[:, h * GROUP_DIM:(h + 1) * GROUP_DIM]

    def spectra(k0s):
        ms = []
        for k0 in k0s:
            p = [scr[pl.ds(ri * (kh2 // 2) + k0 + kk, CHUNK, stride=kh2), :]
                 for kk in range(FFT_KB) for ri in range(2)]
            ms.append(_dot(ff_ref[...], jnp.concatenate(p, axis=1).astype(BF16)))
        ys = []
        for m in ms:
            x = []
            for kk in range(FFT_KB):
                c0 = 2 * kk * GROUP_DIM
                xr = m[:CHUNK, c0:c0 + GROUP_DIM] - m[CHUNK:, c0 + GROUP_DIM:c0 + 2 * GROUP_DIM]
                xi = m[:CHUNK, c0 + GROUP_DIM:c0 + 2 * GROUP_DIM] + m[CHUNK:, c0:c0 + GROUP_DIM]
                x.append(jnp.concatenate([xr, xi], axis=1))
            ys.append(_dot(jnp.concatenate(x, axis=0).astype(BF16), cs_ref[...]))
        return ys

    def store(y, kk, k1, lane0):
        for kb in range(CHUNK // FFT_NB):
            r0 = kk * CHUNK + kb * FFT_NB
            y_ref[kb, k1 * FFT_NB:(k1 + 1) * FFT_NB, :] = y[r0:r0 + FFT_NB, lane0:lane0 + GROUP_DIM]

    gb = max(1, min(FFT_GB, half // FFT_KB))

    @pl.when(j == a_steps)
    def _():
        for it in range(half // (FFT_KB * gb)):
            k0s = [(it * gb + gi) * FFT_KB for gi in range(gb)]
            for k0, y in zip(k0s, spectra(k0s)):
                for kk in range(FFT_KB):
                    k1 = k0 + kk
                    if k1 > 0:
                        store(y, kk, n1 - k1, GROUP_DIM)
                    store(y, kk, k1, 0)
        (y,) = spectra([half])
        store(y, 0, half, GROUP_DIM)


def _fourier(z5, seq):
    bsz = z5.shape[0]
    n1 = seq // CHUNK
    nblk = CHUNK // FFT_NB
    a_steps = nblk // FFT_SB
    ga, ff, cs = (jnp.asarray(a).astype(BF16) for a in _fft_tables(seq))
    return pl.pallas_call(
        _fft_kernel,
        grid=(bsz, N_GROUPS, a_steps + 1),
        in_specs=[pl.BlockSpec((None, None, FFT_SB, n1 * FFT_NB, GROUP_DIM),
                               lambda b, g, j: (b, g, jnp.minimum(j, a_steps - 1), 0, 0)),
                  pl.BlockSpec(ga.shape, lambda b, g, j: (0, 0, 0), pipeline_mode=pl.Buffered(1)),
                  pl.BlockSpec(ff.shape, lambda b, g, j: (0, 0)),
                  pl.BlockSpec(cs.shape, lambda b, g, j: (0, 0))],
        out_specs=pl.BlockSpec((None, None, nblk, n1 * FFT_NB, GROUP_DIM), lambda b, g, j: (b, g, 0, 0, 0)),
        out_shape=jax.ShapeDtypeStruct(z5.shape, F32),
        scratch_shapes=[pltpu.VMEM((CHUNK * 2 * _fft_kh(n1), GROUP_DIM), F32)],
        compiler_params=_vmem_params(("parallel", "parallel", "arbitrary"), "fft"),
        name="fft",
    )(z5, ga, ff, cs)


def _gather_rows(y_refs, k2l, n1):
    half = n1 // 2
    low = jnp.concatenate([a[pl.ds(k2l, half, stride=FFT_NB), :] for a, _ in y_refs], axis=1)
    high = jnp.concatenate([b[pl.ds(half * FFT_NB + FFT_NB - 1 - k2l, half, stride=FFT_NB), :] for _, b in y_refs],
                           axis=1)
    return jnp.concatenate([low, high], axis=0)


def _outmlp_kernel(alpha, ff_chunk, spb, x_ref, hf_ref, hb_ref, o_ref,
                   y0_ref, y1_ref, y2_ref, y3_ref, ym0_ref, ym1_ref, ym2_ref, ym3_ref,
                   g1_ref, sc2_ref, sh2_ref, g2_ref,
                   nw_ref, wout_ref, l1g_ref, l1b_ref, w1_ref, b1_ref, w2_ref, b2_ref, l2g_ref, l2b_ref,
                   out_ref):
    n1 = y0_ref.shape[0] // FFT_NB
    kpt = x_ref.shape[0] // n1
    k2_off = ((pl.program_id(0) % spb) * kpt) % FFT_NB
    sizes = OUTMLP_SUBTILE_K2 if sum(OUTMLP_SUBTILE_K2) == kpt else (kpt,)
    subs, k2l = [], 0
    for kps in sizes:
        subs.append(dict(k2l=k2l, kps=kps, rs=slice(k2l * n1, (k2l + kps) * n1)))
        k2l += kps
    for s in subs:
        rs = s["rs"]
        y_refs = ((y0_ref, ym0_ref), (y1_ref, ym1_ref), (y2_ref, ym2_ref), (y3_ref, ym3_ref))
        yf = jnp.concatenate([_gather_rows(y_refs, k2_off + s["k2l"] + kk, n1) for kk in range(s["kps"])],
                             axis=0).astype(BF16)
        hsum = hf_ref[rs, :].astype(F32) + hb_ref[rs, :].astype(F32)
        heads = [_ln_plain(hsum[:, j * V_DIM:(j + 1) * V_DIM]) for j in range(N_HEADS)]
        ym = jnp.concatenate(heads, axis=1) * nw_ref[...] * jax.nn.sigmoid(o_ref[rs, :].astype(F32))
        s["mix"] = _dot(jnp.concatenate([ym.astype(BF16), yf], axis=1), wout_ref[...])
    for s in subs:
        x1 = _ln_plain(alpha * x_ref[s["rs"], :] + (1.0 + g1_ref[...]) * s["mix"]) * l1g_ref[...] + l1b_ref[...]
        s["x1"] = x1
        s["h2"] = (_ln_plain(x1) * (1.0 + sc2_ref[...]) + sh2_ref[...]).astype(BF16)
    for s in subs:
        ff = b2_ref[...]
        for j in range(w1_ref.shape[1] // ff_chunk):
            sl = slice(j * ff_chunk, (j + 1) * ff_chunk)
            hid = jnp.maximum(_dot(s["h2"], w1_ref[:, sl]) + b1_ref[:, sl], 0.0)
            ff = ff + _dot((hid * hid).astype(BF16), w2_ref[sl, :])
        s["ff"] = ff
    for s in subs:
        out_ref[s["rs"], :] = (_ln_plain(alpha * s["x1"] + (1.0 + g2_ref[...]) * s["ff"]) * l2g_ref[...]
                               + l2b_ref[...])


def _outmlp(x2d, hf, hb, o, y5, mod4, w, seq, tm, alpha, ff_chunk):
    t, d = x2d.shape
    spb = seq // tm
    n1 = seq // CHUNK
    kpt = tm // n1
    assert FFT_NB % kpt == 0
    vw = N_HEADS * V_DIM
    row = lambda width: pl.BlockSpec((tm, width), lambda i: (i, 0))
    nblk = CHUNK // FFT_NB
    kblk = lambda i: ((i % spb) * kpt) // FFT_NB
    yspec = lambda g: pl.BlockSpec((None, None, None, n1 * FFT_NB, GROUP_DIM),
                                   lambda i: (i // spb, g, kblk(i), 0, 0))
    ymspec = lambda g: pl.BlockSpec((None, None, None, n1 * FFT_NB, GROUP_DIM),
                                    lambda i: (i // spb, g, nblk - 1 - kblk(i), 0, 0))
    modspec = lambda j: pl.BlockSpec((None, None, 1, d), lambda i: (i // spb, j, 0, 0))
    const = lambda a: pl.BlockSpec(a.shape, lambda i: (0,) * a.ndim, pipeline_mode=pl.Buffered(1))
    names = ["nw", "wout", "l1g", "l1b", "w1", "b1", "w2", "b2", "l2g", "l2b"]
    return pl.pallas_call(
        functools.partial(_outmlp_kernel, alpha, ff_chunk, spb),
        grid=(t // tm,),
        in_specs=[row(d), row(vw), row(vw), row(vw)]
                 + [yspec(g) for g in range(N_GROUPS)] + [ymspec(g) for g in range(N_GROUPS)]
                 + [modspec(2), modspec(4), modspec(3), modspec(5)] + [const(w[k]) for k in names],
        out_specs=row(d),
        out_shape=jax.ShapeDtypeStruct((t, d), F32),
        compiler_params=_vmem_params(("parallel",), "outmlp"),
        name="outmlp",
    )(x2d, hf, hb, o, *([y5] * (2 * N_GROUPS)), mod4, mod4, mod4, mod4, *[w[k] for k in names])


def _layer_weights(w_in, b_gate, w_out, w_ff1, w_ff2, b_ff1, b_ff2, mlstm_norm_w, ln1_g, ln1_b, ln2_g, ln2_b):
    qk = N_HEADS * QK_DIM
    vw = N_HEADS * V_DIM
    fw = N_GROUPS * GROUP_DIM
    o0, o1, o2, o3, o4 = qk, 2 * qk, 2 * qk + vw, 2 * qk + 2 * vw, 2 * qk + 2 * vw + fw
    wg = w_in[:, o4:].T.reshape(2, 2, N_HEADS, -1)
    bg = b_gate.astype(F32).reshape(2, 2, N_HEADS, 1)
    dup = lambda a: jnp.concatenate([a, a], axis=1)
    r = lambda a: a.astype(F32).reshape(1, -1)
    wq = w_in[:, :o0] * (QK_DIM ** -0.5)
    gate_rows = lambda a: jnp.concatenate([dup(a[:, 0]), dup(a[:, 1])], axis=1).reshape(32, -1)
    return {
        "wrow": jnp.concatenate([wq, w_in[:, o1:o4]], axis=1).astype(BF16),
        "wlane": jnp.concatenate([w_in[:, o0:o1].T, gate_rows(wg)], axis=0).astype(BF16),
        "bg": gate_rows(bg),
        "nw": r(mlstm_norm_w), "wout": w_out, "l1g": r(ln1_g), "l1b": r(ln1_b),
        "w1": w_ff1, "b1": r(b_ff1), "w2": w_ff2, "b2": r(b_ff2),
        "l2g": r(ln2_g), "l2b": r(ln2_b),
    }


def kernel(x, c, w_ada, b_ada, w_in, b_gate, mlstm_norm_w, w_out, ln1_g, ln1_b,
           w_ff1, b_ff1, w_ff2, b_ff2, ln2_g, ln2_b):
    bsz, seq, d = x.shape
    depth = w_ada.shape[0]
    alpha = (2 * depth) ** 0.25
    assert seq % (MLSTM_CPB * CHUNK) == 0 and (seq // CHUNK) % 8 == 0
    assert (seq // CHUNK // 2) % (FFT_KB * FFT_GB) == 0
    assert d == N_HEADS * V_DIM + N_GROUPS * GROUP_DIM
    tm_out = min(OUTMLP_ROWS, FFT_NB * (seq // CHUNK))
    x2d = x.reshape(bsz * seq, d)
    for l in range(depth):
        w = _layer_weights(w_in[l], b_gate[l], w_out[l], w_ff1[l], w_ff2[l], b_ff1[l], b_ff2[l],
                           mlstm_norm_w[l], ln1_g[l], ln1_b[l], ln2_g[l], ln2_b[l])
        mod4 = _adaln(c, w_ada[l], b_ada[l]).reshape(bsz, N_MOD, 1, d)
        q, kt, v, o, z5, gi, gf, w["wout"], w["w1"], w["w2"] = _inproj(x2d, mod4, w, seq, min(INPROJ_ROWS, seq))
        ra, rb, rc = _gate_prep(gi, gf, bsz)
        hf, hb = _mlstm(q, kt, v, ra, rb, rc, bsz, seq, MLSTM_CPB)
        y5 = _fourier(z5, seq)
        x2d = _outmlp(x2d, hf, hb, o, y5, mod4, w, seq, tm_out, alpha, ff_chunk=1024)
    return x2d.reshape(bsz, seq, d)
```

```python
import functools

import numpy as np
import jax
import jax.numpy as jnp
from jax import lax
from jax.experimental import pallas as pl
from jax.experimental.pallas import tpu as pltpu

F32 = jnp.float32
BF16 = jnp.bfloat16

CHUNK = 128
N_HEADS = 4
QK_DIM = 64
V_DIM = 128
N_GROUPS = 4
GROUP_DIM = 128
N_MOD = 6
LN_EPS = 1e-5
MASKED = -1e30
FFT_NB = 8
FFT_KB = 4
FFT_SB = 16
FFT_GB = 4
MLSTM_CPB = 16
OUTMLP_SUBTILE_K2 = (2, 2)
INPROJ_SUBTILES = 4

_NT = (((1,), (1,)), ((), ()))


def _dot(a, b):
    return jnp.dot(a, b, preferred_element_type=F32)


def _dot_nt(a, b):
    return lax.dot_general(a, b, _NT, preferred_element_type=F32)


def _ln_plain(x):
    mu = jnp.mean(x, axis=-1, keepdims=True)
    xc = x - mu
    var = jnp.mean(xc * xc, axis=-1, keepdims=True)
    return xc * lax.rsqrt(var + LN_EPS)


def _log_sigmoid(x):
    return jnp.minimum(x, 0.0) - jnp.log1p(jnp.exp(-jnp.abs(x)))


def _split3(x):
    hi = x.astype(BF16)
    r1 = x - hi.astype(F32)
    mid = r1.astype(BF16)
    lo = (r1 - mid.astype(F32)).astype(BF16)
    return hi, mid, lo


INPROJ_ROWS = 1024
OUTMLP_ROWS = 512
VMEM_LIMIT_MIB = dict(adaln=32, inproj=48, gate_prep=32, mlstm=32, fft=56, outmlp=60)


def _vmem_params(semantics, call):
    return pltpu.CompilerParams(dimension_semantics=semantics, vmem_limit_bytes=VMEM_LIMIT_MIB[call] * 1024 * 1024)


def _adaln_kernel(ct_ref, w_ref, b_ref, o_ref):
    ct = ct_ref[...]
    act = ct * jax.nn.sigmoid(ct)
    w = w_ref[...]
    for b in range(ct.shape[1]):
        o_ref[b:b + 1, :] = jnp.sum(act[:, b:b + 1] * w, axis=0, keepdims=True) + b_ref[...]


def _adaln(c, w_ada, b_ada):
    bsz, d = c.shape
    n = w_ada.shape[1]
    tn = 1024
    return pl.pallas_call(
        _adaln_kernel,
        grid=(n // tn,),
        in_specs=[pl.BlockSpec((d, bsz), lambda j: (0, 0)),
                  pl.BlockSpec((d, tn), lambda j: (0, j)),
                  pl.BlockSpec((1, tn), lambda j: (0, j))],
        out_specs=pl.BlockSpec((bsz, tn), lambda j: (0, j)),
        out_shape=jax.ShapeDtypeStruct((bsz, n), F32),
        compiler_params=_vmem_params(("parallel",), "adaln"),
        name="adaln",
    )(c.T, w_ada, b_ada.reshape(1, n))


def _inproj_kernel(x_ref, sh_ref, sc_ref, wrow_ref, wlane_ref, bg_ref, wout_ref, w1_ref, w2_ref,
                   q_ref, kt_ref, v_ref, o_ref, z_ref, gi_ref, gf_ref, woutb_ref, w1b_ref, w2b_ref):
    woutb_ref[...] = wout_ref[...].astype(BF16)
    w1b_ref[...] = w1_ref[...].astype(BF16)
    w2b_ref[...] = w2_ref[...].astype(BF16)
    qk, vw = q_ref.shape[1], v_ref.shape[1]
    n_chunks = x_ref.shape[0] // CHUNK
    cps = max(1, n_chunks // INPROJ_SUBTILES)
    for c0 in range(0, n_chunks, cps):
        rs = slice(c0 * CHUNK, (c0 + cps) * CHUNK)
        h = _ln_plain(x_ref[rs, :]) * (1.0 + sc_ref[...]) + sh_ref[...]
        hb = h.astype(BF16)
        rowp = _dot(hb, wrow_ref[...])
        q_ref[rs, :] = rowp[:, :qk].astype(BF16)
        v_ref[rs, :] = rowp[:, qk:qk + vw].astype(BF16)
        o_ref[rs, :] = rowp[:, qk + vw:qk + 2 * vw].astype(BF16)
        fz = rowp[:, qk + 2 * vw:]
        for j in range(cps):
            for g in range(N_GROUPS):
                for jb in range(CHUNK // FFT_NB):
                    r0 = j * CHUNK + jb * FFT_NB
                    z_ref[g, jb, (c0 + j) * FFT_NB:(c0 + j + 1) * FFT_NB, :] = (
                        fz[r0:r0 + FFT_NB, g * GROUP_DIM:(g + 1) * GROUP_DIM])
        lanep = _dot_nt(wlane_ref[...], hb)
        gates = lanep[qk:, :] + bg_ref[...]
        for j in range(cps):
            cl = slice(j * CHUNK, (j + 1) * CHUNK)
            kt_ref[c0 + j] = lanep[:qk, cl].astype(BF16)
            for d in range(2):
                gi_ref[d, (c0 + j) * 8:(c0 + j + 1) * 8, :] = gates[16 * d:16 * d + 8, cl]
                gf_ref[d, (c0 + j) * 8:(c0 + j + 1) * 8, :] = gates[16 * d + 8:16 * d + 16, cl]


def _inproj(x2d, mod4, w, seq, tm):
    t, d = x2d.shape
    spb = seq // tm
    qk = N_HEADS * QK_DIM
    vw = N_HEADS * V_DIM
    nblk = CHUNK // FFT_NB
    cpt = tm // CHUNK
    full = lambda a: pl.BlockSpec(a.shape, lambda i: (0,) * a.ndim)
    modspec = lambda j: pl.BlockSpec((None, None, 1, d), lambda i: (i // spb, j, 0, 0))
    names = ["wrow", "wlane", "bg"]
    steps = t // tm
    casts = [w["wout"], w["w1"], w["w2"]]
    assert all(a.shape[0] % (16 * steps) == 0 for a in casts)
    slab = lambda a: pl.BlockSpec((a.shape[0] // steps, a.shape[1]), lambda i: (i, 0))
    return pl.pallas_call(
        _inproj_kernel,
        grid=(steps,),
        in_specs=[pl.BlockSpec((tm, d), lambda i: (i, 0)), modspec(0), modspec(1)] + [full(w[k]) for k in names]
                 + [slab(a) for a in casts],
        out_specs=[pl.BlockSpec((tm, qk), lambda i: (i, 0)),
                   pl.BlockSpec((cpt, qk, CHUNK), lambda i: (i, 0, 0)),
                   pl.BlockSpec((tm, vw), lambda i: (i, 0)),
                   pl.BlockSpec((tm, vw), lambda i: (i, 0)),
                   pl.BlockSpec((None, N_GROUPS, nblk, cpt * FFT_NB, GROUP_DIM),
                                lambda i: (i // spb, 0, 0, i % spb, 0)),
                   pl.BlockSpec((2, cpt * 8, CHUNK), lambda i: (0, i, 0)),
                   pl.BlockSpec((2, cpt * 8, CHUNK), lambda i: (0, i, 0))] + [slab(a) for a in casts],
        out_shape=[jax.ShapeDtypeStruct((t, qk), BF16),
                   jax.ShapeDtypeStruct((t // CHUNK, qk, CHUNK), BF16),
                   jax.ShapeDtypeStruct((t, vw), BF16),
                   jax.ShapeDtypeStruct((t, vw), BF16),
                   jax.ShapeDtypeStruct((t // seq, N_GROUPS, nblk, (seq // CHUNK) * FFT_NB, GROUP_DIM), F32),
                   jax.ShapeDtypeStruct((2, t // CHUNK * 8, CHUNK), F32),
                   jax.ShapeDtypeStruct((2, t // CHUNK * 8, CHUNK), F32)]
                  + [jax.ShapeDtypeStruct(a.shape, BF16) for a in casts],
        compiler_params=_vmem_params(("parallel",), "inproj"),
        name="inproj",
    )(x2d, mod4, mod4, *[w[k] for k in names], *casts)


def _gate_prep_kernel(gi_ref, gf_ref, ra_ref, rb_ref, rc_ref, bend_scr, amax_scr, mprev_scr):
    rows = gi_ref.shape[1]
    n_chunks = rows // 8
    lane = lax.broadcasted_iota(jnp.int32, (rows, CHUNK), 1)
    first = (lax.broadcasted_iota(jnp.int32, (rows, CHUNK), 0) & 7) < N_HEADS
    src = lax.broadcasted_iota(jnp.int32, (CHUNK, 2 * CHUNK), 0)
    dst = lax.broadcasted_iota(jnp.int32, (CHUNK, 2 * CHUNK), 1)
    per_dir = []
    for d in range(2):
        feeds = (src <= dst) if d == 0 else (src >= dst)
        cum_and_total = jnp.where((dst >= CHUNK) | feeds, 1.0, 0.0).astype(BF16)
        bc = sum(_dot(p, cum_and_total) for p in _split3(_log_sigmoid(gf_ref[d])))
        b, b_end = bc[:, :CHUNK], bc[:, CHUNK:]
        a = gi_ref[d] - b
        cm = a
        for k in range(7):
            sh = 1 << k
            if d == 0:
                cm = jnp.maximum(cm, jnp.where(lane >= sh, pltpu.roll(cm, sh, 1), MASKED))
            else:
                cm = jnp.maximum(cm, jnp.where(lane < CHUNK - sh, pltpu.roll(cm, CHUNK - sh, 1), MASKED))
        a_max = jnp.broadcast_to(jnp.max(a, axis=1, keepdims=True), a.shape)
        bend_scr[d] = b_end
        amax_scr[d] = a_max
        per_dir.append((a, b, b_end, cm, a_max))

    def scan(c, ms):
        out = []
        for d, m in enumerate(ms):
            r = pl.ds(pl.multiple_of((c if d == 0 else n_chunks - 1 - c) * 8, 8), 8)
            mprev_scr[d, r, :] = m
            out.append(bend_scr[d, r, :] + jnp.maximum(m, amax_scr[d, r, :]))
        return tuple(out)

    lax.fori_loop(0, n_chunks, scan, (jnp.zeros((8, CHUNK), F32),) * 2, unroll=8)
    for d, (a, b, b_end, cm, a_max) in enumerate(per_dir):
        m_prev = mprev_scr[d]
        u = -jnp.maximum(m_prev, cm)
        m_new = b_end + jnp.maximum(m_prev, a_max)
        ra_ref[d] = jnp.where(first, a, jnp.exp(b_end + a - m_new))
        rb_ref[d] = jnp.where(first, u, jnp.exp(m_prev + u))
        rc_ref[d] = jnp.where(first, u - b, jnp.exp(b_end + m_prev - m_new))


def _gate_prep(gi, gf, bsz):
    rows = gi.shape[1] // bsz
    spec = pl.BlockSpec((2, rows, CHUNK), lambda b: (0, b, 0))
    shape = jax.ShapeDtypeStruct(gi.shape, F32)
    return pl.pallas_call(
        _gate_prep_kernel,
        grid=(bsz,),
        in_specs=[spec, spec],
        out_specs=[spec, spec, spec],
        out_shape=[shape, shape, shape],
        scratch_shapes=[pltpu.VMEM((2, rows, CHUNK), F32)] * 3,
        compiler_params=_vmem_params(("parallel",), "gate_prep"),
        name="gate_prep",
    )(gi, gf)


@functools.lru_cache(maxsize=None)
def _column_selector():
    sel = np.zeros((48, (N_HEADS // 2) * 4 * CHUNK), np.float32)

    def pick(arr, row, lane0, width):
        for piece in range(3):
            sel[arr * 24 + piece * 8 + row, lane0:lane0 + width] = 1.0

    for hd in range(N_HEADS):
        base = 4 * (hd // 2) * CHUNK
        pick(0, hd, base + (hd % 2) * CHUNK, CHUNK)
        pick(1, hd, base + 2 * CHUNK + (hd % 2) * QK_DIM, QK_DIM)
        pick(0, N_HEADS + hd, base + 3 * CHUNK + (hd % 2) * QK_DIM, QK_DIM)
    return sel


def _mlstm_kernel(sel_ref, qf_ref, ktf_ref, vf_ref, raf_ref, rbf_ref, rcf_ref,
                  qb_ref, ktb_ref, vb_ref, rab_ref, rbb_ref, rcb_ref, hf_ref, hb_ref, c_scr):
    @pl.when(pl.program_id(1) == 0)
    def _():
        c_scr[...] = jnp.zeros_like(c_scr)

    cpb = ktf_ref.shape[0]
    t_i = lax.broadcasted_iota(jnp.int32, (CHUNK, CHUNK), 0)
    s_i = lax.broadcasted_iota(jnp.int32, (CHUNK, CHUNK), 1)
    visible = (s_i <= t_i, s_i >= t_i)
    low_lanes = s_i < QK_DIM
    ones = jnp.ones((CHUNK, V_DIM), BF16)
    kzero = jnp.zeros((QK_DIM, CHUNK), BF16)
    czero = jnp.zeros((QK_DIM, 2 * V_DIM), BF16)
    dirs = ((qf_ref, ktf_ref, vf_ref, raf_ref, rbf_ref, rcf_ref, hf_ref),
            (qb_ref, ktb_ref, vb_ref, rab_ref, rbb_ref, rcb_ref, hb_ref))

    def body(i, carry):
        units = []
        for d, (q_ref, kt_ref, v_ref, ra_ref, rb_ref, rc_ref, h_ref) in enumerate(dirs):
            li = i if d == 0 else cpb - 1 - i
            r8 = pl.ds(pl.multiple_of(li * 8, 8), 8)
            rl = pl.ds(pl.multiple_of(li * CHUNK, CHUNK), CHUNK)
            ra = ra_ref[r8, :]
            rc = rc_ref[r8, :]
            pieces = [p.astype(F32) for arr in (rb_ref[r8, :], rc) for p in _split3(arr)]
            colb = lax.dot_general(jnp.concatenate(pieces, axis=0).astype(BF16), sel_ref[...],
                                   (((0,), (0,)), ((), ())), preferred_element_type=F32)
            kt = kt_ref[li]
            for pair in range(N_HEADS // 2):
                k_a = kt[2 * pair * QK_DIM:(2 * pair + 1) * QK_DIM, :]
                k_b = kt[(2 * pair + 1) * QK_DIM:(2 * pair + 2) * QK_DIM, :]
                kdiag = jnp.concatenate([jnp.concatenate([k_a, kzero], axis=1),
                                         jnp.concatenate([kzero, k_b], axis=1)], axis=0)
                q2 = q_ref[rl, 2 * pair * QK_DIM:2 * (pair + 1) * QK_DIM]
                qk2 = _dot(q2, kdiag)
                w_inter2 = colb[:, (4 * pair + 3) * CHUNK:(4 * pair + 4) * CHUNK]
                q_inter2 = (q2.astype(F32) * w_inter2).astype(BF16)
                for hd in (2 * pair, 2 * pair + 1):
                    kth = kt[hd * QK_DIM:(hd + 1) * QK_DIM, :]
                    units.append(dict(d=d, hd=hd, rl=rl, h_ref=h_ref, ra=ra, rc=rc, colb=colb, kth=kth,
                                      q_inter2=q_inter2, v_ref=v_ref,
                                      qk=qk2[:, (hd % 2) * CHUNK:(hd % 2 + 1) * CHUNK]))
        for u in units:
            d, hd, colb = u["d"], u["hd"], u["colb"]
            c0 = (4 * (hd // 2) + hd % 2) * CHUNK
            u_b = colb[:, c0:c0 + CHUNK]
            w_intra = jnp.exp(jnp.where(visible[d], u_b + u["ra"][hd:hd + 1, :], MASKED))
            s = (u["qk"] * w_intra).astype(BF16)
            u["c_prev"] = c_scr[d, hd]
            c_bf = u["c_prev"].astype(BF16)
            c_rows = [c_bf, czero] if hd % 2 == 0 else [czero, c_bf]
            u["v_aug"] = jnp.concatenate([u["v_ref"][u["rl"], hd * V_DIM:(hd + 1) * V_DIM], ones], axis=1)
            u["nd"] = _dot(jnp.concatenate([s, u["q_inter2"]], axis=1),
                           jnp.concatenate([u["v_aug"]] + c_rows, axis=0))
        for u in units:
            hd, nd = u["hd"], u["nd"]
            e0 = (4 * (hd // 2) + 2) * CHUNK
            e2 = jnp.exp(u["colb"][:, e0:e0 + CHUNK])
            own = low_lanes if hd % 2 == 0 else ~low_lanes
            exp_neg_m = jnp.where(own, e2, pltpu.roll(e2, QK_DIM, 1))
            den = jnp.maximum(jnp.abs(nd[:, V_DIM:]), exp_neg_m)
            u["h_ref"][u["rl"], hd * V_DIM:(hd + 1) * V_DIM] = (nd[:, :V_DIM] / den).astype(u["h_ref"].dtype)
            wk_row = u["ra"][N_HEADS + hd:N_HEADS + hd + 1, :]
            decay = jnp.broadcast_to(u["rc"][N_HEADS + hd:N_HEADS + hd + 1, :], (QK_DIM, CHUNK))
            kw = (u["kth"].astype(F32) * wk_row).astype(BF16)
            c_scr[u["d"], hd] = jnp.concatenate([decay, decay], axis=1) * u["c_prev"] + _dot(kw, u["v_aug"])
        return carry

    lax.fori_loop(0, cpb, body, 0, unroll=True)


def _mlstm(q, kt, v, ra, rb, rc, bsz, seq, cpb):
    t = q.shape[0]
    nblk = seq // (cpb * CHUNK)
    qk = N_HEADS * QK_DIM
    vw = N_HEADS * V_DIM
    fwd = lambda b, j: b * nblk + j
    bwd = lambda b, j: b * nblk + nblk - 1 - j

    def specs(blk, d):
        return [pl.BlockSpec((cpb * CHUNK, qk), lambda b, j: (blk(b, j), 0)),
                pl.BlockSpec((cpb, qk, CHUNK), lambda b, j: (blk(b, j), 0, 0)),
                pl.BlockSpec((cpb * CHUNK, vw), lambda b, j: (blk(b, j), 0))] + \
               [pl.BlockSpec((None, cpb * 8, CHUNK), lambda b, j: (d, blk(b, j), 0))] * 3

    sel = jnp.asarray(_column_selector()).astype(BF16)
    return pl.pallas_call(
        _mlstm_kernel,
        grid=(bsz, nblk),
        in_specs=[pl.BlockSpec(sel.shape, lambda b, j: (0, 0))] + specs(fwd, 0) + specs(bwd, 1),
        out_specs=[pl.BlockSpec((cpb * CHUNK, vw), lambda b, j: (fwd(b, j), 0)),
                   pl.BlockSpec((cpb * CHUNK, vw), lambda b, j: (bwd(b, j), 0))],
        out_shape=[jax.ShapeDtypeStruct((t, vw), BF16)] * 2,
        scratch_shapes=[pltpu.VMEM((2, N_HEADS, QK_DIM, 2 * V_DIM), F32)],
        compiler_params=_vmem_params(("parallel", "arbitrary"), "mlstm"),
        name="mlstm",
    )(sel, q, kt, v, ra, rb, rc, q, kt, v, ra, rb, rc)


def _fft_kh(n1):
    return n1 // 2 + FFT_KB


@functools.lru_cache(maxsize=None)
def _fft_tables(seq):
    n1 = seq // CHUNK
    kh = _fft_kh(n1)
    two_pi = 2.0 * np.pi
    k1 = np.arange(kh, dtype=np.int64)
    n = 128 * np.arange(n1, dtype=np.int64)[None, None, :] + np.arange(128, dtype=np.int64)[:, None, None]
    ang = two_pi * ((k1[None, :, None] * n) % seq).astype(np.float64) / seq
    ga = np.concatenate([np.cos(ang), -np.sin(ang)], axis=1)
    ga = ga.reshape(64, 2, 2 * kh, n1).transpose(0, 2, 1, 3).reshape(64, 2 * kh, 2 * n1)
    j = np.arange(128, dtype=np.int64)
    a128 = two_pi * ((j[:, None] * j[None, :]) % 128).astype(np.float64) / 128.0
    ff = np.concatenate([np.cos(a128), -np.sin(a128)], axis=0)
    cc, sc = np.cos(a128), np.sin(a128)
    cs = np.block([[cc, cc], [sc, -sc]]) / np.sqrt(128.0 * seq)
    return tuple(np.asarray(a, dtype=np.float32) for a in (ga, ff, cs))


def _fft_kernel(z_ref, ga_ref, ff_ref, cs_ref, y_ref, scr):
    j = pl.program_id(2)
    n1 = ga_ref.shape[2] // 2
    kh2 = ga_ref.shape[1]
    half = n1 // 2
    a_steps = CHUNK // (FFT_NB * z_ref.shape[0])

    @pl.when(j < a_steps)
    def _():
        zero = jnp.zeros((n1, GROUP_DIM), BF16)
        for sb in range(z_ref.shape[0]):
            for l in range(0, FFT_NB, 2):
                n2 = (j * z_ref.shape[0] + sb) * FFT_NB + l
                z_a = z_ref[sb, pl.ds(l, n1, stride=FFT_NB), :].astype(BF16)
                z_b = z_ref[sb, pl.ds(l + 1, n1, stride=FFT_NB), :].astype(BF16)
                zd = jnp.concatenate([jnp.concatenate([z_a, zero], axis=1),
                                      jnp.concatenate([zero, z_b], axis=1)], axis=0)
                p2 = _dot(ga_ref[n2 // 2], zd)
                for h in range(2):
                    row0 = pl.multiple_of((n2 + h) * kh2, 8)
                    scr[pl.ds(row0, kh2), :] = p2[:, h * GROUP_DIM:(h + 1) * GROUP_DIM]

    def spectra(k0s):
        ms = []
        for k0 in k0s:
            p = [scr[pl.ds(ri * (kh2 // 2) + k0 + kk, CHUNK, stride=kh2), :]
                 for kk in range(FFT_KB) for ri in range(2)]
            ms.append(_dot(ff_ref[...], jnp.concatenate(p, axis=1).astype(BF16)))
        ys = []
        for m in ms:
            x = []
            for kk in range(FFT_KB):
                c0 = 2 * kk * GROUP_DIM
                xr = m[:CHUNK, c0:c0 + GROUP_DIM] - m[CHUNK:, c0 + GROUP_DIM:c0 + 2 * GROUP_DIM]
                xi = m[:CHUNK, c0 + GROUP_DIM:c0 + 2 * GROUP_DIM] + m[CHUNK:, c0:c0 + GROUP_DIM]
                x.append(jnp.concatenate([xr, xi], axis=1))
            ys.append(_dot(jnp.concatenate(x, axis=0).astype(BF16), cs_ref[...]))
        return ys

    def store(y, kk, k1, lane0):
        for kb in range(CHUNK // FFT_NB):
            r0 = kk * CHUNK + kb * FFT_NB
            y_ref[kb, k1 * FFT_NB:(k1 + 1) * FFT_NB, :] = y[r0:r0 + FFT_NB, lane0:lane0 + GROUP_DIM]

    gb = max(1, min(FFT_GB, half // FFT_KB))

    @pl.when(j == a_steps)
    def _():
        for it in range(half // (FFT_KB * gb)):
            k0s = [(it * gb + gi) * FFT_KB for gi in range(gb)]
            for k0, y in zip(k0s, spectra(k0s)):
                for kk in range(FFT_KB):
                    k1 = k0 + kk
                    if k1 > 0:
                        store(y, kk, n1 - k1, GROUP_DIM)
                    store(y, kk, k1, 0)
        (y,) = spectra([half])
        store(y, 0, half, GROUP_DIM)


def _fourier(z5, seq):
    bsz = z5.shape[0]
    n1 = seq // CHUNK
    nblk = CHUNK // FFT_NB
    a_steps = nblk // FFT_SB
    ga, ff, cs = (jnp.asarray(a).astype(BF16) for a in _fft_tables(seq))
    return pl.pallas_call(
        _fft_kernel,
        grid=(bsz, N_GROUPS, a_steps + 1),
        in_specs=[pl.BlockSpec((None, None, FFT_SB, n1 * FFT_NB, GROUP_DIM),
                               lambda b, g, j: (b, g, jnp.minimum(j, a_steps - 1), 0, 0)),
                  pl.BlockSpec(ga.shape, lambda b, g, j: (0, 0, 0), pipeline_mode=pl.Buffered(1)),
                  pl.BlockSpec(ff.shape, lambda b, g, j: (0, 0)),
                  pl.BlockSpec(cs.shape, lambda b, g, j: (0, 0))],
        out_specs=pl.BlockSpec((None, None, nblk, n1 * FFT_NB, GROUP_DIM), lambda b, g, j: (b, g, 0, 0, 0)),
        out_shape=jax.ShapeDtypeStruct(z5.shape, F32),
        scratch_shapes=[pltpu.VMEM((CHUNK * 2 * _fft_kh(n1), GROUP_DIM), F32)],
        compiler_params=_vmem_params(("parallel", "parallel", "arbitrary"), "fft"),
        name="fft",
    )(z5, ga, ff, cs)


def _gather_rows(y_refs, k2l, n1):
    half = n1 // 2
    low = jnp.concatenate([a[pl.ds(k2l, half, stride=FFT_NB), :] for a, _ in y_refs], axis=1)
    high = jnp.concatenate([b[pl.ds(half * FFT_NB + FFT_NB - 1 - k2l, half, stride=FFT_NB), :] for _, b in y_refs],
                           axis=1)
    return jnp.concatenate([low, high], axis=0)


def _outmlp_kernel(alpha, ff_chunk, spb, x_ref, hf_ref, hb_ref, o_ref,
                   y0_ref, y1_ref, y2_ref, y3_ref, ym0_ref, ym1_ref, ym2_ref, ym3_ref,
                   g1_ref, sc2_ref, sh2_ref, g2_ref,
                   nw_ref, wout_ref, l1g_ref, l1b_ref, w1_ref, b1_ref, w2_ref, b2_ref, l2g_ref, l2b_ref,
                   out_ref):
    n1 = y0_ref.shape[0] // FFT_NB
    kpt = x_ref.shape[0] // n1
    k2_off = ((pl.program_id(0) % spb) * kpt) % FFT_NB
    sizes = OUTMLP_SUBTILE_K2 if sum(OUTMLP_SUBTILE_K2) == kpt else (kpt,)
    subs, k2l = [], 0
    for kps in sizes:
        subs.append(dict(k2l=k2l, kps=kps, rs=slice(k2l * n1, (k2l + kps) * n1)))
        k2l += kps
    for s in subs:
        rs = s["rs"]
        y_refs = ((y0_ref, ym0_ref), (y1_ref, ym1_ref), (y2_ref, ym2_ref), (y3_ref, ym3_ref))
        yf = jnp.concatenate([_gather_rows(y_refs, k2_off + s["k2l"] + kk, n1) for kk in range(s["kps"])],
                             axis=0).astype(BF16)
        hsum = hf_ref[rs, :].astype(F32) + hb_ref[rs, :].astype(F32)
        heads = [_ln_plain(hsum[:, j * V_DIM:(j + 1) * V_DIM]) for j in range(N_HEADS)]
        ym = jnp.concatenate(heads, axis=1) * nw_ref[...] * jax.nn.sigmoid(o_ref[rs, :].astype(F32))
        s["mix"] = _dot(jnp.concatenate([ym.astype(BF16), yf], axis=1), wout_ref[...])
    for s in subs:
        x1 = _ln_plain(alpha * x_ref[s["rs"], :] + (1.0 + g1_ref[...]) * s["mix"]) * l1g_ref[...] + l1b_ref[...]
        s["x1"] = x1
        s["h2"] = (_ln_plain(x1) * (1.0 + sc2_ref[...]) + sh2_ref[...]).astype(BF16)
    for s in subs:
        ff = b2_ref[...]
        for j in range(w1_ref.shape[1] // ff_chunk):
            sl = slice(j * ff_chunk, (j + 1) * ff_chunk)
            hid = jnp.maximum(_dot(s["h2"], w1_ref[:, sl]) + b1_ref[:, sl], 0.0)
            ff = ff + _dot((hid * hid).astype(BF16), w2_ref[sl, :])
        s["ff"] = ff
    for s in subs:
        out_ref[s["rs"], :] = (_ln_plain(alpha * s["x1"] + (1.0 + g2_ref[...]) * s["ff"]) * l2g_ref[...]
                               + l2b_ref[...])


def _outmlp(x2d, hf, hb, o, y5, mod4, w, seq, tm, alpha, ff_chunk):
    t, d = x2d.shape
    spb = seq // tm
    n1 = seq // CHUNK
    kpt = tm // n1
    assert FFT_NB % kpt == 0
    vw = N_HEADS * V_DIM
    row = lambda width: pl.BlockSpec((tm, width), lambda i: (i, 0))
    nblk = CHUNK // FFT_NB
    kblk = lambda i: ((i % spb) * kpt) // FFT_NB
    yspec = lambda g: pl.BlockSpec((None, None, None, n1 * FFT_NB, GROUP_DIM),
                                   lambda i: (i // spb, g, kblk(i), 0, 0))
    ymspec = lambda g: pl.BlockSpec((None, None, None, n1 * FFT_NB, GROUP_DIM),
                                    lambda i: (i // spb, g, nblk - 1 - kblk(i), 0, 0))
    modspec = lambda j: pl.BlockSpec((None, None, 1, d), lambda i: (i // spb, j, 0, 0))
    const = lambda a: pl.BlockSpec(a.shape, lambda i: (0,) * a.ndim, pipeline_mode=pl.Buffered(1))
    names = ["nw", "wout", "l1g", "l1b", "w1", "b1", "w2", "b2", "l2g", "l2b"]
    return pl.pallas_call(
        functools.partial(_outmlp_kernel, alpha, ff_chunk, spb),
        grid=(t // tm,),
        in_specs=[row(d), row(vw), row(vw), row(vw)]
                 + [yspec(g) for g in range(N_GROUPS)] + [ymspec(g) for g in range(N_GROUPS)]
                 + [modspec(2), modspec(4), modspec(3), modspec(5)] + [const(w[k]) for k in names],
        out_specs=row(d),
        out_shape=jax.ShapeDtypeStruct((t, d), F32),
        compiler_params=_vmem_params(("parallel",), "outmlp"),
        name="outmlp",
    )(x2d, hf, hb, o, *([y5] * (2 * N_GROUPS)), mod4, mod4, mod4, mod4, *[w[k] for k in names])


def _layer_weights(w_in, b_gate, w_out, w_ff1, w_ff2, b_ff1, b_ff2, mlstm_norm_w, ln1_g, ln1_b, ln2_g, ln2_b):
    qk = N_HEADS * QK_DIM
    vw = N_HEADS * V_DIM
    fw = N_GROUPS * GROUP_DIM
    o0, o1, o2, o3, o4 = qk, 2 * qk, 2 * qk + vw, 2 * qk + 2 * vw, 2 * qk + 2 * vw + fw
    wg = w_in[:, o4:].T.reshape(2, 2, N_HEADS, -1)
    bg = b_gate.astype(F32).reshape(2, 2, N_HEADS, 1)
    dup = lambda a: jnp.concatenate([a, a], axis=1)
    r = lambda a: a.astype(F32).reshape(1, -1)
    wq = w_in[:, :o0] * (QK_DIM ** -0.5)
    gate_rows = lambda a: jnp.concatenate([dup(a[:, 0]), dup(a[:, 1])], axis=1).reshape(32, -1)
    return {
        "wrow": jnp.concatenate([wq, w_in[:, o1:o4]], axis=1).astype(BF16),
        "wlane": jnp.concatenate([w_in[:, o0:o1].T, gate_rows(wg)], axis=0).astype(BF16),
        "bg": gate_rows(bg),
        "nw": r(mlstm_norm_w), "wout": w_out, "l1g": r(ln1_g), "l1b": r(ln1_b),
        "w1": w_ff1, "b1": r(b_ff1), "w2": w_ff2, "b2": r(b_ff2),
        "l2g": r(ln2_g), "l2b": r(ln2_b),
    }


def kernel(x, c, w_ada, b_ada, w_in, b_gate, mlstm_norm_w, w_out, ln1_g, ln1_b,
           w_ff1, b_ff1, w_ff2, b_ff2, ln2_g, ln2_b):
    bsz, seq, d = x.shape
    depth = w_ada.shape[0]
    alpha = (2 * depth) ** 0.25
    assert seq % (MLSTM_CPB * CHUNK) == 0 and (seq // CHUNK) % 8 == 0
    assert (seq // CHUNK // 2) % (FFT_KB * FFT_GB) == 0
    assert d == N_HEADS * V_DIM + N_GROUPS * GROUP_DIM
    tm_out = min(OUTMLP_ROWS, FFT_NB * (seq // CHUNK))
    x2d = x.reshape(bsz * seq, d)
    for l in range(depth):
        w = _layer_weights(w_in[l], b_gate[l], w_out[l], w_ff1[l], w_ff2[l], b_ff1[l], b_ff2[l],
                           mlstm_norm_w[l], ln1_g[l], ln1_b[l], ln2_g[l], ln2_b[l])
        mod4 = _adaln(c, w_ada[l], b_ada[l]).reshape(bsz, N_MOD, 1, d)
        q, kt, v, o, z5, gi, gf, w["wout"], w["w1"], w["w2"] = _inproj(x2d, mod4, w, seq, min(INPROJ_ROWS, seq))
        ra, rb, rc = _gate_prep(gi, gf, bsz)
        hf, hb = _mlstm(q, kt, v, ra, rb, rc, bsz, seq, MLSTM_CPB)
        y5 = _fourier(z5, seq)
        x2d = _outmlp(x2d, hf, hb, o, y5, mod4, w, seq, tm_out, alpha, ff_chunk=1024)
    return x2d.reshape(bsz, seq, d)
```
